```python
import math
import jax, jax.numpy as jnp
from jax import lax
import numpy as np

D_MODEL = 1024
BATCH = 2
SEQ = 8192
DEPTH = 1

CHUNK = 64
Q_BLOCK = 128
GDN_HEADS = 4
GDN_DK = 128
GDN_DV = 128
GDN_CONV = 4
DIFF_HEADS = 4
DIFF_DH = 64
N_EXPERTS = 256
TOP_K = 8
N_GROUPS = 8
TOPK_GROUPS = 4
EXPERT_FF = 256
SHARED_FF = 256
ROUTED_SCALE = 2.5
EXPERT_BLOCK = 128
NORM_EPS = 1e-6

GDN_QK = GDN_HEADS * GDN_DK
GDN_V = GDN_HEADS * GDN_DV
CONV_CH = 2 * GDN_QK + GDN_V
DIFF_QK = DIFF_HEADS * 2 * DIFF_DH
DIFF_V = DIFF_HEADS * 2 * DIFF_DH
IN_SPLITS = (GDN_QK, GDN_QK, GDN_V, GDN_V, GDN_HEADS, GDN_HEADS, DIFF_QK, DIFF_QK, DIFF_V, D_MODEL, D_MODEL)
IN_COLS = sum(IN_SPLITS)

kernel_name = "hybrid_gdn_diffattn_moe_block"


def rms_norm(x, gain):
    xf = x.astype(jnp.float32)
    y = xf * lax.rsqrt(jnp.mean(xf * xf, axis=-1, keepdims=True) + NORM_EPS)
    return (y * gain.astype(jnp.float32)).astype(x.dtype)


def l2_normalize(x):
    xf = x.astype(jnp.float32)
    return (xf * lax.rsqrt(jnp.sum(xf * xf, axis=-1, keepdims=True) + NORM_EPS)).astype(x.dtype)


def causal_depthwise_conv(x, w):
    taps = w.shape[0]
    return lax.conv_general_dilated(
        x, w[:, None, :].astype(x.dtype), window_strides=(1,), padding=[(taps - 1, 0)],
        dimension_numbers=("NWC", "WIO", "NWC"), feature_group_count=x.shape[-1])


def gated_delta_rule_chunked(q, k, v, g, beta):
    b, s, h, dk = q.shape
    dv = v.shape[-1]
    n = s // CHUNK
    f32 = jnp.float32

    def chunks(t):
        return t.astype(f32).reshape(b, n, CHUNK, h, -1).transpose(0, 3, 1, 2, 4)

    q, k, v = chunks(q), chunks(k), chunks(v)
    g = chunks(g[..., None])[..., 0]
    beta = chunks(beta[..., None])[..., 0]
    G = jnp.cumsum(g, axis=-1)
    ii = jnp.arange(CHUNK)
    causal = ii[:, None] >= ii[None, :]
    strict = ii[:, None] > ii[None, :]
    decay = jnp.exp(jnp.where(causal, G[..., :, None] - G[..., None, :], -jnp.inf))
    kk = jnp.einsum("bhnik,bhnjk->bhnij", k, k)
    lower = jnp.where(strict, beta[..., :, None] * kk * decay, 0.0)
    a_mat = jnp.eye(CHUNK, dtype=f32) + lower
    rhs = jnp.concatenate([v * beta[..., None], k * (beta * jnp.exp(G))[..., None]], axis=-1)
    sol = lax.linalg.triangular_solve(a_mat, rhs, left_side=True, lower=True, unit_diagonal=True)
    u, w = sol[..., :dv], sol[..., dv:]
    a_qk = jnp.einsum("bhnik,bhnjk->bhnij", q, k) * decay

    def step(state, inp):
        qc, kc, uc, wc, gc, ac = inp
        v_new = uc - jnp.einsum("bhck,bhkv->bhcv", wc, state)
        out = (jnp.einsum("bhck,bhkv->bhcv", qc * jnp.exp(gc)[..., None], state)
               + jnp.einsum("bhcj,bhjv->bhcv", ac, v_new))
        g_last = gc[..., -1:]
        state = (state * jnp.exp(g_last)[..., None]
                 + jnp.einsum("bhck,bhcv->bhkv", kc * jnp.exp(g_last - gc)[..., None], v_new))
        return state, out

    mv = lambda t: jnp.moveaxis(t, 2, 0)
    state0 = jnp.zeros((b, h, dk, dv), f32)
    _, o = lax.scan(step, state0, (mv(q), mv(k), mv(u), mv(w), mv(G), mv(a_qk)))
    return o.transpose(1, 0, 3, 2, 4).reshape(b, s, h, dv)


def differential_attention(q, k, v, lam):
    b, s, h, _, d = q.shape
    n_blocks = s // Q_BLOCK
    pos = jnp.arange(s)
    slopes = 2.0 ** (-8.0 * jnp.arange(1, h + 1, dtype=jnp.float32) / h)
    scale = d ** -0.5

    def one_block(bi):
        start = bi * Q_BLOCK
        qb = lax.dynamic_slice_in_dim(q, start, Q_BLOCK, axis=1)
        tq = start + jnp.arange(Q_BLOCK)
        sc = jnp.einsum("bqhcd,bkhcd->bhcqk", qb, k, preferred_element_type=jnp.float32) * scale
        dist = jnp.abs(tq[:, None] - pos[None, :]).astype(jnp.float32)
        allowed = (pos[None, :] // CHUNK) <= (tq[:, None] // CHUNK)
        sc = jnp.where(allowed, sc - slopes[:, None, None, None] * dist, -jnp.inf)
        p = jax.nn.softmax(sc, axis=-1)
        a = p[:, :, 0] - lam * p[:, :, 1]
        return jnp.einsum("bhqk,bkhe->bqhe", a.astype(v.dtype), v)

    o = lax.map(one_block, jnp.arange(n_blocks))
    return o.transpose(1, 0, 2, 3, 4).reshape(b, s, h, 2 * d)


def hybrid_mixer(h, w_in, conv_w, a_log, dt_bias, gdn_norm_g, w_o_gdn, q_norm_g, k_norm_g,
                 lambda_q1, lambda_k1, lambda_q2, lambda_k2, subln_g, w_o_diff, w_out, lam_init):
    b, s, _ = h.shape
    f32 = jnp.float32
    proj = jnp.einsum("bsd,de->bse", h, w_in)
    split_at = np.cumsum(IN_SPLITS)[:-1].tolist()
    (qa, ka, va, za, beta_lin, decay_lin, qb, kb, vb, gate_a, gate_b) = jnp.split(proj, split_at, axis=-1)

    qkv = jax.nn.silu(causal_depthwise_conv(jnp.concatenate([qa, ka, va], axis=-1), conv_w))
    qa, ka, va = jnp.split(qkv, [GDN_QK, 2 * GDN_QK], axis=-1)
    qa = l2_normalize(qa.reshape(b, s, GDN_HEADS, GDN_DK)) * (GDN_DK ** -0.5)
    ka = l2_normalize(ka.reshape(b, s, GDN_HEADS, GDN_DK))
    va = va.reshape(b, s, GDN_HEADS, GDN_DV)
    beta = jax.nn.sigmoid(beta_lin.astype(f32))
    log_decay = -jnp.exp(a_log.astype(f32)) * jax.nn.softplus(decay_lin.astype(f32) + dt_bias.astype(f32))
    oa = gated_delta_rule_chunked(qa, ka, va, log_decay, beta)
    oa = rms_norm(oa, gdn_norm_g) * jax.nn.silu(za.reshape(b, s, GDN_HEADS, GDN_DV).astype(f32))
    ya = jnp.einsum("bse,ed->bsd", oa.reshape(b, s, GDN_V).astype(h.dtype), w_o_gdn)

    qb = rms_norm(qb.reshape(b, s, DIFF_HEADS, 2, DIFF_DH), q_norm_g)
    kb = rms_norm(kb.reshape(b, s, DIFF_HEADS, 2, DIFF_DH), k_norm_g)
    vb = vb.reshape(b, s, DIFF_HEADS, 2 * DIFF_DH)
    lam = (jnp.exp(jnp.sum(lambda_q1.astype(f32) * lambda_k1.astype(f32)))
           - jnp.exp(jnp.sum(lambda_q2.astype(f32) * lambda_k2.astype(f32))) + lam_init)
    ob = differential_attention(qb, kb, vb, lam)
    ob = rms_norm(ob, subln_g) * (1.0 - lam_init)
    yb = jnp.einsum("bse,ed->bsd", ob.reshape(b, s, DIFF_V), w_o_diff)

    merged = jax.nn.sigmoid(gate_a) * ya + jax.nn.sigmoid(gate_b) * yb
    return jnp.einsum("bsd,de->bse", merged, w_out)


def moe_ffn(h, w_router, router_bias, w_gate_up, w_down, ws_gate_up, ws_down):
    b, s, d = h.shape
    t = b * s
    f32 = jnp.float32
    hf = h.reshape(t, d)
    scores = jax.nn.sigmoid(jnp.einsum("td,de->te", hf, w_router, preferred_element_type=f32))
    choice = scores + router_bias.astype(f32)
    per_group = N_EXPERTS // N_GROUPS
    group_score = lax.top_k(choice.reshape(t, N_GROUPS, per_group), 2)[0].sum(-1)
    _, top_groups = lax.top_k(group_score, TOPK_GROUPS)
    group_mask = jnp.any(top_groups[..., None] == jnp.arange(N_GROUPS), axis=-2)
    expert_mask = jnp.repeat(group_mask, per_group, axis=-1)
    _, top_idx = lax.top_k(jnp.where(expert_mask, choice, -jnp.inf), TOP_K)
    top_w = jnp.take_along_axis(scores, top_idx, axis=-1)
    top_w = top_w / jnp.sum(top_w, axis=-1, keepdims=True) * ROUTED_SCALE

    n_assign = t * TOP_K
    flat_e = top_idx.reshape(-1).astype(jnp.int32)
    flat_tok = jnp.repeat(jnp.arange(t, dtype=jnp.int32), TOP_K)
    flat_w = top_w.reshape(-1)
    order = jnp.argsort(flat_e)
    sorted_e = flat_e[order]
    counts = jax.ops.segment_sum(jnp.ones_like(flat_e), flat_e, num_segments=N_EXPERTS)
    padded = (counts + EXPERT_BLOCK - 1) // EXPERT_BLOCK * EXPERT_BLOCK
    starts = jnp.cumsum(counts) - counts
    padded_ends = jnp.cumsum(padded)
    padded_starts = padded_ends - padded
    dest = padded_starts[sorted_e] + jnp.arange(n_assign, dtype=jnp.int32) - starts[sorted_e]
    n_blocks = -(-n_assign // EXPERT_BLOCK) + N_EXPERTS
    rows = n_blocks * EXPERT_BLOCK
    row_tok = jnp.zeros((rows,), jnp.int32).at[dest].set(flat_tok[order])
    row_w = jnp.zeros((rows,), f32).at[dest].set(flat_w[order])
    block_e = jnp.minimum(
        jnp.searchsorted(padded_ends, jnp.arange(n_blocks, dtype=jnp.int32) * EXPERT_BLOCK, side="right"),
        N_EXPERTS - 1)

    def expert_block(acc, blk):
        tok, wt, e = blk
        xb = hf[tok]
        gate, up = jnp.split(xb @ w_gate_up[e], 2, axis=-1)
        yb = (jax.nn.silu(gate) * up) @ w_down[e]
        return acc.at[tok].add(yb.astype(f32) * wt[:, None]), None

    routed, _ = lax.scan(expert_block, jnp.zeros((t, d), f32),
                         (row_tok.reshape(n_blocks, EXPERT_BLOCK), row_w.reshape(n_blocks, EXPERT_BLOCK), block_e))
    sg, su = jnp.split(hf @ ws_gate_up, 2, axis=-1)
    shared = (jax.nn.silu(sg) * su) @ ws_down
    return (routed + shared.astype(f32)).astype(h.dtype).reshape(b, s, d)


def setup_inputs(seed: int = 0) -> dict:
    key = jax.random.key(seed)
    ks = jax.random.split(key, 32)
    f32 = jnp.float32
    L = DEPTH

    def normal(k, shape, scale):
        return jax.random.normal(k, shape, f32) * scale

    def gain(k, shape):
        return 1.0 + 0.01 * jax.random.normal(k, shape, f32)

    dt = jnp.exp(jax.random.uniform(ks[8], (L, GDN_HEADS), f32, minval=math.log(1e-3), maxval=math.log(1e-1)))
    return {
        "x": normal(ks[0], (BATCH, SEQ, D_MODEL), 1.0),
        "c": normal(ks[1], (BATCH, D_MODEL), 1.0),
        "w_ada": normal(ks[2], (L, D_MODEL, 6 * D_MODEL), 0.5 * D_MODEL ** -0.5),
        "b_ada": normal(ks[3], (L, 6 * D_MODEL), 0.01),
        "norm1_g": gain(ks[4], (L, D_MODEL)),
        "w_in": normal(ks[5], (L, D_MODEL, IN_COLS), D_MODEL ** -0.5),
        "conv_w": normal(ks[6], (L, GDN_CONV, CONV_CH), GDN_CONV ** -0.5),
        "a_log": jnp.log(jax.random.uniform(ks[7], (L, GDN_HEADS), f32, minval=1.0, maxval=16.0)),
        "dt_bias": dt + jnp.log(-jnp.expm1(-dt)),
        "gdn_norm_g": gain(ks[9], (L, GDN_DV)),
        "w_o_gdn": normal(ks[10], (L, GDN_V, D_MODEL), GDN_V ** -0.5),
        "q_norm_g": gain(ks[11], (L, DIFF_DH)),
        "k_norm_g": gain(ks[12], (L, DIFF_DH)),
        "lambda_q1": normal(ks[13], (L, DIFF_DH), 0.1),
        "lambda_k1": normal(ks[14], (L, DIFF_DH), 0.1),
        "lambda_q2": normal(ks[15], (L, DIFF_DH), 0.1),
        "lambda_k2": normal(ks[16], (L, DIFF_DH), 0.1),
        "subln_g": gain(ks[17], (L, 2 * DIFF_DH)),
        "w_o_diff": normal(ks[18], (L, DIFF_V, D_MODEL), DIFF_V ** -0.5),
        "w_out": normal(ks[19], (L, D_MODEL, D_MODEL), D_MODEL ** -0.5),
        "norm2_g": gain(ks[20], (L, D_MODEL)),
        "w_router": normal(ks[21], (L, D_MODEL, N_EXPERTS), D_MODEL ** -0.5),
        "router_bias": normal(ks[22], (L, N_EXPERTS), 0.01),
        "w_exp_gate_up": normal(ks[23], (L, N_EXPERTS, D_MODEL, 2 * EXPERT_FF), D_MODEL ** -0.5),
        "w_exp_down": normal(ks[24], (L, N_EXPERTS, EXPERT_FF, D_MODEL), EXPERT_FF ** -0.5),
        "w_shared_gate_up": normal(ks[25], (L, D_MODEL, 2 * SHARED_FF), D_MODEL ** -0.5),
        "w_shared_down": normal(ks[26], (L, SHARED_FF, D_MODEL), SHARED_FF ** -0.5),
    }


def reference(x, c, w_ada, b_ada, norm1_g, w_in, conv_w, a_log, dt_bias, gdn_norm_g, w_o_gdn,
              q_norm_g, k_norm_g, lambda_q1, lambda_k1, lambda_q2, lambda_k2, subln_g, w_o_diff,
              w_out, norm2_g, w_router, router_bias, w_exp_gate_up, w_exp_down,
              w_shared_gate_up, w_shared_down):
    for layer in range(DEPTH):
        mod = jnp.einsum("bd,de->be", jax.nn.silu(c), w_ada[layer]) + b_ada[layer]
        shift1, scale1, gate1, shift2, scale2, gate2 = jnp.split(mod[:, None, :], 6, axis=-1)
        lam_init = 0.8 - 0.6 * math.exp(-0.3 * layer)

        h = rms_norm(x, norm1_g[layer]) * (1.0 + scale1) + shift1
        y = hybrid_mixer(h, w_in[layer], conv_w[layer], a_log[layer], dt_bias[layer], gdn_norm_g[layer],
                         w_o_gdn[layer], q_norm_g[layer], k_norm_g[layer], lambda_q1[layer], lambda_k1[layer],
                         lambda_q2[layer], lambda_k2[layer], subln_g[layer], w_o_diff[layer], w_out[layer],
                         lam_init)
        x = x + gate1 * y

        h = rms_norm(x, norm2_g[layer]) * (1.0 + scale2) + shift2
        y = moe_ffn(h, w_router[layer], router_bias[layer], w_exp_gate_up[layer], w_exp_down[layer],
                    w_shared_gate_up[layer], w_shared_down[layer])
        x = x + gate2 * y
    return x
```

```python
import functools
import math

import jax
import jax.numpy as jnp
import numpy as np
from jax import lax
from jax.experimental import pallas as pl
from jax.experimental.pallas import tpu as pltpu

F32 = jnp.float32
BF16 = jnp.bfloat16
HIGHEST = lax.Precision.HIGHEST

D_MODEL = 1024
CHUNK = 64
GDN_HEADS = 4
GDN_DK = 128
GDN_DV = 128
GDN_CONV = 4
DIFF_HEADS = 4
DIFF_DH = 64
N_EXPERTS = 256
TOP_K = 8
N_GROUPS = 8
TOPK_GROUPS = 4
EXPERT_FF = 256
SHARED_FF = 256
ROUTED_SCALE = 2.5
NORM_EPS = 1e-6
GROUP_SIZE = N_EXPERTS // N_GROUPS

GDN_QK = GDN_HEADS * GDN_DK
GDN_V = GDN_HEADS * GDN_DV
CONV_CH = 2 * GDN_QK + GDN_V
DIFF_QK = DIFF_HEADS * 2 * DIFF_DH
DIFF_V = DIFF_HEADS * 2 * DIFF_DH

LANES = 128
SUBLANES = 8
EXPERT_BLOCK = 128
LOG2E = math.log2(math.e)
VMEM_LIMIT = 56 * 1024 * 1024


def _cparams(sem):
    return pltpu.CompilerParams(dimension_semantics=sem, vmem_limit_bytes=VMEM_LIMIT)


def _dot(a, b):
    return jnp.dot(a, b, preferred_element_type=F32)


def _dot_nt(a, b):
    return lax.dot_general(a, b, (((1,), (1,)), ((), ())), preferred_element_type=F32)


def _dot_tn(a, b):
    return lax.dot_general(a, b, (((0,), (0,)), ((), ())), preferred_element_type=F32)


def _silu(x):
    return x * jax.nn.sigmoid(x)


def _bf16_split3(x):
    rnd = lambda v: float(np.float32(v).astype(BF16).astype(np.float32))
    a = rnd(x)
    b = rnd(x - a)
    c = rnd(x - a - b)
    return a, b, c


def _ada_kernel(c_ref, w_ref, b_ref, o_ref):
    s = _silu(c_ref[...])
    o_ref[...] = jnp.dot(s, w_ref[...], precision=HIGHEST, preferred_element_type=F32) + b_ref[...]


def _ada(c_pad, w_ada, b_ada):
    n = w_ada.shape[1]
    tn = 1024
    return pl.pallas_call(
        _ada_kernel,
        out_shape=jax.ShapeDtypeStruct((c_pad.shape[0], n), F32),
        grid=(n // tn,),
        in_specs=[pl.BlockSpec(c_pad.shape, lambda j: (0, 0)),
                  pl.BlockSpec((D_MODEL, tn), lambda j: (0, j)),
                  pl.BlockSpec((1, tn), lambda j: (0, j))],
        out_specs=pl.BlockSpec((c_pad.shape[0], tn), lambda j: (0, j)),
        compiler_params=_cparams(("arbitrary",)),
        name="ada",
    )(c_pad, w_ada, b_ada)


def _inproj_kernel(x_ref, shift_ref, scale_ref, g_ref, wg_ref, wbd_ref, wa_ref, wgt_ref,
                   gdn_ref, bd_ref, att_ref, gate_ref):
    x = x_ref[...]
    y = x * lax.rsqrt(jnp.mean(x * x, axis=-1, keepdims=True) + NORM_EPS) * g_ref[...]
    h = (y * (1.0 + scale_ref[0]) + shift_ref[0]).astype(BF16)
    gdn_ref[...] = _dot(h, wg_ref[...])
    bd_ref[...] = _dot(h, wbd_ref[...])
    att_ref[...] = _dot(h, wa_ref[...])
    gate_ref[...] = _dot(h, wgt_ref[...])


def _inproj(x2, mod3, norm1_g, w_gdn, w_bd, w_att, w_gate, seq):
    t = x2.shape[0]
    tm = min(256, seq)
    tiles_per_b = seq // tm
    const = lambda i: (0, 0)
    wspec = lambda w: pl.BlockSpec(w.shape, const, pipeline_mode=pl.Buffered(1))
    row = lambda n: pl.BlockSpec((tm, n), lambda i: (i, 0))
    return pl.pallas_call(
        _inproj_kernel,
        out_shape=(jax.ShapeDtypeStruct((t, w_gdn.shape[1]), F32),
                   jax.ShapeDtypeStruct((t, LANES), F32),
                   jax.ShapeDtypeStruct((t, w_att.shape[1]), F32),
                   jax.ShapeDtypeStruct((t, w_gate.shape[1]), F32)),
        grid=(t // tm,),
        in_specs=[row(D_MODEL),
                  pl.BlockSpec((1, 1, D_MODEL), lambda i: (i // tiles_per_b, 0, 0)),
                  pl.BlockSpec((1, 1, D_MODEL), lambda i: (i // tiles_per_b, 0, 1)),
                  pl.BlockSpec((1, D_MODEL), const),
                  wspec(w_gdn), wspec(w_bd), wspec(w_att), wspec(w_gate)],
        out_specs=(row(w_gdn.shape[1]), row(LANES), row(w_att.shape[1]), row(w_gate.shape[1])),
        compiler_params=_cparams(("arbitrary",)),
        name="inproj",
    )(x2, mod3, mod3, norm1_g, w_gdn, w_bd, w_att, w_gate)


def _gdn_kernel(x_ref, bd_ref, cw_ref, ap_ref, ng_ref, o_ref, cbuf, state, *, nb, lg):
    step = pl.program_id(0)
    nc = lg // CHUNK

    @pl.when(step == 0)
    def _():
        cbuf[...] = jnp.zeros_like(cbuf)
        state[...] = jnp.zeros_like(state)

    ri = lax.broadcasted_iota(jnp.int32, (CHUNK, CHUNK), 0)
    ci = lax.broadcasted_iota(jnp.int32, (CHUNK, CHUNK), 1)
    causal = ri >= ci
    strict = ri > ci
    eye = (ri == ci).astype(F32)
    rl = lax.broadcasted_iota(jnp.int32, (lg, lg), 0)
    cl = lax.broadcasted_iota(jnp.int32, (lg, lg), 1)
    blocktri = ((rl >= cl) & ((rl // CHUNK) == (cl // CHUNK))).astype(F32)
    cw = cw_ref[...]
    a_row = ap_ref[0:1, :]
    dt_row = ap_ref[1:2, :]
    ng = ng_ref[...]

    for b in range(nb):
        cbuf[b, SUBLANES:SUBLANES + lg, :] = x_ref[b, :, 0:CONV_CH]
        acc = cw[GDN_CONV - 1:GDN_CONV, :] * cbuf[b, SUBLANES:SUBLANES + lg, :]
        for j in range(GDN_CONV - 1):
            off = SUBLANES - (GDN_CONV - 1) + j
            acc = acc + cw[j:j + 1, :] * cbuf[b, off:off + lg, :]
        cbuf[b, 0:SUBLANES, :] = cbuf[b, lg:lg + SUBLANES, :]
        qkv = _silu(acc)

        bd = bd_ref[b]
        beta_t = jax.nn.sigmoid(bd)
        g_t = -jnp.exp(a_row) * jax.nn.softplus(bd + dt_row)
        gcum = jnp.dot(blocktri, g_t, precision=HIGHEST, preferred_element_type=F32)
        gcum_t = gcum.T
        egcum = jnp.exp(gcum)

        for h in range(GDN_HEADS):
            qh = qkv[:, h * GDN_DK:(h + 1) * GDN_DK]
            kh = qkv[:, GDN_QK + h * GDN_DK:GDN_QK + (h + 1) * GDN_DK]
            vh = qkv[:, 2 * GDN_QK + h * GDN_DV:2 * GDN_QK + (h + 1) * GDN_DV]
            zh = x_ref[b, :, CONV_CH + h * GDN_DV:CONV_CH + (h + 1) * GDN_DV]
            qh = qh * lax.rsqrt(jnp.sum(qh * qh, axis=-1, keepdims=True) + NORM_EPS) * (GDN_DK ** -0.5)
            kh = kh * lax.rsqrt(jnp.sum(kh * kh, axis=-1, keepdims=True) + NORM_EPS)
            for c in range(nc):
                r0, r1 = c * CHUNK, (c + 1) * CHUNK
                q, k, v = qh[r0:r1], kh[r0:r1], vh[r0:r1]
                gc = gcum[r0:r1, 4 + h:5 + h]
                gr = gcum_t[4 + h:5 + h, r0:r1]
                egc = egcum[r0:r1, 4 + h:5 + h]
                bcol = beta_t[r0:r1, h:h + 1]
                decay = jnp.exp(jnp.where(causal, gc - gr, -jnp.inf))
                kb = k.astype(BF16)
                kk = _dot_nt(kb, kb)
                qk = _dot_nt(q.astype(BF16), kb)
                lower = jnp.where(strict, bcol * kk * decay, 0.0)
                xinv = eye - lower
                pw = lower
                for _ in range(5):
                    pwb = pw.astype(BF16)
                    pw = _dot(pwb, pwb)
                    xinv = xinv + _dot(xinv.astype(BF16), pw.astype(BF16))
                rhs = jnp.concatenate([v * bcol, k * (bcol * egc)], axis=-1).astype(BF16)
                sol = _dot(xinv.astype(BF16), rhs)
                u, w = sol[:, :GDN_DV], sol[:, GDN_DV:]
                a_qk = (qk * decay).astype(BF16)
                st = state[b, h]
                stb = st.astype(BF16)
                v_new = u - _dot(w.astype(BF16), stb)
                vnb = v_new.astype(BF16)
                out = _dot((q * egc).astype(BF16), stb) + _dot(a_qk, vnb)
                gl = gcum[r1 - 1:r1, 4 + h:5 + h]
                state[b, h] = st * jnp.exp(gl) + _dot_tn((k * jnp.exp(gl - gc)).astype(BF16), vnb)
                on = out * lax.rsqrt(jnp.mean(out * out, axis=-1, keepdims=True) + NORM_EPS) * ng
                o_ref[b, r0:r1, h * GDN_DV:(h + 1) * GDN_DV] = on * _silu(zh[r0:r1])


def _gdn(gdn_in3, bd3, conv_w, aparams, gdn_norm_g):
    nb, seq, _ = gdn_in3.shape
    lg = min(128, seq)
    const = lambda i: (0, 0)
    return pl.pallas_call(
        functools.partial(_gdn_kernel, nb=nb, lg=lg),
        out_shape=jax.ShapeDtypeStruct((nb, seq, GDN_V), F32),
        grid=(seq // lg,),
        in_specs=[pl.BlockSpec((nb, lg, gdn_in3.shape[2]), lambda i: (0, i, 0)),
                  pl.BlockSpec((nb, lg, LANES), lambda i: (0, i, 0)),
                  pl.BlockSpec(conv_w.shape, const),
                  pl.BlockSpec(aparams.shape, const),
                  pl.BlockSpec(gdn_norm_g.shape, const)],
        out_specs=pl.BlockSpec((nb, lg, GDN_V), lambda i: (0, i, 0)),
        scratch_shapes=[pltpu.VMEM((nb, lg + SUBLANES, CONV_CH), F32),
                        pltpu.VMEM((nb, GDN_HEADS, GDN_DK, GDN_DV), F32)],
        compiler_params=_cparams(("arbitrary",)),
        name="gdn",
    )(gdn_in3, bd3, conv_w, aparams, gdn_norm_g)


def _aprep_kernel(x_ref, gq_ref, gk_ref, qf_ref, bdm_ref, q_ref, k_ref, vt_ref, *, tk):
    x = x_ref[...]
    tm = x.shape[0]
    bdm = bdm_ref[...]

    def qknorm(v, gain):
        ms = jnp.dot(v * v, bdm, precision=HIGHEST, preferred_element_type=F32) * (1.0 / DIFF_DH)
        return v * lax.rsqrt(ms + NORM_EPS) * gain

    qn = qknorm(x[:, 0:DIFF_QK], gq_ref[...]) * (DIFF_DH ** -0.5 * LOG2E)
    kn = qknorm(x[:, DIFF_QK:2 * DIFF_QK], gk_ref[...])
    pos = pl.program_id(0) * tm + lax.broadcasted_iota(jnp.int32, (tm, DIFF_DH), 0)
    krel = pos % tk
    lane = lax.broadcasted_iota(jnp.int32, (tm, DIFF_DH), 1)
    hi = ((krel // 256) * 256).astype(F32)
    lo = (krel % 256).astype(F32)
    kfeat = jnp.where(lane < 6, jnp.where(lane % 2 == 0, hi, lo), 0.0)
    qfeat = qf_ref[...]
    qparts, kparts = [], []
    for g in range(2 * DIFF_HEADS):
        qparts += [qn[:, g * DIFF_DH:(g + 1) * DIFF_DH],
                   jnp.broadcast_to(qfeat[:, g * DIFF_DH:(g + 1) * DIFF_DH], (tm, DIFF_DH))]
        kparts += [kn[:, g * DIFF_DH:(g + 1) * DIFF_DH], kfeat]
    q_ref[...] = jnp.concatenate(qparts, axis=-1).astype(BF16)
    k_ref[...] = jnp.concatenate(kparts, axis=-1).astype(BF16)
    vt_ref[0, 0] = x[:, 2 * DIFF_QK:].T.astype(BF16)


def _aprep(att_in, gq, gk, qfeat, bdm, nb, seq, tk):
    t = att_in.shape[0]
    tm = tk
    nk = seq // tk
    const = lambda i: (0, 0)
    return pl.pallas_call(
        functools.partial(_aprep_kernel, tk=tk),
        out_shape=(jax.ShapeDtypeStruct((t, 2 * DIFF_QK), BF16),
                   jax.ShapeDtypeStruct((t, 2 * DIFF_QK), BF16),
                   jax.ShapeDtypeStruct((nb, nk, DIFF_V, tk), BF16)),
        grid=(t // tm,),
        in_specs=[pl.BlockSpec((tm, att_in.shape[1]), lambda i: (i, 0)),
                  pl.BlockSpec(gq.shape, const), pl.BlockSpec(gk.shape, const),
                  pl.BlockSpec(qfeat.shape, const), pl.BlockSpec(bdm.shape, const)],
        out_specs=(pl.BlockSpec((tm, 2 * DIFF_QK), lambda i: (i, 0)),
                   pl.BlockSpec((tm, 2 * DIFF_QK), lambda i: (i, 0)),
                   pl.BlockSpec((1, 1, DIFF_V, tk), lambda i: (i // nk, i % nk, 0, 0))),
        compiler_params=_cparams(("arbitrary",)),
        name="aprep",
    )(att_in, gq, gk, qfeat, bdm)


def _attn_kernel(slope_ref, q_ref, k_ref, vt_ref, lamv_ref, sg_ref, o_ref, m_s, l_s, acc_s,
                 *, tq, lam_init):
    h = pl.program_id(1)
    qi = pl.program_id(2)
    slope2 = slope_ref[h]
    dv = 2 * DIFF_DH
    qs = [q_ref[:, 0:LANES], q_ref[:, LANES:2 * LANES]]

    m_s[...] = jnp.full_like(m_s, -jnp.inf)
    l_s[...] = jnp.zeros_like(l_s)
    acc_s[...] = jnp.zeros_like(acc_s)

    def tile(kj, add):
        k0 = pl.multiple_of(kj * tq, tq)
        vt = vt_ref[0, kj]
        off = slope2 * ((kj - qi) * tq).astype(F32)
        for c in range(2):
            kc = k_ref[pl.ds(k0, tq), c * LANES:(c + 1) * LANES]
            s = _dot_nt(kc, qs[c])
            if add is not None:
                s = s + add
            m_old = m_s[c]
            m_new = jnp.maximum(m_old, jnp.max(s, axis=0, keepdims=True) + off)
            alpha = jnp.exp2(m_old - m_new)
            p = jnp.exp2(s - (m_new - off))
            l_s[c] = alpha * l_s[c] + jnp.sum(p, axis=0, keepdims=True)
            acc_s[c] = alpha * acc_s[c] + _dot(vt, p.astype(BF16))
            m_s[c] = m_new

    kpos = lax.broadcasted_iota(jnp.int32, (tq, tq), 0)
    qpos = lax.broadcasted_iota(jnp.int32, (tq, tq), 1)
    allowed = (kpos // CHUNK) <= (qpos // CHUNK)
    fut = jnp.maximum(kpos - qpos, 0).astype(F32)
    tile(qi, jnp.where(allowed, (-2.0 * slope2) * fut, -jnp.inf))

    def body(kj, carry):
        tile(kj, None)
        return carry
    lax.fori_loop(0, qi, body, 0)

    lq1, lk1, lq2, lk2 = (lamv_ref[i:i + 1, :] for i in range(4))
    lam = (jnp.exp(jnp.sum(lq1 * lk1, axis=-1, keepdims=True))
           - jnp.exp(jnp.sum(lq2 * lk2, axis=-1, keepdims=True)) + lam_init)
    o = acc_s[0] / l_s[0] - lam * (acc_s[1] / l_s[1])
    on = o * lax.rsqrt(jnp.mean(o * o, axis=0, keepdims=True) + NORM_EPS) * sg_ref[...] * (1.0 - lam_init)
    o_ref[...] = on.T


def _attn(slopes2, qa, ka, vt, lamv, sg_col, nb, seq, tq, lam_init):
    t = qa.shape[0]
    nq = seq // tq
    dv = 2 * DIFF_DH
    grid_spec = pltpu.PrefetchScalarGridSpec(
        num_scalar_prefetch=1,
        grid=(nb, DIFF_HEADS, nq),
        in_specs=[pl.BlockSpec((tq, 2 * LANES), lambda b, h, i, s: (b * nq + i, h)),
                  pl.BlockSpec((seq, 2 * LANES), lambda b, h, i, s: (b, h)),
                  pl.BlockSpec((1, nq, dv, tq), lambda b, h, i, s: (b, 0, h, 0)),
                  pl.BlockSpec(lamv.shape, lambda b, h, i, s: (0, 0)),
                  pl.BlockSpec(sg_col.shape, lambda b, h, i, s: (0, 0))],
        out_specs=pl.BlockSpec((tq, dv), lambda b, h, i, s: (b * nq + i, h)),
        scratch_shapes=[pltpu.VMEM((2, 1, tq), F32), pltpu.VMEM((2, 1, tq), F32),
                        pltpu.VMEM((2, dv, tq), F32)],
    )
    return pl.pallas_call(
        functools.partial(_attn_kernel, tq=tq, lam_init=lam_init),
        out_shape=jax.ShapeDtypeStruct((t, DIFF_V), F32),
        grid_spec=grid_spec,
        compiler_params=_cparams(("arbitrary", "arbitrary", "arbitrary")),
        name="attn",
    )(slopes2, qa, ka, vt, lamv, sg_col)


def _post_kernel(x_ref, oa_ref, ob_ref, gate_ref, g1_ref, sh2_ref, sc2_ref, n2_ref,
                 woa_ref, wob_ref, wout_ref, x1_ref, h2_ref):
    ya = _dot(oa_ref[...].astype(BF16), woa_ref[...])
    yb = _dot(ob_ref[...].astype(BF16), wob_ref[...])
    merged = jax.nn.sigmoid(gate_ref[:, 0:D_MODEL]) * ya + jax.nn.sigmoid(gate_ref[:, D_MODEL:]) * yb
    y = _dot(merged.astype(BF16), wout_ref[...])
    x1 = x_ref[...] + g1_ref[0] * y
    x1_ref[...] = x1
    n = x1 * lax.rsqrt(jnp.mean(x1 * x1, axis=-1, keepdims=True) + NORM_EPS) * n2_ref[...]
    h2_ref[...] = n * (1.0 + sc2_ref[0]) + sh2_ref[0]


def _post(x2, oa, ob, gates, mod3, norm2_g, woa, wob, wout, seq):
    t = x2.shape[0]
    tm = min(256, seq)
    tpb = seq // tm
    const = lambda i: (0, 0)
    row = lambda n: pl.BlockSpec((tm, n), lambda i: (i, 0))
    modspec = lambda j: pl.BlockSpec((1, 1, D_MODEL), lambda i: (i // tpb, 0, j))
    wspec = lambda w: pl.BlockSpec(w.shape, const, pipeline_mode=pl.Buffered(1))
    return pl.pallas_call(
        _post_kernel,
        out_shape=(jax.ShapeDtypeStruct((t, D_MODEL), F32), jax.ShapeDtypeStruct((t, D_MODEL), F32)),
        grid=(t // tm,),
        in_specs=[row(D_MODEL), row(GDN_V), row(DIFF_V), row(2 * D_MODEL),
                  modspec(2), modspec(3), modspec(4), pl.BlockSpec((1, D_MODEL), const),
                  wspec(woa), wspec(wob), wspec(wout)],
        out_specs=(row(D_MODEL), row(D_MODEL)),
        compiler_params=_cparams(("arbitrary",)),
        name="post",
    )(x2, oa, ob, gates, mod3, mod3, mod3, norm2_g, woa, wob, wout)


def _router_kernel(h_ref, wr_ref, rb_ref, idx_ref, w_ref, cnt_ref, cnt_s):
    step = pl.program_id(0)

    @pl.when(step == 0)
    def _():
        cnt_s[...] = jnp.zeros_like(cnt_s)

    tm = h_ref.shape[0]
    logits = jnp.dot(h_ref[...], wr_ref[...], precision=HIGHEST, preferred_element_type=F32)
    scores = jax.nn.sigmoid(logits)
    choice = scores + rb_ref[...]
    lane = lax.broadcasted_iota(jnp.int32, (tm, N_EXPERTS), 1)
    grp = lane // GROUP_SIZE
    neg = -jnp.inf

    def first_max(v):
        m = jnp.max(v, axis=-1, keepdims=True)
        i = jnp.min(jnp.where(v == m, lane, N_EXPERTS), axis=-1, keepdims=True)
        return m, i

    gl = lax.broadcasted_iota(jnp.int32, (tm, LANES), 1)
    gscore = jnp.full((tm, LANES), neg, F32)
    for g in range(N_GROUPS):
        vg = jnp.where(grp == g, choice, neg)
        m1, i1 = first_max(vg)
        m2 = jnp.max(jnp.where(lane == i1, neg, vg), axis=-1, keepdims=True)
        gscore = jnp.where(gl == g, m1 + m2, gscore)
    emask = jnp.zeros((tm, N_EXPERTS), jnp.bool_)
    for _ in range(TOPK_GROUPS):
        m = jnp.max(gscore, axis=-1, keepdims=True)
        gi = jnp.min(jnp.where(gscore == m, gl, LANES), axis=-1, keepdims=True)
        emask = emask | (grp == gi)
        gscore = jnp.where(gl == gi, neg, gscore)
    masked = jnp.where(emask, choice, neg)
    idxs = jnp.zeros((tm, LANES), jnp.int32)
    ws = jnp.zeros((tm, LANES), F32)
    sel = jnp.zeros((tm, N_EXPERTS), F32)
    for k in range(TOP_K):
        _, i = first_max(masked)
        hit = lane == i
        wk = jnp.sum(jnp.where(hit, scores, 0.0), axis=-1, keepdims=True)
        idxs = jnp.where(gl == k, i, idxs)
        ws = jnp.where(gl == k, wk, ws)
        sel = jnp.where(hit, 1.0, sel)
        masked = jnp.where(hit, neg, masked)
    wsum = jnp.sum(ws, axis=-1, keepdims=True)
    idx_ref[...] = idxs
    w_ref[...] = ws / wsum * ROUTED_SCALE
    cnt_s[...] = cnt_s[...] + jnp.sum(sel, axis=0, keepdims=True)
    cnt_ref[...] = cnt_s[...]


def _router(h2, w_router, rbias):
    t = h2.shape[0]
    tm = min(256, t)
    const = lambda i: (0, 0)
    return pl.pallas_call(
        _router_kernel,
        out_shape=(jax.ShapeDtypeStruct((t, LANES), jnp.int32),
                   jax.ShapeDtypeStruct((t, LANES), F32),
                   jax.ShapeDtypeStruct((1, N_EXPERTS), F32)),
        grid=(t // tm,),
        in_specs=[pl.BlockSpec((tm, D_MODEL), lambda i: (i, 0)),
                  pl.BlockSpec(w_router.shape, const), pl.BlockSpec(rbias.shape, const)],
        out_specs=(pl.BlockSpec((tm, LANES), lambda i: (i, 0)),
                   pl.BlockSpec((tm, LANES), lambda i: (i, 0)),
                   pl.BlockSpec((1, N_EXPERTS), const)),
        scratch_shapes=[pltpu.VMEM((1, N_EXPERTS), F32)],
        compiler_params=_cparams(("arbitrary",)),
        name="router",
    )(h2, w_router, rbias)


def _rank_kernel(idx_ref, ps_ref, dest_ref, run_s):
    step = pl.program_id(0)

    @pl.when(step == 0)
    def _():
        run_s[...] = jnp.zeros_like(run_s)

    tm = idx_ref.shape[0]
    idx = idx_ref[...]
    lane = lax.broadcasted_iota(jnp.int32, (tm, N_EXPERTS), 1)
    gl = lax.broadcasted_iota(jnp.int32, (tm, LANES), 1)
    hits = [lane == idx[:, k:k + 1] for k in range(TOP_K)]
    sel = jnp.zeros((tm, N_EXPERTS), F32)
    for hit in hits:
        sel = jnp.where(hit, 1.0, sel)
    ri = lax.broadcasted_iota(jnp.int32, (tm, tm), 0)
    ci = lax.broadcasted_iota(jnp.int32, (tm, tm), 1)
    before = _dot((ri > ci).astype(BF16), sel.astype(BF16))
    base = before + run_s[...] + ps_ref[...]
    dest = jnp.zeros((tm, LANES), F32)
    for k, hit in enumerate(hits):
        dk = jnp.sum(jnp.where(hit, base, 0.0), axis=-1, keepdims=True)
        dest = jnp.where(gl == k, dk, dest)
    dest_ref[...] = dest.astype(jnp.int32)
    run_s[...] = run_s[...] + jnp.sum(sel, axis=0, keepdims=True)


def _rank(idx, pstart):
    t = idx.shape[0]
    tm = min(256, t)
    const = lambda i: (0, 0)
    return pl.pallas_call(
        _rank_kernel,
        out_shape=jax.ShapeDtypeStruct((t, LANES), jnp.int32),
        grid=(t // tm,),
        in_specs=[pl.BlockSpec((tm, LANES), lambda i: (i, 0)), pl.BlockSpec(pstart.shape, const)],
        out_specs=pl.BlockSpec((tm, LANES), lambda i: (i, 0)),
        scratch_shapes=[pltpu.VMEM((1, N_EXPERTS), F32)],
        compiler_params=_cparams(("arbitrary",)),
        name="rank",
    )(idx, pstart)


def _row_copy(src, dst, sem):
    return pltpu.make_async_copy(src, dst, sem)


def _dispatch_kernel(dest_ref, h_ref, xs_in, xs_ref, sem, *, tm):
    del xs_in

    def copy(i):
        r = i // TOP_K
        d = dest_ref[i]
        return _row_copy(h_ref.at[pl.ds(r, 1), :], xs_ref.at[pl.ds(d, 1), :], sem)

    def start(i, c):
        copy(i).start()
        return c

    def wait(i, c):
        copy(i).wait()
        return c

    lax.fori_loop(0, tm * TOP_K, start, 0)
    lax.fori_loop(0, tm * TOP_K, wait, 0)


def _dispatch(dest_flat, h2, xs0):
    t = h2.shape[0]
    tm = min(256, t)
    return pl.pallas_call(
        functools.partial(_dispatch_kernel, tm=tm),
        out_shape=jax.ShapeDtypeStruct(xs0.shape, xs0.dtype),
        grid=(t // tm,),
        in_specs=[pl.BlockSpec((tm * TOP_K,), lambda i: (i,), memory_space=pltpu.SMEM),
                  pl.BlockSpec((tm, D_MODEL), lambda i: (i, 0)),
                  pl.BlockSpec(memory_space=pl.ANY)],
        out_specs=pl.BlockSpec(memory_space=pl.ANY),
        scratch_shapes=[pltpu.SemaphoreType.DMA(())],
        input_output_aliases={2: 0},
        compiler_params=_cparams(("arbitrary",)),
        name="dispatch",
    )(dest_flat, h2, xs0)


def _experts_kernel(be_ref, nb_ref, xs_ref, wgu_ref, wdn_ref, ys_ref, wgu_s, wdn_s):
    b = pl.program_id(0)
    prev = be_ref[jnp.maximum(b - 1, 0)]
    changed = (b == 0) | (be_ref[b] != prev)

    @pl.when(changed)
    def _():
        wgu_s[...] = wgu_ref[0].astype(BF16)
        wdn_s[...] = wdn_ref[0].astype(BF16)

    @pl.when(b < nb_ref[0])
    def _():
        gu = _dot(xs_ref[...].astype(BF16), wgu_s[...])
        act = _silu(gu[:, :EXPERT_FF]) * gu[:, EXPERT_FF:]
        ys_ref[...] = _dot(act.astype(BF16), wdn_s[...])

    @pl.when(b >= nb_ref[0])
    def _():
        ys_ref[...] = jnp.zeros_like(ys_ref)


def _experts(block_e, nb_used, xs, w_gu, w_dn):
    rows = xs.shape[0]
    nblk = rows // EXPERT_BLOCK
    grid_spec = pltpu.PrefetchScalarGridSpec(
        num_scalar_prefetch=2,
        grid=(nblk,),
        in_specs=[pl.BlockSpec((EXPERT_BLOCK, D_MODEL), lambda b, be, nb: (b, 0)),
                  pl.BlockSpec((1, D_MODEL, 2 * EXPERT_FF), lambda b, be, nb: (be[b], 0, 0)),
                  pl.BlockSpec((1, EXPERT_FF, D_MODEL), lambda b, be, nb: (be[b], 0, 0))],
        out_specs=pl.BlockSpec((EXPERT_BLOCK, D_MODEL), lambda b, be, nb: (b, 0)),
        scratch_shapes=[pltpu.VMEM((D_MODEL, 2 * EXPERT_FF), BF16), pltpu.VMEM((EXPERT_FF, D_MODEL), BF16)],
    )
    return pl.pallas_call(
        _experts_kernel,
        out_shape=jax.ShapeDtypeStruct((rows, D_MODEL), F32),
        grid_spec=grid_spec,
        compiler_params=_cparams(("arbitrary",)),
        name="experts",
    )(block_e, nb_used, xs, w_gu, w_dn)


def _combine_kernel(dest_ref, x1_ref, h_ref, w_ref, g2_ref, wsgu_ref, wsdn_ref, ys_ref, o_ref, buf, sem, *, tm):
    def copy(i):
        r = i // TOP_K
        k = i % TOP_K
        d = dest_ref[i]
        return _row_copy(ys_ref.at[pl.ds(d, 1), :], buf.at[k, pl.ds(r, 1), :], sem)

    def start(i, c):
        copy(i).start()
        return c

    def wait(i, c):
        copy(i).wait()
        return c

    lax.fori_loop(0, tm * TOP_K, start, 0)
    su = _dot(h_ref[...].astype(BF16), wsgu_ref[...])
    y = _dot((_silu(su[:, :SHARED_FF]) * su[:, SHARED_FF:]).astype(BF16), wsdn_ref[...])
    lax.fori_loop(0, tm * TOP_K, wait, 0)
    w = w_ref[...]
    for k in range(TOP_K):
        y = y + w[:, k:k + 1] * buf[k]
    o_ref[...] = x1_ref[...] + g2_ref[0] * y


def _combine(dest_flat, x1, h2, wts, mod3, wsgu, wsdn, ys, seq):
    t = x1.shape[0]
    tm = min(128, seq)
    tpb = seq // tm
    const = lambda i: (0, 0)
    row = lambda n: pl.BlockSpec((tm, n), lambda i: (i, 0))
    return pl.pallas_call(
        functools.partial(_combine_kernel, tm=tm),
        out_shape=jax.ShapeDtypeStruct((t, D_MODEL), F32),
        grid=(t // tm,),
        in_specs=[pl.BlockSpec((tm * TOP_K,), lambda i: (i,), memory_space=pltpu.SMEM),
                  row(D_MODEL), row(D_MODEL), row(LANES),
                  pl.BlockSpec((1, 1, D_MODEL), lambda i: (i // tpb, 0, 5)),
                  pl.BlockSpec(wsgu.shape, const), pl.BlockSpec(wsdn.shape, const),
                  pl.BlockSpec(memory_space=pl.ANY)],
        out_specs=row(D_MODEL),
        scratch_shapes=[pltpu.VMEM((TOP_K, tm, D_MODEL), F32), pltpu.SemaphoreType.DMA(())],
        compiler_params=_cparams(("arbitrary",)),
        name="combine",
    )(dest_flat, x1, h2, wts, mod3, wsgu, wsdn, ys)


def _layer(x, c, layer, w_ada, b_ada, norm1_g, w_in, conv_w, a_log, dt_bias, gdn_norm_g, w_o_gdn,
           q_norm_g, k_norm_g, lambda_q1, lambda_k1, lambda_q2, lambda_k2, subln_g, w_o_diff,
           w_out, norm2_g, w_router, router_bias, w_exp_gate_up, w_exp_down,
           w_shared_gate_up, w_shared_down):
    nb, seq, d = x.shape
    t = nb * seq
    lam_init = 0.8 - 0.6 * math.exp(-0.3 * layer)
    x2 = x.reshape(t, d)

    c_pad = jnp.pad(c, ((0, SUBLANES - nb % SUBLANES if nb % SUBLANES else 0), (0, 0)))
    mod = _ada(c_pad, w_ada, b_ada.reshape(1, -1))[:nb]
    mod3 = mod.reshape(nb, 1, 6 * d)

    o_bd = 2 * GDN_QK + 2 * GDN_V
    o_att = o_bd + 2 * GDN_HEADS
    o_gate = o_att + 2 * DIFF_QK + DIFF_V
    w_gdn = w_in[:, :o_bd].astype(BF16)
    w_bd = jnp.pad(w_in[:, o_bd:o_att], ((0, 0), (0, LANES - 2 * GDN_HEADS))).astype(BF16)
    w_att = w_in[:, o_att:o_gate].astype(BF16)
    w_gate = w_in[:, o_gate:].astype(BF16)
    gdn_in, bd, att_in, gates = _inproj(x2, mod3, norm1_g.reshape(1, d), w_gdn, w_bd, w_att, w_gate, seq)

    aparams = jnp.zeros((SUBLANES, LANES), F32)
    aparams = aparams.at[0, GDN_HEADS:2 * GDN_HEADS].set(a_log).at[1, GDN_HEADS:2 * GDN_HEADS].set(dt_bias)
    oa = _gdn(gdn_in.reshape(nb, seq, -1), bd.reshape(nb, seq, LANES), conv_w, aparams,
              gdn_norm_g.reshape(1, GDN_DV)).reshape(t, GDN_V)

    tq = min(512, seq)
    slopes = [2.0 ** (-8.0 * (h + 1) / DIFF_HEADS) for h in range(DIFF_HEADS)]
    c3 = _bf16_split3(LOG2E)
    qfeat = np.zeros((1, 2 * DIFF_HEADS * DIFF_DH), np.float32)
    for g in range(2 * DIFF_HEADS):
        for j in range(6):
            qfeat[0, g * DIFF_DH + j] = slopes[g // 2] * c3[j // 2]
    grp = np.arange(DIFF_QK) // DIFF_DH
    bdm = jnp.asarray((grp[:, None] == grp[None, :]).astype(np.float32))
    qa, ka, vt = _aprep(att_in, jnp.tile(q_norm_g, 2 * DIFF_HEADS).reshape(1, -1),
                        jnp.tile(k_norm_g, 2 * DIFF_HEADS).reshape(1, -1), jnp.asarray(qfeat), bdm, nb, seq, tq)
    slopes2 = jnp.asarray([s * (c3[0] + c3[1] + c3[2]) for s in slopes], F32)
    lamv = jnp.zeros((SUBLANES, DIFF_DH), F32)
    lamv = lamv.at[0].set(lambda_q1).at[1].set(lambda_k1).at[2].set(lambda_q2).at[3].set(lambda_k2)
    ob = _attn(slopes2, qa, ka, vt, lamv, subln_g.reshape(-1, 1), nb, seq, tq, lam_init)

    x1, h2 = _post(x2, oa, ob, gates, mod3, norm2_g.reshape(1, d), w_o_gdn.astype(BF16),
                   w_o_diff.astype(BF16), w_out.astype(BF16), seq)

    idx, wts, counts = _router(h2, w_router, router_bias.reshape(1, -1))
    cnt = counts[0].astype(jnp.int32)
    padded = (cnt + EXPERT_BLOCK - 1) // EXPERT_BLOCK * EXPERT_BLOCK
    pends = jnp.cumsum(padded)
    pstart = (pends - padded).astype(F32).reshape(1, -1)
    nblk = -(-(t * TOP_K) // EXPERT_BLOCK) + N_EXPERTS
    block_e = jnp.minimum(
        jnp.searchsorted(pends, jnp.arange(nblk, dtype=jnp.int32) * EXPERT_BLOCK, side="right"),
        N_EXPERTS - 1).astype(jnp.int32)
    nb_used = (pends[-1:] // EXPERT_BLOCK).astype(jnp.int32)
    dest = _rank(idx, pstart)
    dest_flat = dest[:, :TOP_K].reshape(-1)
    xs = _dispatch(dest_flat, h2, jnp.zeros((nblk * EXPERT_BLOCK, d), F32))
    ys = _experts(block_e, nb_used, xs, w_exp_gate_up, w_exp_down)
    out = _combine(dest_flat, x1, h2, wts, mod3, w_shared_gate_up.astype(BF16),
                   w_shared_down.astype(BF16), ys, seq)
    return out.reshape(nb, seq, d)


def kernel(x, c, w_ada, b_ada, norm1_g, w_in, conv_w, a_log, dt_bias, gdn_norm_g, w_o_gdn, q_norm_g, k_norm_g, lambda_q1, lambda_k1, lambda_q2, lambda_k2, subln_g, w_o_diff, w_out, norm2_g, w_router, router_bias, w_exp_gate_up, w_exp_down, w_shared_gate_up, w_shared_down):
    params = (w_ada, b_ada, norm1_g, w_in, conv_w, a_log, dt_bias, gdn_norm_g, w_o_gdn, q_norm_g,
              k_norm_g, lambda_q1, lambda_k1, lambda_q2, lambda_k2, subln_g, w_o_diff, w_out, norm2_g,
              w_router, router_bias, w_exp_gate_up, w_exp_down, w_shared_gate_up, w_shared_down)
    for layer in range(w_ada.shape[0]):
        x = _layer(x, c, layer, *(p[layer] for p in params))
    return x
```

```python
import functools
import math

import jax
import jax.numpy as jnp
import numpy as np
from jax import lax
from jax.experimental import pallas as pl
from jax.experimental.pallas import tpu as pltpu

F32 = jnp.float32
BF16 = jnp.bfloat16
HIGHEST = lax.Precision.HIGHEST

D_MODEL = 1024
CHUNK = 64
GDN_HEADS = 4
GDN_DK = 128
GDN_DV = 128
GDN_CONV = 4
DIFF_HEADS = 4
DIFF_DH = 64
N_EXPERTS = 256
TOP_K = 8
N_GROUPS = 8
TOPK_GROUPS = 4
EXPERT_FF = 256
SHARED_FF = 256
ROUTED_SCALE = 2.5
NORM_EPS = 1e-6
GROUP_SIZE = N_EXPERTS // N_GROUPS

GDN_QK = GDN_HEADS * GDN_DK
GDN_V = GDN_HEADS * GDN_DV
CONV_CH = 2 * GDN_QK + GDN_V
DIFF_QK = DIFF_HEADS * 2 * DIFF_DH
DIFF_V = DIFF_HEADS * 2 * DIFF_DH

LANES = 128
SUBLANES = 8
EXPERT_BLOCK = 128
LOG2E = math.log2(math.e)
VMEM_LIMIT = 56 * 1024 * 1024


def _cparams(sem):
    return pltpu.CompilerParams(dimension_semantics=sem, vmem_limit_bytes=VMEM_LIMIT)


def _dot(a, b):
    return jnp.dot(a, b, preferred_element_type=F32)


def _dot_nt(a, b):
    return lax.dot_general(a, b, (((1,), (1,)), ((), ())), preferred_element_type=F32)


def _dot_tn(a, b):
    return lax.dot_general(a, b, (((0,), (0,)), ((), ())), preferred_element_type=F32)


def _silu(x):
    return x * jax.nn.sigmoid(x)


def _bf16_split3(x):
    rnd = lambda v: float(np.float32(v).astype(BF16).astype(np.float32))
    a = rnd(x)
    b = rnd(x - a)
    c = rnd(x - a - b)
    return a, b, c


def _ada_kernel(c_ref, w_ref, b_ref, o_ref):
    s = _silu(c_ref[...])
    o_ref[...] = jnp.dot(s, w_ref[...], precision=HIGHEST, preferred_element_type=F32) + b_ref[...]


def _ada(c_pad, w_ada, b_ada):
    n = w_ada.shape[1]
    tn = 1024
    return pl.pallas_call(
        _ada_kernel,
        out_shape=jax.ShapeDtypeStruct((c_pad.shape[0], n), F32),
        grid=(n // tn,),
        in_specs=[pl.BlockSpec(c_pad.shape, lambda j: (0, 0)),
                  pl.BlockSpec((D_MODEL, tn), lambda j: (0, j)),
                  pl.BlockSpec((1, tn), lambda j: (0, j))],
        out_specs=pl.BlockSpec((c_pad.shape[0], tn), lambda j: (0, j)),
        compiler_params=_cparams(("arbitrary",)),
        name="ada",
    )(c_pad, w_ada, b_ada)


def _inproj_kernel(x_ref, shift_ref, scale_ref, g_ref, wg_ref, wbd_ref, wa_ref, wgt_ref,
                   gdn_ref, bd_ref, att_ref, gate_ref):
    x = x_ref[...]
    y = x * lax.rsqrt(jnp.mean(x * x, axis=-1, keepdims=True) + NORM_EPS) * g_ref[...]
    h = (y * (1.0 + scale_ref[0]) + shift_ref[0]).astype(BF16)
    gdn_ref[...] = _dot(h, wg_ref[...])
    bd_ref[...] = _dot(h, wbd_ref[...])
    att_ref[...] = _dot(h, wa_ref[...])
    gate_ref[...] = _dot(h, wgt_ref[...])


def _inproj(x2, mod3, norm1_g, w_gdn, w_bd, w_att, w_gate, seq):
    t = x2.shape[0]
    tm = min(256, seq)
    tiles_per_b = seq // tm
    const = lambda i: (0, 0)
    wspec = lambda w: pl.BlockSpec(w.shape, const, pipeline_mode=pl.Buffered(1))
    row = lambda n: pl.BlockSpec((tm, n), lambda i: (i, 0))
    return pl.pallas_call(
        _inproj_kernel,
        out_shape=(jax.ShapeDtypeStruct((t, w_gdn.shape[1]), F32),
                   jax.ShapeDtypeStruct((t, LANES), F32),
                   jax.ShapeDtypeStruct((t, w_att.shape[1]), F32),
                   jax.ShapeDtypeStruct((t, w_gate.shape[1]), F32)),
        grid=(t // tm,),
        in_specs=[row(D_MODEL),
                  pl.BlockSpec((1, 1, D_MODEL), lambda i: (i // tiles_per_b, 0, 0)),
                  pl.BlockSpec((1, 1, D_MODEL), lambda i: (i // tiles_per_b, 0, 1)),
                  pl.BlockSpec((1, D_MODEL), const),
                  wspec(w_gdn), wspec(w_bd), wspec(w_att), wspec(w_gate)],
        out_specs=(row(w_gdn.shape[1]), row(LANES), row(w_att.shape[1]), row(w_gate.shape[1])),
        compiler_params=_cparams(("arbitrary",)),
        name="inproj",
    )(x2, mod3, mod3, norm1_g, w_gdn, w_bd, w_att, w_gate)


def _gdn_kernel(x_ref, bd_ref, cw_ref, ap_ref, ng_ref, o_ref, cbuf, state, *, nb, lg):
    step = pl.program_id(0)
    nc = lg // CHUNK

    @pl.when(step == 0)
    def _():
        cbuf[...] = jnp.zeros_like(cbuf)
        state[...] = jnp.zeros_like(state)

    ri = lax.broadcasted_iota(jnp.int32, (CHUNK, CHUNK), 0)
    ci = lax.broadcasted_iota(jnp.int32, (CHUNK, CHUNK), 1)
    causal = ri >= ci
    strict = ri > ci
    eye = (ri == ci).astype(F32)
    rl = lax.broadcasted_iota(jnp.int32, (lg, lg), 0)
    cl = lax.broadcasted_iota(jnp.int32, (lg, lg), 1)
    blocktri = ((rl >= cl) & ((rl // CHUNK) == (cl // CHUNK))).astype(F32)
    cw = cw_ref[...]
    a_row = ap_ref[0:1, :]
    dt_row = ap_ref[1:2, :]
    ng = ng_ref[...]

    for b in range(nb):
        cbuf[b, SUBLANES:SUBLANES + lg, :] = x_ref[b, :, 0:CONV_CH]
        acc = cw[GDN_CONV - 1:GDN_CONV, :] * cbuf[b, SUBLANES:SUBLANES + lg, :]
        for j in range(GDN_CONV - 1):
            off = SUBLANES - (GDN_CONV - 1) + j
            acc = acc + cw[j:j + 1, :] * cbuf[b, off:off + lg, :]
        cbuf[b, 0:SUBLANES, :] = cbuf[b, lg:lg + SUBLANES, :]
        qkv = _silu(acc)

        bd = bd_ref[b]
        beta_t = jax.nn.sigmoid(bd)
        g_t = -jnp.exp(a_row) * jax.nn.softplus(bd + dt_row)
        gcum = jnp.dot(blocktri, g_t, precision=HIGHEST, preferred_element_type=F32)
        gcum_t = gcum.T
        egcum = jnp.exp(gcum)

        for h in range(GDN_HEADS):
            qh = qkv[:, h * GDN_DK:(h + 1) * GDN_DK]
            kh = qkv[:, GDN_QK + h * GDN_DK:GDN_QK + (h + 1) * GDN_DK]
            vh = qkv[:, 2 * GDN_QK + h * GDN_DV:2 * GDN_QK + (h + 1) * GDN_DV]
            zh = x_ref[b, :, CONV_CH + h * GDN_DV:CONV_CH + (h + 1) * GDN_DV]
            qh = qh * lax.rsqrt(jnp.sum(qh * qh, axis=-1, keepdims=True) + NORM_EPS) * (GDN_DK ** -0.5)
            kh = kh * lax.rsqrt(jnp.sum(kh * kh, axis=-1, keepdims=True) + NORM_EPS)
            for c in range(nc):
                r0, r1 = c * CHUNK, (c + 1) * CHUNK
                q, k, v = qh[r0:r1], kh[r0:r1], vh[r0:r1]
                gc = gcum[r0:r1, 4 + h:5 + h]
                gr = gcum_t[4 + h:5 + h, r0:r1]
                egc = egcum[r0:r1, 4 + h:5 + h]
                bcol = beta_t[r0:r1, h:h + 1]
                decay = jnp.exp(jnp.where(causal, gc - gr, -jnp.inf))
                kb = k.astype(BF16)
                kk = _dot_nt(kb, kb)
                qk = _dot_nt(q.astype(BF16), kb)
                lower = jnp.where(strict, bcol * kk * decay, 0.0)
                xinv = eye - lower
                pw = lower
                for _ in range(5):
                    pwb = pw.astype(BF16)
                    pw = _dot(pwb, pwb)
                    xinv = xinv + _dot(xinv.astype(BF16), pw.astype(BF16))
                rhs = jnp.concatenate([v * bcol, k * (bcol * egc)], axis=-1).astype(BF16)
                sol = _dot(xinv.astype(BF16), rhs)
                u, w = sol[:, :GDN_DV], sol[:, GDN_DV:]
                a_qk = (qk * decay).astype(BF16)
                st = state[b, h]
                stb = st.astype(BF16)
                v_new = u - _dot(w.astype(BF16), stb)
                vnb = v_new.astype(BF16)
                out = _dot((q * egc).astype(BF16), stb) + _dot(a_qk, vnb)
                gl = gcum[r1 - 1:r1, 4 + h:5 + h]
                state[b, h] = st * jnp.exp(gl) + _dot_tn((k * jnp.exp(gl - gc)).astype(BF16), vnb)
                on = out * lax.rsqrt(jnp.mean(out * out, axis=-1, keepdims=True) + NORM_EPS) * ng
                o_ref[b, r0:r1, h * GDN_DV:(h + 1) * GDN_DV] = on * _silu(zh[r0:r1])


def _gdn(gdn_in3, bd3, conv_w, aparams, gdn_norm_g):
    nb, seq, _ = gdn_in3.shape
    lg = min(128, seq)
    const = lambda i: (0, 0)
    return pl.pallas_call(
        functools.partial(_gdn_kernel, nb=nb, lg=lg),
        out_shape=jax.ShapeDtypeStruct((nb, seq, GDN_V), F32),
        grid=(seq // lg,),
        in_specs=[pl.BlockSpec((nb, lg, gdn_in3.shape[2]), lambda i: (0, i, 0)),
                  pl.BlockSpec((nb, lg, LANES), lambda i: (0, i, 0)),
                  pl.BlockSpec(conv_w.shape, const),
                  pl.BlockSpec(aparams.shape, const),
                  pl.BlockSpec(gdn_norm_g.shape, const)],
        out_specs=pl.BlockSpec((nb, lg, GDN_V), lambda i: (0, i, 0)),
        scratch_shapes=[pltpu.VMEM((nb, lg + SUBLANES, CONV_CH), F32),
                        pltpu.VMEM((nb, GDN_HEADS, GDN_DK, GDN_DV), F32)],
        compiler_params=_cparams(("arbitrary",)),
        name="gdn",
    )(gdn_in3, bd3, conv_w, aparams, gdn_norm_g)


def _aprep_kernel(x_ref, gq_ref, gk_ref, qf_ref, bdm_ref, q_ref, k_ref, vt_ref, *, tk):
    x = x_ref[...]
    tm = x.shape[0]
    bdm = bdm_ref[...]

    def qknorm(v, gain):
        ms = jnp.dot(v * v, bdm, precision=HIGHEST, preferred_element_type=F32) * (1.0 / DIFF_DH)
        return v * lax.rsqrt(ms + NORM_EPS) * gain

    qn = qknorm(x[:, 0:DIFF_QK], gq_ref[...]) * (DIFF_DH ** -0.5 * LOG2E)
    kn = qknorm(x[:, DIFF_QK:2 * DIFF_QK], gk_ref[...])
    pos = pl.program_id(0) * tm + lax.broadcasted_iota(jnp.int32, (tm, DIFF_DH), 0)
    krel = pos % tk
    lane = lax.broadcasted_iota(jnp.int32, (tm, DIFF_DH), 1)
    hi = ((krel // 256) * 256).astype(F32)
    lo = (krel % 256).astype(F32)
    kfeat = jnp.where(lane < 6, jnp.where(lane % 2 == 0, hi, lo), 0.0)
    qfeat = qf_ref[...]
    qparts, kparts = [], []
    for g in range(2 * DIFF_HEADS):
        qparts += [qn[:, g * DIFF_DH:(g + 1) * DIFF_DH],
                   jnp.broadcast_to(qfeat[:, g * DIFF_DH:(g + 1) * DIFF_DH], (tm, DIFF_DH))]
        kparts += [kn[:, g * DIFF_DH:(g + 1) * DIFF_DH], kfeat]
    q_ref[...] = jnp.concatenate(qparts, axis=-1).astype(BF16)
    k_ref[...] = jnp.concatenate(kparts, axis=-1).astype(BF16)
    vt_ref[0, 0] = x[:, 2 * DIFF_QK:].T.astype(BF16)


def _aprep(att_in, gq, gk, qfeat, bdm, nb, seq, tk):
    t = att_in.shape[0]
    tm = tk
    nk = seq // tk
    const = lambda i: (0, 0)
    return pl.pallas_call(
        functools.partial(_aprep_kernel, tk=tk),
        out_shape=(jax.ShapeDtypeStruct((t, 2 * DIFF_QK), BF16),
                   jax.ShapeDtypeStruct((t, 2 * DIFF_QK), BF16),
                   jax.ShapeDtypeStruct((nb, nk, DIFF_V, tk), BF16)),
        grid=(t // tm,),
        in_specs=[pl.BlockSpec((tm, att_in.shape[1]), lambda i: (i, 0)),
                  pl.BlockSpec(gq.shape, const), pl.BlockSpec(gk.shape, const),
                  pl.BlockSpec(qfeat.shape, const), pl.BlockSpec(bdm.shape, const)],
        out_specs=(pl.BlockSpec((tm, 2 * DIFF_QK), lambda i: (i, 0)),
                   pl.BlockSpec((tm, 2 * DIFF_QK), lambda i: (i, 0)),
                   pl.BlockSpec((1, 1, DIFF_V, tk), lambda i: (i // nk, i % nk, 0, 0))),
        compiler_params=_cparams(("arbitrary",)),
        name="aprep",
    )(att_in, gq, gk, qfeat, bdm)


def _attn_kernel(slope_ref, q_ref, k_ref, vt_ref, lamv_ref, sg_ref, o_ref, m_s, l_s, acc_s,
                 *, tq, lam_init):
    h = pl.program_id(1)
    qi = pl.program_id(2)
    slope2 = slope_ref[h]
    dv = 2 * DIFF_DH
    qs = [q_ref[:, 0:LANES], q_ref[:, LANES:2 * LANES]]

    m_s[...] = jnp.full_like(m_s, -jnp.inf)
    l_s[...] = jnp.zeros_like(l_s)
    acc_s[...] = jnp.zeros_like(acc_s)

    def tile(kj, add):
        k0 = pl.multiple_of(kj * tq, tq)
        vt = vt_ref[0, kj]
        off = slope2 * ((kj - qi) * tq).astype(F32)
        for c in range(2):
            kc = k_ref[pl.ds(k0, tq), c * LANES:(c + 1) * LANES]
            s = _dot_nt(kc, qs[c])
            if add is not None:
                s = s + add
            m_old = m_s[c]
            m_new = jnp.maximum(m_old, jnp.max(s, axis=0, keepdims=True) + off)
            alpha = jnp.exp2(m_old - m_new)
            p = jnp.exp2(s - (m_new - off))
            l_s[c] = alpha * l_s[c] + jnp.sum(p, axis=0, keepdims=True)
            acc_s[c] = alpha * acc_s[c] + _dot(vt, p.astype(BF16))
            m_s[c] = m_new

    kpos = lax.broadcasted_iota(jnp.int32, (tq, tq), 0)
    qpos = lax.broadcasted_iota(jnp.int32, (tq, tq), 1)
    allowed = (kpos // CHUNK) <= (qpos // CHUNK)
    fut = jnp.maximum(kpos - qpos, 0).astype(F32)
    tile(qi, jnp.where(allowed, (-2.0 * slope2) * fut, -jnp.inf))

    def body(kj, carry):
        tile(kj, None)
        return carry
    lax.fori_loop(0, qi, body, 0)

    lq1, lk1, lq2, lk2 = (lamv_ref[i:i + 1, :] for i in range(4))
    lam = (jnp.exp(jnp.sum(lq1 * lk1, axis=-1, keepdims=True))
           - jnp.exp(jnp.sum(lq2 * lk2, axis=-1, keepdims=True)) + lam_init)
    o = acc_s[0] / l_s[0] - lam * (acc_s[1] / l_s[1])
    on = o * lax.rsqrt(jnp.mean(o * o, axis=0, keepdims=True) + NORM_EPS) * sg_ref[...] * (1.0 - lam_init)
    o_ref[...] = on.T


def _attn(slopes2, qa, ka, vt, lamv, sg_col, nb, seq, tq, lam_init):
    t = qa.shape[0]
    nq = seq // tq
    dv = 2 * DIFF_DH
    grid_spec = pltpu.PrefetchScalarGridSpec(
        num_scalar_prefetch=1,
        grid=(nb, DIFF_HEADS, nq),
        in_specs=[pl.BlockSpec((tq, 2 * LANES), lambda b, h, i, s: (b * nq + i, h)),
                  pl.BlockSpec((seq, 2 * LANES), lambda b, h, i, s: (b, h)),
                  pl.BlockSpec((1, nq, dv, tq), lambda b, h, i, s: (b, 0, h, 0)),
                  pl.BlockSpec(lamv.shape, lambda b, h, i, s: (0, 0)),
                  pl.BlockSpec(sg_col.shape, lambda b, h, i, s: (0, 0))],
        out_specs=pl.BlockSpec((tq, dv), lambda b, h, i, s: (b * nq + i, h)),
        scratch_shapes=[pltpu.VMEM((2, 1, tq), F32), pltpu.VMEM((2, 1, tq), F32),
                        pltpu.VMEM((2, dv, tq), F32)],
    )
    return pl.pallas_call(
        functools.partial(_attn_kernel, tq=tq, lam_init=lam_init),
        out_shape=jax.ShapeDtypeStruct((t, DIFF_V), F32),
        grid_spec=grid_spec,
        compiler_params=_cparams(("arbitrary", "arbitrary", "arbitrary")),
        name="attn",
    )(slopes2, qa, ka, vt, lamv, sg_col)


def _post_kernel(x_ref, oa_ref, ob_ref, gate_ref, g1_ref, sh2_ref, sc2_ref, n2_ref,
                 woa_ref, wob_ref, wout_ref, x1_ref, h2_ref):
    ya = _dot(oa_ref[...].astype(BF16), woa_ref[...])
    yb = _dot(ob_ref[...].astype(BF16), wob_ref[...])
    merged = jax.nn.sigmoid(gate_ref[:, 0:D_MODEL]) * ya + jax.nn.sigmoid(gate_ref[:, D_MODEL:]) * yb
    y = _dot(merged.astype(BF16), wout_ref[...])
    x1 = x_ref[...] + g1_ref[0] * y
    x1_ref[...] = x1
    n = x1 * lax.rsqrt(jnp.mean(x1 * x1, axis=-1, keepdims=True) + NORM_EPS) * n2_ref[...]
    h2_ref[...] = n * (1.0 + sc2_ref[0]) + sh2_ref[0]


def _post(x2, oa, ob, gates, mod3, norm2_g, woa, wob, wout, seq):
    t = x2.shape[0]
    tm = min(256, seq)
    tpb = seq // tm
    const = lambda i: (0, 0)
    row = lambda n: pl.BlockSpec((tm, n), lambda i: (i, 0))
    modspec = lambda j: pl.BlockSpec((1, 1, D_MODEL), lambda i: (i // tpb, 0, j))
    wspec = lambda w: pl.BlockSpec(w.shape, const, pipeline_mode=pl.Buffered(1))
    return pl.pallas_call(
        _post_kernel,
        out_shape=(jax.ShapeDtypeStruct((t, D_MODEL), F32), jax.ShapeDtypeStruct((t, D_MODEL), F32)),
        grid=(t // tm,),
        in_specs=[row(D_MODEL), row(GDN_V), row(DIFF_V), row(2 * D_MODEL),
                  modspec(2), modspec(3), modspec(4), pl.BlockSpec((1, D_MODEL), const),
                  wspec(woa), wspec(wob), wspec(wout)],
        out_specs=(row(D_MODEL), row(D_MODEL)),
        compiler_params=_cparams(("arbitrary",)),
        name="post",
    )(x2, oa, ob, gates, mod3, mod3, mod3, norm2_g, woa, wob, wout)


def _router_kernel(h_ref, wr_ref, rb_ref, idx_ref, w_ref, cnt_ref, cnt_s):
    step = pl.program_id(0)

    @pl.when(step == 0)
    def _():
        cnt_s[...] = jnp.zeros_like(cnt_s)

    tm = h_ref.shape[0]
    logits = jnp.dot(h_ref[...], wr_ref[...], precision=HIGHEST, preferred_element_type=F32)
    scores = jax.nn.sigmoid(logits)
    choice = scores + rb_ref[...]
    lane = lax.broadcasted_iota(jnp.int32, (tm, N_EXPERTS), 1)
    grp = lane // GROUP_SIZE
    neg = -jnp.inf

    def first_max(v):
        m = jnp.max(v, axis=-1, keepdims=True)
        i = jnp.min(jnp.where(v == m, lane, N_EXPERTS), axis=-1, keepdims=True)
        return m, i

    gl = lax.broadcasted_iota(jnp.int32, (tm, LANES), 1)
    gscore = jnp.full((tm, LANES), neg, F32)
    for g in range(N_GROUPS):
        vg = jnp.where(grp == g, choice, neg)
        m1, i1 = first_max(vg)
        m2 = jnp.max(jnp.where(lane == i1, neg, vg), axis=-1, keepdims=True)
        gscore = jnp.where(gl == g, m1 + m2, gscore)
    emask = jnp.zeros((tm, N_EXPERTS), jnp.bool_)
    for _ in range(TOPK_GROUPS):
        m = jnp.max(gscore, axis=-1, keepdims=True)
        gi = jnp.min(jnp.where(gscore == m, gl, LANES), axis=-1, keepdims=True)
        emask = emask | (grp == gi)
        gscore = jnp.where(gl == gi, neg, gscore)
    masked = jnp.where(emask, choice, neg)
    idxs = jnp.zeros((tm, LANES), jnp.int32)
    ws = jnp.zeros((tm, LANES), F32)
    sel = jnp.zeros((tm, N_EXPERTS), F32)
    for k in range(TOP_K):
        _, i = first_max(masked)
        hit = lane == i
        wk = jnp.sum(jnp.where(hit, scores, 0.0), axis=-1, keepdims=True)
        idxs = jnp.where(gl == k, i, idxs)
        ws = jnp.where(gl == k, wk, ws)
        sel = jnp.where(hit, 1.0, sel)
        masked = jnp.where(hit, neg, masked)
    wsum = jnp.sum(ws, axis=-1, keepdims=True)
    idx_ref[...] = idxs
    w_ref[...] = ws / wsum * ROUTED_SCALE
    cnt_s[...] = cnt_s[...] + jnp.sum(sel, axis=0, keepdims=True)
    cnt_ref[...] = cnt_s[...]


def _router(h2, w_router, rbias):
    t = h2.shape[0]
    tm = min(256, t)
    const = lambda i: (0, 0)
    return pl.pallas_call(
        _router_kernel,
        out_shape=(jax.ShapeDtypeStruct((t, LANES), jnp.int32),
                   jax.ShapeDtypeStruct((t, LANES), F32),
                   jax.ShapeDtypeStruct((1, N_EXPERTS), F32)),
        grid=(t // tm,),
        in_specs=[pl.BlockSpec((tm, D_MODEL), lambda i: (i, 0)),
                  pl.BlockSpec(w_router.shape, const), pl.BlockSpec(rbias.shape, const)],
        out_specs=(pl.BlockSpec((tm, LANES), lambda i: (i, 0)),
                   pl.BlockSpec((tm, LANES), lambda i: (i, 0)),
                   pl.BlockSpec((1, N_EXPERTS), const)),
        scratch_shapes=[pltpu.VMEM((1, N_EXPERTS), F32)],
        compiler_params=_cparams(("arbitrary",)),
        name="router",
    )(h2, w_router, rbias)


def _rank_kernel(idx_ref, ps_ref, dest_ref, run_s):
    step = pl.program_id(0)

    @pl.when(step == 0)
    def _():
        run_s[...] = jnp.zeros_like(run_s)

    tm = idx_ref.shape[0]
    idx = idx_ref[...]
    lane = lax.broadcasted_iota(jnp.int32, (tm, N_EXPERTS), 1)
    gl = lax.broadcasted_iota(jnp.int32, (tm, LANES), 1)
    hits = [lane == idx[:, k:k + 1] for k in range(TOP_K)]
    sel = jnp.zeros((tm, N_EXPERTS), F32)
    for hit in hits:
        sel = jnp.where(hit, 1.0, sel)
    ri = lax.broadcasted_iota(jnp.int32, (tm, tm), 0)
    ci = lax.broadcasted_iota(jnp.int32, (tm, tm), 1)
    before = _dot((ri > ci).astype(BF16), sel.astype(BF16))
    base = before + run_s[...] + ps_ref[...]
    dest = jnp.zeros((tm, LANES), F32)
    for k, hit in enumerate(hits):
        dk = jnp.sum(jnp.where(hit, base, 0.0), axis=-1, keepdims=True)
        dest = jnp.where(gl == k, dk, dest)
    dest_ref[...] = dest.astype(jnp.int32)
    run_s[...] = run_s[...] + jnp.sum(sel, axis=0, keepdims=True)


def _rank(idx, pstart):
    t = idx.shape[0]
    tm = min(256, t)
    const = lambda i: (0, 0)
    return pl.pallas_call(
        _rank_kernel,
        out_shape=jax.ShapeDtypeStruct((t, LANES), jnp.int32),
        grid=(t // tm,),
        in_specs=[pl.BlockSpec((tm, LANES), lambda i: (i, 0)), pl.BlockSpec(pstart.shape, const)],
        out_specs=pl.BlockSpec((tm, LANES), lambda i: (i, 0)),
        scratch_shapes=[pltpu.VMEM((1, N_EXPERTS), F32)],
        compiler_params=_cparams(("arbitrary",)),
        name="rank",
    )(idx, pstart)


def _row_copy(src, dst, sem):
    return pltpu.make_async_copy(src, dst, sem)


def _dispatch_kernel(dest_ref, h_ref, xs_in, xs_ref, sem, *, tm):
    del xs_in

    def start_row(r, c):
        for k in range(TOP_K):
            d = dest_ref[r * TOP_K + k]
            _row_copy(h_ref.at[pl.ds(r, 1), :], xs_ref.at[pl.ds(d, 1), :], sem).start(priority=k % 2)
        return c

    lax.fori_loop(0, tm, start_row, 0)
    for k in range(TOP_K):
        _row_copy(h_ref, xs_ref.at[pl.ds(0, tm), :], sem).wait()


def _dispatch(dest_flat, h2, xs0):
    t = h2.shape[0]
    tm = min(256, t)
    return pl.pallas_call(
        functools.partial(_dispatch_kernel, tm=tm),
        out_shape=jax.ShapeDtypeStruct(xs0.shape, xs0.dtype),
        grid=(t // tm,),
        in_specs=[pl.BlockSpec((tm * TOP_K,), lambda i: (i,), memory_space=pltpu.SMEM),
                  pl.BlockSpec((tm, D_MODEL), lambda i: (i, 0)),
                  pl.BlockSpec(memory_space=pl.ANY)],
        out_specs=pl.BlockSpec(memory_space=pl.ANY),
        scratch_shapes=[pltpu.SemaphoreType.DMA(())],
        input_output_aliases={2: 0},
        compiler_params=_cparams(("arbitrary",)),
        name="dispatch",
    )(dest_flat, h2, xs0)


def _experts_kernel(be_ref, nb_ref, xs_ref, wgu_ref, wdn_ref, ys_ref, wgu_s, wdn_s):
    b = pl.program_id(0)
    prev = be_ref[jnp.maximum(b - 1, 0)]
    changed = (b == 0) | (be_ref[b] != prev)

    @pl.when(changed)
    def _():
        wgu_s[...] = wgu_ref[0].astype(BF16)
        wdn_s[...] = wdn_ref[0].astype(BF16)

    @pl.when(b < nb_ref[0])
    def _():
        gu = _dot(xs_ref[...].astype(BF16), wgu_s[...])
        act = _silu(gu[:, :EXPERT_FF]) * gu[:, EXPERT_FF:]
        ys_ref[...] = _dot(act.astype(BF16), wdn_s[...])

    @pl.when(b >= nb_ref[0])
    def _():
        ys_ref[...] = jnp.zeros_like(ys_ref)


def _experts(block_e, nb_used, xs, w_gu, w_dn):
    rows = xs.shape[0]
    nblk = rows // EXPERT_BLOCK
    grid_spec = pltpu.PrefetchScalarGridSpec(
        num_scalar_prefetch=2,
        grid=(nblk,),
        in_specs=[pl.BlockSpec((EXPERT_BLOCK, D_MODEL), lambda b, be, nb: (b, 0)),
                  pl.BlockSpec((1, D_MODEL, 2 * EXPERT_FF), lambda b, be, nb: (be[b], 0, 0)),
                  pl.BlockSpec((1, EXPERT_FF, D_MODEL), lambda b, be, nb: (be[b], 0, 0))],
        out_specs=pl.BlockSpec((EXPERT_BLOCK, D_MODEL), lambda b, be, nb: (b, 0)),
        scratch_shapes=[pltpu.VMEM((D_MODEL, 2 * EXPERT_FF), BF16), pltpu.VMEM((EXPERT_FF, D_MODEL), BF16)],
    )
    return pl.pallas_call(
        _experts_kernel,
        out_shape=jax.ShapeDtypeStruct((rows, D_MODEL), F32),
        grid_spec=grid_spec,
        compiler_params=_cparams(("arbitrary",)),
        name="experts",
    )(block_e, nb_used, xs, w_gu, w_dn)


def _combine_kernel(dest_ref, x1_ref, h_ref, w_ref, g2_ref, wsgu_ref, wsdn_ref, ys_ref, o_ref, buf, sem, *, tm):
    def start_row(r, c):
        for k in range(TOP_K):
            d = dest_ref[r * TOP_K + k]
            _row_copy(ys_ref.at[pl.ds(d, 1), :], buf.at[k, pl.ds(r, 1), :], sem).start(priority=k % 2)
        return c

    lax.fori_loop(0, tm, start_row, 0)
    su = _dot(h_ref[...].astype(BF16), wsgu_ref[...])
    y = _dot((_silu(su[:, :SHARED_FF]) * su[:, SHARED_FF:]).astype(BF16), wsdn_ref[...])
    for k in range(TOP_K):
        _row_copy(ys_ref.at[pl.ds(0, tm), :], buf.at[k], sem).wait()
    w = w_ref[...]
    for k in range(TOP_K):
        y = y + w[:, k:k + 1] * buf[k]
    o_ref[...] = x1_ref[...] + g2_ref[0] * y


def _combine(dest_flat, x1, h2, wts, mod3, wsgu, wsdn, ys, seq):
    t = x1.shape[0]
    tm = min(128, seq)
    tpb = seq // tm
    const = lambda i: (0, 0)
    row = lambda n: pl.BlockSpec((tm, n), lambda i: (i, 0))
    return pl.pallas_call(
        functools.partial(_combine_kernel, tm=tm),
        out_shape=jax.ShapeDtypeStruct((t, D_MODEL), F32),
        grid=(t // tm,),
        in_specs=[pl.BlockSpec((tm * TOP_K,), lambda i: (i,), memory_space=pltpu.SMEM),
                  row(D_MODEL), row(D_MODEL), row(LANES),
                  pl.BlockSpec((1, 1, D_MODEL), lambda i: (i // tpb, 0, 5)),
                  pl.BlockSpec(wsgu.shape, const), pl.BlockSpec(wsdn.shape, const),
                  pl.BlockSpec(memory_space=pl.ANY)],
        out_specs=row(D_MODEL),
        scratch_shapes=[pltpu.VMEM((TOP_K, tm, D_MODEL), F32), pltpu.SemaphoreType.DMA(())],
        compiler_params=_cparams(("arbitrary",)),
        name="combine",
    )(dest_flat, x1, h2, wts, mod3, wsgu, wsdn, ys)


def _layer(x, c, layer, w_ada, b_ada, norm1_g, w_in, conv_w, a_log, dt_bias, gdn_norm_g, w_o_gdn,
           q_norm_g, k_norm_g, lambda_q1, lambda_k1, lambda_q2, lambda_k2, subln_g, w_o_diff,
           w_out, norm2_g, w_router, router_bias, w_exp_gate_up, w_exp_down,
           w_shared_gate_up, w_shared_down):
    nb, seq, d = x.shape
    t = nb * seq
    lam_init = 0.8 - 0.6 * math.exp(-0.3 * layer)
    x2 = x.reshape(t, d)

    c_pad = jnp.pad(c, ((0, SUBLANES - nb % SUBLANES if nb % SUBLANES else 0), (0, 0)))
    mod = _ada(c_pad, w_ada, b_ada.reshape(1, -1))[:nb]
    mod3 = mod.reshape(nb, 1, 6 * d)

    o_bd = 2 * GDN_QK + 2 * GDN_V
    o_att = o_bd + 2 * GDN_HEADS
    o_gate = o_att + 2 * DIFF_QK + DIFF_V
    w_gdn = w_in[:, :o_bd].astype(BF16)
    w_bd = jnp.pad(w_in[:, o_bd:o_att], ((0, 0), (0, LANES - 2 * GDN_HEADS))).astype(BF16)
    w_att = w_in[:, o_att:o_gate].astype(BF16)
    w_gate = w_in[:, o_gate:].astype(BF16)
    gdn_in, bd, att_in, gates = _inproj(x2, mod3, norm1_g.reshape(1, d), w_gdn, w_bd, w_att, w_gate, seq)

    aparams = jnp.zeros((SUBLANES, LANES), F32)
    aparams = aparams.at[0, GDN_HEADS:2 * GDN_HEADS].set(a_log).at[1, GDN_HEADS:2 * GDN_HEADS].set(dt_bias)
    oa = _gdn(gdn_in.reshape(nb, seq, -1), bd.reshape(nb, seq, LANES), conv_w, aparams,
              gdn_norm_g.reshape(1, GDN_DV)).reshape(t, GDN_V)

    tq = min(512, seq)
    slopes = [2.0 ** (-8.0 * (h + 1) / DIFF_HEADS) for h in range(DIFF_HEADS)]
    c3 = _bf16_split3(LOG2E)
    qfeat = np.zeros((1, 2 * DIFF_HEADS * DIFF_DH), np.float32)
    for g in range(2 * DIFF_HEADS):
        for j in range(6):
            qfeat[0, g * DIFF_DH + j] = slopes[g // 2] * c3[j // 2]
    grp = np.arange(DIFF_QK) // DIFF_DH
    bdm = jnp.asarray((grp[:, None] == grp[None, :]).astype(np.float32))
    qa, ka, vt = _aprep(att_in, jnp.tile(q_norm_g, 2 * DIFF_HEADS).reshape(1, -1),
                        jnp.tile(k_norm_g, 2 * DIFF_HEADS).reshape(1, -1), jnp.asarray(qfeat), bdm, nb, seq, tq)
    slopes2 = jnp.asarray([s * (c3[0] + c3[1] + c3[2]) for s in slopes], F32)
    lamv = jnp.zeros((SUBLANES, DIFF_DH), F32)
    lamv = lamv.at[0].set(lambda_q1).at[1].set(lambda_k1).at[2].set(lambda_q2).at[3].set(lambda_k2)
    ob = _attn(slopes2, qa, ka, vt, lamv, subln_g.reshape(-1, 1), nb, seq, tq, lam_init)

    x1, h2 = _post(x2, oa, ob, gates, mod3, norm2_g.reshape(1, d), w_o_gdn.astype(BF16),
                   w_o_diff.astype(BF16), w_out.astype(BF16), seq)

    idx, wts, counts = _router(h2, w_router, router_bias.reshape(1, -1))
    cnt = counts[0].astype(jnp.int32)
    padded = (cnt + EXPERT_BLOCK - 1) // EXPERT_BLOCK * EXPERT_BLOCK
    pends = jnp.cumsum(padded)
    pstart = (pends - padded).astype(F32).reshape(1, -1)
    nblk = -(-(t * TOP_K) // EXPERT_BLOCK) + N_EXPERTS
    blk_row = jnp.arange(nblk, dtype=jnp.int32) * EXPERT_BLOCK
    block_e = jnp.minimum(jnp.sum((pends[None, :] <= blk_row[:, None]).astype(jnp.int32), axis=1),
                          N_EXPERTS - 1).astype(jnp.int32)
    nb_used = (pends[-1:] // EXPERT_BLOCK).astype(jnp.int32)
    dest = _rank(idx, pstart)
    dest_flat = dest[:, :TOP_K].reshape(-1)
    xs = _dispatch(dest_flat, h2, jnp.zeros((nblk * EXPERT_BLOCK, d), F32))
    ys = _experts(block_e, nb_used, xs, w_exp_gate_up, w_exp_down)
    out = _combine(dest_flat, x1, h2, wts, mod3, w_shared_gate_up.astype(BF16),
                   w_shared_down.astype(BF16), ys, seq)
    return out.reshape(nb, seq, d)


def kernel(x, c, w_ada, b_ada, norm1_g, w_in, conv_w, a_log, dt_bias, gdn_norm_g, w_o_gdn, q_norm_g, k_norm_g, lambda_q1, lambda_k1, lambda_q2, lambda_k2, subln_g, w_o_diff, w_out, norm2_g, w_router, router_bias, w_exp_gate_up, w_exp_down, w_shared_gate_up, w_shared_down):
    params = (w_ada, b_ada, norm1_g, w_in, conv_w, a_log, dt_bias, gdn_norm_g, w_o_gdn, q_norm_g,
              k_norm_g, lambda_q1, lambda_k1, lambda_q2, lambda_k2, subln_g, w_o_diff, w_out, norm2_g,
              w_router, router_bias, w_exp_gate_up, w_exp_down, w_shared_gate_up, w_shared_down)
    for layer in range(w_ada.shape[0]):
        x = _layer(x, c, layer, *(p[layer] for p in params))
    return x
```

```python
import functools
import math

import jax
import jax.numpy as jnp
import numpy as np
from jax import lax
from jax.experimental import pallas as pl
from jax.experimental.pallas import tpu as pltpu

F32 = jnp.float32
BF16 = jnp.bfloat16
HIGHEST = lax.Precision.HIGHEST

D_MODEL = 1024
CHUNK = 64
GDN_HEADS = 4
GDN_DK = 128
GDN_DV = 128
GDN_CONV = 4
DIFF_HEADS = 4
DIFF_DH = 64
N_EXPERTS = 256
TOP_K = 8
N_GROUPS = 8
TOPK_GROUPS = 4
EXPERT_FF = 256
SHARED_FF = 256
ROUTED_SCALE = 2.5
NORM_EPS = 1e-6
GROUP_SIZE = N_EXPERTS // N_GROUPS

GDN_QK = GDN_HEADS * GDN_DK
GDN_V = GDN_HEADS * GDN_DV
CONV_CH = 2 * GDN_QK + GDN_V
DIFF_QK = DIFF_HEADS * 2 * DIFF_DH
DIFF_V = DIFF_HEADS * 2 * DIFF_DH

LANES = 128
SUBLANES = 8
EXPERT_BLOCK = 128
INV_BASE = 8
LOG2E = math.log2(math.e)
VMEM_LIMIT = 56 * 1024 * 1024


def _cparams(sem):
    return pltpu.CompilerParams(dimension_semantics=sem, vmem_limit_bytes=VMEM_LIMIT)


def _dot(a, b):
    return jnp.dot(a, b, preferred_element_type=F32)


def _dot_nt(a, b):
    return lax.dot_general(a, b, (((1,), (1,)), ((), ())), preferred_element_type=F32)


def _dot_tn(a, b):
    return lax.dot_general(a, b, (((0,), (0,)), ((), ())), preferred_element_type=F32)


def _silu(x):
    return x * jax.nn.sigmoid(x)


def _bf16_split3(x):
    rnd = lambda v: float(np.float32(v).astype(BF16).astype(np.float32))
    a = rnd(x)
    b = rnd(x - a)
    c = rnd(x - a - b)
    return a, b, c


def _ada_kernel(c_ref, w_ref, b_ref, o_ref):
    s = _silu(c_ref[...])
    o_ref[...] = jnp.dot(s, w_ref[...], precision=HIGHEST, preferred_element_type=F32) + b_ref[...]


def _ada(c_pad, w_ada, b_ada):
    n = w_ada.shape[1]
    tn = 1024
    return pl.pallas_call(
        _ada_kernel,
        out_shape=jax.ShapeDtypeStruct((c_pad.shape[0], n), F32),
        grid=(n // tn,),
        in_specs=[pl.BlockSpec(c_pad.shape, lambda j: (0, 0)),
                  pl.BlockSpec((D_MODEL, tn), lambda j: (0, j)),
                  pl.BlockSpec((1, tn), lambda j: (0, j))],
        out_specs=pl.BlockSpec((c_pad.shape[0], tn), lambda j: (0, j)),
        compiler_params=_cparams(("arbitrary",)),
        name="ada",
    )(c_pad, w_ada, b_ada)


def _inproj_kernel(x_ref, shift_ref, scale_ref, g_ref, wg_ref, wbd_ref, wa_ref, wgt_ref,
                   gdn_ref, bd_ref, att_ref, gate_ref):
    x = x_ref[...]
    y = x * lax.rsqrt(jnp.mean(x * x, axis=-1, keepdims=True) + NORM_EPS) * g_ref[...]
    h = (y * (1.0 + scale_ref[0]) + shift_ref[0]).astype(BF16)
    gdn_ref[...] = _dot(h, wg_ref[...])
    bd_ref[...] = _dot(h, wbd_ref[...])
    att_ref[...] = _dot(h, wa_ref[...])
    gate_ref[...] = _dot(h, wgt_ref[...])


def _inproj(x2, mod3, norm1_g, w_gdn, w_bd, w_att, w_gate, seq):
    t = x2.shape[0]
    tm = min(256, seq)
    tiles_per_b = seq // tm
    const = lambda i: (0, 0)
    wspec = lambda w: pl.BlockSpec(w.shape, const, pipeline_mode=pl.Buffered(1))
    row = lambda n: pl.BlockSpec((tm, n), lambda i: (i, 0))
    return pl.pallas_call(
        _inproj_kernel,
        out_shape=(jax.ShapeDtypeStruct((t, w_gdn.shape[1]), F32),
                   jax.ShapeDtypeStruct((t, LANES), F32),
                   jax.ShapeDtypeStruct((t, w_att.shape[1]), F32),
                   jax.ShapeDtypeStruct((t, w_gate.shape[1]), F32)),
        grid=(t // tm,),
        in_specs=[row(D_MODEL),
                  pl.BlockSpec((1, 1, D_MODEL), lambda i: (i // tiles_per_b, 0, 0)),
                  pl.BlockSpec((1, 1, D_MODEL), lambda i: (i // tiles_per_b, 0, 1)),
                  pl.BlockSpec((1, D_MODEL), const),
                  wspec(w_gdn), wspec(w_bd), wspec(w_att), wspec(w_gate)],
        out_specs=(row(w_gdn.shape[1]), row(LANES), row(w_att.shape[1]), row(w_gate.shape[1])),
        compiler_params=_cparams(("arbitrary",)),
        name="inproj",
    )(x2, mod3, mod3, norm1_g, w_gdn, w_bd, w_att, w_gate)


def _gdn_kernel(x_ref, bd_ref, cw_ref, ap_ref, ng_ref, o_ref, cbuf, state, *, nb, lg):
    step = pl.program_id(0)
    nc = lg // CHUNK

    @pl.when(step == 0)
    def _():
        cbuf[...] = jnp.zeros_like(cbuf)
        state[...] = jnp.zeros_like(state)

    ri = lax.broadcasted_iota(jnp.int32, (CHUNK, CHUNK), 0)
    ci = lax.broadcasted_iota(jnp.int32, (CHUNK, CHUNK), 1)
    causal = ri >= ci
    strict = ri > ci
    eye = (ri == ci).astype(F32)
    rl = lax.broadcasted_iota(jnp.int32, (lg, lg), 0)
    cl = lax.broadcasted_iota(jnp.int32, (lg, lg), 1)
    blocktri = ((rl >= cl) & ((rl // CHUNK) == (cl // CHUNK))).astype(F32)
    cw = cw_ref[...]
    a_row = ap_ref[0:1, :]
    dt_row = ap_ref[1:2, :]
    ng = ng_ref[...]

    units = []
    for b in range(nb):
        cbuf[b, SUBLANES:SUBLANES + lg, :] = x_ref[b, :, 0:CONV_CH]
        acc = cw[GDN_CONV - 1:GDN_CONV, :] * cbuf[b, SUBLANES:SUBLANES + lg, :]
        for j in range(GDN_CONV - 1):
            off = SUBLANES - (GDN_CONV - 1) + j
            acc = acc + cw[j:j + 1, :] * cbuf[b, off:off + lg, :]
        cbuf[b, 0:SUBLANES, :] = cbuf[b, lg:lg + SUBLANES, :]
        qkv = _silu(acc)

        bd = bd_ref[b]
        beta_t = jax.nn.sigmoid(bd)
        g_t = -jnp.exp(a_row) * jax.nn.softplus(bd + dt_row)
        gcum = jnp.dot(blocktri, g_t, precision=HIGHEST, preferred_element_type=F32)
        gcum_t = gcum.T
        egcum = jnp.exp(gcum)

        for h in range(GDN_HEADS):
            qh = qkv[:, h * GDN_DK:(h + 1) * GDN_DK]
            kh = qkv[:, GDN_QK + h * GDN_DK:GDN_QK + (h + 1) * GDN_DK]
            vh = qkv[:, 2 * GDN_QK + h * GDN_DV:2 * GDN_QK + (h + 1) * GDN_DV]
            qh = qh * lax.rsqrt(jnp.sum(qh * qh, axis=-1, keepdims=True) + NORM_EPS) * (GDN_DK ** -0.5)
            kh = kh * lax.rsqrt(jnp.sum(kh * kh, axis=-1, keepdims=True) + NORM_EPS)
            for c in range(nc):
                r0, r1 = c * CHUNK, (c + 1) * CHUNK
                q, k, v = qh[r0:r1], kh[r0:r1], vh[r0:r1]
                gc = gcum[r0:r1, 4 + h:5 + h]
                gr = gcum_t[4 + h:5 + h, r0:r1]
                egc = egcum[r0:r1, 4 + h:5 + h]
                bcol = beta_t[r0:r1, h:h + 1]
                gl = gcum[r1 - 1:r1, 4 + h:5 + h]
                kb = k.astype(BF16)
                units.append(dict(
                    b=b, h=h, c=c, kb=kb, bcol=bcol,
                    qkb=jnp.concatenate([q.astype(BF16), kb], axis=0),
                    decay=jnp.exp(jnp.where(causal, gc - gr, -jnp.inf)),
                    rhs=jnp.concatenate([v * bcol, k * (bcol * egc)], axis=-1).astype(BF16),
                    qg=(q * egc).astype(BF16), egl=jnp.exp(gl),
                    kdec=(k * jnp.exp(gl - gc)).astype(BF16)))

    for u in units:
        u["kq"] = _dot_nt(u["qkb"], u["kb"])
    def same_block(size):
        return (ri // size) == (ci // size)
    for u in units:
        lower = jnp.where(strict, u["bcol"] * u["kq"][CHUNK:] * u["decay"], 0.0)
        u["lower"] = lower
        diag = jnp.where(same_block(INV_BASE), lower, 0.0)
        u["xinv"] = eye - diag
        u["pw"] = diag.astype(BF16)
    for r in range(INV_BASE.bit_length() - 2):
        for u in units:
            u["pw"] = _dot(u["pw"], u["pw"]).astype(BF16)
        for u in units:
            u["xinv"] = u["xinv"] + _dot(u["xinv"].astype(BF16), u["pw"])
    size = INV_BASE
    while size < CHUNK:
        pair_off = same_block(2 * size) & jnp.logical_not(same_block(size))
        for u in units:
            u["xb"] = u["xinv"].astype(BF16)
            u["cx"] = _dot(jnp.where(pair_off, u["lower"], 0.0).astype(BF16), u["xb"]).astype(BF16)
        for u in units:
            u["xinv"] = u["xinv"] - _dot(u["xb"], u["cx"])
        size *= 2
    for u in units:
        u["sol"] = _dot(u["xinv"].astype(BF16), u["rhs"])
        u["a_qk"] = (u["kq"][:CHUNK] * u["decay"]).astype(BF16)

    st = {(b, h): state[b, h] for b in range(nb) for h in range(GDN_HEADS)}
    for c in range(nc):
        cu = [u for u in units if u["c"] == c]
        for u in cu:
            stb = st[(u["b"], u["h"])].astype(BF16)
            u["ws"] = _dot(jnp.concatenate([u["sol"][:, GDN_DV:].astype(BF16), u["qg"]], axis=0), stb)
        for u in cu:
            u["vnb"] = (u["sol"][:, :GDN_DV] - u["ws"][:CHUNK]).astype(BF16)
        for u in cu:
            u["out"] = u["ws"][CHUNK:] + _dot(u["a_qk"], u["vnb"])
        for u in cu:
            key = (u["b"], u["h"])
            st[key] = st[key] * u["egl"] + _dot_tn(u["kdec"], u["vnb"])
        for u in cu:
            b, h = u["b"], u["h"]
            r0, r1 = c * CHUNK, (c + 1) * CHUNK
            out = u["out"]
            on = out * lax.rsqrt(jnp.mean(out * out, axis=-1, keepdims=True) + NORM_EPS) * ng
            zh = x_ref[b, r0:r1, CONV_CH + h * GDN_DV:CONV_CH + (h + 1) * GDN_DV]
            o_ref[b, r0:r1, h * GDN_DV:(h + 1) * GDN_DV] = on * _silu(zh)
    for (b, h), s in st.items():
        state[b, h] = s


def _gdn(gdn_in3, bd3, conv_w, aparams, gdn_norm_g):
    nb, seq, _ = gdn_in3.shape
    lg = min(128, seq)
    const = lambda i: (0, 0)
    return pl.pallas_call(
        functools.partial(_gdn_kernel, nb=nb, lg=lg),
        out_shape=jax.ShapeDtypeStruct((nb, seq, GDN_V), F32),
        grid=(seq // lg,),
        in_specs=[pl.BlockSpec((nb, lg, gdn_in3.shape[2]), lambda i: (0, i, 0)),
                  pl.BlockSpec((nb, lg, LANES), lambda i: (0, i, 0)),
                  pl.BlockSpec(conv_w.shape, const),
                  pl.BlockSpec(aparams.shape, const),
                  pl.BlockSpec(gdn_norm_g.shape, const)],
        out_specs=pl.BlockSpec((nb, lg, GDN_V), lambda i: (0, i, 0)),
        scratch_shapes=[pltpu.VMEM((nb, lg + SUBLANES, CONV_CH), F32),
                        pltpu.VMEM((nb, GDN_HEADS, GDN_DK, GDN_DV), F32)],
        compiler_params=_cparams(("arbitrary",)),
        name="gdn",
    )(gdn_in3, bd3, conv_w, aparams, gdn_norm_g)


def _aprep_kernel(x_ref, gq_ref, gk_ref, qf_ref, bdm_ref, q_ref, k_ref, vt_ref, *, tk):
    x = x_ref[...]
    tm = x.shape[0]
    bdm = bdm_ref[...]

    def qknorm(v, gain):
        ms = jnp.dot(v * v, bdm, precision=HIGHEST, preferred_element_type=F32) * (1.0 / DIFF_DH)
        return v * lax.rsqrt(ms + NORM_EPS) * gain

    qn = qknorm(x[:, 0:DIFF_QK], gq_ref[...]) * (DIFF_DH ** -0.5 * LOG2E)
    kn = qknorm(x[:, DIFF_QK:2 * DIFF_QK], gk_ref[...])
    pos = pl.program_id(0) * tm + lax.broadcasted_iota(jnp.int32, (tm, DIFF_DH), 0)
    krel = pos % tk
    lane = lax.broadcasted_iota(jnp.int32, (tm, DIFF_DH), 1)
    hi = ((krel // 256) * 256).astype(F32)
    lo = (krel % 256).astype(F32)
    kfeat = jnp.where(lane < 6, jnp.where(lane % 2 == 0, hi, lo), 0.0)
    qfeat = qf_ref[...]
    qparts, kparts = [], []
    for g in range(2 * DIFF_HEADS):
        qparts += [qn[:, g * DIFF_DH:(g + 1) * DIFF_DH],
                   jnp.broadcast_to(qfeat[:, g * DIFF_DH:(g + 1) * DIFF_DH], (tm, DIFF_DH))]
        kparts += [kn[:, g * DIFF_DH:(g + 1) * DIFF_DH], kfeat]
    q_ref[...] = jnp.concatenate(qparts, axis=-1).astype(BF16)
    k_ref[...] = jnp.concatenate(kparts, axis=-1).astype(BF16)
    vt_ref[0, 0] = x[:, 2 * DIFF_QK:].T.astype(BF16)


def _aprep(att_in, gq, gk, qfeat, bdm, nb, seq, tk):
    t = att_in.shape[0]
    tm = tk
    nk = seq // tk
    const = lambda i: (0, 0)
    return pl.pallas_call(
        functools.partial(_aprep_kernel, tk=tk),
        out_shape=(jax.ShapeDtypeStruct((t, 2 * DIFF_QK), BF16),
                   jax.ShapeDtypeStruct((t, 2 * DIFF_QK), BF16),
                   jax.ShapeDtypeStruct((nb, nk, DIFF_V, tk), BF16)),
        grid=(t // tm,),
        in_specs=[pl.BlockSpec((tm, att_in.shape[1]), lambda i: (i, 0)),
                  pl.BlockSpec(gq.shape, const), pl.BlockSpec(gk.shape, const),
                  pl.BlockSpec(qfeat.shape, const), pl.BlockSpec(bdm.shape, const)],
        out_specs=(pl.BlockSpec((tm, 2 * DIFF_QK), lambda i: (i, 0)),
                   pl.BlockSpec((tm, 2 * DIFF_QK), lambda i: (i, 0)),
                   pl.BlockSpec((1, 1, DIFF_V, tk), lambda i: (i // nk, i % nk, 0, 0))),
        compiler_params=_cparams(("arbitrary",)),
        name="aprep",
    )(att_in, gq, gk, qfeat, bdm)


def _attn_kernel(slope_ref, q_ref, k_ref, vt_ref, lamv_ref, sg_ref, o_ref, m_s, l_s, acc_s,
                 *, tq, lam_init):
    h = pl.program_id(1)
    qi = pl.program_id(2)
    slope2 = slope_ref[h]
    dv = 2 * DIFF_DH
    qs = [q_ref[:, 0:LANES], q_ref[:, LANES:2 * LANES]]

    m_s[...] = jnp.full_like(m_s, -jnp.inf)
    l_s[...] = jnp.zeros_like(l_s)
    acc_s[...] = jnp.zeros_like(acc_s)

    def scores(kj):
        k0 = pl.multiple_of(kj * tq, tq)
        return tuple(_dot_nt(k_ref[pl.ds(k0, tq), c * LANES:(c + 1) * LANES], qs[c]) for c in range(2))

    def accumulate(ss, kj):
        vt = vt_ref[0, kj]
        off = slope2 * ((kj - qi) * tq).astype(F32)
        for c in range(2):
            s = ss[c]
            m_old = m_s[c]
            m_new = jnp.maximum(m_old, jnp.max(s, axis=0, keepdims=True) + off)
            alpha = jnp.exp2(m_old - m_new)
            p = jnp.exp2(s - (m_new - off))
            l_s[c] = alpha * l_s[c] + jnp.sum(p, axis=0, keepdims=True)
            acc_s[c] = alpha * acc_s[c] + _dot(vt, p.astype(BF16))
            m_s[c] = m_new

    kpos = lax.broadcasted_iota(jnp.int32, (tq, tq), 0)
    qpos = lax.broadcasted_iota(jnp.int32, (tq, tq), 1)
    allowed = (kpos // CHUNK) <= (qpos // CHUNK)
    fut = jnp.maximum(kpos - qpos, 0).astype(F32)
    add = jnp.where(allowed, (-2.0 * slope2) * fut, -jnp.inf)
    ss0 = tuple(s + add for s in scores(qi))

    def body(kj, ss):
        nxt = scores(kj)
        accumulate(ss, jnp.where(kj == 0, qi, kj - 1))
        return nxt
    ss_last = lax.fori_loop(0, qi, body, ss0)
    accumulate(ss_last, jnp.where(qi == 0, qi, qi - 1))

    lq1, lk1, lq2, lk2 = (lamv_ref[i:i + 1, :] for i in range(4))
    lam = (jnp.exp(jnp.sum(lq1 * lk1, axis=-1, keepdims=True))
           - jnp.exp(jnp.sum(lq2 * lk2, axis=-1, keepdims=True)) + lam_init)
    o = acc_s[0] / l_s[0] - lam * (acc_s[1] / l_s[1])
    on = o * lax.rsqrt(jnp.mean(o * o, axis=0, keepdims=True) + NORM_EPS) * sg_ref[...] * (1.0 - lam_init)
    o_ref[...] = on.T


def _attn(slopes2, qa, ka, vt, lamv, sg_col, nb, seq, tq, lam_init):
    t = qa.shape[0]
    nq = seq // tq
    dv = 2 * DIFF_DH
    grid_spec = pltpu.PrefetchScalarGridSpec(
        num_scalar_prefetch=1,
        grid=(nb, DIFF_HEADS, nq),
        in_specs=[pl.BlockSpec((tq, 2 * LANES), lambda b, h, i, s: (b * nq + i, h)),
                  pl.BlockSpec((seq, 2 * LANES), lambda b, h, i, s: (b, h)),
                  pl.BlockSpec((1, nq, dv, tq), lambda b, h, i, s: (b, 0, h, 0)),
                  pl.BlockSpec(lamv.shape, lambda b, h, i, s: (0, 0)),
                  pl.BlockSpec(sg_col.shape, lambda b, h, i, s: (0, 0))],
        out_specs=pl.BlockSpec((tq, dv), lambda b, h, i, s: (b * nq + i, h)),
        scratch_shapes=[pltpu.VMEM((2, 1, tq), F32), pltpu.VMEM((2, 1, tq), F32),
                        pltpu.VMEM((2, dv, tq), F32)],
    )
    return pl.pallas_call(
        functools.partial(_attn_kernel, tq=tq, lam_init=lam_init),
        out_shape=jax.ShapeDtypeStruct((t, DIFF_V), F32),
        grid_spec=grid_spec,
        compiler_params=_cparams(("arbitrary", "arbitrary", "arbitrary")),
        name="attn",
    )(slopes2, qa, ka, vt, lamv, sg_col)


def _post_kernel(x_ref, oa_ref, ob_ref, gate_ref, g1_ref, sh2_ref, sc2_ref, n2_ref,
                 woa_ref, wob_ref, wout_ref, x1_ref, h2_ref):
    ya = _dot(oa_ref[...].astype(BF16), woa_ref[...])
    yb = _dot(ob_ref[...].astype(BF16), wob_ref[...])
    merged = jax.nn.sigmoid(gate_ref[:, 0:D_MODEL]) * ya + jax.nn.sigmoid(gate_ref[:, D_MODEL:]) * yb
    y = _dot(merged.astype(BF16), wout_ref[...])
    x1 = x_ref[...] + g1_ref[0] * y
    x1_ref[...] = x1
    n = x1 * lax.rsqrt(jnp.mean(x1 * x1, axis=-1, keepdims=True) + NORM_EPS) * n2_ref[...]
    h2_ref[...] = n * (1.0 + sc2_ref[0]) + sh2_ref[0]


def _post(x2, oa, ob, gates, mod3, norm2_g, woa, wob, wout, seq):
    t = x2.shape[0]
    tm = min(256, seq)
    tpb = seq // tm
    const = lambda i: (0, 0)
    row = lambda n: pl.BlockSpec((tm, n), lambda i: (i, 0))
    modspec = lambda j: pl.BlockSpec((1, 1, D_MODEL), lambda i: (i // tpb, 0, j))
    wspec = lambda w: pl.BlockSpec(w.shape, const, pipeline_mode=pl.Buffered(1))
    return pl.pallas_call(
        _post_kernel,
        out_shape=(jax.ShapeDtypeStruct((t, D_MODEL), F32), jax.ShapeDtypeStruct((t, D_MODEL), F32)),
        grid=(t // tm,),
        in_specs=[row(D_MODEL), row(GDN_V), row(DIFF_V), row(2 * D_MODEL),
                  modspec(2), modspec(3), modspec(4), pl.BlockSpec((1, D_MODEL), const),
                  wspec(woa), wspec(wob), wspec(wout)],
        out_specs=(row(D_MODEL), row(D_MODEL)),
        compiler_params=_cparams(("arbitrary",)),
        name="post",
    )(x2, oa, ob, gates, mod3, mod3, mod3, norm2_g, woa, wob, wout)


def _router_kernel(h_ref, wr_ref, rb_ref, idx_ref, w_ref, cnt_ref, cnt_s):
    step = pl.program_id(0)

    @pl.when(step == 0)
    def _():
        cnt_s[...] = jnp.zeros_like(cnt_s)

    tm = h_ref.shape[0]
    logits = jnp.dot(h_ref[...], wr_ref[...], precision=HIGHEST, preferred_element_type=F32)
    scores = jax.nn.sigmoid(logits)
    choice = scores + rb_ref[...]
    lane = lax.broadcasted_iota(jnp.int32, (tm, N_EXPERTS), 1)
    grp = lane // GROUP_SIZE
    neg = -jnp.inf

    def first_max(v):
        m = jnp.max(v, axis=-1, keepdims=True)
        i = jnp.min(jnp.where(v == m, lane, N_EXPERTS), axis=-1, keepdims=True)
        return m, i

    gl = lax.broadcasted_iota(jnp.int32, (tm, LANES), 1)
    gscore = jnp.full((tm, LANES), neg, F32)
    for g in range(N_GROUPS):
        vg = jnp.where(grp == g, choice, neg)
        m1, i1 = first_max(vg)
        m2 = jnp.max(jnp.where(lane == i1, neg, vg), axis=-1, keepdims=True)
        gscore = jnp.where(gl == g, m1 + m2, gscore)
    emask = jnp.zeros((tm, N_EXPERTS), jnp.bool_)
    for _ in range(TOPK_GROUPS):
        m = jnp.max(gscore, axis=-1, keepdims=True)
        gi = jnp.min(jnp.where(gscore == m, gl, LANES), axis=-1, keepdims=True)
        emask = emask | (grp == gi)
        gscore = jnp.where(gl == gi, neg, gscore)
    masked = jnp.where(emask, choice, neg)
    idxs = jnp.zeros((tm, LANES), jnp.int32)
    ws = jnp.zeros((tm, LANES), F32)
    sel = jnp.zeros((tm, N_EXPERTS), F32)
    for k in range(TOP_K):
        _, i = first_max(masked)
        hit = lane == i
        wk = jnp.sum(jnp.where(hit, scores, 0.0), axis=-1, keepdims=True)
        idxs = jnp.where(gl == k, i, idxs)
        ws = jnp.where(gl == k, wk, ws)
        sel = jnp.where(hit, 1.0, sel)
        masked = jnp.where(hit, neg, masked)
    wsum = jnp.sum(ws, axis=-1, keepdims=True)
    idx_ref[...] = idxs
    w_ref[...] = ws / wsum * ROUTED_SCALE
    cnt_s[...] = cnt_s[...] + jnp.sum(sel, axis=0, keepdims=True)
    cnt_ref[...] = cnt_s[...]


def _router(h2, w_router, rbias):
    t = h2.shape[0]
    tm = min(256, t)
    const = lambda i: (0, 0)
    return pl.pallas_call(
        _router_kernel,
        out_shape=(jax.ShapeDtypeStruct((t, LANES), jnp.int32),
                   jax.ShapeDtypeStruct((t, LANES), F32),
                   jax.ShapeDtypeStruct((1, N_EXPERTS), F32)),
        grid=(t // tm,),
        in_specs=[pl.BlockSpec((tm, D_MODEL), lambda i: (i, 0)),
                  pl.BlockSpec(w_router.shape, const), pl.BlockSpec(rbias.shape, const)],
        out_specs=(pl.BlockSpec((tm, LANES), lambda i: (i, 0)),
                   pl.BlockSpec((tm, LANES), lambda i: (i, 0)),
                   pl.BlockSpec((1, N_EXPERTS), const)),
        scratch_shapes=[pltpu.VMEM((1, N_EXPERTS), F32)],
        compiler_params=_cparams(("arbitrary",)),
        name="router",
    )(h2, w_router, rbias)


def _rank_kernel(idx_ref, ps_ref, dest_ref, run_s):
    step = pl.program_id(0)

    @pl.when(step == 0)
    def _():
        run_s[...] = jnp.zeros_like(run_s)

    tm = idx_ref.shape[0]
    idx = idx_ref[...]
    lane = lax.broadcasted_iota(jnp.int32, (tm, N_EXPERTS), 1)
    gl = lax.broadcasted_iota(jnp.int32, (tm, LANES), 1)
    hits = [lane == idx[:, k:k + 1] for k in range(TOP_K)]
    sel = jnp.zeros((tm, N_EXPERTS), F32)
    for hit in hits:
        sel = jnp.where(hit, 1.0, sel)
    ri = lax.broadcasted_iota(jnp.int32, (tm, tm), 0)
    ci = lax.broadcasted_iota(jnp.int32, (tm, tm), 1)
    before = _dot((ri > ci).astype(BF16), sel.astype(BF16))
    base = before + run_s[...] + ps_ref[...]
    dest = jnp.zeros((tm, LANES), F32)
    for k, hit in enumerate(hits):
        dk = jnp.sum(jnp.where(hit, base, 0.0), axis=-1, keepdims=True)
        dest = jnp.where(gl == k, dk, dest)
    dest_ref[...] = dest.astype(jnp.int32)
    run_s[...] = run_s[...] + jnp.sum(sel, axis=0, keepdims=True)


def _rank(idx, pstart):
    t = idx.shape[0]
    tm = min(256, t)
    const = lambda i: (0, 0)
    return pl.pallas_call(
        _rank_kernel,
        out_shape=jax.ShapeDtypeStruct((t, LANES), jnp.int32),
        grid=(t // tm,),
        in_specs=[pl.BlockSpec((tm, LANES), lambda i: (i, 0)), pl.BlockSpec(pstart.shape, const)],
        out_specs=pl.BlockSpec((tm, LANES), lambda i: (i, 0)),
        scratch_shapes=[pltpu.VMEM((1, N_EXPERTS), F32)],
        compiler_params=_cparams(("arbitrary",)),
        name="rank",
    )(idx, pstart)


def _row_copy(src, dst, sem):
    return pltpu.make_async_copy(src, dst, sem)


def _dispatch_kernel(dest_ref, h_ref, xs_in, xs_ref, sem, *, tm):
    del xs_in

    def start_row(r, c):
        for k in range(TOP_K):
            d = dest_ref[r * TOP_K + k]
            _row_copy(h_ref.at[pl.ds(r, 1), :], xs_ref.at[pl.ds(d, 1), :], sem).start(priority=k % 2)
        return c

    lax.fori_loop(0, tm, start_row, 0)
    for k in range(TOP_K):
        _row_copy(h_ref, xs_ref.at[pl.ds(0, tm), :], sem).wait()


def _dispatch(dest_flat, h2, xs0):
    t = h2.shape[0]
    tm = min(256, t)
    return pl.pallas_call(
        functools.partial(_dispatch_kernel, tm=tm),
        out_shape=jax.ShapeDtypeStruct(xs0.shape, xs0.dtype),
        grid=(t // tm,),
        in_specs=[pl.BlockSpec((tm * TOP_K,), lambda i: (i,), memory_space=pltpu.SMEM),
                  pl.BlockSpec((tm, D_MODEL), lambda i: (i, 0)),
                  pl.BlockSpec(memory_space=pl.ANY)],
        out_specs=pl.BlockSpec(memory_space=pl.ANY),
        scratch_shapes=[pltpu.SemaphoreType.DMA(())],
        input_output_aliases={2: 0},
        compiler_params=_cparams(("arbitrary",)),
        name="dispatch",
    )(dest_flat, h2, xs0)


def _experts_kernel(be_ref, nb_ref, xs_ref, wgu_ref, wdn_ref, ys_ref, wgu_s, wdn_s):
    b = pl.program_id(0)
    prev = be_ref[jnp.maximum(b - 1, 0)]
    changed = (b == 0) | (be_ref[b] != prev)

    @pl.when(changed)
    def _():
        wgu_s[...] = wgu_ref[0].astype(BF16)
        wdn_s[...] = wdn_ref[0].astype(BF16)

    @pl.when(b < nb_ref[0])
    def _():
        gu = _dot(xs_ref[...].astype(BF16), wgu_s[...])
        act = _silu(gu[:, :EXPERT_FF]) * gu[:, EXPERT_FF:]
        ys_ref[...] = _dot(act.astype(BF16), wdn_s[...])

    @pl.when(b >= nb_ref[0])
    def _():
        ys_ref[...] = jnp.zeros_like(ys_ref)


def _experts(block_e, nb_used, xs, w_gu, w_dn):
    rows = xs.shape[0]
    nblk = rows // EXPERT_BLOCK
    grid_spec = pltpu.PrefetchScalarGridSpec(
        num_scalar_prefetch=2,
        grid=(nblk,),
        in_specs=[pl.BlockSpec((EXPERT_BLOCK, D_MODEL), lambda b, be, nb: (b, 0)),
                  pl.BlockSpec((1, D_MODEL, 2 * EXPERT_FF), lambda b, be, nb: (be[b], 0, 0)),
                  pl.BlockSpec((1, EXPERT_FF, D_MODEL), lambda b, be, nb: (be[b], 0, 0))],
        out_specs=pl.BlockSpec((EXPERT_BLOCK, D_MODEL), lambda b, be, nb: (b, 0)),
        scratch_shapes=[pltpu.VMEM((D_MODEL, 2 * EXPERT_FF), BF16), pltpu.VMEM((EXPERT_FF, D_MODEL), BF16)],
    )
    return pl.pallas_call(
        _experts_kernel,
        out_shape=jax.ShapeDtypeStruct((rows, D_MODEL), F32),
        grid_spec=grid_spec,
        compiler_params=_cparams(("arbitrary",)),
        name="experts",
    )(block_e, nb_used, xs, w_gu, w_dn)


def _combine_kernel(dest_ref, x1_ref, h_ref, w_ref, g2_ref, wsgu_ref, wsdn_ref, ys_ref, o_ref, buf, sem, *, tm):
    def start_row(r, c):
        for k in range(TOP_K):
            d = dest_ref[r * TOP_K + k]
            _row_copy(ys_ref.at[pl.ds(d, 1), :], buf.at[k, pl.ds(r, 1), :], sem).start(priority=k % 2)
        return c

    lax.fori_loop(0, tm, start_row, 0)
    su = _dot(h_ref[...].astype(BF16), wsgu_ref[...])
    y = _dot((_silu(su[:, :SHARED_FF]) * su[:, SHARED_FF:]).astype(BF16), wsdn_ref[...])
    for k in range(TOP_K):
        _row_copy(ys_ref.at[pl.ds(0, tm), :], buf.at[k], sem).wait()
    w = w_ref[...]
    for k in range(TOP_K):
        y = y + w[:, k:k + 1] * buf[k]
    o_ref[...] = x1_ref[...] + g2_ref[0] * y


def _combine(dest_flat, x1, h2, wts, mod3, wsgu, wsdn, ys, seq):
    t = x1.shape[0]
    tm = min(128, seq)
    tpb = seq // tm
    const = lambda i: (0, 0)
    row = lambda n: pl.BlockSpec((tm, n), lambda i: (i, 0))
    return pl.pallas_call(
        functools.partial(_combine_kernel, tm=tm),
        out_shape=jax.ShapeDtypeStruct((t, D_MODEL), F32),
        grid=(t // tm,),
        in_specs=[pl.BlockSpec((tm * TOP_K,), lambda i: (i,), memory_space=pltpu.SMEM),
                  row(D_MODEL), row(D_MODEL), row(LANES),
                  pl.BlockSpec((1, 1, D_MODEL), lambda i: (i // tpb, 0, 5)),
                  pl.BlockSpec(wsgu.shape, const), pl.BlockSpec(wsdn.shape, const),
                  pl.BlockSpec(memory_space=pl.ANY)],
        out_specs=row(D_MODEL),
        scratch_shapes=[pltpu.VMEM((TOP_K, tm, D_MODEL), F32), pltpu.SemaphoreType.DMA(())],
        compiler_params=_cparams(("arbitrary",)),
        name="combine",
    )(dest_flat, x1, h2, wts, mod3, wsgu, wsdn, ys)


def _layer(x, c, layer, w_ada, b_ada, norm1_g, w_in, conv_w, a_log, dt_bias, gdn_norm_g, w_o_gdn,
           q_norm_g, k_norm_g, lambda_q1, lambda_k1, lambda_q2, lambda_k2, subln_g, w_o_diff,
           w_out, norm2_g, w_router, router_bias, w_exp_gate_up, w_exp_down,
           w_shared_gate_up, w_shared_down):
    nb, seq, d = x.shape
    t = nb * seq
    lam_init = 0.8 - 0.6 * math.exp(-0.3 * layer)
    x2 = x.reshape(t, d)

    c_pad = jnp.pad(c, ((0, SUBLANES - nb % SUBLANES if nb % SUBLANES else 0), (0, 0)))
    mod = _ada(c_pad, w_ada, b_ada.reshape(1, -1))[:nb]
    mod3 = mod.reshape(nb, 1, 6 * d)

    o_bd = 2 * GDN_QK + 2 * GDN_V
    o_att = o_bd + 2 * GDN_HEADS
    o_gate = o_att + 2 * DIFF_QK + DIFF_V
    w_gdn = w_in[:, :o_bd].astype(BF16)
    w_bd = jnp.pad(w_in[:, o_bd:o_att], ((0, 0), (0, LANES - 2 * GDN_HEADS))).astype(BF16)
    w_att = w_in[:, o_att:o_gate].astype(BF16)
    w_gate = w_in[:, o_gate:].astype(BF16)
    gdn_in, bd, att_in, gates = _inproj(x2, mod3, norm1_g.reshape(1, d), w_gdn, w_bd, w_att, w_gate, seq)

    aparams = jnp.zeros((SUBLANES, LANES), F32)
    aparams = aparams.at[0, GDN_HEADS:2 * GDN_HEADS].set(a_log).at[1, GDN_HEADS:2 * GDN_HEADS].set(dt_bias)
    oa = _gdn(gdn_in.reshape(nb, seq, -1), bd.reshape(nb, seq, LANES), conv_w, aparams,
              gdn_norm_g.reshape(1, GDN_DV)).reshape(t, GDN_V)

    tq = min(512, seq)
    slopes = [2.0 ** (-8.0 * (h + 1) / DIFF_HEADS) for h in range(DIFF_HEADS)]
    c3 = _bf16_split3(LOG2E)
    qfeat = np.zeros((1, 2 * DIFF_HEADS * DIFF_DH), np.float32)
    for g in range(2 * DIFF_HEADS):
        for j in range(6):
            qfeat[0, g * DIFF_DH + j] = slopes[g // 2] * c3[j // 2]
    grp = np.arange(DIFF_QK) // DIFF_DH
    bdm = jnp.asarray((grp[:, None] == grp[None, :]).astype(np.float32))
    qa, ka, vt = _aprep(att_in, jnp.tile(q_norm_g, 2 * DIFF_HEADS).reshape(1, -1),
                        jnp.tile(k_norm_g, 2 * DIFF_HEADS).reshape(1, -1), jnp.asarray(qfeat), bdm, nb, seq, tq)
    slopes2 = jnp.asarray([s * (c3[0] + c3[1] + c3[2]) for s in slopes], F32)
    lamv = jnp.zeros((SUBLANES, DIFF_DH), F32)
    lamv = lamv.at[0].set(lambda_q1).at[1].set(lambda_k1).at[2].set(lambda_q2).at[3].set(lambda_k2)
    ob = _attn(slopes2, qa, ka, vt, lamv, subln_g.reshape(-1, 1), nb, seq, tq, lam_init)

    x1, h2 = _post(x2, oa, ob, gates, mod3, norm2_g.reshape(1, d), w_o_gdn.astype(BF16),
                   w_o_diff.astype(BF16), w_out.astype(BF16), seq)

    idx, wts, counts = _router(h2, w_router, router_bias.reshape(1, -1))
    cnt = counts[0].astype(jnp.int32)
    padded = (cnt + EXPERT_BLOCK - 1) // EXPERT_BLOCK * EXPERT_BLOCK
    pends = jnp.cumsum(padded)
    pstart = (pends - padded).astype(F32).reshape(1, -1)
    nblk = -(-(t * TOP_K) // EXPERT_BLOCK) + N_EXPERTS
    blk_row = jnp.arange(nblk, dtype=jnp.int32) * EXPERT_BLOCK
    block_e = jnp.minimum(jnp.sum((pends[None, :] <= blk_row[:, None]).astype(jnp.int32), axis=1),
                          N_EXPERTS - 1).astype(jnp.int32)
    nb_used = (pends[-1:] // EXPERT_BLOCK).astype(jnp.int32)
    dest = _rank(idx, pstart)
    dest_flat = dest[:, :TOP_K].reshape(-1)
    xs = _dispatch(dest_flat, h2, jnp.zeros((nblk * EXPERT_BLOCK, d), F32))
    ys = _experts(block_e, nb_used, xs, w_exp_gate_up, w_exp_down)
    out = _combine(dest_flat, x1, h2, wts, mod3, w_shared_gate_up.astype(BF16),
                   w_shared_down.astype(BF16), ys, seq)
    return out.reshape(nb, seq, d)


def kernel(x, c, w_ada, b_ada, norm1_g, w_in, conv_w, a_log, dt_bias, gdn_norm_g, w_o_gdn, q_norm_g, k_norm_g, lambda_q1, lambda_k1, lambda_q2, lambda_k2, subln_g, w_o_diff, w_out, norm2_g, w_router, router_bias, w_exp_gate_up, w_exp_down, w_shared_gate_up, w_shared_down):
    params = (w_ada, b_ada, norm1_g, w_in, conv_w, a_log, dt_bias, gdn_norm_g, w_o_gdn, q_norm_g,
              k_norm_g, lambda_q1, lambda_k1, lambda_q2, lambda_k2, subln_g, w_o_diff, w_out, norm2_g,
              w_router, router_bias, w_exp_gate_up, w_exp_down, w_shared_gate_up, w_shared_down)
    for layer in range(w_ada.shape[0]):
        x = _layer(x, c, layer, *(p[layer] for p in params))
    return x
```

```python
import functools
import math

import jax
import jax.numpy as jnp
import numpy as np
from jax import lax
from jax.experimental import pallas as pl
from jax.experimental.pallas import tpu as pltpu

F32 = jnp.float32
BF16 = jnp.bfloat16
HIGHEST = lax.Precision.HIGHEST

D_MODEL = 1024
CHUNK = 64
GDN_HEADS = 4
GDN_DK = 128
GDN_DV = 128
GDN_CONV = 4
DIFF_HEADS = 4
DIFF_DH = 64
N_EXPERTS = 256
TOP_K = 8
N_GROUPS = 8
TOPK_GROUPS = 4
EXPERT_FF = 256
SHARED_FF = 256
ROUTED_SCALE = 2.5
NORM_EPS = 1e-6
GROUP_SIZE = N_EXPERTS // N_GROUPS

GDN_QK = GDN_HEADS * GDN_DK
GDN_V = GDN_HEADS * GDN_DV
CONV_CH = 2 * GDN_QK + GDN_V
DIFF_QK = DIFF_HEADS * 2 * DIFF_DH
DIFF_V = DIFF_HEADS * 2 * DIFF_DH

LANES = 128
SUBLANES = 8
EXPERT_BLOCK = 128
INV_BASE = 8
LOG2E = math.log2(math.e)
VMEM_LIMIT = 56 * 1024 * 1024


def _cparams(sem):
    return pltpu.CompilerParams(dimension_semantics=sem, vmem_limit_bytes=VMEM_LIMIT)


def _dot(a, b):
    return jnp.dot(a, b, preferred_element_type=F32)


def _dot_nt(a, b):
    return lax.dot_general(a, b, (((1,), (1,)), ((), ())), preferred_element_type=F32)


def _dot_tn(a, b):
    return lax.dot_general(a, b, (((0,), (0,)), ((), ())), preferred_element_type=F32)


def _silu(x):
    return x * jax.nn.sigmoid(x)


def _bf16_split3(x):
    rnd = lambda v: float(np.float32(v).astype(BF16).astype(np.float32))
    a = rnd(x)
    b = rnd(x - a)
    c = rnd(x - a - b)
    return a, b, c


def _ada_kernel(c_ref, w_ref, b_ref, o_ref):
    s = _silu(c_ref[...])
    o_ref[...] = jnp.dot(s, w_ref[...], precision=HIGHEST, preferred_element_type=F32) + b_ref[...]


def _ada(c_pad, w_ada, b_ada):
    n = w_ada.shape[1]
    tn = 1024
    return pl.pallas_call(
        _ada_kernel,
        out_shape=jax.ShapeDtypeStruct((c_pad.shape[0], n), F32),
        grid=(n // tn,),
        in_specs=[pl.BlockSpec(c_pad.shape, lambda j: (0, 0)),
                  pl.BlockSpec((D_MODEL, tn), lambda j: (0, j)),
                  pl.BlockSpec((1, tn), lambda j: (0, j))],
        out_specs=pl.BlockSpec((c_pad.shape[0], tn), lambda j: (0, j)),
        compiler_params=_cparams(("arbitrary",)),
        name="ada",
    )(c_pad, w_ada, b_ada)


def _inproj_kernel(x_ref, shift_ref, scale_ref, g_ref, wg_ref, wbd_ref, wa_ref, wgt_ref,
                   gdn_ref, bd_ref, att_ref, gate_ref):
    x = x_ref[...]
    y = x * lax.rsqrt(jnp.mean(x * x, axis=-1, keepdims=True) + NORM_EPS) * g_ref[...]
    h = (y * (1.0 + scale_ref[0]) + shift_ref[0]).astype(BF16)
    gdn_ref[...] = _dot(h, wg_ref[...])
    bd_ref[...] = _dot(h, wbd_ref[...])
    att_ref[...] = _dot(h, wa_ref[...])
    gate_ref[...] = _dot(h, wgt_ref[...])


def _inproj(x2, mod3, norm1_g, w_gdn, w_bd, w_att, w_gate, seq):
    t = x2.shape[0]
    tm = min(256, seq)
    tiles_per_b = seq // tm
    const = lambda i: (0, 0)
    wspec = lambda w: pl.BlockSpec(w.shape, const, pipeline_mode=pl.Buffered(1))
    row = lambda n: pl.BlockSpec((tm, n), lambda i: (i, 0))
    return pl.pallas_call(
        _inproj_kernel,
        out_shape=(jax.ShapeDtypeStruct((t, w_gdn.shape[1]), F32),
                   jax.ShapeDtypeStruct((t, LANES), F32),
                   jax.ShapeDtypeStruct((t, w_att.shape[1]), F32),
                   jax.ShapeDtypeStruct((t, w_gate.shape[1]), F32)),
        grid=(t // tm,),
        in_specs=[row(D_MODEL),
                  pl.BlockSpec((1, 1, D_MODEL), lambda i: (i // tiles_per_b, 0, 0)),
                  pl.BlockSpec((1, 1, D_MODEL), lambda i: (i // tiles_per_b, 0, 1)),
                  pl.BlockSpec((1, D_MODEL), const),
                  wspec(w_gdn), wspec(w_bd), wspec(w_att), wspec(w_gate)],
        out_specs=(row(w_gdn.shape[1]), row(LANES), row(w_att.shape[1]), row(w_gate.shape[1])),
        compiler_params=_cparams(("arbitrary",)),
        name="inproj",
    )(x2, mod3, mod3, norm1_g, w_gdn, w_bd, w_att, w_gate)


def _gdn_kernel(x_ref, bd_ref, cw_ref, ap_ref, ng_ref, o_ref, cbuf, state, *, nb, lg):
    step = pl.program_id(0)
    nc = lg // CHUNK

    @pl.when(step == 0)
    def _():
        cbuf[...] = jnp.zeros_like(cbuf)
        state[...] = jnp.zeros_like(state)

    ri = lax.broadcasted_iota(jnp.int32, (CHUNK, CHUNK), 0)
    ci = lax.broadcasted_iota(jnp.int32, (CHUNK, CHUNK), 1)
    causal = ri >= ci
    strict = ri > ci
    eye = (ri == ci).astype(F32)
    rl = lax.broadcasted_iota(jnp.int32, (lg, lg), 0)
    cl = lax.broadcasted_iota(jnp.int32, (lg, lg), 1)
    blocktri = ((rl >= cl) & ((rl // CHUNK) == (cl // CHUNK))).astype(F32)
    cw = cw_ref[...]
    a_row = ap_ref[0:1, :]
    dt_row = ap_ref[1:2, :]
    ng = ng_ref[...]

    units = []
    for b in range(nb):
        cbuf[b, SUBLANES:SUBLANES + lg, :] = x_ref[b, :, 0:CONV_CH]
        acc = cw[GDN_CONV - 1:GDN_CONV, :] * cbuf[b, SUBLANES:SUBLANES + lg, :]
        for j in range(GDN_CONV - 1):
            off = SUBLANES - (GDN_CONV - 1) + j
            acc = acc + cw[j:j + 1, :] * cbuf[b, off:off + lg, :]
        cbuf[b, 0:SUBLANES, :] = cbuf[b, lg:lg + SUBLANES, :]
        qkv = _silu(acc)

        bd = bd_ref[b]
        beta_t = jax.nn.sigmoid(bd)
        g_t = -jnp.exp(a_row) * jax.nn.softplus(bd + dt_row)
        gcum = jnp.dot(blocktri, g_t, precision=HIGHEST, preferred_element_type=F32)
        gcum_t = gcum.T
        egcum = jnp.exp(gcum)

        for h in range(GDN_HEADS):
            qh = qkv[:, h * GDN_DK:(h + 1) * GDN_DK]
            kh = qkv[:, GDN_QK + h * GDN_DK:GDN_QK + (h + 1) * GDN_DK]
            vh = qkv[:, 2 * GDN_QK + h * GDN_DV:2 * GDN_QK + (h + 1) * GDN_DV]
            qh = qh * lax.rsqrt(jnp.sum(qh * qh, axis=-1, keepdims=True) + NORM_EPS) * (GDN_DK ** -0.5)
            kh = kh * lax.rsqrt(jnp.sum(kh * kh, axis=-1, keepdims=True) + NORM_EPS)
            for c in range(nc):
                r0, r1 = c * CHUNK, (c + 1) * CHUNK
                q, k, v = qh[r0:r1], kh[r0:r1], vh[r0:r1]
                gc = gcum[r0:r1, 4 + h:5 + h]
                gr = gcum_t[4 + h:5 + h, r0:r1]
                egc = egcum[r0:r1, 4 + h:5 + h]
                bcol = beta_t[r0:r1, h:h + 1]
                gl = gcum[r1 - 1:r1, 4 + h:5 + h]
                kb = k.astype(BF16)
                units.append(dict(
                    b=b, h=h, c=c, kb=kb, bcol=bcol,
                    qkb=jnp.concatenate([q.astype(BF16), kb], axis=0),
                    decay=jnp.exp(jnp.where(causal, gc - gr, -jnp.inf)),
                    rhs=jnp.concatenate([v * bcol, k * (bcol * egc)], axis=-1).astype(BF16),
                    qg=(q * egc).astype(BF16), egl=jnp.exp(gl),
                    kdec=(k * jnp.exp(gl - gc)).astype(BF16)))

    for u in units:
        u["kq"] = _dot_nt(u["qkb"], u["kb"])
    def same_block(size):
        return (ri // size) == (ci // size)
    for u in units:
        lower = jnp.where(strict, u["bcol"] * u["kq"][CHUNK:] * u["decay"], 0.0)
        u["lower"] = lower
        diag = jnp.where(same_block(INV_BASE), lower, 0.0)
        u["xinv"] = eye - diag
        u["pw"] = diag.astype(BF16)
    for r in range(INV_BASE.bit_length() - 2):
        for u in units:
            u["pw"] = _dot(u["pw"], u["pw"]).astype(BF16)
        for u in units:
            u["xinv"] = u["xinv"] + _dot(u["xinv"].astype(BF16), u["pw"])
    size = INV_BASE
    while size < CHUNK:
        pair_off = same_block(2 * size) & jnp.logical_not(same_block(size))
        for u in units:
            u["xb"] = u["xinv"].astype(BF16)
            u["cx"] = _dot(jnp.where(pair_off, u["lower"], 0.0).astype(BF16), u["xb"]).astype(BF16)
        for u in units:
            u["xinv"] = u["xinv"] - _dot(u["xb"], u["cx"])
        size *= 2
    for u in units:
        u["sol"] = _dot(u["xinv"].astype(BF16), u["rhs"])
        u["a_qk"] = (u["kq"][:CHUNK] * u["decay"]).astype(BF16)

    st = {(b, h): state[b, h] for b in range(nb) for h in range(GDN_HEADS)}
    for c in range(nc):
        cu = [u for u in units if u["c"] == c]
        for u in cu:
            stb = st[(u["b"], u["h"])].astype(BF16)
            u["ws"] = _dot(jnp.concatenate([u["sol"][:, GDN_DV:].astype(BF16), u["qg"]], axis=0), stb)
        for u in cu:
            u["vnb"] = (u["sol"][:, :GDN_DV] - u["ws"][:CHUNK]).astype(BF16)
        for u in cu:
            u["out"] = u["ws"][CHUNK:] + _dot(u["a_qk"], u["vnb"])
        for u in cu:
            key = (u["b"], u["h"])
            st[key] = st[key] * u["egl"] + _dot_tn(u["kdec"], u["vnb"])
        for u in cu:
            b, h = u["b"], u["h"]
            r0, r1 = c * CHUNK, (c + 1) * CHUNK
            out = u["out"]
            on = out * lax.rsqrt(jnp.mean(out * out, axis=-1, keepdims=True) + NORM_EPS) * ng
            zh = x_ref[b, r0:r1, CONV_CH + h * GDN_DV:CONV_CH + (h + 1) * GDN_DV]
            o_ref[b, r0:r1, h * GDN_DV:(h + 1) * GDN_DV] = on * _silu(zh)
    for (b, h), s in st.items():
        state[b, h] = s


def _gdn(gdn_in3, bd3, conv_w, aparams, gdn_norm_g):
    nb, seq, _ = gdn_in3.shape
    lg = min(128, seq)
    const = lambda i: (0, 0)
    return pl.pallas_call(
        functools.partial(_gdn_kernel, nb=nb, lg=lg),
        out_shape=jax.ShapeDtypeStruct((nb, seq, GDN_V), F32),
        grid=(seq // lg,),
        in_specs=[pl.BlockSpec((nb, lg, gdn_in3.shape[2]), lambda i: (0, i, 0)),
                  pl.BlockSpec((nb, lg, LANES), lambda i: (0, i, 0)),
                  pl.BlockSpec(conv_w.shape, const),
                  pl.BlockSpec(aparams.shape, const),
                  pl.BlockSpec(gdn_norm_g.shape, const)],
        out_specs=pl.BlockSpec((nb, lg, GDN_V), lambda i: (0, i, 0)),
        scratch_shapes=[pltpu.VMEM((nb, lg + SUBLANES, CONV_CH), F32),
                        pltpu.VMEM((nb, GDN_HEADS, GDN_DK, GDN_DV), F32)],
        compiler_params=_cparams(("arbitrary",)),
        name="gdn",
    )(gdn_in3, bd3, conv_w, aparams, gdn_norm_g)


def _aprep_kernel(x_ref, gq_ref, gk_ref, qf_ref, bdm_ref, q_ref, k_ref, vt_ref, *, tk):
    x = x_ref[...]
    tm = x.shape[0]
    bdm = bdm_ref[...]

    def qknorm(v, gain):
        ms = jnp.dot(v * v, bdm, precision=HIGHEST, preferred_element_type=F32) * (1.0 / DIFF_DH)
        return v * lax.rsqrt(ms + NORM_EPS) * gain

    qn = qknorm(x[:, 0:DIFF_QK], gq_ref[...]) * (DIFF_DH ** -0.5 * LOG2E)
    kn = qknorm(x[:, DIFF_QK:2 * DIFF_QK], gk_ref[...])
    pos = pl.program_id(0) * tm + lax.broadcasted_iota(jnp.int32, (tm, DIFF_DH), 0)
    krel = pos % tk
    lane = lax.broadcasted_iota(jnp.int32, (tm, DIFF_DH), 1)
    hi = ((krel // 256) * 256).astype(F32)
    lo = (krel % 256).astype(F32)
    kfeat = jnp.where(lane < 6, jnp.where(lane % 2 == 0, hi, lo), 0.0)
    qfeat = qf_ref[...]
    qparts, kparts = [], []
    for g in range(2 * DIFF_HEADS):
        qparts += [qn[:, g * DIFF_DH:(g + 1) * DIFF_DH],
                   jnp.broadcast_to(qfeat[:, g * DIFF_DH:(g + 1) * DIFF_DH], (tm, DIFF_DH))]
        kparts += [kn[:, g * DIFF_DH:(g + 1) * DIFF_DH], kfeat]
    q_ref[...] = jnp.concatenate(qparts, axis=-1).astype(BF16)
    k_ref[...] = jnp.concatenate(kparts, axis=-1).astype(BF16)
    vt_ref[0, 0] = x[:, 2 * DIFF_QK:].T.astype(BF16)


def _aprep(att_in, gq, gk, qfeat, bdm, nb, seq, tk):
    t = att_in.shape[0]
    tm = tk
    nk = seq // tk
    const = lambda i: (0, 0)
    return pl.pallas_call(
        functools.partial(_aprep_kernel, tk=tk),
        out_shape=(jax.ShapeDtypeStruct((t, 2 * DIFF_QK), BF16),
                   jax.ShapeDtypeStruct((t, 2 * DIFF_QK), BF16),
                   jax.ShapeDtypeStruct((nb, nk, DIFF_V, tk), BF16)),
        grid=(t // tm,),
        in_specs=[pl.BlockSpec((tm, att_in.shape[1]), lambda i: (i, 0)),
                  pl.BlockSpec(gq.shape, const), pl.BlockSpec(gk.shape, const),
                  pl.BlockSpec(qfeat.shape, const), pl.BlockSpec(bdm.shape, const)],
        out_specs=(pl.BlockSpec((tm, 2 * DIFF_QK), lambda i: (i, 0)),
                   pl.BlockSpec((tm, 2 * DIFF_QK), lambda i: (i, 0)),
                   pl.BlockSpec((1, 1, DIFF_V, tk), lambda i: (i // nk, i % nk, 0, 0))),
        compiler_params=_cparams(("arbitrary",)),
        name="aprep",
    )(att_in, gq, gk, qfeat, bdm)


def _attn_kernel(slope_ref, q_ref, k_ref, vt_ref, lamv_ref, sg_ref, o_ref, m_s, l_s, acc_s,
                 *, tq, lam_init):
    h = pl.program_id(1)
    qi = pl.program_id(2)
    slope2 = slope_ref[h]
    dv = 2 * DIFF_DH
    qs = [q_ref[:, 0:LANES], q_ref[:, LANES:2 * LANES]]

    m_s[...] = jnp.full_like(m_s, -jnp.inf)
    l_s[...] = jnp.zeros_like(l_s)
    acc_s[...] = jnp.zeros_like(acc_s)

    def scores(kj):
        k0 = pl.multiple_of(kj * tq, tq)
        return tuple(_dot_nt(k_ref[pl.ds(k0, tq), c * LANES:(c + 1) * LANES], qs[c]) for c in range(2))

    def accumulate(ss, kj):
        vt = vt_ref[0, kj]
        off = slope2 * ((kj - qi) * tq).astype(F32)
        for c in range(2):
            s = ss[c]
            m_old = m_s[c]
            m_new = jnp.maximum(m_old, jnp.max(s, axis=0, keepdims=True) + off)
            alpha = jnp.exp2(m_old - m_new)
            p = jnp.exp2(s - (m_new - off))
            l_s[c] = alpha * l_s[c] + jnp.sum(p, axis=0, keepdims=True)
            acc_s[c] = alpha * acc_s[c] + _dot(vt, p.astype(BF16))
            m_s[c] = m_new

    kpos = lax.broadcasted_iota(jnp.int32, (tq, tq), 0)
    qpos = lax.broadcasted_iota(jnp.int32, (tq, tq), 1)
    allowed = (kpos // CHUNK) <= (qpos // CHUNK)
    fut = jnp.maximum(kpos - qpos, 0).astype(F32)
    add = jnp.where(allowed, (-2.0 * slope2) * fut, -jnp.inf)
    accumulate(tuple(s + add for s in scores(qi)), qi)

    def body(kj, carry):
        accumulate(scores(kj), kj)
        return carry
    lax.fori_loop(0, qi, body, 0)

    lq1, lk1, lq2, lk2 = (lamv_ref[i:i + 1, :] for i in range(4))
    lam = (jnp.exp(jnp.sum(lq1 * lk1, axis=-1, keepdims=True))
           - jnp.exp(jnp.sum(lq2 * lk2, axis=-1, keepdims=True)) + lam_init)
    o = acc_s[0] / l_s[0] - lam * (acc_s[1] / l_s[1])
    on = o * lax.rsqrt(jnp.mean(o * o, axis=0, keepdims=True) + NORM_EPS) * sg_ref[...] * (1.0 - lam_init)
    o_ref[...] = on.T


def _attn(slopes2, qa, ka, vt, lamv, sg_col, nb, seq, tq, lam_init):
    t = qa.shape[0]
    nq = seq // tq
    dv = 2 * DIFF_DH
    grid_spec = pltpu.PrefetchScalarGridSpec(
        num_scalar_prefetch=1,
        grid=(nb, DIFF_HEADS, nq),
        in_specs=[pl.BlockSpec((tq, 2 * LANES), lambda b, h, i, s: (b * nq + i, h)),
                  pl.BlockSpec((seq, 2 * LANES), lambda b, h, i, s: (b, h)),
                  pl.BlockSpec((1, nq, dv, tq), lambda b, h, i, s: (b, 0, h, 0)),
                  pl.BlockSpec(lamv.shape, lambda b, h, i, s: (0, 0)),
                  pl.BlockSpec(sg_col.shape, lambda b, h, i, s: (0, 0))],
        out_specs=pl.BlockSpec((tq, dv), lambda b, h, i, s: (b * nq + i, h)),
        scratch_shapes=[pltpu.VMEM((2, 1, tq), F32), pltpu.VMEM((2, 1, tq), F32),
                        pltpu.VMEM((2, dv, tq), F32)],
    )
    return pl.pallas_call(
        functools.partial(_attn_kernel, tq=tq, lam_init=lam_init),
        out_shape=jax.ShapeDtypeStruct((t, DIFF_V), F32),
        grid_spec=grid_spec,
        compiler_params=_cparams(("arbitrary", "arbitrary", "arbitrary")),
        name="attn",
    )(slopes2, qa, ka, vt, lamv, sg_col)


def _pack_halves(x):
    n = x.shape[1] // 2
    lo = pltpu.bitcast(x[:, :n].astype(BF16).astype(F32), jnp.uint32)
    hi = pltpu.bitcast(x[:, n:].astype(BF16).astype(F32), jnp.uint32)
    return (lo >> 16) | (hi & jnp.uint32(0xFFFF0000))


def _unpack_halves(p):
    lo = pltpu.bitcast(p << 16, F32)
    hi = pltpu.bitcast(p & jnp.uint32(0xFFFF0000), F32)
    return lo, hi


def _post_kernel(x_ref, oa_ref, ob_ref, gate_ref, g1_ref, sh2_ref, sc2_ref, n2_ref,
                 woa_ref, wob_ref, wout_ref, x1_ref, h2_ref, hp_ref):
    ya = _dot(oa_ref[...].astype(BF16), woa_ref[...])
    yb = _dot(ob_ref[...].astype(BF16), wob_ref[...])
    merged = jax.nn.sigmoid(gate_ref[:, 0:D_MODEL]) * ya + jax.nn.sigmoid(gate_ref[:, D_MODEL:]) * yb
    y = _dot(merged.astype(BF16), wout_ref[...])
    x1 = x_ref[...] + g1_ref[0] * y
    x1_ref[...] = x1
    n = x1 * lax.rsqrt(jnp.mean(x1 * x1, axis=-1, keepdims=True) + NORM_EPS) * n2_ref[...]
    h2 = n * (1.0 + sc2_ref[0]) + sh2_ref[0]
    h2_ref[...] = h2
    hp_ref[...] = _pack_halves(h2)


def _post(x2, oa, ob, gates, mod3, norm2_g, woa, wob, wout, seq):
    t = x2.shape[0]
    tm = min(256, seq)
    tpb = seq // tm
    const = lambda i: (0, 0)
    row = lambda n: pl.BlockSpec((tm, n), lambda i: (i, 0))
    modspec = lambda j: pl.BlockSpec((1, 1, D_MODEL), lambda i: (i // tpb, 0, j))
    wspec = lambda w: pl.BlockSpec(w.shape, const, pipeline_mode=pl.Buffered(1))
    return pl.pallas_call(
        _post_kernel,
        out_shape=(jax.ShapeDtypeStruct((t, D_MODEL), F32), jax.ShapeDtypeStruct((t, D_MODEL), F32),
                   jax.ShapeDtypeStruct((t, D_MODEL // 2), jnp.uint32)),
        grid=(t // tm,),
        in_specs=[row(D_MODEL), row(GDN_V), row(DIFF_V), row(2 * D_MODEL),
                  modspec(2), modspec(3), modspec(4), pl.BlockSpec((1, D_MODEL), const),
                  wspec(woa), wspec(wob), wspec(wout)],
        out_specs=(row(D_MODEL), row(D_MODEL), row(D_MODEL // 2)),
        compiler_params=_cparams(("arbitrary",)),
        name="post",
    )(x2, oa, ob, gates, mod3, mod3, mod3, norm2_g, woa, wob, wout)


def _router_kernel(h_ref, wr_ref, rb_ref, idx_ref, w_ref, cnt_ref, cnt_s):
    step = pl.program_id(0)

    @pl.when(step == 0)
    def _():
        cnt_s[...] = jnp.zeros_like(cnt_s)

    tm = h_ref.shape[0]
    logits = jnp.dot(h_ref[...], wr_ref[...], precision=HIGHEST, preferred_element_type=F32)
    scores = jax.nn.sigmoid(logits)
    choice = scores + rb_ref[...]
    lane = lax.broadcasted_iota(jnp.int32, (tm, N_EXPERTS), 1)
    grp = lane // GROUP_SIZE
    neg = -jnp.inf

    def first_max(v):
        m = jnp.max(v, axis=-1, keepdims=True)
        i = jnp.min(jnp.where(v == m, lane, N_EXPERTS), axis=-1, keepdims=True)
        return m, i

    gl = lax.broadcasted_iota(jnp.int32, (tm, LANES), 1)
    gscore = jnp.full((tm, LANES), neg, F32)
    for g in range(N_GROUPS):
        vg = jnp.where(grp == g, choice, neg)
        m1, i1 = first_max(vg)
        m2 = jnp.max(jnp.where(lane == i1, neg, vg), axis=-1, keepdims=True)
        gscore = jnp.where(gl == g, m1 + m2, gscore)
    emask = jnp.zeros((tm, N_EXPERTS), jnp.bool_)
    for _ in range(TOPK_GROUPS):
        m = jnp.max(gscore, axis=-1, keepdims=True)
        gi = jnp.min(jnp.where(gscore == m, gl, LANES), axis=-1, keepdims=True)
        emask = emask | (grp == gi)
        gscore = jnp.where(gl == gi, neg, gscore)
    masked = jnp.where(emask, choice, neg)
    idxs = jnp.zeros((tm, LANES), jnp.int32)
    ws = jnp.zeros((tm, LANES), F32)
    sel = jnp.zeros((tm, N_EXPERTS), F32)
    for k in range(TOP_K):
        _, i = first_max(masked)
        hit = lane == i
        wk = jnp.sum(jnp.where(hit, scores, 0.0), axis=-1, keepdims=True)
        idxs = jnp.where(gl == k, i, idxs)
        ws = jnp.where(gl == k, wk, ws)
        sel = jnp.where(hit, 1.0, sel)
        masked = jnp.where(hit, neg, masked)
    wsum = jnp.sum(ws, axis=-1, keepdims=True)
    idx_ref[...] = idxs
    w_ref[...] = ws / wsum * ROUTED_SCALE
    cnt_s[...] = cnt_s[...] + jnp.sum(sel, axis=0, keepdims=True)
    cnt_ref[...] = cnt_s[...]


def _router(h2, w_router, rbias):
    t = h2.shape[0]
    tm = min(256, t)
    const = lambda i: (0, 0)
    return pl.pallas_call(
        _router_kernel,
        out_shape=(jax.ShapeDtypeStruct((t, LANES), jnp.int32),
                   jax.ShapeDtypeStruct((t, LANES), F32),
                   jax.ShapeDtypeStruct((1, N_EXPERTS), F32)),
        grid=(t // tm,),
        in_specs=[pl.BlockSpec((tm, D_MODEL), lambda i: (i, 0)),
                  pl.BlockSpec(w_router.shape, const), pl.BlockSpec(rbias.shape, const)],
        out_specs=(pl.BlockSpec((tm, LANES), lambda i: (i, 0)),
                   pl.BlockSpec((tm, LANES), lambda i: (i, 0)),
                   pl.BlockSpec((1, N_EXPERTS), const)),
        scratch_shapes=[pltpu.VMEM((1, N_EXPERTS), F32)],
        compiler_params=_cparams(("arbitrary",)),
        name="router",
    )(h2, w_router, rbias)


def _rank_kernel(idx_ref, ps_ref, dest_ref, run_s):
    step = pl.program_id(0)

    @pl.when(step == 0)
    def _():
        run_s[...] = jnp.zeros_like(run_s)

    tm = idx_ref.shape[0]
    idx = idx_ref[...]
    lane = lax.broadcasted_iota(jnp.int32, (tm, N_EXPERTS), 1)
    gl = lax.broadcasted_iota(jnp.int32, (tm, LANES), 1)
    hits = [lane == idx[:, k:k + 1] for k in range(TOP_K)]
    sel = jnp.zeros((tm, N_EXPERTS), F32)
    for hit in hits:
        sel = jnp.where(hit, 1.0, sel)
    ri = lax.broadcasted_iota(jnp.int32, (tm, tm), 0)
    ci = lax.broadcasted_iota(jnp.int32, (tm, tm), 1)
    before = _dot((ri > ci).astype(BF16), sel.astype(BF16))
    base = before + run_s[...] + ps_ref[...]
    dest = jnp.zeros((tm, LANES), F32)
    for k, hit in enumerate(hits):
        dk = jnp.sum(jnp.where(hit, base, 0.0), axis=-1, keepdims=True)
        dest = jnp.where(gl == k, dk, dest)
    dest_ref[...] = dest.astype(jnp.int32)
    run_s[...] = run_s[...] + jnp.sum(sel, axis=0, keepdims=True)


def _rank(idx, pstart):
    t = idx.shape[0]
    tm = min(256, t)
    const = lambda i: (0, 0)
    return pl.pallas_call(
        _rank_kernel,
        out_shape=jax.ShapeDtypeStruct((t, LANES), jnp.int32),
        grid=(t // tm,),
        in_specs=[pl.BlockSpec((tm, LANES), lambda i: (i, 0)), pl.BlockSpec(pstart.shape, const)],
        out_specs=pl.BlockSpec((tm, LANES), lambda i: (i, 0)),
        scratch_shapes=[pltpu.VMEM((1, N_EXPERTS), F32)],
        compiler_params=_cparams(("arbitrary",)),
        name="rank",
    )(idx, pstart)


def _row_copy(src, dst, sem):
    return pltpu.make_async_copy(src, dst, sem)


def _dispatch_kernel(dest_ref, h_ref, xs_in, xs_ref, sem, *, tm):
    del xs_in

    def start_row(r, c):
        for k in range(TOP_K):
            d = dest_ref[r * TOP_K + k]
            _row_copy(h_ref.at[pl.ds(r, 1), :], xs_ref.at[pl.ds(d, 1), :], sem).start(priority=k % 2)
        return c

    lax.fori_loop(0, tm, start_row, 0)
    for k in range(TOP_K):
        _row_copy(h_ref, xs_ref.at[pl.ds(0, tm), :], sem).wait()


def _dispatch(dest_flat, h2, xs0):
    t = h2.shape[0]
    tm = min(256, t)
    return pl.pallas_call(
        functools.partial(_dispatch_kernel, tm=tm),
        out_shape=jax.ShapeDtypeStruct(xs0.shape, xs0.dtype),
        grid=(t // tm,),
        in_specs=[pl.BlockSpec((tm * TOP_K,), lambda i: (i,), memory_space=pltpu.SMEM),
                  pl.BlockSpec((tm, h2.shape[1]), lambda i: (i, 0)),
                  pl.BlockSpec(memory_space=pl.ANY)],
        out_specs=pl.BlockSpec(memory_space=pl.ANY),
        scratch_shapes=[pltpu.SemaphoreType.DMA(())],
        input_output_aliases={2: 0},
        compiler_params=_cparams(("arbitrary",)),
        name="dispatch",
    )(dest_flat, h2, xs0)


def _experts_kernel(start_ref, count_ref, xs_ref, wgu_ref, wdn_ref, ys_ref,
                    wgu_s, wdn_s, xbuf, ybuf, zbuf, sem_in, sem_out, sem_z, *, nblk_total):
    e = pl.program_id(0)
    first = start_ref[e]
    n = count_ref[e]
    half = D_MODEL // 2

    def rows(i):
        return pl.ds(pl.multiple_of((first + i) * EXPERT_BLOCK, EXPERT_BLOCK), EXPERT_BLOCK)

    def in_copy(i, slot):
        return pltpu.make_async_copy(xs_ref.at[rows(i), :], xbuf.at[slot], sem_in.at[slot])

    def out_copy(i, slot):
        return pltpu.make_async_copy(ybuf.at[slot], ys_ref.at[rows(i), :], sem_out.at[slot])

    @pl.when(n > 0)
    def _():
        in_copy(0, 0).start()

    wgu_s[...] = wgu_ref[0].astype(BF16)
    wdn_s[...] = wdn_ref[0].astype(BF16)

    def block(i, carry):
        slot = i % 2
        in_copy(i, slot).wait()

        @pl.when(i + 1 < n)
        def _():
            in_copy(i + 1, 1 - slot).start()

        @pl.when(i >= 2)
        def _():
            out_copy(i - 2, slot).wait()

        lo, hi = _unpack_halves(xbuf[slot])
        gu = _dot(lo.astype(BF16), wgu_s[0:half, :]) + _dot(hi.astype(BF16), wgu_s[half:, :])
        act = _silu(gu[:, :EXPERT_FF]) * gu[:, EXPERT_FF:]
        ybuf[slot] = _pack_halves(_dot(act.astype(BF16), wdn_s[...]))
        out_copy(i, slot).start()
        return carry

    lax.fori_loop(0, n, block, 0)

    @pl.when(n >= 2)
    def _():
        out_copy(n - 2, n % 2).wait()

    @pl.when(n >= 1)
    def _():
        out_copy(n - 1, (n - 1) % 2).wait()

    @pl.when(e == pl.num_programs(0) - 1)
    def _():
        zbuf[...] = jnp.zeros_like(zbuf)
        used = first + n

        def zcopy(j):
            r = pl.ds(pl.multiple_of(j * EXPERT_BLOCK, EXPERT_BLOCK), EXPERT_BLOCK)
            return pltpu.make_async_copy(zbuf, ys_ref.at[r, :], sem_z)

        def zstart(j, c):
            zcopy(j).start()
            return c

        def zwait(j, c):
            zcopy(j).wait()
            return c

        lax.fori_loop(used, nblk_total, zstart, 0)
        lax.fori_loop(used, nblk_total, zwait, 0)


def _experts(blk_start, blk_count, xs, w_gu, w_dn):
    rows, half = xs.shape
    nblk = rows // EXPERT_BLOCK
    grid_spec = pltpu.PrefetchScalarGridSpec(
        num_scalar_prefetch=2,
        grid=(N_EXPERTS,),
        in_specs=[pl.BlockSpec(memory_space=pl.ANY),
                  pl.BlockSpec((1, D_MODEL, 2 * EXPERT_FF), lambda e, s, c: (e, 0, 0)),
                  pl.BlockSpec((1, EXPERT_FF, D_MODEL), lambda e, s, c: (e, 0, 0))],
        out_specs=pl.BlockSpec(memory_space=pl.ANY),
        scratch_shapes=[pltpu.VMEM((D_MODEL, 2 * EXPERT_FF), BF16), pltpu.VMEM((EXPERT_FF, D_MODEL), BF16),
                        pltpu.VMEM((2, EXPERT_BLOCK, half), jnp.uint32),
                        pltpu.VMEM((2, EXPERT_BLOCK, half), jnp.uint32),
                        pltpu.VMEM((EXPERT_BLOCK, half), jnp.uint32),
                        pltpu.SemaphoreType.DMA((2,)), pltpu.SemaphoreType.DMA((2,)),
                        pltpu.SemaphoreType.DMA(())],
    )
    return pl.pallas_call(
        functools.partial(_experts_kernel, nblk_total=nblk),
        out_shape=jax.ShapeDtypeStruct((rows, half), jnp.uint32),
        grid_spec=grid_spec,
        compiler_params=_cparams(("arbitrary",)),
        name="experts",
    )(blk_start, blk_count, xs, w_gu, w_dn)


def _combine_kernel(dest_ref, x1_ref, h_ref, w_ref, g2_ref, wsgu_ref, wsdn_ref, ys_ref, o_ref, buf, sem, *, tm):
    def start_row(r, c):
        for k in range(TOP_K):
            d = dest_ref[r * TOP_K + k]
            _row_copy(ys_ref.at[pl.ds(d, 1), :], buf.at[k, pl.ds(r, 1), :], sem).start(priority=k % 2)
        return c

    lax.fori_loop(0, tm, start_row, 0)
    su = _dot(h_ref[...].astype(BF16), wsgu_ref[...])
    y = _dot((_silu(su[:, :SHARED_FF]) * su[:, SHARED_FF:]).astype(BF16), wsdn_ref[...])
    for k in range(TOP_K):
        _row_copy(ys_ref.at[pl.ds(0, tm), :], buf.at[k], sem).wait()
    w = w_ref[...]
    half = D_MODEL // 2
    ylo, yhi = y[:, :half], y[:, half:]
    for k in range(TOP_K):
        lo, hi = _unpack_halves(buf[k])
        wk = w[:, k:k + 1]
        ylo = ylo + wk * lo
        yhi = yhi + wk * hi
    g2 = g2_ref[0]
    o_ref[:, :half] = x1_ref[:, :half] + g2[:, :half] * ylo
    o_ref[:, half:] = x1_ref[:, half:] + g2[:, half:] * yhi


def _combine(dest_flat, x1, h2, wts, mod3, wsgu, wsdn, ys, seq):
    t = x1.shape[0]
    tm = min(128, seq)
    tpb = seq // tm
    const = lambda i: (0, 0)
    row = lambda n: pl.BlockSpec((tm, n), lambda i: (i, 0))
    return pl.pallas_call(
        functools.partial(_combine_kernel, tm=tm),
        out_shape=jax.ShapeDtypeStruct((t, D_MODEL), F32),
        grid=(t // tm,),
        in_specs=[pl.BlockSpec((tm * TOP_K,), lambda i: (i,), memory_space=pltpu.SMEM),
                  row(D_MODEL), row(D_MODEL), row(LANES),
                  pl.BlockSpec((1, 1, D_MODEL), lambda i: (i // tpb, 0, 5)),
                  pl.BlockSpec(wsgu.shape, const), pl.BlockSpec(wsdn.shape, const),
                  pl.BlockSpec(memory_space=pl.ANY)],
        out_specs=row(D_MODEL),
        scratch_shapes=[pltpu.VMEM((TOP_K, tm, D_MODEL // 2), jnp.uint32), pltpu.SemaphoreType.DMA(())],
        compiler_params=_cparams(("arbitrary",)),
        name="combine",
    )(dest_flat, x1, h2, wts, mod3, wsgu, wsdn, ys)


def _layer(x, c, layer, w_ada, b_ada, norm1_g, w_in, conv_w, a_log, dt_bias, gdn_norm_g, w_o_gdn,
           q_norm_g, k_norm_g, lambda_q1, lambda_k1, lambda_q2, lambda_k2, subln_g, w_o_diff,
           w_out, norm2_g, w_router, router_bias, w_exp_gate_up, w_exp_down,
           w_shared_gate_up, w_shared_down):
    nb, seq, d = x.shape
    t = nb * seq
    lam_init = 0.8 - 0.6 * math.exp(-0.3 * layer)
    x2 = x.reshape(t, d)

    c_pad = jnp.pad(c, ((0, SUBLANES - nb % SUBLANES if nb % SUBLANES else 0), (0, 0)))
    mod = _ada(c_pad, w_ada, b_ada.reshape(1, -1))[:nb]
    mod3 = mod.reshape(nb, 1, 6 * d)

    o_bd = 2 * GDN_QK + 2 * GDN_V
    o_att = o_bd + 2 * GDN_HEADS
    o_gate = o_att + 2 * DIFF_QK + DIFF_V
    w_gdn = w_in[:, :o_bd].astype(BF16)
    w_bd = jnp.pad(w_in[:, o_bd:o_att], ((0, 0), (0, LANES - 2 * GDN_HEADS))).astype(BF16)
    w_att = w_in[:, o_att:o_gate].astype(BF16)
    w_gate = w_in[:, o_gate:].astype(BF16)
    gdn_in, bd, att_in, gates = _inproj(x2, mod3, norm1_g.reshape(1, d), w_gdn, w_bd, w_att, w_gate, seq)

    aparams = jnp.zeros((SUBLANES, LANES), F32)
    aparams = aparams.at[0, GDN_HEADS:2 * GDN_HEADS].set(a_log).at[1, GDN_HEADS:2 * GDN_HEADS].set(dt_bias)
    oa = _gdn(gdn_in.reshape(nb, seq, -1), bd.reshape(nb, seq, LANES), conv_w, aparams,
              gdn_norm_g.reshape(1, GDN_DV)).reshape(t, GDN_V)

    tq = min(512, seq)
    slopes = [2.0 ** (-8.0 * (h + 1) / DIFF_HEADS) for h in range(DIFF_HEADS)]
    c3 = _bf16_split3(LOG2E)
    qfeat = np.zeros((1, 2 * DIFF_HEADS * DIFF_DH), np.float32)
    for g in range(2 * DIFF_HEADS):
        for j in range(6):
            qfeat[0, g * DIFF_DH + j] = slopes[g // 2] * c3[j // 2]
    grp = np.arange(DIFF_QK) // DIFF_DH
    bdm = jnp.asarray((grp[:, None] == grp[None, :]).astype(np.float32))
    qa, ka, vt = _aprep(att_in, jnp.tile(q_norm_g, 2 * DIFF_HEADS).reshape(1, -1),
                        jnp.tile(k_norm_g, 2 * DIFF_HEADS).reshape(1, -1), jnp.asarray(qfeat), bdm, nb, seq, tq)
    slopes2 = jnp.asarray([s * (c3[0] + c3[1] + c3[2]) for s in slopes], F32)
    lamv = jnp.zeros((SUBLANES, DIFF_DH), F32)
    lamv = lamv.at[0].set(lambda_q1).at[1].set(lambda_k1).at[2].set(lambda_q2).at[3].set(lambda_k2)
    ob = _attn(slopes2, qa, ka, vt, lamv, subln_g.reshape(-1, 1), nb, seq, tq, lam_init)

    x1, h2, hp = _post(x2, oa, ob, gates, mod3, norm2_g.reshape(1, d), w_o_gdn.astype(BF16),
                       w_o_diff.astype(BF16), w_out.astype(BF16), seq)

    idx, wts, counts = _router(h2, w_router, router_bias.reshape(1, -1))
    cnt = counts[0].astype(jnp.int32)
    padded = (cnt + EXPERT_BLOCK - 1) // EXPERT_BLOCK * EXPERT_BLOCK
    pends = jnp.cumsum(padded)
    pstart = (pends - padded).astype(F32).reshape(1, -1)
    nblk = -(-(t * TOP_K) // EXPERT_BLOCK) + N_EXPERTS
    blk_start = ((pends - padded) // EXPERT_BLOCK).astype(jnp.int32)
    blk_count = (padded // EXPERT_BLOCK).astype(jnp.int32)
    dest = _rank(idx, pstart)
    dest_flat = dest[:, :TOP_K].reshape(-1)
    xs = _dispatch(dest_flat, hp, jnp.zeros((nblk * EXPERT_BLOCK, d // 2), jnp.uint32))
    ys = _experts(blk_start, blk_count, xs, w_exp_gate_up, w_exp_down)
    out = _combine(dest_flat, x1, h2, wts, mod3, w_shared_gate_up.astype(BF16),
                   w_shared_down.astype(BF16), ys, seq)
    return out.reshape(nb, seq, d)


def kernel(x, c, w_ada, b_ada, norm1_g, w_in, conv_w, a_log, dt_bias, gdn_norm_g, w_o_gdn, q_norm_g, k_norm_g, lambda_q1, lambda_k1, lambda_q2, lambda_k2, subln_g, w_o_diff, w_out, norm2_g, w_router, router_bias, w_exp_gate_up, w_exp_down, w_shared_gate_up, w_shared_down):
    params = (w_ada, b_ada, norm1_g, w_in, conv_w, a_log, dt_bias, gdn_norm_g, w_o_gdn, q_norm_g,
              k_norm_g, lambda_q1, lambda_k1, lambda_q2, lambda_k2, subln_g, w_o_diff, w_out, norm2_g,
              w_router, router_bias, w_exp_gate_up, w_exp_down, w_shared_gate_up, w_shared_down)
    for layer in range(w_ada.shape[0]):
        x = _layer(x, c, layer, *(p[layer] for p in params))
    return x
```

```python
import functools
import math

import jax
import jax.numpy as jnp
import numpy as np
from jax import lax
from jax.experimental import pallas as pl
from jax.experimental.pallas import tpu as pltpu

F32 = jnp.float32
BF16 = jnp.bfloat16
HIGHEST = lax.Precision.HIGHEST

D_MODEL = 1024
CHUNK = 64
GDN_HEADS = 4
GDN_DK = 128
GDN_DV = 128
GDN_CONV = 4
DIFF_HEADS = 4
DIFF_DH = 64
N_EXPERTS = 256
TOP_K = 8
N_GROUPS = 8
TOPK_GROUPS = 4
EXPERT_FF = 256
SHARED_FF = 256
ROUTED_SCALE = 2.5
NORM_EPS = 1e-6
GROUP_SIZE = N_EXPERTS // N_GROUPS

GDN_QK = GDN_HEADS * GDN_DK
GDN_V = GDN_HEADS * GDN_DV
CONV_CH = 2 * GDN_QK + GDN_V
DIFF_QK = DIFF_HEADS * 2 * DIFF_DH
DIFF_V = DIFF_HEADS * 2 * DIFF_DH

LANES = 128
SUBLANES = 8
EXPERT_BLOCK = 128
INV_BASE = 8
LOG2E = math.log2(math.e)
VMEM_LIMIT = 56 * 1024 * 1024


def _cparams(sem):
    return pltpu.CompilerParams(dimension_semantics=sem, vmem_limit_bytes=VMEM_LIMIT)


def _dot(a, b):
    return jnp.dot(a, b, preferred_element_type=F32)


def _dot_nt(a, b):
    return lax.dot_general(a, b, (((1,), (1,)), ((), ())), preferred_element_type=F32)


def _dot_tn(a, b):
    return lax.dot_general(a, b, (((0,), (0,)), ((), ())), preferred_element_type=F32)


def _silu(x):
    return x * jax.nn.sigmoid(x)


def _bf16_split3(x):
    rnd = lambda v: float(np.float32(v).astype(BF16).astype(np.float32))
    a = rnd(x)
    b = rnd(x - a)
    c = rnd(x - a - b)
    return a, b, c


def _ada_kernel(c_ref, w_ref, b_ref, o_ref):
    s = _silu(c_ref[...])
    o_ref[...] = jnp.dot(s, w_ref[...], precision=HIGHEST, preferred_element_type=F32) + b_ref[...]


def _ada(c_pad, w_ada, b_ada):
    n = w_ada.shape[1]
    tn = 1024
    return pl.pallas_call(
        _ada_kernel,
        out_shape=jax.ShapeDtypeStruct((c_pad.shape[0], n), F32),
        grid=(n // tn,),
        in_specs=[pl.BlockSpec(c_pad.shape, lambda j: (0, 0)),
                  pl.BlockSpec((D_MODEL, tn), lambda j: (0, j)),
                  pl.BlockSpec((1, tn), lambda j: (0, j))],
        out_specs=pl.BlockSpec((c_pad.shape[0], tn), lambda j: (0, j)),
        compiler_params=_cparams(("arbitrary",)),
        name="ada",
    )(c_pad, w_ada, b_ada)


def _inproj_kernel(x_ref, shift_ref, scale_ref, g_ref, wg_ref, wbd_ref, wa_ref, wgt_ref,
                   gdn_ref, bd_ref, att_ref, gate_ref):
    x = x_ref[...]
    y = x * lax.rsqrt(jnp.mean(x * x, axis=-1, keepdims=True) + NORM_EPS) * g_ref[...]
    h = (y * (1.0 + scale_ref[0]) + shift_ref[0]).astype(BF16)
    gdn_ref[...] = _dot(h, wg_ref[...])
    bd_ref[...] = _dot(h, wbd_ref[...])
    att_ref[...] = _dot(h, wa_ref[...])
    gate_ref[...] = _dot(h, wgt_ref[...])


def _inproj(x2, mod3, norm1_g, w_gdn, w_bd, w_att, w_gate, seq):
    t = x2.shape[0]
    tm = min(256, seq)
    tiles_per_b = seq // tm
    const = lambda i: (0, 0)
    wspec = lambda w: pl.BlockSpec(w.shape, const, pipeline_mode=pl.Buffered(1))
    row = lambda n: pl.BlockSpec((tm, n), lambda i: (i, 0))
    return pl.pallas_call(
        _inproj_kernel,
        out_shape=(jax.ShapeDtypeStruct((t, w_gdn.shape[1]), F32),
                   jax.ShapeDtypeStruct((t, LANES), F32),
                   jax.ShapeDtypeStruct((t, w_att.shape[1]), F32),
                   jax.ShapeDtypeStruct((t, w_gate.shape[1]), F32)),
        grid=(t // tm,),
        in_specs=[row(D_MODEL),
                  pl.BlockSpec((1, 1, D_MODEL), lambda i: (i // tiles_per_b, 0, 0)),
                  pl.BlockSpec((1, 1, D_MODEL), lambda i: (i // tiles_per_b, 0, 1)),
                  pl.BlockSpec((1, D_MODEL), const),
                  wspec(w_gdn), wspec(w_bd), wspec(w_att), wspec(w_gate)],
        out_specs=(row(w_gdn.shape[1]), row(LANES), row(w_att.shape[1]), row(w_gate.shape[1])),
        compiler_params=_cparams(("arbitrary",)),
        name="inproj",
    )(x2, mod3, mod3, norm1_g, w_gdn, w_bd, w_att, w_gate)


def _gdn_kernel(x_ref, bd_ref, cw_ref, ap_ref, ng_ref, o_ref, cbuf, state, *, nb, lg):
    step = pl.program_id(0)
    nc = lg // CHUNK

    @pl.when(step == 0)
    def _():
        cbuf[...] = jnp.zeros_like(cbuf)
        state[...] = jnp.zeros_like(state)

    ri = lax.broadcasted_iota(jnp.int32, (CHUNK, CHUNK), 0)
    ci = lax.broadcasted_iota(jnp.int32, (CHUNK, CHUNK), 1)
    causal = ri >= ci
    strict = ri > ci
    eye = (ri == ci).astype(F32)
    rl = lax.broadcasted_iota(jnp.int32, (lg, lg), 0)
    cl = lax.broadcasted_iota(jnp.int32, (lg, lg), 1)
    blocktri = ((rl >= cl) & ((rl // CHUNK) == (cl // CHUNK))).astype(F32)
    cw = cw_ref[...]
    a_row = ap_ref[0:1, :]
    dt_row = ap_ref[1:2, :]
    ng = ng_ref[...]

    units = []
    for b in range(nb):
        cbuf[b, SUBLANES:SUBLANES + lg, :] = x_ref[b, :, 0:CONV_CH]
        acc = cw[GDN_CONV - 1:GDN_CONV, :] * cbuf[b, SUBLANES:SUBLANES + lg, :]
        for j in range(GDN_CONV - 1):
            off = SUBLANES - (GDN_CONV - 1) + j
            acc = acc + cw[j:j + 1, :] * cbuf[b, off:off + lg, :]
        cbuf[b, 0:SUBLANES, :] = cbuf[b, lg:lg + SUBLANES, :]
        qkv = _silu(acc)

        bd = bd_ref[b]
        beta_t = jax.nn.sigmoid(bd)
        g_t = -jnp.exp(a_row) * jax.nn.softplus(bd + dt_row)
        gcum = jnp.dot(blocktri, g_t, precision=HIGHEST, preferred_element_type=F32)
        gcum_t = gcum.T
        egcum = jnp.exp(gcum)

        for h in range(GDN_HEADS):
            qh = qkv[:, h * GDN_DK:(h + 1) * GDN_DK]
            kh = qkv[:, GDN_QK + h * GDN_DK:GDN_QK + (h + 1) * GDN_DK]
            vh = qkv[:, 2 * GDN_QK + h * GDN_DV:2 * GDN_QK + (h + 1) * GDN_DV]
            qh = qh * lax.rsqrt(jnp.sum(qh * qh, axis=-1, keepdims=True) + NORM_EPS) * (GDN_DK ** -0.5)
            kh = kh * lax.rsqrt(jnp.sum(kh * kh, axis=-1, keepdims=True) + NORM_EPS)
            for c in range(nc):
                r0, r1 = c * CHUNK, (c + 1) * CHUNK
                q, k, v = qh[r0:r1], kh[r0:r1], vh[r0:r1]
                gc = gcum[r0:r1, 4 + h:5 + h]
                gr = gcum_t[4 + h:5 + h, r0:r1]
                egc = egcum[r0:r1, 4 + h:5 + h]
                bcol = beta_t[r0:r1, h:h + 1]
                gl = gcum[r1 - 1:r1, 4 + h:5 + h]
                kb = k.astype(BF16)
                units.append(dict(
                    b=b, h=h, c=c, kb=kb, bcol=bcol,
                    qkb=jnp.concatenate([q.astype(BF16), kb], axis=0),
                    decay=jnp.exp(jnp.where(causal, gc - gr, -jnp.inf)),
                    rhs=jnp.concatenate([v * bcol, k * (bcol * egc)], axis=-1).astype(BF16),
                    qg=(q * egc).astype(BF16), egl=jnp.exp(gl),
                    kdec=(k * jnp.exp(gl - gc)).astype(BF16)))

    for u in units:
        u["kq"] = _dot_nt(u["qkb"], u["kb"])
    def same_block(size):
        return (ri // size) == (ci // size)
    for u in units:
        lower = jnp.where(strict, u["bcol"] * u["kq"][CHUNK:] * u["decay"], 0.0)
        u["lower"] = lower
        diag = jnp.where(same_block(INV_BASE), lower, 0.0)
        u["xinv"] = eye - diag
        u["pw"] = diag.astype(BF16)
    for r in range(INV_BASE.bit_length() - 2):
        for u in units:
            u["pw"] = _dot(u["pw"], u["pw"]).astype(BF16)
        for u in units:
            u["xinv"] = u["xinv"] + _dot(u["xinv"].astype(BF16), u["pw"])
    size = INV_BASE
    while size < CHUNK:
        pair_off = same_block(2 * size) & jnp.logical_not(same_block(size))
        for u in units:
            u["xb"] = u["xinv"].astype(BF16)
            u["cx"] = _dot(jnp.where(pair_off, u["lower"], 0.0).astype(BF16), u["xb"]).astype(BF16)
        for u in units:
            u["xinv"] = u["xinv"] - _dot(u["xb"], u["cx"])
        size *= 2
    for u in units:
        u["sol"] = _dot(u["xinv"].astype(BF16), u["rhs"])
        u["a_qk"] = (u["kq"][:CHUNK] * u["decay"]).astype(BF16)

    st = {(b, h): state[b, h] for b in range(nb) for h in range(GDN_HEADS)}
    for c in range(nc):
        cu = [u for u in units if u["c"] == c]
        for u in cu:
            stb = st[(u["b"], u["h"])].astype(BF16)
            u["ws"] = _dot(jnp.concatenate([u["sol"][:, GDN_DV:].astype(BF16), u["qg"]], axis=0), stb)
        for u in cu:
            u["vnb"] = (u["sol"][:, :GDN_DV] - u["ws"][:CHUNK]).astype(BF16)
        for u in cu:
            u["out"] = u["ws"][CHUNK:] + _dot(u["a_qk"], u["vnb"])
        for u in cu:
            key = (u["b"], u["h"])
            st[key] = st[key] * u["egl"] + _dot_tn(u["kdec"], u["vnb"])
        for u in cu:
            b, h = u["b"], u["h"]
            r0, r1 = c * CHUNK, (c + 1) * CHUNK
            out = u["out"]
            on = out * lax.rsqrt(jnp.mean(out * out, axis=-1, keepdims=True) + NORM_EPS) * ng
            zh = x_ref[b, r0:r1, CONV_CH + h * GDN_DV:CONV_CH + (h + 1) * GDN_DV]
            o_ref[b, r0:r1, h * GDN_DV:(h + 1) * GDN_DV] = on * _silu(zh)
    for (b, h), s in st.items():
        state[b, h] = s


def _gdn(gdn_in3, bd3, conv_w, aparams, gdn_norm_g):
    nb, seq, _ = gdn_in3.shape
    lg = min(128, seq)
    const = lambda i: (0, 0)
    return pl.pallas_call(
        functools.partial(_gdn_kernel, nb=nb, lg=lg),
        out_shape=jax.ShapeDtypeStruct((nb, seq, GDN_V), F32),
        grid=(seq // lg,),
        in_specs=[pl.BlockSpec((nb, lg, gdn_in3.shape[2]), lambda i: (0, i, 0)),
                  pl.BlockSpec((nb, lg, LANES), lambda i: (0, i, 0)),
                  pl.BlockSpec(conv_w.shape, const),
                  pl.BlockSpec(aparams.shape, const),
                  pl.BlockSpec(gdn_norm_g.shape, const)],
        out_specs=pl.BlockSpec((nb, lg, GDN_V), lambda i: (0, i, 0)),
        scratch_shapes=[pltpu.VMEM((nb, lg + SUBLANES, CONV_CH), F32),
                        pltpu.VMEM((nb, GDN_HEADS, GDN_DK, GDN_DV), F32)],
        compiler_params=_cparams(("arbitrary",)),
        name="gdn",
    )(gdn_in3, bd3, conv_w, aparams, gdn_norm_g)


def _aprep_kernel(x_ref, gq_ref, gk_ref, qf_ref, bdm_ref, q_ref, k_ref, vt_ref, *, tk):
    x = x_ref[...]
    tm = x.shape[0]
    bdm = bdm_ref[...]

    def qknorm(v, gain):
        ms = jnp.dot(v * v, bdm, precision=HIGHEST, preferred_element_type=F32) * (1.0 / DIFF_DH)
        return v * lax.rsqrt(ms + NORM_EPS) * gain

    qn = qknorm(x[:, 0:DIFF_QK], gq_ref[...]) * (DIFF_DH ** -0.5 * LOG2E)
    kn = qknorm(x[:, DIFF_QK:2 * DIFF_QK], gk_ref[...])
    pos = pl.program_id(0) * tm + lax.broadcasted_iota(jnp.int32, (tm, DIFF_DH), 0)
    krel = pos % tk
    lane = lax.broadcasted_iota(jnp.int32, (tm, DIFF_DH), 1)
    hi = ((krel // 256) * 256).astype(F32)
    lo = (krel % 256).astype(F32)
    kfeat = jnp.where(lane < 6, jnp.where(lane % 2 == 0, hi, lo), 0.0)
    qfeat = qf_ref[...]
    qparts, kparts = [], []
    for g in range(2 * DIFF_HEADS):
        qparts += [qn[:, g * DIFF_DH:(g + 1) * DIFF_DH],
                   jnp.broadcast_to(qfeat[:, g * DIFF_DH:(g + 1) * DIFF_DH], (tm, DIFF_DH))]
        kparts += [kn[:, g * DIFF_DH:(g + 1) * DIFF_DH], kfeat]
    q_ref[...] = jnp.concatenate(qparts, axis=-1).astype(BF16)
    k_ref[...] = jnp.concatenate(kparts, axis=-1).astype(BF16)
    vt_ref[0, 0] = x[:, 2 * DIFF_QK:].T.astype(BF16)


def _aprep(att_in, gq, gk, qfeat, bdm, nb, seq, tk):
    t = att_in.shape[0]
    tm = tk
    nk = seq // tk
    const = lambda i: (0, 0)
    return pl.pallas_call(
        functools.partial(_aprep_kernel, tk=tk),
        out_shape=(jax.ShapeDtypeStruct((t, 2 * DIFF_QK), BF16),
                   jax.ShapeDtypeStruct((t, 2 * DIFF_QK), BF16),
                   jax.ShapeDtypeStruct((nb, nk, DIFF_V, tk), BF16)),
        grid=(t // tm,),
        in_specs=[pl.BlockSpec((tm, att_in.shape[1]), lambda i: (i, 0)),
                  pl.BlockSpec(gq.shape, const), pl.BlockSpec(gk.shape, const),
                  pl.BlockSpec(qfeat.shape, const), pl.BlockSpec(bdm.shape, const)],
        out_specs=(pl.BlockSpec((tm, 2 * DIFF_QK), lambda i: (i, 0)),
                   pl.BlockSpec((tm, 2 * DIFF_QK), lambda i: (i, 0)),
                   pl.BlockSpec((1, 1, DIFF_V, tk), lambda i: (i // nk, i % nk, 0, 0))),
        compiler_params=_cparams(("arbitrary",)),
        name="aprep",
    )(att_in, gq, gk, qfeat, bdm)


def _attn_kernel(slope_ref, q_ref, k_ref, vt_ref, lamv_ref, sg_ref, o_ref, m_s, l_s, acc_s,
                 *, tq, lam_init):
    h = pl.program_id(1)
    qi = pl.program_id(2)
    slope2 = slope_ref[h]
    dv = 2 * DIFF_DH
    qs = [q_ref[:, 0:LANES], q_ref[:, LANES:2 * LANES]]

    m_s[...] = jnp.full_like(m_s, -jnp.inf)
    l_s[...] = jnp.zeros_like(l_s)
    acc_s[...] = jnp.zeros_like(acc_s)

    def scores(kj):
        k0 = pl.multiple_of(kj * tq, tq)
        return tuple(_dot_nt(k_ref[pl.ds(k0, tq), c * LANES:(c + 1) * LANES], qs[c]) for c in range(2))

    def accumulate(ss, kj):
        vt = vt_ref[0, kj]
        off = slope2 * ((kj - qi) * tq).astype(F32)
        for c in range(2):
            s = ss[c]
            m_old = m_s[c]
            m_new = jnp.maximum(m_old, jnp.max(s, axis=0, keepdims=True) + off)
            alpha = jnp.exp2(m_old - m_new)
            p = jnp.exp2(s - (m_new - off))
            l_s[c] = alpha * l_s[c] + jnp.sum(p, axis=0, keepdims=True)
            acc_s[c] = alpha * acc_s[c] + _dot(vt, p.astype(BF16))
            m_s[c] = m_new

    kpos = lax.broadcasted_iota(jnp.int32, (tq, tq), 0)
    qpos = lax.broadcasted_iota(jnp.int32, (tq, tq), 1)
    allowed = (kpos // CHUNK) <= (qpos // CHUNK)
    fut = jnp.maximum(kpos - qpos, 0).astype(F32)
    add = jnp.where(allowed, (-2.0 * slope2) * fut, -jnp.inf)
    accumulate(tuple(s + add for s in scores(qi)), qi)

    def body(kj, carry):
        accumulate(scores(kj), kj)
        return carry
    lax.fori_loop(0, qi, body, 0)

    lq1, lk1, lq2, lk2 = (lamv_ref[i:i + 1, :] for i in range(4))
    lam = (jnp.exp(jnp.sum(lq1 * lk1, axis=-1, keepdims=True))
           - jnp.exp(jnp.sum(lq2 * lk2, axis=-1, keepdims=True)) + lam_init)
    o = acc_s[0] / l_s[0] - lam * (acc_s[1] / l_s[1])
    on = o * lax.rsqrt(jnp.mean(o * o, axis=0, keepdims=True) + NORM_EPS) * sg_ref[...] * (1.0 - lam_init)
    o_ref[...] = on.T


def _attn(slopes2, qa, ka, vt, lamv, sg_col, nb, seq, tq, lam_init):
    t = qa.shape[0]
    nq = seq // tq
    dv = 2 * DIFF_DH
    grid_spec = pltpu.PrefetchScalarGridSpec(
        num_scalar_prefetch=1,
        grid=(nb, DIFF_HEADS, nq),
        in_specs=[pl.BlockSpec((tq, 2 * LANES), lambda b, h, i, s: (b * nq + i, h)),
                  pl.BlockSpec((seq, 2 * LANES), lambda b, h, i, s: (b, h)),
                  pl.BlockSpec((1, nq, dv, tq), lambda b, h, i, s: (b, 0, h, 0)),
                  pl.BlockSpec(lamv.shape, lambda b, h, i, s: (0, 0)),
                  pl.BlockSpec(sg_col.shape, lambda b, h, i, s: (0, 0))],
        out_specs=pl.BlockSpec((tq, dv), lambda b, h, i, s: (b * nq + i, h)),
        scratch_shapes=[pltpu.VMEM((2, 1, tq), F32), pltpu.VMEM((2, 1, tq), F32),
                        pltpu.VMEM((2, dv, tq), F32)],
    )
    return pl.pallas_call(
        functools.partial(_attn_kernel, tq=tq, lam_init=lam_init),
        out_shape=jax.ShapeDtypeStruct((t, DIFF_V), F32),
        grid_spec=grid_spec,
        compiler_params=_cparams(("arbitrary", "arbitrary", "arbitrary")),
        name="attn",
    )(slopes2, qa, ka, vt, lamv, sg_col)


def _pack_halves(x):
    n = x.shape[1] // 2
    lo = pltpu.bitcast(x[:, :n].astype(BF16).astype(F32), jnp.uint32)
    hi = pltpu.bitcast(x[:, n:].astype(BF16).astype(F32), jnp.uint32)
    return (lo >> 16) | (hi & jnp.uint32(0xFFFF0000))


def _unpack_halves(p):
    lo = pltpu.bitcast(p << 16, F32)
    hi = pltpu.bitcast(p & jnp.uint32(0xFFFF0000), F32)
    return lo, hi


def _post_kernel(x_ref, oa_ref, ob_ref, gate_ref, g1_ref, sh2_ref, sc2_ref, n2_ref,
                 woa_ref, wob_ref, wout_ref, x1_ref, h2_ref, hp_ref):
    ya = _dot(oa_ref[...].astype(BF16), woa_ref[...])
    yb = _dot(ob_ref[...].astype(BF16), wob_ref[...])
    merged = jax.nn.sigmoid(gate_ref[:, 0:D_MODEL]) * ya + jax.nn.sigmoid(gate_ref[:, D_MODEL:]) * yb
    y = _dot(merged.astype(BF16), wout_ref[...])
    x1 = x_ref[...] + g1_ref[0] * y
    x1_ref[...] = x1
    n = x1 * lax.rsqrt(jnp.mean(x1 * x1, axis=-1, keepdims=True) + NORM_EPS) * n2_ref[...]
    h2 = n * (1.0 + sc2_ref[0]) + sh2_ref[0]
    h2_ref[...] = h2
    hp_ref[...] = _pack_halves(h2)


def _post(x2, oa, ob, gates, mod3, norm2_g, woa, wob, wout, seq):
    t = x2.shape[0]
    tm = min(256, seq)
    tpb = seq // tm
    const = lambda i: (0, 0)
    row = lambda n: pl.BlockSpec((tm, n), lambda i: (i, 0))
    modspec = lambda j: pl.BlockSpec((1, 1, D_MODEL), lambda i: (i // tpb, 0, j))
    wspec = lambda w: pl.BlockSpec(w.shape, const, pipeline_mode=pl.Buffered(1))
    return pl.pallas_call(
        _post_kernel,
        out_shape=(jax.ShapeDtypeStruct((t, D_MODEL), F32), jax.ShapeDtypeStruct((t, D_MODEL), F32),
                   jax.ShapeDtypeStruct((t, D_MODEL // 2), jnp.uint32)),
        grid=(t // tm,),
        in_specs=[row(D_MODEL), row(GDN_V), row(DIFF_V), row(2 * D_MODEL),
                  modspec(2), modspec(3), modspec(4), pl.BlockSpec((1, D_MODEL), const),
                  wspec(woa), wspec(wob), wspec(wout)],
        out_specs=(row(D_MODEL), row(D_MODEL), row(D_MODEL // 2)),
        compiler_params=_cparams(("arbitrary",)),
        name="post",
    )(x2, oa, ob, gates, mod3, mod3, mod3, norm2_g, woa, wob, wout)


def _router_kernel(h_ref, wr_ref, rb_ref, idx_ref, w_ref, cnt_ref, cnt_s):
    step = pl.program_id(0)

    @pl.when(step == 0)
    def _():
        cnt_s[...] = jnp.zeros_like(cnt_s)

    tm = h_ref.shape[0]
    logits = jnp.dot(h_ref[...], wr_ref[...], precision=HIGHEST, preferred_element_type=F32)
    scores = jax.nn.sigmoid(logits)
    choice = scores + rb_ref[...]
    lane = lax.broadcasted_iota(jnp.int32, (tm, N_EXPERTS), 1)
    grp = lane // GROUP_SIZE
    neg = -jnp.inf

    def first_max(v):
        m = jnp.max(v, axis=-1, keepdims=True)
        i = jnp.min(jnp.where(v == m, lane, N_EXPERTS), axis=-1, keepdims=True)
        return m, i

    gl = lax.broadcasted_iota(jnp.int32, (tm, LANES), 1)
    gscore = jnp.full((tm, LANES), neg, F32)
    for g in range(N_GROUPS):
        vg = jnp.where(grp == g, choice, neg)
        m1, i1 = first_max(vg)
        m2 = jnp.max(jnp.where(lane == i1, neg, vg), axis=-1, keepdims=True)
        gscore = jnp.where(gl == g, m1 + m2, gscore)
    emask = jnp.zeros((tm, N_EXPERTS), jnp.bool_)
    for _ in range(TOPK_GROUPS):
        m = jnp.max(gscore, axis=-1, keepdims=True)
        gi = jnp.min(jnp.where(gscore == m, gl, LANES), axis=-1, keepdims=True)
        emask = emask | (grp == gi)
        gscore = jnp.where(gl == gi, neg, gscore)
    masked = jnp.where(emask, choice, neg)
    idxs = jnp.zeros((tm, LANES), jnp.int32)
    ws = jnp.zeros((tm, LANES), F32)
    sel = jnp.zeros((tm, N_EXPERTS), F32)
    for k in range(TOP_K):
        _, i = first_max(masked)
        hit = lane == i
        wk = jnp.sum(jnp.where(hit, scores, 0.0), axis=-1, keepdims=True)
        idxs = jnp.where(gl == k, i, idxs)
        ws = jnp.where(gl == k, wk, ws)
        sel = jnp.where(hit, 1.0, sel)
        masked = jnp.where(hit, neg, masked)
    wsum = jnp.sum(ws, axis=-1, keepdims=True)
    idx_ref[...] = idxs
    w_ref[...] = ws / wsum * ROUTED_SCALE
    cnt_s[...] = cnt_s[...] + jnp.sum(sel, axis=0, keepdims=True)
    cnt_ref[...] = cnt_s[...]


def _router(h2, w_router, rbias):
    t = h2.shape[0]
    tm = min(256, t)
    const = lambda i: (0, 0)
    return pl.pallas_call(
        _router_kernel,
        out_shape=(jax.ShapeDtypeStruct((t, LANES), jnp.int32),
                   jax.ShapeDtypeStruct((t, LANES), F32),
                   jax.ShapeDtypeStruct((1, N_EXPERTS), F32)),
        grid=(t // tm,),
        in_specs=[pl.BlockSpec((tm, D_MODEL), lambda i: (i, 0)),
                  pl.BlockSpec(w_router.shape, const), pl.BlockSpec(rbias.shape, const)],
        out_specs=(pl.BlockSpec((tm, LANES), lambda i: (i, 0)),
                   pl.BlockSpec((tm, LANES), lambda i: (i, 0)),
                   pl.BlockSpec((1, N_EXPERTS), const)),
        scratch_shapes=[pltpu.VMEM((1, N_EXPERTS), F32)],
        compiler_params=_cparams(("arbitrary",)),
        name="router",
    )(h2, w_router, rbias)


def _rank_kernel(idx_ref, ps_ref, dest_ref, run_s):
    step = pl.program_id(0)

    @pl.when(step == 0)
    def _():
        run_s[...] = jnp.zeros_like(run_s)

    tm = idx_ref.shape[0]
    idx = idx_ref[...]
    lane = lax.broadcasted_iota(jnp.int32, (tm, N_EXPERTS), 1)
    gl = lax.broadcasted_iota(jnp.int32, (tm, LANES), 1)
    hits = [lane == idx[:, k:k + 1] for k in range(TOP_K)]
    sel = jnp.zeros((tm, N_EXPERTS), F32)
    for hit in hits:
        sel = jnp.where(hit, 1.0, sel)
    ri = lax.broadcasted_iota(jnp.int32, (tm, tm), 0)
    ci = lax.broadcasted_iota(jnp.int32, (tm, tm), 1)
    before = _dot((ri > ci).astype(BF16), sel.astype(BF16))
    base = before + run_s[...] + ps_ref[...]
    dest = jnp.zeros((tm, LANES), F32)
    for k, hit in enumerate(hits):
        dk = jnp.sum(jnp.where(hit, base, 0.0), axis=-1, keepdims=True)
        dest = jnp.where(gl == k, dk, dest)
    dest_ref[...] = dest.astype(jnp.int32)
    run_s[...] = run_s[...] + jnp.sum(sel, axis=0, keepdims=True)


def _rank(idx, pstart):
    t = idx.shape[0]
    tm = min(256, t)
    const = lambda i: (0, 0)
    return pl.pallas_call(
        _rank_kernel,
        out_shape=jax.ShapeDtypeStruct((t, LANES), jnp.int32),
        grid=(t // tm,),
        in_specs=[pl.BlockSpec((tm, LANES), lambda i: (i, 0)), pl.BlockSpec(pstart.shape, const)],
        out_specs=pl.BlockSpec((tm, LANES), lambda i: (i, 0)),
        scratch_shapes=[pltpu.VMEM((1, N_EXPERTS), F32)],
        compiler_params=_cparams(("arbitrary",)),
        name="rank",
    )(idx, pstart)


def _row_copy(src, dst, sem):
    return pltpu.make_async_copy(src, dst, sem)


def _dispatch_kernel(dest_ref, h_ref, xs_in, xs_ref, sem, *, tm):
    del xs_in

    def start_row(r, c):
        for k in range(TOP_K):
            d = dest_ref[r * TOP_K + k]
            _row_copy(h_ref.at[pl.ds(r, 1), :], xs_ref.at[pl.ds(d, 1), :], sem).start(priority=k % 2)
        return c

    lax.fori_loop(0, tm, start_row, 0)
    for k in range(TOP_K):
        _row_copy(h_ref, xs_ref.at[pl.ds(0, tm), :], sem).wait()


def _dispatch(dest_flat, h2, xs0):
    t = h2.shape[0]
    tm = min(256, t)
    return pl.pallas_call(
        functools.partial(_dispatch_kernel, tm=tm),
        out_shape=jax.ShapeDtypeStruct(xs0.shape, xs0.dtype),
        grid=(t // tm,),
        in_specs=[pl.BlockSpec((tm * TOP_K,), lambda i: (i,), memory_space=pltpu.SMEM),
                  pl.BlockSpec((tm, h2.shape[1]), lambda i: (i, 0)),
                  pl.BlockSpec(memory_space=pl.ANY)],
        out_specs=pl.BlockSpec(memory_space=pl.ANY),
        scratch_shapes=[pltpu.SemaphoreType.DMA(())],
        input_output_aliases={2: 0},
        compiler_params=_cparams(("arbitrary",)),
        name="dispatch",
    )(dest_flat, h2, xs0)


EXPERT_RING = 4


def _experts_kernel(start_ref, count_ref, xs_ref, wgu_ref, wdn_ref, ys_ref,
                    wgu_f, wdn_f, wgu_s, wdn_s, xbuf, ybuf, sem_w, sem_in, sem_out, *, nblk_total):
    half = D_MODEL // 2
    last_e = N_EXPERTS - 1
    used = start_ref[last_e] + count_ref[last_e]

    def rows(g):
        return pl.ds(pl.multiple_of(g * EXPERT_BLOCK, EXPERT_BLOCK), EXPERT_BLOCK)

    def in_copy(g):
        slot = g % EXPERT_RING
        return pltpu.make_async_copy(xs_ref.at[rows(jnp.minimum(g, nblk_total - 1)), :], xbuf.at[slot],
                                     sem_in.at[slot])

    def out_copy(g):
        slot = g % EXPERT_RING
        return pltpu.make_async_copy(ybuf.at[slot], ys_ref.at[rows(g), :], sem_out.at[slot])

    def w_copies(e):
        slot = e % 2
        return (pltpu.make_async_copy(wgu_ref.at[e], wgu_f.at[slot], sem_w.at[0, slot]),
                pltpu.make_async_copy(wdn_ref.at[e], wdn_f.at[slot], sem_w.at[1, slot]))

    for cp in w_copies(0):
        cp.start()
    for g in range(EXPERT_RING - 1):
        in_copy(g).start()

    def expert(e, carry):
        for cp in w_copies(e):
            cp.wait()

        @pl.when(e < last_e)
        def _():
            for cp in w_copies(e + 1):
                cp.start()

        slot = e % 2
        wgu_s[...] = wgu_f[slot].astype(BF16)
        wdn_s[...] = wdn_f[slot].astype(BF16)
        first = start_ref[e]

        def block(i, c):
            g = first + i
            in_copy(g).wait()
            in_copy(g + EXPERT_RING - 1).start()

            @pl.when(g >= EXPERT_RING)
            def _():
                out_copy(g - EXPERT_RING).wait()

            s = g % EXPERT_RING
            lo, hi = _unpack_halves(xbuf[s])
            gu = _dot(lo.astype(BF16), wgu_s[0:half, :]) + _dot(hi.astype(BF16), wgu_s[half:, :])
            act = _silu(gu[:, :EXPERT_FF]) * gu[:, EXPERT_FF:]
            ybuf[s] = _pack_halves(_dot(act.astype(BF16), wdn_s[...]))
            out_copy(g).start()
            return c

        lax.fori_loop(0, count_ref[e], block, 0)
        return carry

    lax.fori_loop(0, N_EXPERTS, expert, 0)

    for j in range(EXPERT_RING - 1):
        in_copy(used + j).wait()
    for j in range(EXPERT_RING):
        @pl.when(used > j)
        def _():
            out_copy(used - 1 - j).wait()

    ybuf[0] = jnp.zeros((EXPERT_BLOCK, half), jnp.uint32)

    def zcopy(g):
        return pltpu.make_async_copy(ybuf.at[0], ys_ref.at[rows(g), :], sem_out.at[0])

    def zstart(g, c):
        zcopy(g).start()
        return c

    def zwait(g, c):
        zcopy(g).wait()
        return c

    lax.fori_loop(used, nblk_total, zstart, 0)
    lax.fori_loop(used, nblk_total, zwait, 0)


def _experts(blk_start, blk_count, xs, w_gu, w_dn):
    rows, half = xs.shape
    nblk = rows // EXPERT_BLOCK
    grid_spec = pltpu.PrefetchScalarGridSpec(
        num_scalar_prefetch=2,
        grid=(1,),
        in_specs=[pl.BlockSpec(memory_space=pl.ANY), pl.BlockSpec(memory_space=pl.ANY),
                  pl.BlockSpec(memory_space=pl.ANY)],
        out_specs=pl.BlockSpec(memory_space=pl.ANY),
        scratch_shapes=[pltpu.VMEM((2, D_MODEL, 2 * EXPERT_FF), F32), pltpu.VMEM((2, EXPERT_FF, D_MODEL), F32),
                        pltpu.VMEM((D_MODEL, 2 * EXPERT_FF), BF16), pltpu.VMEM((EXPERT_FF, D_MODEL), BF16),
                        pltpu.VMEM((EXPERT_RING, EXPERT_BLOCK, half), jnp.uint32),
                        pltpu.VMEM((EXPERT_RING, EXPERT_BLOCK, half), jnp.uint32),
                        pltpu.SemaphoreType.DMA((2, 2)), pltpu.SemaphoreType.DMA((EXPERT_RING,)),
                        pltpu.SemaphoreType.DMA((EXPERT_RING,))],
    )
    return pl.pallas_call(
        functools.partial(_experts_kernel, nblk_total=nblk),
        out_shape=jax.ShapeDtypeStruct((rows, half), jnp.uint32),
        grid_spec=grid_spec,
        compiler_params=_cparams(("arbitrary",)),
        name="experts",
    )(blk_start, blk_count, xs, w_gu, w_dn)


def _combine_kernel(dest_ref, x1_ref, h_ref, w_ref, g2_ref, wsgu_ref, wsdn_ref, ys_ref, o_ref, buf, sem, *, tm):
    def start_row(r, c):
        for k in range(TOP_K):
            d = dest_ref[r * TOP_K + k]
            _row_copy(ys_ref.at[pl.ds(d, 1), :], buf.at[k, pl.ds(r, 1), :], sem).start(priority=k % 2)
        return c

    lax.fori_loop(0, tm, start_row, 0)
    su = _dot(h_ref[...].astype(BF16), wsgu_ref[...])
    y = _dot((_silu(su[:, :SHARED_FF]) * su[:, SHARED_FF:]).astype(BF16), wsdn_ref[...])
    for k in range(TOP_K):
        _row_copy(ys_ref.at[pl.ds(0, tm), :], buf.at[k], sem).wait()
    w = w_ref[...]
    half = D_MODEL // 2
    ylo, yhi = y[:, :half], y[:, half:]
    for k in range(TOP_K):
        lo, hi = _unpack_halves(buf[k])
        wk = w[:, k:k + 1]
        ylo = ylo + wk * lo
        yhi = yhi + wk * hi
    g2 = g2_ref[0]
    o_ref[:, :half] = x1_ref[:, :half] + g2[:, :half] * ylo
    o_ref[:, half:] = x1_ref[:, half:] + g2[:, half:] * yhi


def _combine(dest_flat, x1, h2, wts, mod3, wsgu, wsdn, ys, seq):
    t = x1.shape[0]
    tm = min(128, seq)
    tpb = seq // tm
    const = lambda i: (0, 0)
    row = lambda n: pl.BlockSpec((tm, n), lambda i: (i, 0))
    return pl.pallas_call(
        functools.partial(_combine_kernel, tm=tm),
        out_shape=jax.ShapeDtypeStruct((t, D_MODEL), F32),
        grid=(t // tm,),
        in_specs=[pl.BlockSpec((tm * TOP_K,), lambda i: (i,), memory_space=pltpu.SMEM),
                  row(D_MODEL), row(D_MODEL), row(LANES),
                  pl.BlockSpec((1, 1, D_MODEL), lambda i: (i // tpb, 0, 5)),
                  pl.BlockSpec(wsgu.shape, const), pl.BlockSpec(wsdn.shape, const),
                  pl.BlockSpec(memory_space=pl.ANY)],
        out_specs=row(D_MODEL),
        scratch_shapes=[pltpu.VMEM((TOP_K, tm, D_MODEL // 2), jnp.uint32), pltpu.SemaphoreType.DMA(())],
        compiler_params=_cparams(("arbitrary",)),
        name="combine",
    )(dest_flat, x1, h2, wts, mod3, wsgu, wsdn, ys)


def _layer(x, c, layer, w_ada, b_ada, norm1_g, w_in, conv_w, a_log, dt_bias, gdn_norm_g, w_o_gdn,
           q_norm_g, k_norm_g, lambda_q1, lambda_k1, lambda_q2, lambda_k2, subln_g, w_o_diff,
           w_out, norm2_g, w_router, router_bias, w_exp_gate_up, w_exp_down,
           w_shared_gate_up, w_shared_down):
    nb, seq, d = x.shape
    t = nb * seq
    lam_init = 0.8 - 0.6 * math.exp(-0.3 * layer)
    x2 = x.reshape(t, d)

    c_pad = jnp.pad(c, ((0, SUBLANES - nb % SUBLANES if nb % SUBLANES else 0), (0, 0)))
    mod = _ada(c_pad, w_ada, b_ada.reshape(1, -1))[:nb]
    mod3 = mod.reshape(nb, 1, 6 * d)

    o_bd = 2 * GDN_QK + 2 * GDN_V
    o_att = o_bd + 2 * GDN_HEADS
    o_gate = o_att + 2 * DIFF_QK + DIFF_V
    w_gdn = w_in[:, :o_bd].astype(BF16)
    w_bd = jnp.pad(w_in[:, o_bd:o_att], ((0, 0), (0, LANES - 2 * GDN_HEADS))).astype(BF16)
    w_att = w_in[:, o_att:o_gate].astype(BF16)
    w_gate = w_in[:, o_gate:].astype(BF16)
    gdn_in, bd, att_in, gates = _inproj(x2, mod3, norm1_g.reshape(1, d), w_gdn, w_bd, w_att, w_gate, seq)

    aparams = jnp.zeros((SUBLANES, LANES), F32)
    aparams = aparams.at[0, GDN_HEADS:2 * GDN_HEADS].set(a_log).at[1, GDN_HEADS:2 * GDN_HEADS].set(dt_bias)
    oa = _gdn(gdn_in.reshape(nb, seq, -1), bd.reshape(nb, seq, LANES), conv_w, aparams,
              gdn_norm_g.reshape(1, GDN_DV)).reshape(t, GDN_V)

    tq = min(512, seq)
    slopes = [2.0 ** (-8.0 * (h + 1) / DIFF_HEADS) for h in range(DIFF_HEADS)]
    c3 = _bf16_split3(LOG2E)
    qfeat = np.zeros((1, 2 * DIFF_HEADS * DIFF_DH), np.float32)
    for g in range(2 * DIFF_HEADS):
        for j in range(6):
            qfeat[0, g * DIFF_DH + j] = slopes[g // 2] * c3[j // 2]
    grp = np.arange(DIFF_QK) // DIFF_DH
    bdm = jnp.asarray((grp[:, None] == grp[None, :]).astype(np.float32))
    qa, ka, vt = _aprep(att_in, jnp.tile(q_norm_g, 2 * DIFF_HEADS).reshape(1, -1),
                        jnp.tile(k_norm_g, 2 * DIFF_HEADS).reshape(1, -1), jnp.asarray(qfeat), bdm, nb, seq, tq)
    slopes2 = jnp.asarray([s * (c3[0] + c3[1] + c3[2]) for s in slopes], F32)
    lamv = jnp.zeros((SUBLANES, DIFF_DH), F32)
    lamv = lamv.at[0].set(lambda_q1).at[1].set(lambda_k1).at[2].set(lambda_q2).at[3].set(lambda_k2)
    ob = _attn(slopes2, qa, ka, vt, lamv, subln_g.reshape(-1, 1), nb, seq, tq, lam_init)

    x1, h2, hp = _post(x2, oa, ob, gates, mod3, norm2_g.reshape(1, d), w_o_gdn.astype(BF16),
                       w_o_diff.astype(BF16), w_out.astype(BF16), seq)

    idx, wts, counts = _router(h2, w_router, router_bias.reshape(1, -1))
    cnt = counts[0].astype(jnp.int32)
    padded = (cnt + EXPERT_BLOCK - 1) // EXPERT_BLOCK * EXPERT_BLOCK
    pends = jnp.cumsum(padded)
    pstart = (pends - padded).astype(F32).reshape(1, -1)
    nblk = -(-(t * TOP_K) // EXPERT_BLOCK) + N_EXPERTS
    blk_start = ((pends - padded) // EXPERT_BLOCK).astype(jnp.int32)
    blk_count = (padded // EXPERT_BLOCK).astype(jnp.int32)
    dest = _rank(idx, pstart)
    dest_flat = dest[:, :TOP_K].reshape(-1)
    xs = _dispatch(dest_flat, hp, jnp.zeros((nblk * EXPERT_BLOCK, d // 2), jnp.uint32))
    ys = _experts(blk_start, blk_count, xs, w_exp_gate_up, w_exp_down)
    out = _combine(dest_flat, x1, h2, wts, mod3, w_shared_gate_up.astype(BF16),
                   w_shared_down.astype(BF16), ys, seq)
    return out.reshape(nb, seq, d)


def kernel(x, c, w_ada, b_ada, norm1_g, w_in, conv_w, a_log, dt_bias, gdn_norm_g, w_o_gdn, q_norm_g, k_norm_g, lambda_q1, lambda_k1, lambda_q2, lambda_k2, subln_g, w_o_diff, w_out, norm2_g, w_router, router_bias, w_exp_gate_up, w_exp_down, w_shared_gate_up, w_shared_down):
    params = (w_ada, b_ada, norm1_g, w_in, conv_w, a_log, dt_bias, gdn_norm_g, w_o_gdn, q_norm_g,
              k_norm_g, lambda_q1, lambda_k1, lambda_q2, lambda_k2, subln_g, w_o_diff, w_out, norm2_g,
              w_router, router_bias, w_exp_gate_up, w_exp_down, w_shared_gate_up, w_shared_down)
    for layer in range(w_ada.shape[0]):
        x = _layer(x, c, layer, *(p[layer] for p in params))
    return x
```

```python
import functools
import math

import jax
import jax.numpy as jnp
import numpy as np
from jax import lax
from jax.experimental import pallas as pl
from jax.experimental.pallas import tpu as pltpu

F32 = jnp.float32
BF16 = jnp.bfloat16
HIGHEST = lax.Precision.HIGHEST

D_MODEL = 1024
CHUNK = 64
GDN_HEADS = 4
GDN_DK = 128
GDN_DV = 128
GDN_CONV = 4
DIFF_HEADS = 4
DIFF_DH = 64
N_EXPERTS = 256
TOP_K = 8
N_GROUPS = 8
TOPK_GROUPS = 4
EXPERT_FF = 256
SHARED_FF = 256
ROUTED_SCALE = 2.5
NORM_EPS = 1e-6
GROUP_SIZE = N_EXPERTS // N_GROUPS

GDN_QK = GDN_HEADS * GDN_DK
GDN_V = GDN_HEADS * GDN_DV
CONV_CH = 2 * GDN_QK + GDN_V
DIFF_QK = DIFF_HEADS * 2 * DIFF_DH
DIFF_V = DIFF_HEADS * 2 * DIFF_DH

LANES = 128
SUBLANES = 8
EXPERT_BLOCK = 128
INV_BASE = 8
LOG2E = math.log2(math.e)
VMEM_LIMIT = 56 * 1024 * 1024


def _cparams(sem):
    return pltpu.CompilerParams(dimension_semantics=sem, vmem_limit_bytes=VMEM_LIMIT)


def _dot(a, b):
    return jnp.dot(a, b, preferred_element_type=F32)


def _dot_nt(a, b):
    return lax.dot_general(a, b, (((1,), (1,)), ((), ())), preferred_element_type=F32)


def _dot_tn(a, b):
    return lax.dot_general(a, b, (((0,), (0,)), ((), ())), preferred_element_type=F32)


def _silu(x):
    return x * jax.nn.sigmoid(x)


def _bf16_split3(x):
    rnd = lambda v: float(np.float32(v).astype(BF16).astype(np.float32))
    a = rnd(x)
    b = rnd(x - a)
    c = rnd(x - a - b)
    return a, b, c


def _ada_kernel(c_ref, w_ref, b_ref, o_ref):
    s = _silu(c_ref[...])
    o_ref[...] = jnp.dot(s, w_ref[...], precision=HIGHEST, preferred_element_type=F32) + b_ref[...]


def _ada(c_pad, w_ada, b_ada):
    n = w_ada.shape[1]
    tn = 1024
    return pl.pallas_call(
        _ada_kernel,
        out_shape=jax.ShapeDtypeStruct((c_pad.shape[0], n), F32),
        grid=(n // tn,),
        in_specs=[pl.BlockSpec(c_pad.shape, lambda j: (0, 0)),
                  pl.BlockSpec((D_MODEL, tn), lambda j: (0, j)),
                  pl.BlockSpec((1, tn), lambda j: (0, j))],
        out_specs=pl.BlockSpec((c_pad.shape[0], tn), lambda j: (0, j)),
        compiler_params=_cparams(("arbitrary",)),
        name="ada",
    )(c_pad, w_ada, b_ada)


def _inproj_kernel(x_ref, shift_ref, scale_ref, g_ref, wg_ref, wbd_ref, wa_ref, wgt_ref,
                   gdn_ref, bd_ref, att_ref, gate_ref):
    x = x_ref[...]
    y = x * lax.rsqrt(jnp.mean(x * x, axis=-1, keepdims=True) + NORM_EPS) * g_ref[...]
    h = (y * (1.0 + scale_ref[0]) + shift_ref[0]).astype(BF16)
    gdn_ref[...] = _dot(h, wg_ref[...])
    bd_ref[...] = _dot(h, wbd_ref[...])
    att_ref[...] = _dot(h, wa_ref[...])
    gate_ref[...] = _dot(h, wgt_ref[...])


def _inproj(x2, mod3, norm1_g, w_gdn, w_bd, w_att, w_gate, seq):
    t = x2.shape[0]
    tm = min(256, seq)
    tiles_per_b = seq // tm
    const = lambda i: (0, 0)
    wspec = lambda w: pl.BlockSpec(w.shape, const, pipeline_mode=pl.Buffered(1))
    row = lambda n: pl.BlockSpec((tm, n), lambda i: (i, 0))
    return pl.pallas_call(
        _inproj_kernel,
        out_shape=(jax.ShapeDtypeStruct((t, w_gdn.shape[1]), F32),
                   jax.ShapeDtypeStruct((t, LANES), F32),
                   jax.ShapeDtypeStruct((t, w_att.shape[1]), F32),
                   jax.ShapeDtypeStruct((t, w_gate.shape[1]), F32)),
        grid=(t // tm,),
        in_specs=[row(D_MODEL),
                  pl.BlockSpec((1, 1, D_MODEL), lambda i: (i // tiles_per_b, 0, 0)),
                  pl.BlockSpec((1, 1, D_MODEL), lambda i: (i // tiles_per_b, 0, 1)),
                  pl.BlockSpec((1, D_MODEL), const),
                  wspec(w_gdn), wspec(w_bd), wspec(w_att), wspec(w_gate)],
        out_specs=(row(w_gdn.shape[1]), row(LANES), row(w_att.shape[1]), row(w_gate.shape[1])),
        compiler_params=_cparams(("arbitrary",)),
        name="inproj",
    )(x2, mod3, mod3, norm1_g, w_gdn, w_bd, w_att, w_gate)


def _gdn_kernel(x_ref, bd_ref, cw_ref, ap_ref, ng_ref, o_ref, cbuf, state, *, nb, lg):
    step = pl.program_id(0)
    nc = lg // CHUNK

    @pl.when(step == 0)
    def _():
        cbuf[...] = jnp.zeros_like(cbuf)
        state[...] = jnp.zeros_like(state)

    ri = lax.broadcasted_iota(jnp.int32, (CHUNK, CHUNK), 0)
    ci = lax.broadcasted_iota(jnp.int32, (CHUNK, CHUNK), 1)
    causal = ri >= ci
    strict = ri > ci
    eye = (ri == ci).astype(F32)
    rl = lax.broadcasted_iota(jnp.int32, (lg, lg), 0)
    cl = lax.broadcasted_iota(jnp.int32, (lg, lg), 1)
    blocktri = ((rl >= cl) & ((rl // CHUNK) == (cl // CHUNK))).astype(F32)
    cw = cw_ref[...]
    a_row = ap_ref[0:1, :]
    dt_row = ap_ref[1:2, :]
    ng = ng_ref[...]

    units = []
    for b in range(nb):
        cbuf[b, SUBLANES:SUBLANES + lg, :] = x_ref[b, :, 0:CONV_CH]
        acc = cw[GDN_CONV - 1:GDN_CONV, :] * cbuf[b, SUBLANES:SUBLANES + lg, :]
        for j in range(GDN_CONV - 1):
            off = SUBLANES - (GDN_CONV - 1) + j
            acc = acc + cw[j:j + 1, :] * cbuf[b, off:off + lg, :]
        cbuf[b, 0:SUBLANES, :] = cbuf[b, lg:lg + SUBLANES, :]
        qkv = _silu(acc)

        bd = bd_ref[b]
        beta_t = jax.nn.sigmoid(bd)
        g_t = -jnp.exp(a_row) * jax.nn.softplus(bd + dt_row)
        gcum = jnp.dot(blocktri, g_t, precision=HIGHEST, preferred_element_type=F32)
        gcum_t = gcum.T
        egcum = jnp.exp(gcum)

        for h in range(GDN_HEADS):
            qh = qkv[:, h * GDN_DK:(h + 1) * GDN_DK]
            kh = qkv[:, GDN_QK + h * GDN_DK:GDN_QK + (h + 1) * GDN_DK]
            vh = qkv[:, 2 * GDN_QK + h * GDN_DV:2 * GDN_QK + (h + 1) * GDN_DV]
            qh = qh * lax.rsqrt(jnp.sum(qh * qh, axis=-1, keepdims=True) + NORM_EPS) * (GDN_DK ** -0.5)
            kh = kh * lax.rsqrt(jnp.sum(kh * kh, axis=-1, keepdims=True) + NORM_EPS)
            for c in range(nc):
                r0, r1 = c * CHUNK, (c + 1) * CHUNK
                q, k, v = qh[r0:r1], kh[r0:r1], vh[r0:r1]
                gc = gcum[r0:r1, 4 + h:5 + h]
                gr = gcum_t[4 + h:5 + h, r0:r1]
                egc = egcum[r0:r1, 4 + h:5 + h]
                bcol = beta_t[r0:r1, h:h + 1]
                gl = gcum[r1 - 1:r1, 4 + h:5 + h]
                kb = k.astype(BF16)
                units.append(dict(
                    b=b, h=h, c=c, kb=kb, bcol=bcol,
                    qkb=jnp.concatenate([q.astype(BF16), kb], axis=0),
                    decay=jnp.exp(jnp.where(causal, gc - gr, -jnp.inf)),
                    rhs=jnp.concatenate([v * bcol, k * (bcol * egc)], axis=-1).astype(BF16),
                    qg=(q * egc).astype(BF16), egl=jnp.exp(gl),
                    kdec=(k * jnp.exp(gl - gc)).astype(BF16)))

    for u in units:
        u["kq"] = _dot_nt(u["qkb"], u["kb"])
    def same_block(size):
        return (ri // size) == (ci // size)
    for u in units:
        lower = jnp.where(strict, u["bcol"] * u["kq"][CHUNK:] * u["decay"], 0.0)
        u["lower"] = lower
        diag = jnp.where(same_block(INV_BASE), lower, 0.0)
        u["xinv"] = eye - diag
        u["pw"] = diag.astype(BF16)
    for r in range(INV_BASE.bit_length() - 2):
        for u in units:
            u["pw"] = _dot(u["pw"], u["pw"]).astype(BF16)
        for u in units:
            u["xinv"] = u["xinv"] + _dot(u["xinv"].astype(BF16), u["pw"])
    size = INV_BASE
    while size < CHUNK:
        pair_off = same_block(2 * size) & jnp.logical_not(same_block(size))
        for u in units:
            u["xb"] = u["xinv"].astype(BF16)
            u["cx"] = _dot(jnp.where(pair_off, u["lower"], 0.0).astype(BF16), u["xb"]).astype(BF16)
        for u in units:
            u["xinv"] = u["xinv"] - _dot(u["xb"], u["cx"])
        size *= 2
    for u in units:
        u["sol"] = _dot(u["xinv"].astype(BF16), u["rhs"])
        u["a_qk"] = (u["kq"][:CHUNK] * u["decay"]).astype(BF16)

    st = {(b, h): state[b, h] for b in range(nb) for h in range(GDN_HEADS)}
    for c in range(nc):
        cu = [u for u in units if u["c"] == c]
        for u in cu:
            stb = st[(u["b"], u["h"])].astype(BF16)
            u["ws"] = _dot(jnp.concatenate([u["sol"][:, GDN_DV:].astype(BF16), u["qg"]], axis=0), stb)
        for u in cu:
            u["vnb"] = (u["sol"][:, :GDN_DV] - u["ws"][:CHUNK]).astype(BF16)
        for u in cu:
            u["out"] = u["ws"][CHUNK:] + _dot(u["a_qk"], u["vnb"])
        for u in cu:
            key = (u["b"], u["h"])
            st[key] = st[key] * u["egl"] + _dot_tn(u["kdec"], u["vnb"])
        for u in cu:
            b, h = u["b"], u["h"]
            r0, r1 = c * CHUNK, (c + 1) * CHUNK
            out = u["out"]
            on = out * lax.rsqrt(jnp.mean(out * out, axis=-1, keepdims=True) + NORM_EPS) * ng
            zh = x_ref[b, r0:r1, CONV_CH + h * GDN_DV:CONV_CH + (h + 1) * GDN_DV]
            o_ref[b, r0:r1, h * GDN_DV:(h + 1) * GDN_DV] = on * _silu(zh)
    for (b, h), s in st.items():
        state[b, h] = s


def _gdn(gdn_in3, bd3, conv_w, aparams, gdn_norm_g):
    nb, seq, _ = gdn_in3.shape
    lg = min(128, seq)
    const = lambda i: (0, 0)
    return pl.pallas_call(
        functools.partial(_gdn_kernel, nb=nb, lg=lg),
        out_shape=jax.ShapeDtypeStruct((nb, seq, GDN_V), F32),
        grid=(seq // lg,),
        in_specs=[pl.BlockSpec((nb, lg, gdn_in3.shape[2]), lambda i: (0, i, 0)),
                  pl.BlockSpec((nb, lg, LANES), lambda i: (0, i, 0)),
                  pl.BlockSpec(conv_w.shape, const),
                  pl.BlockSpec(aparams.shape, const),
                  pl.BlockSpec(gdn_norm_g.shape, const)],
        out_specs=pl.BlockSpec((nb, lg, GDN_V), lambda i: (0, i, 0)),
        scratch_shapes=[pltpu.VMEM((nb, lg + SUBLANES, CONV_CH), F32),
                        pltpu.VMEM((nb, GDN_HEADS, GDN_DK, GDN_DV), F32)],
        compiler_params=_cparams(("arbitrary",)),
        name="gdn",
    )(gdn_in3, bd3, conv_w, aparams, gdn_norm_g)


def _aprep_kernel(x_ref, gq_ref, gk_ref, qf_ref, bdm_ref, q_ref, k_ref, vt_ref, *, tk):
    x = x_ref[...]
    tm = x.shape[0]
    bdm = bdm_ref[...]

    def qknorm(v, gain):
        ms = jnp.dot(v * v, bdm, precision=HIGHEST, preferred_element_type=F32) * (1.0 / DIFF_DH)
        return v * lax.rsqrt(ms + NORM_EPS) * gain

    qn = qknorm(x[:, 0:DIFF_QK], gq_ref[...]) * (DIFF_DH ** -0.5 * LOG2E)
    kn = qknorm(x[:, DIFF_QK:2 * DIFF_QK], gk_ref[...])
    pos = pl.program_id(0) * tm + lax.broadcasted_iota(jnp.int32, (tm, DIFF_DH), 0)
    krel = pos % tk
    lane = lax.broadcasted_iota(jnp.int32, (tm, DIFF_DH), 1)
    hi = ((krel // 256) * 256).astype(F32)
    lo = (krel % 256).astype(F32)
    kfeat = jnp.where(lane < 6, jnp.where(lane % 2 == 0, hi, lo), 0.0)
    qfeat = qf_ref[...]
    qparts, kparts = [], []
    for g in range(2 * DIFF_HEADS):
        qparts += [qn[:, g * DIFF_DH:(g + 1) * DIFF_DH],
                   jnp.broadcast_to(qfeat[:, g * DIFF_DH:(g + 1) * DIFF_DH], (tm, DIFF_DH))]
        kparts += [kn[:, g * DIFF_DH:(g + 1) * DIFF_DH], kfeat]
    q_ref[...] = jnp.concatenate(qparts, axis=-1).astype(BF16)
    k_ref[...] = jnp.concatenate(kparts, axis=-1).astype(BF16)
    vt_ref[0, 0] = x[:, 2 * DIFF_QK:].T.astype(BF16)


def _aprep(att_in, gq, gk, qfeat, bdm, nb, seq, tk):
    t = att_in.shape[0]
    tm = tk
    nk = seq // tk
    const = lambda i: (0, 0)
    return pl.pallas_call(
        functools.partial(_aprep_kernel, tk=tk),
        out_shape=(jax.ShapeDtypeStruct((t, 2 * DIFF_QK), BF16),
                   jax.ShapeDtypeStruct((t, 2 * DIFF_QK), BF16),
                   jax.ShapeDtypeStruct((nb, nk, DIFF_V, tk), BF16)),
        grid=(t // tm,),
        in_specs=[pl.BlockSpec((tm, att_in.shape[1]), lambda i: (i, 0)),
                  pl.BlockSpec(gq.shape, const), pl.BlockSpec(gk.shape, const),
                  pl.BlockSpec(qfeat.shape, const), pl.BlockSpec(bdm.shape, const)],
        out_specs=(pl.BlockSpec((tm, 2 * DIFF_QK), lambda i: (i, 0)),
                   pl.BlockSpec((tm, 2 * DIFF_QK), lambda i: (i, 0)),
                   pl.BlockSpec((1, 1, DIFF_V, tk), lambda i: (i // nk, i % nk, 0, 0))),
        compiler_params=_cparams(("arbitrary",)),
        name="aprep",
    )(att_in, gq, gk, qfeat, bdm)


def _attn_kernel(slope_ref, q_ref, k_ref, vt_ref, lamv_ref, sg_ref, o_ref, m_s, l_s, acc_s, sa, sb,
                 *, tq, lam_init):
    h = pl.program_id(1)
    qi = pl.program_id(2)
    slope2 = slope_ref[h]
    dv = 2 * DIFF_DH
    qs = [q_ref[:, 0:LANES], q_ref[:, LANES:2 * LANES]]

    m_s[...] = jnp.full_like(m_s, -jnp.inf)
    l_s[...] = jnp.zeros_like(l_s)
    acc_s[...] = jnp.zeros_like(acc_s)

    def scores(kj):
        k0 = pl.multiple_of(kj * tq, tq)
        return tuple(_dot_nt(k_ref[pl.ds(k0, tq), c * LANES:(c + 1) * LANES], qs[c]) for c in range(2))

    def accumulate(ss, kj):
        vt = vt_ref[0, kj]
        off = slope2 * ((kj - qi) * tq).astype(F32)
        for c in range(2):
            s = ss[c]
            m_old = m_s[c]
            m_new = jnp.maximum(m_old, jnp.max(s, axis=0, keepdims=True) + off)
            alpha = jnp.exp2(m_old - m_new)
            p = jnp.exp2(s - (m_new - off))
            l_s[c] = alpha * l_s[c] + jnp.sum(p, axis=0, keepdims=True)
            acc_s[c] = alpha * acc_s[c] + _dot(vt, p.astype(BF16))
            m_s[c] = m_new

    kpos = lax.broadcasted_iota(jnp.int32, (tq, tq), 0)
    qpos = lax.broadcasted_iota(jnp.int32, (tq, tq), 1)
    allowed = (kpos // CHUNK) <= (qpos // CHUNK)
    fut = jnp.maximum(kpos - qpos, 0).astype(F32)
    add = jnp.where(allowed, (-2.0 * slope2) * fut, -jnp.inf)
    accumulate(tuple(s + add for s in scores(qi)), qi)

    def put(slot, ss):
        for c in range(2):
            slot[c] = ss[c]

    npairs = qi // 2
    put(sa, scores(0))

    def pair(j, carry):
        a = 2 * j
        put(sb, scores(a + 1))
        accumulate((sa[0], sa[1]), a)
        put(sa, scores(jnp.minimum(a + 2, qi - 1)))
        accumulate((sb[0], sb[1]), a + 1)
        return carry
    lax.fori_loop(0, npairs, pair, 0)

    @pl.when(qi % 2 == 1)
    def _():
        accumulate((sa[0], sa[1]), qi - 1)

    lq1, lk1, lq2, lk2 = (lamv_ref[i:i + 1, :] for i in range(4))
    lam = (jnp.exp(jnp.sum(lq1 * lk1, axis=-1, keepdims=True))
           - jnp.exp(jnp.sum(lq2 * lk2, axis=-1, keepdims=True)) + lam_init)
    o = acc_s[0] / l_s[0] - lam * (acc_s[1] / l_s[1])
    on = o * lax.rsqrt(jnp.mean(o * o, axis=0, keepdims=True) + NORM_EPS) * sg_ref[...] * (1.0 - lam_init)
    o_ref[...] = on.T


def _attn(slopes2, qa, ka, vt, lamv, sg_col, nb, seq, tq, lam_init):
    t = qa.shape[0]
    nq = seq // tq
    dv = 2 * DIFF_DH
    grid_spec = pltpu.PrefetchScalarGridSpec(
        num_scalar_prefetch=1,
        grid=(nb, DIFF_HEADS, nq),
        in_specs=[pl.BlockSpec((tq, 2 * LANES), lambda b, h, i, s: (b * nq + i, h)),
                  pl.BlockSpec((seq, 2 * LANES), lambda b, h, i, s: (b, h)),
                  pl.BlockSpec((1, nq, dv, tq), lambda b, h, i, s: (b, 0, h, 0)),
                  pl.BlockSpec(lamv.shape, lambda b, h, i, s: (0, 0)),
                  pl.BlockSpec(sg_col.shape, lambda b, h, i, s: (0, 0))],
        out_specs=pl.BlockSpec((tq, dv), lambda b, h, i, s: (b * nq + i, h)),
        scratch_shapes=[pltpu.VMEM((2, 1, tq), F32), pltpu.VMEM((2, 1, tq), F32),
                        pltpu.VMEM((2, dv, tq), F32),
                        pltpu.VMEM((2, tq, tq), F32), pltpu.VMEM((2, tq, tq), F32)],
    )
    return pl.pallas_call(
        functools.partial(_attn_kernel, tq=tq, lam_init=lam_init),
        out_shape=jax.ShapeDtypeStruct((t, DIFF_V), F32),
        grid_spec=grid_spec,
        compiler_params=_cparams(("arbitrary", "arbitrary", "arbitrary")),
        name="attn",
    )(slopes2, qa, ka, vt, lamv, sg_col)


def _pack_halves(x):
    n = x.shape[1] // 2
    lo = pltpu.bitcast(x[:, :n].astype(BF16).astype(F32), jnp.uint32)
    hi = pltpu.bitcast(x[:, n:].astype(BF16).astype(F32), jnp.uint32)
    return (lo >> 16) | (hi & jnp.uint32(0xFFFF0000))


def _unpack_halves(p):
    lo = pltpu.bitcast(p << 16, F32)
    hi = pltpu.bitcast(p & jnp.uint32(0xFFFF0000), F32)
    return lo, hi


def _post_kernel(x_ref, oa_ref, ob_ref, gate_ref, g1_ref, sh2_ref, sc2_ref, n2_ref,
                 woa_ref, wob_ref, wout_ref, x1_ref, h2_ref, hp_ref):
    ya = _dot(oa_ref[...].astype(BF16), woa_ref[...])
    yb = _dot(ob_ref[...].astype(BF16), wob_ref[...])
    merged = jax.nn.sigmoid(gate_ref[:, 0:D_MODEL]) * ya + jax.nn.sigmoid(gate_ref[:, D_MODEL:]) * yb
    y = _dot(merged.astype(BF16), wout_ref[...])
    x1 = x_ref[...] + g1_ref[0] * y
    x1_ref[...] = x1
    n = x1 * lax.rsqrt(jnp.mean(x1 * x1, axis=-1, keepdims=True) + NORM_EPS) * n2_ref[...]
    h2 = n * (1.0 + sc2_ref[0]) + sh2_ref[0]
    h2_ref[...] = h2
    hp_ref[...] = _pack_halves(h2)


def _post(x2, oa, ob, gates, mod3, norm2_g, woa, wob, wout, seq):
    t = x2.shape[0]
    tm = min(256, seq)
    tpb = seq // tm
    const = lambda i: (0, 0)
    row = lambda n: pl.BlockSpec((tm, n), lambda i: (i, 0))
    modspec = lambda j: pl.BlockSpec((1, 1, D_MODEL), lambda i: (i // tpb, 0, j))
    wspec = lambda w: pl.BlockSpec(w.shape, const, pipeline_mode=pl.Buffered(1))
    return pl.pallas_call(
        _post_kernel,
        out_shape=(jax.ShapeDtypeStruct((t, D_MODEL), F32), jax.ShapeDtypeStruct((t, D_MODEL), F32),
                   jax.ShapeDtypeStruct((t, D_MODEL // 2), jnp.uint32)),
        grid=(t // tm,),
        in_specs=[row(D_MODEL), row(GDN_V), row(DIFF_V), row(2 * D_MODEL),
                  modspec(2), modspec(3), modspec(4), pl.BlockSpec((1, D_MODEL), const),
                  wspec(woa), wspec(wob), wspec(wout)],
        out_specs=(row(D_MODEL), row(D_MODEL), row(D_MODEL // 2)),
        compiler_params=_cparams(("arbitrary",)),
        name="post",
    )(x2, oa, ob, gates, mod3, mod3, mod3, norm2_g, woa, wob, wout)


def _router_kernel(h_ref, wr_ref, rb_ref, idx_ref, w_ref, cnt_ref, cnt_s):
    step = pl.program_id(0)

    @pl.when(step == 0)
    def _():
        cnt_s[...] = jnp.zeros_like(cnt_s)

    tm = h_ref.shape[0]
    logits = jnp.dot(h_ref[...], wr_ref[...], precision=HIGHEST, preferred_element_type=F32)
    scores = jax.nn.sigmoid(logits)
    choice = scores + rb_ref[...]
    lane = lax.broadcasted_iota(jnp.int32, (tm, N_EXPERTS), 1)
    grp = lane // GROUP_SIZE
    neg = -jnp.inf

    def first_max(v):
        m = jnp.max(v, axis=-1, keepdims=True)
        i = jnp.min(jnp.where(v == m, lane, N_EXPERTS), axis=-1, keepdims=True)
        return m, i

    gl = lax.broadcasted_iota(jnp.int32, (tm, LANES), 1)
    gscore = jnp.full((tm, LANES), neg, F32)
    for g in range(N_GROUPS):
        vg = jnp.where(grp == g, choice, neg)
        m1, i1 = first_max(vg)
        m2 = jnp.max(jnp.where(lane == i1, neg, vg), axis=-1, keepdims=True)
        gscore = jnp.where(gl == g, m1 + m2, gscore)
    emask = jnp.zeros((tm, N_EXPERTS), jnp.bool_)
    for _ in range(TOPK_GROUPS):
        m = jnp.max(gscore, axis=-1, keepdims=True)
        gi = jnp.min(jnp.where(gscore == m, gl, LANES), axis=-1, keepdims=True)
        emask = emask | (grp == gi)
        gscore = jnp.where(gl == gi, neg, gscore)
    masked = jnp.where(emask, choice, neg)
    idxs = jnp.zeros((tm, LANES), jnp.int32)
    ws = jnp.zeros((tm, LANES), F32)
    sel = jnp.zeros((tm, N_EXPERTS), F32)
    for k in range(TOP_K):
        _, i = first_max(masked)
        hit = lane == i
        wk = jnp.sum(jnp.where(hit, scores, 0.0), axis=-1, keepdims=True)
        idxs = jnp.where(gl == k, i, idxs)
        ws = jnp.where(gl == k, wk, ws)
        sel = jnp.where(hit, 1.0, sel)
        masked = jnp.where(hit, neg, masked)
    wsum = jnp.sum(ws, axis=-1, keepdims=True)
    idx_ref[...] = idxs
    w_ref[...] = ws / wsum * ROUTED_SCALE
    cnt_s[...] = cnt_s[...] + jnp.sum(sel, axis=0, keepdims=True)
    cnt_ref[...] = cnt_s[...]


def _router(h2, w_router, rbias):
    t = h2.shape[0]
    tm = min(256, t)
    const = lambda i: (0, 0)
    return pl.pallas_call(
        _router_kernel,
        out_shape=(jax.ShapeDtypeStruct((t, LANES), jnp.int32),
                   jax.ShapeDtypeStruct((t, LANES), F32),
                   jax.ShapeDtypeStruct((1, N_EXPERTS), F32)),
        grid=(t // tm,),
        in_specs=[pl.BlockSpec((tm, D_MODEL), lambda i: (i, 0)),
                  pl.BlockSpec(w_router.shape, const), pl.BlockSpec(rbias.shape, const)],
        out_specs=(pl.BlockSpec((tm, LANES), lambda i: (i, 0)),
                   pl.BlockSpec((tm, LANES), lambda i: (i, 0)),
                   pl.BlockSpec((1, N_EXPERTS), const)),
        scratch_shapes=[pltpu.VMEM((1, N_EXPERTS), F32)],
        compiler_params=_cparams(("arbitrary",)),
        name="router",
    )(h2, w_router, rbias)


def _rank_kernel(idx_ref, ps_ref, dest_ref, run_s):
    step = pl.program_id(0)

    @pl.when(step == 0)
    def _():
        run_s[...] = jnp.zeros_like(run_s)

    tm = idx_ref.shape[0]
    idx = idx_ref[...]
    lane = lax.broadcasted_iota(jnp.int32, (tm, N_EXPERTS), 1)
    gl = lax.broadcasted_iota(jnp.int32, (tm, LANES), 1)
    hits = [lane == idx[:, k:k + 1] for k in range(TOP_K)]
    sel = jnp.zeros((tm, N_EXPERTS), F32)
    for hit in hits:
        sel = jnp.where(hit, 1.0, sel)
    ri = lax.broadcasted_iota(jnp.int32, (tm, tm), 0)
    ci = lax.broadcasted_iota(jnp.int32, (tm, tm), 1)
    before = _dot((ri > ci).astype(BF16), sel.astype(BF16))
    base = before + run_s[...] + ps_ref[...]
    dest = jnp.zeros((tm, LANES), F32)
    for k, hit in enumerate(hits):
        dk = jnp.sum(jnp.where(hit, base, 0.0), axis=-1, keepdims=True)
        dest = jnp.where(gl == k, dk, dest)
    dest_ref[...] = dest.astype(jnp.int32)
    run_s[...] = run_s[...] + jnp.sum(sel, axis=0, keepdims=True)


def _rank(idx, pstart):
    t = idx.shape[0]
    tm = min(256, t)
    const = lambda i: (0, 0)
    return pl.pallas_call(
        _rank_kernel,
        out_shape=jax.ShapeDtypeStruct((t, LANES), jnp.int32),
        grid=(t // tm,),
        in_specs=[pl.BlockSpec((tm, LANES), lambda i: (i, 0)), pl.BlockSpec(pstart.shape, const)],
        out_specs=pl.BlockSpec((tm, LANES), lambda i: (i, 0)),
        scratch_shapes=[pltpu.VMEM((1, N_EXPERTS), F32)],
        compiler_params=_cparams(("arbitrary",)),
        name="rank",
    )(idx, pstart)


def _row_copy(src, dst, sem):
    return pltpu.make_async_copy(src, dst, sem)


def _dispatch_kernel(dest_ref, h_ref, xs_in, xs_ref, sem, *, tm):
    del xs_in

    def start_row(r, c):
        for k in range(TOP_K):
            d = dest_ref[r * TOP_K + k]
            _row_copy(h_ref.at[pl.ds(r, 1), :], xs_ref.at[pl.ds(d, 1), :], sem).start(priority=k % 2)
        return c

    lax.fori_loop(0, tm, start_row, 0)
    for k in range(TOP_K):
        _row_copy(h_ref, xs_ref.at[pl.ds(0, tm), :], sem).wait()


def _dispatch(dest_flat, h2, xs0):
    t = h2.shape[0]
    tm = min(256, t)
    return pl.pallas_call(
        functools.partial(_dispatch_kernel, tm=tm),
        out_shape=jax.ShapeDtypeStruct(xs0.shape, xs0.dtype),
        grid=(t // tm,),
        in_specs=[pl.BlockSpec((tm * TOP_K,), lambda i: (i,), memory_space=pltpu.SMEM),
                  pl.BlockSpec((tm, h2.shape[1]), lambda i: (i, 0)),
                  pl.BlockSpec(memory_space=pl.ANY)],
        out_specs=pl.BlockSpec(memory_space=pl.ANY),
        scratch_shapes=[pltpu.SemaphoreType.DMA(())],
        input_output_aliases={2: 0},
        compiler_params=_cparams(("arbitrary",)),
        name="dispatch",
    )(dest_flat, h2, xs0)


EXPERT_RING = 4


def _experts_kernel(start_ref, count_ref, xs_ref, wgu_ref, wdn_ref, ys_ref,
                    wgu_f, wdn_f, wgu_s, wdn_s, xbuf, ybuf, sem_w, sem_in, sem_out, *, nblk_total):
    half = D_MODEL // 2
    last_e = N_EXPERTS - 1
    used = start_ref[last_e] + count_ref[last_e]

    def rows(g):
        return pl.ds(pl.multiple_of(g * EXPERT_BLOCK, EXPERT_BLOCK), EXPERT_BLOCK)

    def in_copy(g):
        slot = g % EXPERT_RING
        return pltpu.make_async_copy(xs_ref.at[rows(jnp.minimum(g, nblk_total - 1)), :], xbuf.at[slot],
                                     sem_in.at[slot])

    def out_copy(g):
        slot = g % EXPERT_RING
        return pltpu.make_async_copy(ybuf.at[slot], ys_ref.at[rows(g), :], sem_out.at[slot])

    def w_copies(e):
        slot = e % 2
        return (pltpu.make_async_copy(wgu_ref.at[e], wgu_f.at[slot], sem_w.at[0, slot]),
                pltpu.make_async_copy(wdn_ref.at[e], wdn_f.at[slot], sem_w.at[1, slot]))

    for cp in w_copies(0):
        cp.start()
    for g in range(EXPERT_RING - 1):
        in_copy(g).start()

    def expert(e, carry):
        for cp in w_copies(e):
            cp.wait()

        @pl.when(e < last_e)
        def _():
            for cp in w_copies(e + 1):
                cp.start()

        slot = e % 2
        wgu_s[...] = wgu_f[slot].astype(BF16)
        wdn_s[...] = wdn_f[slot].astype(BF16)
        first = start_ref[e]

        def block(i, c):
            g = first + i
            in_copy(g).wait()
            in_copy(g + EXPERT_RING - 1).start()

            @pl.when(g >= EXPERT_RING)
            def _():
                out_copy(g - EXPERT_RING).wait()

            s = g % EXPERT_RING
            lo, hi = _unpack_halves(xbuf[s])
            gu = _dot(lo.astype(BF16), wgu_s[0:half, :]) + _dot(hi.astype(BF16), wgu_s[half:, :])
            act = _silu(gu[:, :EXPERT_FF]) * gu[:, EXPERT_FF:]
            ybuf[s] = _pack_halves(_dot(act.astype(BF16), wdn_s[...]))
            out_copy(g).start()
            return c

        lax.fori_loop(0, count_ref[e], block, 0)
        return carry

    lax.fori_loop(0, N_EXPERTS, expert, 0)

    for j in range(EXPERT_RING - 1):
        in_copy(used + j).wait()
    for j in range(EXPERT_RING):
        @pl.when(used > j)
        def _():
            out_copy(used - 1 - j).wait()

    ybuf[0] = jnp.zeros((EXPERT_BLOCK, half), jnp.uint32)

    def zcopy(g):
        return pltpu.make_async_copy(ybuf.at[0], ys_ref.at[rows(g), :], sem_out.at[0])

    def zstart(g, c):
        zcopy(g).start()
        return c

    def zwait(g, c):
        zcopy(g).wait()
        return c

    lax.fori_loop(used, nblk_total, zstart, 0)
    lax.fori_loop(used, nblk_total, zwait, 0)


def _experts(blk_start, blk_count, xs, w_gu, w_dn):
    rows, half = xs.shape
    nblk = rows // EXPERT_BLOCK
    grid_spec = pltpu.PrefetchScalarGridSpec(
        num_scalar_prefetch=2,
        grid=(1,),
        in_specs=[pl.BlockSpec(memory_space=pl.ANY), pl.BlockSpec(memory_space=pl.ANY),
                  pl.BlockSpec(memory_space=pl.ANY)],
        out_specs=pl.BlockSpec(memory_space=pl.ANY),
        scratch_shapes=[pltpu.VMEM((2, D_MODEL, 2 * EXPERT_FF), F32), pltpu.VMEM((2, EXPERT_FF, D_MODEL), F32),
                        pltpu.VMEM((D_MODEL, 2 * EXPERT_FF), BF16), pltpu.VMEM((EXPERT_FF, D_MODEL), BF16),
                        pltpu.VMEM((EXPERT_RING, EXPERT_BLOCK, half), jnp.uint32),
                        pltpu.VMEM((EXPERT_RING, EXPERT_BLOCK, half), jnp.uint32),
                        pltpu.SemaphoreType.DMA((2, 2)), pltpu.SemaphoreType.DMA((EXPERT_RING,)),
                        pltpu.SemaphoreType.DMA((EXPERT_RING,))],
    )
    return pl.pallas_call(
        functools.partial(_experts_kernel, nblk_total=nblk),
        out_shape=jax.ShapeDtypeStruct((rows, half), jnp.uint32),
        grid_spec=grid_spec,
        compiler_params=_cparams(("arbitrary",)),
        name="experts",
    )(blk_start, blk_count, xs, w_gu, w_dn)


def _combine_kernel(dest_ref, x1_ref, h_ref, w_ref, g2_ref, wsgu_ref, wsdn_ref, ys_ref, o_ref, buf, sem, *, tm):
    def start_row(r, c):
        for k in range(TOP_K):
            d = dest_ref[r * TOP_K + k]
            _row_copy(ys_ref.at[pl.ds(d, 1), :], buf.at[k, pl.ds(r, 1), :], sem).start(priority=k % 2)
        return c

    lax.fori_loop(0, tm, start_row, 0)
    su = _dot(h_ref[...].astype(BF16), wsgu_ref[...])
    y = _dot((_silu(su[:, :SHARED_FF]) * su[:, SHARED_FF:]).astype(BF16), wsdn_ref[...])
    for k in range(TOP_K):
        _row_copy(ys_ref.at[pl.ds(0, tm), :], buf.at[k], sem).wait()
    w = w_ref[...]
    half = D_MODEL // 2
    ylo, yhi = y[:, :half], y[:, half:]
    for k in range(TOP_K):
        lo, hi = _unpack_halves(buf[k])
        wk = w[:, k:k + 1]
        ylo = ylo + wk * lo
        yhi = yhi + wk * hi
    g2 = g2_ref[0]
    o_ref[:, :half] = x1_ref[:, :half] + g2[:, :half] * ylo
    o_ref[:, half:] = x1_ref[:, half:] + g2[:, half:] * yhi


def _combine(dest_flat, x1, h2, wts, mod3, wsgu, wsdn, ys, seq):
    t = x1.shape[0]
    tm = min(128, seq)
    tpb = seq // tm
    const = lambda i: (0, 0)
    row = lambda n: pl.BlockSpec((tm, n), lambda i: (i, 0))
    return pl.pallas_call(
        functools.partial(_combine_kernel, tm=tm),
        out_shape=jax.ShapeDtypeStruct((t, D_MODEL), F32),
        grid=(t // tm,),
        in_specs=[pl.BlockSpec((tm * TOP_K,), lambda i: (i,), memory_space=pltpu.SMEM),
                  row(D_MODEL), row(D_MODEL), row(LANES),
                  pl.BlockSpec((1, 1, D_MODEL), lambda i: (i // tpb, 0, 5)),
                  pl.BlockSpec(wsgu.shape, const), pl.BlockSpec(wsdn.shape, const),
                  pl.BlockSpec(memory_space=pl.ANY)],
        out_specs=row(D_MODEL),
        scratch_shapes=[pltpu.VMEM((TOP_K, tm, D_MODEL // 2), jnp.uint32), pltpu.SemaphoreType.DMA(())],
        compiler_params=_cparams(("arbitrary",)),
        name="combine",
    )(dest_flat, x1, h2, wts, mod3, wsgu, wsdn, ys)


def _layer(x, c, layer, w_ada, b_ada, norm1_g, w_in, conv_w, a_log, dt_bias, gdn_norm_g, w_o_gdn,
           q_norm_g, k_norm_g, lambda_q1, lambda_k1, lambda_q2, lambda_k2, subln_g, w_o_diff,
           w_out, norm2_g, w_router, router_bias, w_exp_gate_up, w_exp_down,
           w_shared_gate_up, w_shared_down):
    nb, seq, d = x.shape
    t = nb * seq
    lam_init = 0.8 - 0.6 * math.exp(-0.3 * layer)
    x2 = x.reshape(t, d)

    c_pad = jnp.pad(c, ((0, SUBLANES - nb % SUBLANES if nb % SUBLANES else 0), (0, 0)))
    mod = _ada(c_pad, w_ada, b_ada.reshape(1, -1))[:nb]
    mod3 = mod.reshape(nb, 1, 6 * d)

    o_bd = 2 * GDN_QK + 2 * GDN_V
    o_att = o_bd + 2 * GDN_HEADS
    o_gate = o_att + 2 * DIFF_QK + DIFF_V
    w_gdn = w_in[:, :o_bd].astype(BF16)
    w_bd = jnp.pad(w_in[:, o_bd:o_att], ((0, 0), (0, LANES - 2 * GDN_HEADS))).astype(BF16)
    w_att = w_in[:, o_att:o_gate].astype(BF16)
    w_gate = w_in[:, o_gate:].astype(BF16)
    gdn_in, bd, att_in, gates = _inproj(x2, mod3, norm1_g.reshape(1, d), w_gdn, w_bd, w_att, w_gate, seq)

    aparams = jnp.zeros((SUBLANES, LANES), F32)
    aparams = aparams.at[0, GDN_HEADS:2 * GDN_HEADS].set(a_log).at[1, GDN_HEADS:2 * GDN_HEADS].set(dt_bias)
    oa = _gdn(gdn_in.reshape(nb, seq, -1), bd.reshape(nb, seq, LANES), conv_w, aparams,
              gdn_norm_g.reshape(1, GDN_DV)).reshape(t, GDN_V)

    tq = min(512, seq)
    slopes = [2.0 ** (-8.0 * (h + 1) / DIFF_HEADS) for h in range(DIFF_HEADS)]
    c3 = _bf16_split3(LOG2E)
    qfeat = np.zeros((1, 2 * DIFF_HEADS * DIFF_DH), np.float32)
    for g in range(2 * DIFF_HEADS):
        for j in range(6):
            qfeat[0, g * DIFF_DH + j] = slopes[g // 2] * c3[j // 2]
    grp = np.arange(DIFF_QK) // DIFF_DH
    bdm = jnp.asarray((grp[:, None] == grp[None, :]).astype(np.float32))
    qa, ka, vt = _aprep(att_in, jnp.tile(q_norm_g, 2 * DIFF_HEADS).reshape(1, -1),
                        jnp.tile(k_norm_g, 2 * DIFF_HEADS).reshape(1, -1), jnp.asarray(qfeat), bdm, nb, seq, tq)
    slopes2 = jnp.asarray([s * (c3[0] + c3[1] + c3[2]) for s in slopes], F32)
    lamv = jnp.zeros((SUBLANES, DIFF_DH), F32)
    lamv = lamv.at[0].set(lambda_q1).at[1].set(lambda_k1).at[2].set(lambda_q2).at[3].set(lambda_k2)
    ob = _attn(slopes2, qa, ka, vt, lamv, subln_g.reshape(-1, 1), nb, seq, tq, lam_init)

    x1, h2, hp = _post(x2, oa, ob, gates, mod3, norm2_g.reshape(1, d), w_o_gdn.astype(BF16),
                       w_o_diff.astype(BF16), w_out.astype(BF16), seq)

    idx, wts, counts = _router(h2, w_router, router_bias.reshape(1, -1))
    cnt = counts[0].astype(jnp.int32)
    padded = (cnt + EXPERT_BLOCK - 1) // EXPERT_BLOCK * EXPERT_BLOCK
    pends = jnp.cumsum(padded)
    pstart = (pends - padded).astype(F32).reshape(1, -1)
    nblk = -(-(t * TOP_K) // EXPERT_BLOCK) + N_EXPERTS
    blk_start = ((pends - padded) // EXPERT_BLOCK).astype(jnp.int32)
    blk_count = (padded // EXPERT_BLOCK).astype(jnp.int32)
    dest = _rank(idx, pstart)
    dest_flat = dest[:, :TOP_K].reshape(-1)
    xs = _dispatch(dest_flat, hp, jnp.zeros((nblk * EXPERT_BLOCK, d // 2), jnp.uint32))
    ys = _experts(blk_start, blk_count, xs, w_exp_gate_up, w_exp_down)
    out = _combine(dest_flat, x1, h2, wts, mod3, w_shared_gate_up.astype(BF16),
                   w_shared_down.astype(BF16), ys, seq)
    return out.reshape(nb, seq, d)


def kernel(x, c, w_ada, b_ada, norm1_g, w_in, conv_w, a_log, dt_bias, gdn_norm_g, w_o_gdn, q_norm_g, k_norm_g, lambda_q1, lambda_k1, lambda_q2, lambda_k2, subln_g, w_o_diff, w_out, norm2_g, w_router, router_bias, w_exp_gate_up, w_exp_down, w_shared_gate_up, w_shared_down):
    params = (w_ada, b_ada, norm1_g, w_in, conv_w, a_log, dt_bias, gdn_norm_g, w_o_gdn, q_norm_g,
              k_norm_g, lambda_q1, lambda_k1, lambda_q2, lambda_k2, subln_g, w_o_diff, w_out, norm2_g,
              w_router, router_bias, w_exp_gate_up, w_exp_down, w_shared_gate_up, w_shared_down)
    for layer in range(w_ada.shape[0]):
        x = _layer(x, c, layer, *(p[layer] for p in params))
    return x
```

```python
import functools
import math

import jax
import jax.numpy as jnp
import numpy as np
from jax import lax
from jax.experimental import pallas as pl
from jax.experimental.pallas import tpu as pltpu

F32 = jnp.float32
BF16 = jnp.bfloat16
HIGHEST = lax.Precision.HIGHEST

D_MODEL = 1024
CHUNK = 64
GDN_HEADS = 4
GDN_DK = 128
GDN_DV = 128
GDN_CONV = 4
DIFF_HEADS = 4
DIFF_DH = 64
N_EXPERTS = 256
TOP_K = 8
N_GROUPS = 8
TOPK_GROUPS = 4
EXPERT_FF = 256
SHARED_FF = 256
ROUTED_SCALE = 2.5
NORM_EPS = 1e-6
GROUP_SIZE = N_EXPERTS // N_GROUPS

GDN_QK = GDN_HEADS * GDN_DK
GDN_V = GDN_HEADS * GDN_DV
CONV_CH = 2 * GDN_QK + GDN_V
DIFF_QK = DIFF_HEADS * 2 * DIFF_DH
DIFF_V = DIFF_HEADS * 2 * DIFF_DH

LANES = 128
SUBLANES = 8
EXPERT_BLOCK = 128
INV_BASE = 8
LOG2E = math.log2(math.e)
VMEM_LIMIT = 56 * 1024 * 1024


def _cparams(sem):
    return pltpu.CompilerParams(dimension_semantics=sem, vmem_limit_bytes=VMEM_LIMIT)


def _dot(a, b):
    return jnp.dot(a, b, preferred_element_type=F32)


def _dot_nt(a, b):
    return lax.dot_general(a, b, (((1,), (1,)), ((), ())), preferred_element_type=F32)


def _dot_tn(a, b):
    return lax.dot_general(a, b, (((0,), (0,)), ((), ())), preferred_element_type=F32)


def _silu(x):
    return x * jax.nn.sigmoid(x)


def _bf16_split3(x):
    rnd = lambda v: float(np.float32(v).astype(BF16).astype(np.float32))
    a = rnd(x)
    b = rnd(x - a)
    c = rnd(x - a - b)
    return a, b, c


def _ada_kernel(c_ref, w_ref, b_ref, o_ref):
    s = _silu(c_ref[...])
    o_ref[...] = jnp.dot(s, w_ref[...], precision=HIGHEST, preferred_element_type=F32) + b_ref[...]


def _ada(c_pad, w_ada, b_ada):
    n = w_ada.shape[1]
    tn = 1024
    return pl.pallas_call(
        _ada_kernel,
        out_shape=jax.ShapeDtypeStruct((c_pad.shape[0], n), F32),
        grid=(n // tn,),
        in_specs=[pl.BlockSpec(c_pad.shape, lambda j: (0, 0)),
                  pl.BlockSpec((D_MODEL, tn), lambda j: (0, j)),
                  pl.BlockSpec((1, tn), lambda j: (0, j))],
        out_specs=pl.BlockSpec((c_pad.shape[0], tn), lambda j: (0, j)),
        compiler_params=_cparams(("arbitrary",)),
        name="ada",
    )(c_pad, w_ada, b_ada)


def _inproj_kernel(x_ref, shift_ref, scale_ref, g_ref, wg_ref, wbd_ref, wa_ref, wgt_ref,
                   gdn_ref, bd_ref, att_ref, gate_ref):
    x = x_ref[...]
    y = x * lax.rsqrt(jnp.mean(x * x, axis=-1, keepdims=True) + NORM_EPS) * g_ref[...]
    h = (y * (1.0 + scale_ref[0]) + shift_ref[0]).astype(BF16)
    gdn_ref[...] = _dot(h, wg_ref[...])
    bd_ref[...] = _dot(h, wbd_ref[...])
    att_ref[...] = _dot(h, wa_ref[...])
    gate_ref[...] = _dot(h, wgt_ref[...])


def _inproj(x2, mod3, norm1_g, w_gdn, w_bd, w_att, w_gate, seq):
    t = x2.shape[0]
    tm = min(256, seq)
    tiles_per_b = seq // tm
    const = lambda i: (0, 0)
    wspec = lambda w: pl.BlockSpec(w.shape, const, pipeline_mode=pl.Buffered(1))
    row = lambda n: pl.BlockSpec((tm, n), lambda i: (i, 0))
    return pl.pallas_call(
        _inproj_kernel,
        out_shape=(jax.ShapeDtypeStruct((t, w_gdn.shape[1]), F32),
                   jax.ShapeDtypeStruct((t, LANES), F32),
                   jax.ShapeDtypeStruct((t, w_att.shape[1]), F32),
                   jax.ShapeDtypeStruct((t, w_gate.shape[1]), F32)),
        grid=(t // tm,),
        in_specs=[row(D_MODEL),
                  pl.BlockSpec((1, 1, D_MODEL), lambda i: (i // tiles_per_b, 0, 0)),
                  pl.BlockSpec((1, 1, D_MODEL), lambda i: (i // tiles_per_b, 0, 1)),
                  pl.BlockSpec((1, D_MODEL), const),
                  wspec(w_gdn), wspec(w_bd), wspec(w_att), wspec(w_gate)],
        out_specs=(row(w_gdn.shape[1]), row(LANES), row(w_att.shape[1]), row(w_gate.shape[1])),
        compiler_params=_cparams(("arbitrary",)),
        name="inproj",
    )(x2, mod3, mod3, norm1_g, w_gdn, w_bd, w_att, w_gate)


def _gdn_kernel(x_ref, bd_ref, cw_ref, ap_ref, ng_ref, o_ref, cbuf, state, *, nb, lg):
    step = pl.program_id(0)
    nc = lg // CHUNK

    @pl.when(step == 0)
    def _():
        cbuf[...] = jnp.zeros_like(cbuf)
        state[...] = jnp.zeros_like(state)

    ri = lax.broadcasted_iota(jnp.int32, (CHUNK, CHUNK), 0)
    ci = lax.broadcasted_iota(jnp.int32, (CHUNK, CHUNK), 1)
    causal = ri >= ci
    strict = ri > ci
    eye = (ri == ci).astype(F32)
    rl = lax.broadcasted_iota(jnp.int32, (lg, lg), 0)
    cl = lax.broadcasted_iota(jnp.int32, (lg, lg), 1)
    blocktri = ((rl >= cl) & ((rl // CHUNK) == (cl // CHUNK))).astype(F32)
    cw = cw_ref[...]
    a_row = ap_ref[0:1, :]
    dt_row = ap_ref[1:2, :]
    ng = ng_ref[...]

    units = []
    for b in range(nb):
        cbuf[b, SUBLANES:SUBLANES + lg, :] = x_ref[b, :, 0:CONV_CH]
        acc = cw[GDN_CONV - 1:GDN_CONV, :] * cbuf[b, SUBLANES:SUBLANES + lg, :]
        for j in range(GDN_CONV - 1):
            off = SUBLANES - (GDN_CONV - 1) + j
            acc = acc + cw[j:j + 1, :] * cbuf[b, off:off + lg, :]
        cbuf[b, 0:SUBLANES, :] = cbuf[b, lg:lg + SUBLANES, :]
        qkv = _silu(acc)

        bd = bd_ref[b]
        beta_t = jax.nn.sigmoid(bd)
        g_t = -jnp.exp(a_row) * jax.nn.softplus(bd + dt_row)
        gcum = jnp.dot(blocktri, g_t, precision=HIGHEST, preferred_element_type=F32)
        gcum_t = gcum.T
        egcum = jnp.exp(gcum)

        for h in range(GDN_HEADS):
            qh = qkv[:, h * GDN_DK:(h + 1) * GDN_DK]
            kh = qkv[:, GDN_QK + h * GDN_DK:GDN_QK + (h + 1) * GDN_DK]
            vh = qkv[:, 2 * GDN_QK + h * GDN_DV:2 * GDN_QK + (h + 1) * GDN_DV]
            qh = qh * lax.rsqrt(jnp.sum(qh * qh, axis=-1, keepdims=True) + NORM_EPS) * (GDN_DK ** -0.5)
            kh = kh * lax.rsqrt(jnp.sum(kh * kh, axis=-1, keepdims=True) + NORM_EPS)
            for c in range(nc):
                r0, r1 = c * CHUNK, (c + 1) * CHUNK
                q, k, v = qh[r0:r1], kh[r0:r1], vh[r0:r1]
                gc = gcum[r0:r1, 4 + h:5 + h]
                gr = gcum_t[4 + h:5 + h, r0:r1]
                egc = egcum[r0:r1, 4 + h:5 + h]
                bcol = beta_t[r0:r1, h:h + 1]
                gl = gcum[r1 - 1:r1, 4 + h:5 + h]
                kb = k.astype(BF16)
                units.append(dict(
                    b=b, h=h, c=c, kb=kb, bcol=bcol,
                    qkb=jnp.concatenate([q.astype(BF16), kb], axis=0),
                    decay=jnp.exp(jnp.where(causal, gc - gr, -jnp.inf)),
                    rhs=jnp.concatenate([v * bcol, k * (bcol * egc)], axis=-1).astype(BF16),
                    qg=(q * egc).astype(BF16), egl=jnp.exp(gl),
                    kdec=(k * jnp.exp(gl - gc)).astype(BF16)))

    for u in units:
        u["kq"] = _dot_nt(u["qkb"], u["kb"])
    def same_block(size):
        return (ri // size) == (ci // size)
    for u in units:
        lower = jnp.where(strict, u["bcol"] * u["kq"][CHUNK:] * u["decay"], 0.0)
        u["lower"] = lower
        diag = jnp.where(same_block(INV_BASE), lower, 0.0)
        u["xinv"] = eye - diag
        u["pw"] = diag.astype(BF16)
    for r in range(INV_BASE.bit_length() - 2):
        for u in units:
            u["pw"] = _dot(u["pw"], u["pw"]).astype(BF16)
        for u in units:
            u["xinv"] = u["xinv"] + _dot(u["xinv"].astype(BF16), u["pw"])
    size = INV_BASE
    while size < CHUNK:
        pair_off = same_block(2 * size) & jnp.logical_not(same_block(size))
        for u in units:
            u["xb"] = u["xinv"].astype(BF16)
            u["cx"] = _dot(jnp.where(pair_off, u["lower"], 0.0).astype(BF16), u["xb"]).astype(BF16)
        for u in units:
            u["xinv"] = u["xinv"] - _dot(u["xb"], u["cx"])
        size *= 2
    for u in units:
        u["sol"] = _dot(u["xinv"].astype(BF16), u["rhs"])
        u["a_qk"] = (u["kq"][:CHUNK] * u["decay"]).astype(BF16)

    st = {(b, h): state[b, h] for b in range(nb) for h in range(GDN_HEADS)}
    for c in range(nc):
        cu = [u for u in units if u["c"] == c]
        for u in cu:
            stb = st[(u["b"], u["h"])].astype(BF16)
            u["ws"] = _dot(jnp.concatenate([u["sol"][:, GDN_DV:].astype(BF16), u["qg"]], axis=0), stb)
        for u in cu:
            u["vnb"] = (u["sol"][:, :GDN_DV] - u["ws"][:CHUNK]).astype(BF16)
        for u in cu:
            u["out"] = u["ws"][CHUNK:] + _dot(u["a_qk"], u["vnb"])
        for u in cu:
            key = (u["b"], u["h"])
            st[key] = st[key] * u["egl"] + _dot_tn(u["kdec"], u["vnb"])
        for u in cu:
            b, h = u["b"], u["h"]
            r0, r1 = c * CHUNK, (c + 1) * CHUNK
            out = u["out"]
            on = out * lax.rsqrt(jnp.mean(out * out, axis=-1, keepdims=True) + NORM_EPS) * ng
            zh = x_ref[b, r0:r1, CONV_CH + h * GDN_DV:CONV_CH + (h + 1) * GDN_DV]
            o_ref[b, r0:r1, h * GDN_DV:(h + 1) * GDN_DV] = on * _silu(zh)
    for (b, h), s in st.items():
        state[b, h] = s


def _gdn(gdn_in3, bd3, conv_w, aparams, gdn_norm_g):
    nb, seq, _ = gdn_in3.shape
    lg = min(128, seq)
    const = lambda i: (0, 0)
    return pl.pallas_call(
        functools.partial(_gdn_kernel, nb=nb, lg=lg),
        out_shape=jax.ShapeDtypeStruct((nb, seq, GDN_V), F32),
        grid=(seq // lg,),
        in_specs=[pl.BlockSpec((nb, lg, gdn_in3.shape[2]), lambda i: (0, i, 0)),
                  pl.BlockSpec((nb, lg, LANES), lambda i: (0, i, 0)),
                  pl.BlockSpec(conv_w.shape, const),
                  pl.BlockSpec(aparams.shape, const),
                  pl.BlockSpec(gdn_norm_g.shape, const)],
        out_specs=pl.BlockSpec((nb, lg, GDN_V), lambda i: (0, i, 0)),
        scratch_shapes=[pltpu.VMEM((nb, lg + SUBLANES, CONV_CH), F32),
                        pltpu.VMEM((nb, GDN_HEADS, GDN_DK, GDN_DV), F32)],
        compiler_params=_cparams(("arbitrary",)),
        name="gdn",
    )(gdn_in3, bd3, conv_w, aparams, gdn_norm_g)


def _aprep_kernel(x_ref, gq_ref, gk_ref, qf_ref, bdm_ref, q_ref, k_ref, vt_ref, *, tk):
    x = x_ref[...]
    tm = x.shape[0]
    bdm = bdm_ref[...]

    def qknorm(v, gain):
        ms = jnp.dot(v * v, bdm, precision=HIGHEST, preferred_element_type=F32) * (1.0 / DIFF_DH)
        return v * lax.rsqrt(ms + NORM_EPS) * gain

    qn = qknorm(x[:, 0:DIFF_QK], gq_ref[...]) * (DIFF_DH ** -0.5 * LOG2E)
    kn = qknorm(x[:, DIFF_QK:2 * DIFF_QK], gk_ref[...])
    pos = pl.program_id(0) * tm + lax.broadcasted_iota(jnp.int32, (tm, DIFF_DH), 0)
    krel = pos % tk
    lane = lax.broadcasted_iota(jnp.int32, (tm, DIFF_DH), 1)
    hi = ((krel // 256) * 256).astype(F32)
    lo = (krel % 256).astype(F32)
    kfeat = jnp.where(lane < 6, jnp.where(lane % 2 == 0, hi, lo), 0.0)
    qfeat = qf_ref[...]
    qparts, kparts = [], []
    for g in range(2 * DIFF_HEADS):
        qparts += [qn[:, g * DIFF_DH:(g + 1) * DIFF_DH],
                   jnp.broadcast_to(qfeat[:, g * DIFF_DH:(g + 1) * DIFF_DH], (tm, DIFF_DH))]
        kparts += [kn[:, g * DIFF_DH:(g + 1) * DIFF_DH], kfeat]
    q_ref[...] = jnp.concatenate(qparts, axis=-1).astype(BF16)
    k_ref[...] = jnp.concatenate(kparts, axis=-1).astype(BF16)
    vt_ref[0, 0] = x[:, 2 * DIFF_QK:].T.astype(BF16)


def _aprep(att_in, gq, gk, qfeat, bdm, nb, seq, tk):
    t = att_in.shape[0]
    tm = tk
    nk = seq // tk
    const = lambda i: (0, 0)
    return pl.pallas_call(
        functools.partial(_aprep_kernel, tk=tk),
        out_shape=(jax.ShapeDtypeStruct((t, 2 * DIFF_QK), BF16),
                   jax.ShapeDtypeStruct((t, 2 * DIFF_QK), BF16),
                   jax.ShapeDtypeStruct((nb, nk, DIFF_V, tk), BF16)),
        grid=(t // tm,),
        in_specs=[pl.BlockSpec((tm, att_in.shape[1]), lambda i: (i, 0)),
                  pl.BlockSpec(gq.shape, const), pl.BlockSpec(gk.shape, const),
                  pl.BlockSpec(qfeat.shape, const), pl.BlockSpec(bdm.shape, const)],
        out_specs=(pl.BlockSpec((tm, 2 * DIFF_QK), lambda i: (i, 0)),
                   pl.BlockSpec((tm, 2 * DIFF_QK), lambda i: (i, 0)),
                   pl.BlockSpec((1, 1, DIFF_V, tk), lambda i: (i // nk, i % nk, 0, 0))),
        compiler_params=_cparams(("arbitrary",)),
        name="aprep",
    )(att_in, gq, gk, qfeat, bdm)


def _attn_kernel(slope_ref, q_ref, k_ref, vt_ref, lamv_ref, sg_ref, o_ref, m_s, l_s, acc_s, sa, sb,
                 *, tq, lam_init):
    h = pl.program_id(1)
    qi = pl.program_id(2)
    slope2 = slope_ref[h]
    dv = 2 * DIFF_DH
    qs = [q_ref[:, 0:LANES], q_ref[:, LANES:2 * LANES]]

    m_s[...] = jnp.full_like(m_s, -jnp.inf)
    l_s[...] = jnp.zeros_like(l_s)
    acc_s[...] = jnp.zeros_like(acc_s)

    def scores(kj):
        k0 = pl.multiple_of(kj * tq, tq)
        return tuple(_dot_nt(k_ref[pl.ds(k0, tq), c * LANES:(c + 1) * LANES], qs[c]) for c in range(2))

    def accumulate(ss, kj):
        vt = vt_ref[0, kj]
        off = slope2 * ((kj - qi) * tq).astype(F32)
        for c in range(2):
            s = ss[c]
            m_old = m_s[c]
            m_new = jnp.maximum(m_old, jnp.max(s, axis=0, keepdims=True) + off)
            alpha = jnp.exp2(m_old - m_new)
            p = jnp.exp2(s - (m_new - off))
            l_s[c] = alpha * l_s[c] + jnp.sum(p, axis=0, keepdims=True)
            acc_s[c] = alpha * acc_s[c] + _dot(vt, p.astype(BF16))
            m_s[c] = m_new

    kpos = lax.broadcasted_iota(jnp.int32, (tq, tq), 0)
    qpos = lax.broadcasted_iota(jnp.int32, (tq, tq), 1)
    allowed = (kpos // CHUNK) <= (qpos // CHUNK)
    fut = jnp.maximum(kpos - qpos, 0).astype(F32)
    add = jnp.where(allowed, (-2.0 * slope2) * fut, -jnp.inf)
    accumulate(tuple(s + add for s in scores(qi)), qi)

    def put(slot, ss):
        for c in range(2):
            slot[c] = ss[c]

    npairs = qi // 2
    put(sa, scores(0))

    def pair(j, carry):
        a = 2 * j
        put(sb, scores(a + 1))
        accumulate((sa[0], sa[1]), a)
        put(sa, scores(jnp.minimum(a + 2, qi - 1)))
        accumulate((sb[0], sb[1]), a + 1)
        return carry
    lax.fori_loop(0, npairs, pair, 0)

    @pl.when(qi % 2 == 1)
    def _():
        accumulate((sa[0], sa[1]), qi - 1)

    lq1, lk1, lq2, lk2 = (lamv_ref[i:i + 1, :] for i in range(4))
    lam = (jnp.exp(jnp.sum(lq1 * lk1, axis=-1, keepdims=True))
           - jnp.exp(jnp.sum(lq2 * lk2, axis=-1, keepdims=True)) + lam_init)
    o = acc_s[0] / l_s[0] - lam * (acc_s[1] / l_s[1])
    on = o * lax.rsqrt(jnp.mean(o * o, axis=0, keepdims=True) + NORM_EPS) * sg_ref[...] * (1.0 - lam_init)
    o_ref[...] = on.T


def _attn(slopes2, qa, ka, vt, lamv, sg_col, nb, seq, tq, lam_init):
    t = qa.shape[0]
    nq = seq // tq
    dv = 2 * DIFF_DH
    grid_spec = pltpu.PrefetchScalarGridSpec(
        num_scalar_prefetch=1,
        grid=(nb, DIFF_HEADS, nq),
        in_specs=[pl.BlockSpec((tq, 2 * LANES), lambda b, h, i, s: (b * nq + i, h)),
                  pl.BlockSpec((seq, 2 * LANES), lambda b, h, i, s: (b, h)),
                  pl.BlockSpec((1, nq, dv, tq), lambda b, h, i, s: (b, 0, h, 0)),
                  pl.BlockSpec(lamv.shape, lambda b, h, i, s: (0, 0)),
                  pl.BlockSpec(sg_col.shape, lambda b, h, i, s: (0, 0))],
        out_specs=pl.BlockSpec((tq, dv), lambda b, h, i, s: (b * nq + i, h)),
        scratch_shapes=[pltpu.VMEM((2, 1, tq), F32), pltpu.VMEM((2, 1, tq), F32),
                        pltpu.VMEM((2, dv, tq), F32),
                        pltpu.VMEM((2, tq, tq), F32), pltpu.VMEM((2, tq, tq), F32)],
    )
    return pl.pallas_call(
        functools.partial(_attn_kernel, tq=tq, lam_init=lam_init),
        out_shape=jax.ShapeDtypeStruct((t, DIFF_V), F32),
        grid_spec=grid_spec,
        compiler_params=_cparams(("arbitrary", "arbitrary", "arbitrary")),
        name="attn",
    )(slopes2, qa, ka, vt, lamv, sg_col)


def _pack_halves(x):
    n = x.shape[1] // 2
    lo = pltpu.bitcast(x[:, :n].astype(BF16).astype(F32), jnp.uint32)
    hi = pltpu.bitcast(x[:, n:].astype(BF16).astype(F32), jnp.uint32)
    return (lo >> 16) | (hi & jnp.uint32(0xFFFF0000))


def _unpack_halves(p):
    lo = pltpu.bitcast(p << 16, F32)
    hi = pltpu.bitcast(p & jnp.uint32(0xFFFF0000), F32)
    return lo, hi


def _post_kernel(x_ref, oa_ref, ob_ref, gate_ref, g1_ref, sh2_ref, sc2_ref, n2_ref,
                 woa_ref, wob_ref, wout_ref, x1_ref, h2_ref, hp_ref):
    ya = _dot(oa_ref[...].astype(BF16), woa_ref[...])
    yb = _dot(ob_ref[...].astype(BF16), wob_ref[...])
    merged = jax.nn.sigmoid(gate_ref[:, 0:D_MODEL]) * ya + jax.nn.sigmoid(gate_ref[:, D_MODEL:]) * yb
    y = _dot(merged.astype(BF16), wout_ref[...])
    x1 = x_ref[...] + g1_ref[0] * y
    x1_ref[...] = x1
    n = x1 * lax.rsqrt(jnp.mean(x1 * x1, axis=-1, keepdims=True) + NORM_EPS) * n2_ref[...]
    h2 = n * (1.0 + sc2_ref[0]) + sh2_ref[0]
    h2_ref[...] = h2
    hp_ref[...] = _pack_halves(h2)


def _post(x2, oa, ob, gates, mod3, norm2_g, woa, wob, wout, seq):
    t = x2.shape[0]
    tm = min(256, seq)
    tpb = seq // tm
    const = lambda i: (0, 0)
    row = lambda n: pl.BlockSpec((tm, n), lambda i: (i, 0))
    modspec = lambda j: pl.BlockSpec((1, 1, D_MODEL), lambda i: (i // tpb, 0, j))
    wspec = lambda w: pl.BlockSpec(w.shape, const, pipeline_mode=pl.Buffered(1))
    return pl.pallas_call(
        _post_kernel,
        out_shape=(jax.ShapeDtypeStruct((t, D_MODEL), F32), jax.ShapeDtypeStruct((t, D_MODEL), F32),
                   jax.ShapeDtypeStruct((t, D_MODEL // 2), jnp.uint32)),
        grid=(t // tm,),
        in_specs=[row(D_MODEL), row(GDN_V), row(DIFF_V), row(2 * D_MODEL),
                  modspec(2), modspec(3), modspec(4), pl.BlockSpec((1, D_MODEL), const),
                  wspec(woa), wspec(wob), wspec(wout)],
        out_specs=(row(D_MODEL), row(D_MODEL), row(D_MODEL // 2)),
        compiler_params=_cparams(("arbitrary",)),
        name="post",
    )(x2, oa, ob, gates, mod3, mod3, mod3, norm2_g, woa, wob, wout)


def _router_kernel(h_ref, wr_ref, rb_ref, idx_ref, w_ref, cnt_ref, cnt_s):
    step = pl.program_id(0)

    @pl.when(step == 0)
    def _():
        cnt_s[...] = jnp.zeros_like(cnt_s)

    tm = h_ref.shape[0]
    logits = jnp.dot(h_ref[...], wr_ref[...], precision=HIGHEST, preferred_element_type=F32)
    scores = jax.nn.sigmoid(logits)
    choice = scores + rb_ref[...]
    lane = lax.broadcasted_iota(jnp.int32, (tm, N_EXPERTS), 1)
    grp = lane // GROUP_SIZE
    neg = -jnp.inf

    def first_max(v):
        m = jnp.max(v, axis=-1, keepdims=True)
        i = jnp.min(jnp.where(v == m, lane, N_EXPERTS), axis=-1, keepdims=True)
        return m, i

    gl = lax.broadcasted_iota(jnp.int32, (tm, LANES), 1)
    gscore = jnp.full((tm, LANES), neg, F32)
    for g in range(N_GROUPS):
        vg = jnp.where(grp == g, choice, neg)
        m1, i1 = first_max(vg)
        m2 = jnp.max(jnp.where(lane == i1, neg, vg), axis=-1, keepdims=True)
        gscore = jnp.where(gl == g, m1 + m2, gscore)
    emask = jnp.zeros((tm, N_EXPERTS), jnp.bool_)
    for _ in range(TOPK_GROUPS):
        m = jnp.max(gscore, axis=-1, keepdims=True)
        gi = jnp.min(jnp.where(gscore == m, gl, LANES), axis=-1, keepdims=True)
        emask = emask | (grp == gi)
        gscore = jnp.where(gl == gi, neg, gscore)
    masked = jnp.where(emask, choice, neg)
    idxs = jnp.zeros((tm, LANES), jnp.int32)
    ws = jnp.zeros((tm, LANES), F32)
    sel = jnp.zeros((tm, N_EXPERTS), F32)
    for k in range(TOP_K):
        _, i = first_max(masked)
        hit = lane == i
        wk = jnp.sum(jnp.where(hit, scores, 0.0), axis=-1, keepdims=True)
        idxs = jnp.where(gl == k, i, idxs)
        ws = jnp.where(gl == k, wk, ws)
        sel = jnp.where(hit, 1.0, sel)
        masked = jnp.where(hit, neg, masked)
    wsum = jnp.sum(ws, axis=-1, keepdims=True)
    idx_ref[...] = idxs
    w_ref[...] = ws / wsum * ROUTED_SCALE
    cnt_s[...] = cnt_s[...] + jnp.sum(sel, axis=0, keepdims=True)
    cnt_ref[...] = cnt_s[...]


def _router(h2, w_router, rbias):
    t = h2.shape[0]
    tm = min(256, t)
    const = lambda i: (0, 0)
    return pl.pallas_call(
        _router_kernel,
        out_shape=(jax.ShapeDtypeStruct((t, LANES), jnp.int32),
                   jax.ShapeDtypeStruct((t, LANES), F32),
                   jax.ShapeDtypeStruct((1, N_EXPERTS), F32)),
        grid=(t // tm,),
        in_specs=[pl.BlockSpec((tm, D_MODEL), lambda i: (i, 0)),
                  pl.BlockSpec(w_router.shape, const), pl.BlockSpec(rbias.shape, const)],
        out_specs=(pl.BlockSpec((tm, LANES), lambda i: (i, 0)),
                   pl.BlockSpec((tm, LANES), lambda i: (i, 0)),
                   pl.BlockSpec((1, N_EXPERTS), const)),
        scratch_shapes=[pltpu.VMEM((1, N_EXPERTS), F32)],
        compiler_params=_cparams(("arbitrary",)),
        name="router",
    )(h2, w_router, rbias)


def _rank_kernel(idx_ref, ps_ref, dest_ref, run_s):
    step = pl.program_id(0)

    @pl.when(step == 0)
    def _():
        run_s[...] = jnp.zeros_like(run_s)

    tm = idx_ref.shape[0]
    idx = idx_ref[...]
    lane = lax.broadcasted_iota(jnp.int32, (tm, N_EXPERTS), 1)
    gl = lax.broadcasted_iota(jnp.int32, (tm, LANES), 1)
    hits = [lane == idx[:, k:k + 1] for k in range(TOP_K)]
    sel = jnp.zeros((tm, N_EXPERTS), F32)
    for hit in hits:
        sel = jnp.where(hit, 1.0, sel)
    ri = lax.broadcasted_iota(jnp.int32, (tm, tm), 0)
    ci = lax.broadcasted_iota(jnp.int32, (tm, tm), 1)
    before = _dot((ri > ci).astype(BF16), sel.astype(BF16))
    base = before + run_s[...] + ps_ref[...]
    dest = jnp.zeros((tm, LANES), F32)
    for k, hit in enumerate(hits):
        dk = jnp.sum(jnp.where(hit, base, 0.0), axis=-1, keepdims=True)
        dest = jnp.where(gl == k, dk, dest)
    dest_ref[...] = dest.astype(jnp.int32)
    run_s[...] = run_s[...] + jnp.sum(sel, axis=0, keepdims=True)


def _rank(idx, pstart):
    t = idx.shape[0]
    tm = min(256, t)
    const = lambda i: (0, 0)
    return pl.pallas_call(
        _rank_kernel,
        out_shape=jax.ShapeDtypeStruct((t, LANES), jnp.int32),
        grid=(t // tm,),
        in_specs=[pl.BlockSpec((tm, LANES), lambda i: (i, 0)), pl.BlockSpec(pstart.shape, const)],
        out_specs=pl.BlockSpec((tm, LANES), lambda i: (i, 0)),
        scratch_shapes=[pltpu.VMEM((1, N_EXPERTS), F32)],
        compiler_params=_cparams(("arbitrary",)),
        name="rank",
    )(idx, pstart)


def _row_copy(src, dst, sem):
    return pltpu.make_async_copy(src, dst, sem)


def _dispatch_kernel(dest_ref, h_ref, xs_in, xs_ref, sem, *, tm):
    del xs_in

    def start_row(r, c):
        for k in range(TOP_K):
            d = dest_ref[r * TOP_K + k]
            _row_copy(h_ref.at[pl.ds(r, 1), :], xs_ref.at[pl.ds(d, 1), :], sem).start(priority=k % 2)
        return c

    lax.fori_loop(0, tm, start_row, 0)
    for k in range(TOP_K):
        _row_copy(h_ref, xs_ref.at[pl.ds(0, tm), :], sem).wait()


def _dispatch(dest_flat, h2, xs0):
    t = h2.shape[0]
    tm = min(256, t)
    return pl.pallas_call(
        functools.partial(_dispatch_kernel, tm=tm),
        out_shape=jax.ShapeDtypeStruct(xs0.shape, xs0.dtype),
        grid=(t // tm,),
        in_specs=[pl.BlockSpec((tm * TOP_K,), lambda i: (i,), memory_space=pltpu.SMEM),
                  pl.BlockSpec((tm, h2.shape[1]), lambda i: (i, 0)),
                  pl.BlockSpec(memory_space=pl.ANY)],
        out_specs=pl.BlockSpec(memory_space=pl.ANY),
        scratch_shapes=[pltpu.SemaphoreType.DMA(())],
        input_output_aliases={2: 0},
        compiler_params=_cparams(("arbitrary",)),
        name="dispatch",
    )(dest_flat, h2, xs0)


EXPERT_RING = 4
WEIGHT_RING = 3


def _experts_kernel(start_ref, count_ref, xs_ref, wgu_ref, wdn_ref, ys_ref,
                    wgu_f, wdn_f, wgu_s, wdn_s, xbuf, ybuf, sem_w, sem_in, sem_out, *, nblk_total):
    half = D_MODEL // 2
    last_e = N_EXPERTS - 1
    used = start_ref[last_e] + count_ref[last_e]

    def rows(g):
        return pl.ds(pl.multiple_of(g * EXPERT_BLOCK, EXPERT_BLOCK), EXPERT_BLOCK)

    def in_copy(g):
        slot = g % EXPERT_RING
        return pltpu.make_async_copy(xs_ref.at[rows(jnp.minimum(g, nblk_total - 1)), :], xbuf.at[slot],
                                     sem_in.at[slot])

    def out_copy(g):
        slot = g % EXPERT_RING
        return pltpu.make_async_copy(ybuf.at[slot], ys_ref.at[rows(g), :], sem_out.at[slot])

    def w_copies(e):
        slot = e % WEIGHT_RING
        return (pltpu.make_async_copy(wgu_ref.at[e], wgu_f.at[slot], sem_w.at[0, slot]),
                pltpu.make_async_copy(wdn_ref.at[e], wdn_f.at[slot], sem_w.at[1, slot]))

    for e0 in range(WEIGHT_RING - 1):
        for cp in w_copies(e0):
            cp.start()
    for g in range(EXPERT_RING - 1):
        in_copy(g).start()

    def expert(e, carry):
        for cp in w_copies(e):
            cp.wait()

        @pl.when(e + WEIGHT_RING - 1 <= last_e)
        def _():
            for cp in w_copies(e + WEIGHT_RING - 1):
                cp.start()

        slot = e % WEIGHT_RING
        wgu_s[...] = wgu_f[slot].astype(BF16)
        wdn_s[...] = wdn_f[slot].astype(BF16)
        first = start_ref[e]

        def block(i, c):
            g = first + i
            in_copy(g).wait()
            in_copy(g + EXPERT_RING - 1).start()

            @pl.when(g >= EXPERT_RING)
            def _():
                out_copy(g - EXPERT_RING).wait()

            s = g % EXPERT_RING
            lo, hi = _unpack_halves(xbuf[s])
            gu = _dot(lo.astype(BF16), wgu_s[0:half, :]) + _dot(hi.astype(BF16), wgu_s[half:, :])
            act = _silu(gu[:, :EXPERT_FF]) * gu[:, EXPERT_FF:]
            ybuf[s] = _pack_halves(_dot(act.astype(BF16), wdn_s[...]))
            out_copy(g).start()
            return c

        lax.fori_loop(0, count_ref[e], block, 0)
        return carry

    lax.fori_loop(0, N_EXPERTS, expert, 0)

    for j in range(EXPERT_RING - 1):
        in_copy(used + j).wait()
    for j in range(EXPERT_RING):
        @pl.when(used > j)
        def _():
            out_copy(used - 1 - j).wait()

    ybuf[0] = jnp.zeros((EXPERT_BLOCK, half), jnp.uint32)

    def zcopy(g):
        return pltpu.make_async_copy(ybuf.at[0], ys_ref.at[rows(g), :], sem_out.at[0])

    def zstart(g, c):
        zcopy(g).start()
        return c

    def zwait(g, c):
        zcopy(g).wait()
        return c

    lax.fori_loop(used, nblk_total, zstart, 0)
    lax.fori_loop(used, nblk_total, zwait, 0)


def _experts(blk_start, blk_count, xs, w_gu, w_dn):
    rows, half = xs.shape
    nblk = rows // EXPERT_BLOCK
    grid_spec = pltpu.PrefetchScalarGridSpec(
        num_scalar_prefetch=2,
        grid=(1,),
        in_specs=[pl.BlockSpec(memory_space=pl.ANY), pl.BlockSpec(memory_space=pl.ANY),
                  pl.BlockSpec(memory_space=pl.ANY)],
        out_specs=pl.BlockSpec(memory_space=pl.ANY),
        scratch_shapes=[pltpu.VMEM((WEIGHT_RING, D_MODEL, 2 * EXPERT_FF), F32),
                        pltpu.VMEM((WEIGHT_RING, EXPERT_FF, D_MODEL), F32),
                        pltpu.VMEM((D_MODEL, 2 * EXPERT_FF), BF16), pltpu.VMEM((EXPERT_FF, D_MODEL), BF16),
                        pltpu.VMEM((EXPERT_RING, EXPERT_BLOCK, half), jnp.uint32),
                        pltpu.VMEM((EXPERT_RING, EXPERT_BLOCK, half), jnp.uint32),
                        pltpu.SemaphoreType.DMA((2, WEIGHT_RING)), pltpu.SemaphoreType.DMA((EXPERT_RING,)),
                        pltpu.SemaphoreType.DMA((EXPERT_RING,))],
    )
    return pl.pallas_call(
        functools.partial(_experts_kernel, nblk_total=nblk),
        out_shape=jax.ShapeDtypeStruct((rows, half), jnp.uint32),
        grid_spec=grid_spec,
        compiler_params=_cparams(("arbitrary",)),
        name="experts",
    )(blk_start, blk_count, xs, w_gu, w_dn)


def _combine_kernel(dest0_ref, destn_ref, x1_ref, h_ref, w_ref, g2_ref, wsgu_ref, wsdn_ref, ys_ref, o_ref,
                    buf, sem, *, tm):
    i = pl.program_id(0)
    slot = i % 2

    def gather(dref, s):
        def start_row(r, c):
            for k in range(TOP_K):
                d = dref[r * TOP_K + k]
                _row_copy(ys_ref.at[pl.ds(d, 1), :], buf.at[s, k, pl.ds(r, 1), :], sem.at[s]).start(priority=k % 2)
            return c
        lax.fori_loop(0, tm, start_row, 0)

    @pl.when(i == 0)
    def _():
        gather(dest0_ref, 0)

    @pl.when(i + 1 < pl.num_programs(0))
    def _():
        gather(destn_ref, 1 - slot)

    su = _dot(h_ref[...].astype(BF16), wsgu_ref[...])
    y = _dot((_silu(su[:, :SHARED_FF]) * su[:, SHARED_FF:]).astype(BF16), wsdn_ref[...])
    for k in range(TOP_K):
        _row_copy(ys_ref.at[pl.ds(0, tm), :], buf.at[slot, k], sem.at[slot]).wait()
    w = w_ref[...]
    half = D_MODEL // 2
    ylo, yhi = y[:, :half], y[:, half:]
    for k in range(TOP_K):
        lo, hi = _unpack_halves(buf[slot, k])
        wk = w[:, k:k + 1]
        ylo = ylo + wk * lo
        yhi = yhi + wk * hi
    g2 = g2_ref[0]
    o_ref[:, :half] = x1_ref[:, :half] + g2[:, :half] * ylo
    o_ref[:, half:] = x1_ref[:, half:] + g2[:, half:] * yhi


def _combine(dest_flat, x1, h2, wts, mod3, wsgu, wsdn, ys, seq):
    t = x1.shape[0]
    tm = min(128, seq)
    tpb = seq // tm
    nsteps = t // tm
    const = lambda i: (0, 0)
    row = lambda n: pl.BlockSpec((tm, n), lambda i: (i, 0))
    return pl.pallas_call(
        functools.partial(_combine_kernel, tm=tm),
        out_shape=jax.ShapeDtypeStruct((t, D_MODEL), F32),
        grid=(nsteps,),
        in_specs=[pl.BlockSpec((tm * TOP_K,), lambda i: (0,), memory_space=pltpu.SMEM),
                  pl.BlockSpec((tm * TOP_K,), lambda i: (jnp.minimum(i + 1, nsteps - 1),), memory_space=pltpu.SMEM),
                  row(D_MODEL), row(D_MODEL), row(LANES),
                  pl.BlockSpec((1, 1, D_MODEL), lambda i: (i // tpb, 0, 5)),
                  pl.BlockSpec(wsgu.shape, const), pl.BlockSpec(wsdn.shape, const),
                  pl.BlockSpec(memory_space=pl.ANY)],
        out_specs=row(D_MODEL),
        scratch_shapes=[pltpu.VMEM((2, TOP_K, tm, D_MODEL // 2), jnp.uint32), pltpu.SemaphoreType.DMA((2,))],
        compiler_params=_cparams(("arbitrary",)),
        name="combine",
    )(dest_flat, dest_flat, x1, h2, wts, mod3, wsgu, wsdn, ys)


def _layer(x, c, layer, w_ada, b_ada, norm1_g, w_in, conv_w, a_log, dt_bias, gdn_norm_g, w_o_gdn,
           q_norm_g, k_norm_g, lambda_q1, lambda_k1, lambda_q2, lambda_k2, subln_g, w_o_diff,
           w_out, norm2_g, w_router, router_bias, w_exp_gate_up, w_exp_down,
           w_shared_gate_up, w_shared_down):
    nb, seq, d = x.shape
    t = nb * seq
    lam_init = 0.8 - 0.6 * math.exp(-0.3 * layer)
    x2 = x.reshape(t, d)

    c_pad = jnp.pad(c, ((0, SUBLANES - nb % SUBLANES if nb % SUBLANES else 0), (0, 0)))
    mod = _ada(c_pad, w_ada, b_ada.reshape(1, -1))[:nb]
    mod3 = mod.reshape(nb, 1, 6 * d)

    o_bd = 2 * GDN_QK + 2 * GDN_V
    o_att = o_bd + 2 * GDN_HEADS
    o_gate = o_att + 2 * DIFF_QK + DIFF_V
    w_gdn = w_in[:, :o_bd].astype(BF16)
    w_bd = jnp.pad(w_in[:, o_bd:o_att], ((0, 0), (0, LANES - 2 * GDN_HEADS))).astype(BF16)
    w_att = w_in[:, o_att:o_gate].astype(BF16)
    w_gate = w_in[:, o_gate:].astype(BF16)
    gdn_in, bd, att_in, gates = _inproj(x2, mod3, norm1_g.reshape(1, d), w_gdn, w_bd, w_att, w_gate, seq)

    aparams = jnp.zeros((SUBLANES, LANES), F32)
    aparams = aparams.at[0, GDN_HEADS:2 * GDN_HEADS].set(a_log).at[1, GDN_HEADS:2 * GDN_HEADS].set(dt_bias)
    oa = _gdn(gdn_in.reshape(nb, seq, -1), bd.reshape(nb, seq, LANES), conv_w, aparams,
              gdn_norm_g.reshape(1, GDN_DV)).reshape(t, GDN_V)

    tq = min(512, seq)
    slopes = [2.0 ** (-8.0 * (h + 1) / DIFF_HEADS) for h in range(DIFF_HEADS)]
    c3 = _bf16_split3(LOG2E)
    qfeat = np.zeros((1, 2 * DIFF_HEADS * DIFF_DH), np.float32)
    for g in range(2 * DIFF_HEADS):
        for j in range(6):
            qfeat[0, g * DIFF_DH + j] = slopes[g // 2] * c3[j // 2]
    grp = np.arange(DIFF_QK) // DIFF_DH
    bdm = jnp.asarray((grp[:, None] == grp[None, :]).astype(np.float32))
    qa, ka, vt = _aprep(att_in, jnp.tile(q_norm_g, 2 * DIFF_HEADS).reshape(1, -1),
                        jnp.tile(k_norm_g, 2 * DIFF_HEADS).reshape(1, -1), jnp.asarray(qfeat), bdm, nb, seq, tq)
    slopes2 = jnp.asarray([s * (c3[0] + c3[1] + c3[2]) for s in slopes], F32)
    lamv = jnp.zeros((SUBLANES, DIFF_DH), F32)
    lamv = lamv.at[0].set(lambda_q1).at[1].set(lambda_k1).at[2].set(lambda_q2).at[3].set(lambda_k2)
    ob = _attn(slopes2, qa, ka, vt, lamv, subln_g.reshape(-1, 1), nb, seq, tq, lam_init)

    x1, h2, hp = _post(x2, oa, ob, gates, mod3, norm2_g.reshape(1, d), w_o_gdn.astype(BF16),
                       w_o_diff.astype(BF16), w_out.astype(BF16), seq)

    idx, wts, counts = _router(h2, w_router, router_bias.reshape(1, -1))
    cnt = counts[0].astype(jnp.int32)
    padded = (cnt + EXPERT_BLOCK - 1) // EXPERT_BLOCK * EXPERT_BLOCK
    pends = jnp.cumsum(padded)
    pstart = (pends - padded).astype(F32).reshape(1, -1)
    nblk = -(-(t * TOP_K) // EXPERT_BLOCK) + N_EXPERTS
    blk_start = ((pends - padded) // EXPERT_BLOCK).astype(jnp.int32)
    blk_count = (padded // EXPERT_BLOCK).astype(jnp.int32)
    dest = _rank(idx, pstart)
    dest_flat = dest[:, :TOP_K].reshape(-1)
    xs = _dispatch(dest_flat, hp, jnp.zeros((nblk * EXPERT_BLOCK, d // 2), jnp.uint32))
    ys = _experts(blk_start, blk_count, xs, w_exp_gate_up, w_exp_down)
    out = _combine(dest_flat, x1, h2, wts, mod3, w_shared_gate_up.astype(BF16),
                   w_shared_down.astype(BF16), ys, seq)
    return out.reshape(nb, seq, d)


def kernel(x, c, w_ada, b_ada, norm1_g, w_in, conv_w, a_log, dt_bias, gdn_norm_g, w_o_gdn, q_norm_g, k_norm_g, lambda_q1, lambda_k1, lambda_q2, lambda_k2, subln_g, w_o_diff, w_out, norm2_g, w_router, router_bias, w_exp_gate_up, w_exp_down, w_shared_gate_up, w_shared_down):
    params = (w_ada, b_ada, norm1_g, w_in, conv_w, a_log, dt_bias, gdn_norm_g, w_o_gdn, q_norm_g,
              k_norm_g, lambda_q1, lambda_k1, lambda_q2, lambda_k2, subln_g, w_o_diff, w_out, norm2_g,
              w_router, router_bias, w_exp_gate_up, w_exp_down, w_shared_gate_up, w_shared_down)
    for layer in range(w_ada.shape[0]):
        x = _layer(x, c, layer, *(p[layer] for p in params))
    return x
```

```python
import functools
import math

import jax
import jax.numpy as jnp
import numpy as np
from jax import lax
from jax.experimental import pallas as pl
from jax.experimental.pallas import tpu as pltpu
from jax.experimental.pallas import tpu_sc as plsc

F32 = jnp.float32
BF16 = jnp.bfloat16
HIGHEST = lax.Precision.HIGHEST

D_MODEL = 1024
CHUNK = 64
GDN_HEADS = 4
GDN_DK = 128
GDN_DV = 128
GDN_CONV = 4
DIFF_HEADS = 4
DIFF_DH = 64
N_EXPERTS = 256
TOP_K = 8
N_GROUPS = 8
TOPK_GROUPS = 4
EXPERT_FF = 256
SHARED_FF = 256
ROUTED_SCALE = 2.5
NORM_EPS = 1e-6
GROUP_SIZE = N_EXPERTS // N_GROUPS

GDN_QK = GDN_HEADS * GDN_DK
GDN_V = GDN_HEADS * GDN_DV
CONV_CH = 2 * GDN_QK + GDN_V
DIFF_QK = DIFF_HEADS * 2 * DIFF_DH
DIFF_V = DIFF_HEADS * 2 * DIFF_DH

LANES = 128
SUBLANES = 8
EXPERT_BLOCK = 128
INV_BASE = 8
LOG2E = math.log2(math.e)
VMEM_LIMIT = 56 * 1024 * 1024


def _cparams(sem):
    return pltpu.CompilerParams(dimension_semantics=sem, vmem_limit_bytes=VMEM_LIMIT)


def _dot(a, b):
    return jnp.dot(a, b, preferred_element_type=F32)


def _dot_nt(a, b):
    return lax.dot_general(a, b, (((1,), (1,)), ((), ())), preferred_element_type=F32)


def _dot_tn(a, b):
    return lax.dot_general(a, b, (((0,), (0,)), ((), ())), preferred_element_type=F32)


def _silu(x):
    return x * jax.nn.sigmoid(x)


def _bf16_split3(x):
    rnd = lambda v: float(np.float32(v).astype(BF16).astype(np.float32))
    a = rnd(x)
    b = rnd(x - a)
    c = rnd(x - a - b)
    return a, b, c


def _ada_kernel(c_ref, w_ref, b_ref, o_ref):
    s = _silu(c_ref[...])
    o_ref[...] = jnp.dot(s, w_ref[...], precision=HIGHEST, preferred_element_type=F32) + b_ref[...]


def _ada(c_pad, w_ada, b_ada):
    n = w_ada.shape[1]
    tn = 1024
    return pl.pallas_call(
        _ada_kernel,
        out_shape=jax.ShapeDtypeStruct((c_pad.shape[0], n), F32),
        grid=(n // tn,),
        in_specs=[pl.BlockSpec(c_pad.shape, lambda j: (0, 0)),
                  pl.BlockSpec((D_MODEL, tn), lambda j: (0, j)),
                  pl.BlockSpec((1, tn), lambda j: (0, j))],
        out_specs=pl.BlockSpec((c_pad.shape[0], tn), lambda j: (0, j)),
        compiler_params=_cparams(("arbitrary",)),
        name="ada",
    )(c_pad, w_ada, b_ada)


def _inproj_kernel(x_ref, shift_ref, scale_ref, g_ref, wg_ref, wbd_ref, wa_ref, wgt_ref,
                   gdn_ref, bd_ref, att_ref, gate_ref):
    x = x_ref[...]
    y = x * lax.rsqrt(jnp.mean(x * x, axis=-1, keepdims=True) + NORM_EPS) * g_ref[...]
    h = (y * (1.0 + scale_ref[0]) + shift_ref[0]).astype(BF16)
    gdn_ref[...] = _dot(h, wg_ref[...])
    bd_ref[...] = _dot(h, wbd_ref[...])
    att_ref[...] = _dot(h, wa_ref[...])
    gate_ref[...] = _dot(h, wgt_ref[...])


def _inproj(x2, mod3, norm1_g, w_gdn, w_bd, w_att, w_gate, seq):
    t = x2.shape[0]
    tm = min(256, seq)
    tiles_per_b = seq // tm
    const = lambda i: (0, 0)
    wspec = lambda w: pl.BlockSpec(w.shape, const, pipeline_mode=pl.Buffered(1))
    row = lambda n: pl.BlockSpec((tm, n), lambda i: (i, 0))
    return pl.pallas_call(
        _inproj_kernel,
        out_shape=(jax.ShapeDtypeStruct((t, w_gdn.shape[1]), F32),
                   jax.ShapeDtypeStruct((t, LANES), F32),
                   jax.ShapeDtypeStruct((t, w_att.shape[1]), F32),
                   jax.ShapeDtypeStruct((t, w_gate.shape[1]), F32)),
        grid=(t // tm,),
        in_specs=[row(D_MODEL),
                  pl.BlockSpec((1, 1, D_MODEL), lambda i: (i // tiles_per_b, 0, 0)),
                  pl.BlockSpec((1, 1, D_MODEL), lambda i: (i // tiles_per_b, 0, 1)),
                  pl.BlockSpec((1, D_MODEL), const),
                  wspec(w_gdn), wspec(w_bd), wspec(w_att), wspec(w_gate)],
        out_specs=(row(w_gdn.shape[1]), row(LANES), row(w_att.shape[1]), row(w_gate.shape[1])),
        compiler_params=_cparams(("arbitrary",)),
        name="inproj",
    )(x2, mod3, mod3, norm1_g, w_gdn, w_bd, w_att, w_gate)


def _gdn_kernel(x_ref, bd_ref, cw_ref, ap_ref, ng_ref, o_ref, cbuf, state, *, nb, lg):
    step = pl.program_id(0)
    nc = lg // CHUNK

    @pl.when(step == 0)
    def _():
        cbuf[...] = jnp.zeros_like(cbuf)
        state[...] = jnp.zeros_like(state)

    ri = lax.broadcasted_iota(jnp.int32, (CHUNK, CHUNK), 0)
    ci = lax.broadcasted_iota(jnp.int32, (CHUNK, CHUNK), 1)
    causal = ri >= ci
    strict = ri > ci
    eye = (ri == ci).astype(F32)
    rl = lax.broadcasted_iota(jnp.int32, (lg, lg), 0)
    cl = lax.broadcasted_iota(jnp.int32, (lg, lg), 1)
    blocktri = ((rl >= cl) & ((rl // CHUNK) == (cl // CHUNK))).astype(F32)
    cw = cw_ref[...]
    a_row = ap_ref[0:1, :]
    dt_row = ap_ref[1:2, :]
    ng = ng_ref[...]

    units = []
    for b in range(nb):
        cbuf[b, SUBLANES:SUBLANES + lg, :] = x_ref[b, :, 0:CONV_CH]
        acc = cw[GDN_CONV - 1:GDN_CONV, :] * cbuf[b, SUBLANES:SUBLANES + lg, :]
        for j in range(GDN_CONV - 1):
            off = SUBLANES - (GDN_CONV - 1) + j
            acc = acc + cw[j:j + 1, :] * cbuf[b, off:off + lg, :]
        cbuf[b, 0:SUBLANES, :] = cbuf[b, lg:lg + SUBLANES, :]
        qkv = _silu(acc)

        bd = bd_ref[b]
        beta_t = jax.nn.sigmoid(bd)
        g_t = -jnp.exp(a_row) * jax.nn.softplus(bd + dt_row)
        gcum = jnp.dot(blocktri, g_t, precision=HIGHEST, preferred_element_type=F32)
        gcum_t = gcum.T
        egcum = jnp.exp(gcum)

        for h in range(GDN_HEADS):
            qh = qkv[:, h * GDN_DK:(h + 1) * GDN_DK]
            kh = qkv[:, GDN_QK + h * GDN_DK:GDN_QK + (h + 1) * GDN_DK]
            vh = qkv[:, 2 * GDN_QK + h * GDN_DV:2 * GDN_QK + (h + 1) * GDN_DV]
            qh = qh * lax.rsqrt(jnp.sum(qh * qh, axis=-1, keepdims=True) + NORM_EPS) * (GDN_DK ** -0.5)
            kh = kh * lax.rsqrt(jnp.sum(kh * kh, axis=-1, keepdims=True) + NORM_EPS)
            for c in range(nc):
                r0, r1 = c * CHUNK, (c + 1) * CHUNK
                q, k, v = qh[r0:r1], kh[r0:r1], vh[r0:r1]
                gc = gcum[r0:r1, 4 + h:5 + h]
                gr = gcum_t[4 + h:5 + h, r0:r1]
                egc = egcum[r0:r1, 4 + h:5 + h]
                bcol = beta_t[r0:r1, h:h + 1]
                gl = gcum[r1 - 1:r1, 4 + h:5 + h]
                kb = k.astype(BF16)
                units.append(dict(
                    b=b, h=h, c=c, kb=kb, bcol=bcol,
                    qkb=jnp.concatenate([q.astype(BF16), kb], axis=0),
                    decay=jnp.exp(jnp.where(causal, gc - gr, -jnp.inf)),
                    rhs=jnp.concatenate([v * bcol, k * (bcol * egc)], axis=-1).astype(BF16),
                    qg=(q * egc).astype(BF16), egl=jnp.exp(gl),
                    kdec=(k * jnp.exp(gl - gc)).astype(BF16)))

    for u in units:
        u["kq"] = _dot_nt(u["qkb"], u["kb"])
    def same_block(size):
        return (ri // size) == (ci // size)
    for u in units:
        lower = jnp.where(strict, u["bcol"] * u["kq"][CHUNK:] * u["decay"], 0.0)
        u["lower"] = lower
        diag = jnp.where(same_block(INV_BASE), lower, 0.0)
        u["xinv"] = eye - diag
        u["pw"] = diag.astype(BF16)
    for r in range(INV_BASE.bit_length() - 2):
        for u in units:
            u["pw"] = _dot(u["pw"], u["pw"]).astype(BF16)
        for u in units:
            u["xinv"] = u["xinv"] + _dot(u["xinv"].astype(BF16), u["pw"])
    size = INV_BASE
    while size < CHUNK:
        pair_off = same_block(2 * size) & jnp.logical_not(same_block(size))
        for u in units:
            u["xb"] = u["xinv"].astype(BF16)
            u["cx"] = _dot(jnp.where(pair_off, u["lower"], 0.0).astype(BF16), u["xb"]).astype(BF16)
        for u in units:
            u["xinv"] = u["xinv"] - _dot(u["xb"], u["cx"])
        size *= 2
    for u in units:
        u["sol"] = _dot(u["xinv"].astype(BF16), u["rhs"])
        u["a_qk"] = (u["kq"][:CHUNK] * u["decay"]).astype(BF16)

    st = {(b, h): state[b, h] for b in range(nb) for h in range(GDN_HEADS)}
    for c in range(nc):
        cu = [u for u in units if u["c"] == c]
        for u in cu:
            stb = st[(u["b"], u["h"])].astype(BF16)
            u["ws"] = _dot(jnp.concatenate([u["sol"][:, GDN_DV:].astype(BF16), u["qg"]], axis=0), stb)
        for u in cu:
            u["vnb"] = (u["sol"][:, :GDN_DV] - u["ws"][:CHUNK]).astype(BF16)
        for u in cu:
            u["out"] = u["ws"][CHUNK:] + _dot(u["a_qk"], u["vnb"])
        for u in cu:
            key = (u["b"], u["h"])
            st[key] = st[key] * u["egl"] + _dot_tn(u["kdec"], u["vnb"])
        for u in cu:
            b, h = u["b"], u["h"]
            r0, r1 = c * CHUNK, (c + 1) * CHUNK
            out = u["out"]
            on = out * lax.rsqrt(jnp.mean(out * out, axis=-1, keepdims=True) + NORM_EPS) * ng
            zh = x_ref[b, r0:r1, CONV_CH + h * GDN_DV:CONV_CH + (h + 1) * GDN_DV]
            o_ref[b, r0:r1, h * GDN_DV:(h + 1) * GDN_DV] = on * _silu(zh)
    for (b, h), s in st.items():
        state[b, h] = s


def _gdn(gdn_in3, bd3, conv_w, aparams, gdn_norm_g):
    nb, seq, _ = gdn_in3.shape
    lg = min(128, seq)
    const = lambda i: (0, 0)
    return pl.pallas_call(
        functools.partial(_gdn_kernel, nb=nb, lg=lg),
        out_shape=jax.ShapeDtypeStruct((nb, seq, GDN_V), F32),
        grid=(seq // lg,),
        in_specs=[pl.BlockSpec((nb, lg, gdn_in3.shape[2]), lambda i: (0, i, 0)),
                  pl.BlockSpec((nb, lg, LANES), lambda i: (0, i, 0)),
                  pl.BlockSpec(conv_w.shape, const),
                  pl.BlockSpec(aparams.shape, const),
                  pl.BlockSpec(gdn_norm_g.shape, const)],
        out_specs=pl.BlockSpec((nb, lg, GDN_V), lambda i: (0, i, 0)),
        scratch_shapes=[pltpu.VMEM((nb, lg + SUBLANES, CONV_CH), F32),
                        pltpu.VMEM((nb, GDN_HEADS, GDN_DK, GDN_DV), F32)],
        compiler_params=_cparams(("arbitrary",)),
        name="gdn",
    )(gdn_in3, bd3, conv_w, aparams, gdn_norm_g)


def _aprep_kernel(x_ref, gq_ref, gk_ref, qf_ref, bdm_ref, q_ref, k_ref, vt_ref, *, tk):
    x = x_ref[...]
    tm = x.shape[0]
    bdm = bdm_ref[...]

    def qknorm(v, gain):
        ms = jnp.dot(v * v, bdm, precision=HIGHEST, preferred_element_type=F32) * (1.0 / DIFF_DH)
        return v * lax.rsqrt(ms + NORM_EPS) * gain

    qn = qknorm(x[:, 0:DIFF_QK], gq_ref[...]) * (DIFF_DH ** -0.5 * LOG2E)
    kn = qknorm(x[:, DIFF_QK:2 * DIFF_QK], gk_ref[...])
    pos = pl.program_id(0) * tm + lax.broadcasted_iota(jnp.int32, (tm, DIFF_DH), 0)
    krel = pos % tk
    lane = lax.broadcasted_iota(jnp.int32, (tm, DIFF_DH), 1)
    hi = ((krel // 256) * 256).astype(F32)
    lo = (krel % 256).astype(F32)
    kfeat = jnp.where(lane < 6, jnp.where(lane % 2 == 0, hi, lo), 0.0)
    qfeat = qf_ref[...]
    qparts, kparts = [], []
    for g in range(2 * DIFF_HEADS):
        qparts += [qn[:, g * DIFF_DH:(g + 1) * DIFF_DH],
                   jnp.broadcast_to(qfeat[:, g * DIFF_DH:(g + 1) * DIFF_DH], (tm, DIFF_DH))]
        kparts += [kn[:, g * DIFF_DH:(g + 1) * DIFF_DH], kfeat]
    q_ref[...] = jnp.concatenate(qparts, axis=-1).astype(BF16)
    k_ref[...] = jnp.concatenate(kparts, axis=-1).astype(BF16)
    vt_ref[0, 0] = x[:, 2 * DIFF_QK:].T.astype(BF16)


def _aprep(att_in, gq, gk, qfeat, bdm, nb, seq, tk):
    t = att_in.shape[0]
    tm = tk
    nk = seq // tk
    const = lambda i: (0, 0)
    return pl.pallas_call(
        functools.partial(_aprep_kernel, tk=tk),
        out_shape=(jax.ShapeDtypeStruct((t, 2 * DIFF_QK), BF16),
                   jax.ShapeDtypeStruct((t, 2 * DIFF_QK), BF16),
                   jax.ShapeDtypeStruct((nb, nk, DIFF_V, tk), BF16)),
        grid=(t // tm,),
        in_specs=[pl.BlockSpec((tm, att_in.shape[1]), lambda i: (i, 0)),
                  pl.BlockSpec(gq.shape, const), pl.BlockSpec(gk.shape, const),
                  pl.BlockSpec(qfeat.shape, const), pl.BlockSpec(bdm.shape, const)],
        out_specs=(pl.BlockSpec((tm, 2 * DIFF_QK), lambda i: (i, 0)),
                   pl.BlockSpec((tm, 2 * DIFF_QK), lambda i: (i, 0)),
                   pl.BlockSpec((1, 1, DIFF_V, tk), lambda i: (i // nk, i % nk, 0, 0))),
        compiler_params=_cparams(("arbitrary",)),
        name="aprep",
    )(att_in, gq, gk, qfeat, bdm)


def _attn_kernel(slope_ref, q_ref, k_ref, vt_ref, lamv_ref, sg_ref, o_ref, m_s, l_s, acc_s, sa, sb,
                 *, tq, lam_init):
    h = pl.program_id(1)
    qi = pl.program_id(2)
    slope2 = slope_ref[h]
    dv = 2 * DIFF_DH
    qs = [q_ref[:, 0:LANES], q_ref[:, LANES:2 * LANES]]

    m_s[...] = jnp.full_like(m_s, -jnp.inf)
    l_s[...] = jnp.zeros_like(l_s)
    acc_s[...] = jnp.zeros_like(acc_s)

    def scores(kj):
        k0 = pl.multiple_of(kj * tq, tq)
        return tuple(_dot_nt(k_ref[pl.ds(k0, tq), c * LANES:(c + 1) * LANES], qs[c]) for c in range(2))

    def accumulate(ss, kj):
        vt = vt_ref[0, kj]
        off = slope2 * ((kj - qi) * tq).astype(F32)
        for c in range(2):
            s = ss[c]
            m_old = m_s[c]
            m_new = jnp.maximum(m_old, jnp.max(s, axis=0, keepdims=True) + off)
            alpha = jnp.exp2(m_old - m_new)
            p = jnp.exp2(s - (m_new - off))
            l_s[c] = alpha * l_s[c] + jnp.sum(p, axis=0, keepdims=True)
            acc_s[c] = alpha * acc_s[c] + _dot(vt, p.astype(BF16))
            m_s[c] = m_new

    kpos = lax.broadcasted_iota(jnp.int32, (tq, tq), 0)
    qpos = lax.broadcasted_iota(jnp.int32, (tq, tq), 1)
    allowed = (kpos // CHUNK) <= (qpos // CHUNK)
    fut = jnp.maximum(kpos - qpos, 0).astype(F32)
    add = jnp.where(allowed, (-2.0 * slope2) * fut, -jnp.inf)
    accumulate(tuple(s + add for s in scores(qi)), qi)

    def put(slot, ss):
        for c in range(2):
            slot[c] = ss[c]

    npairs = qi // 2
    put(sa, scores(0))

    def pair(j, carry):
        a = 2 * j
        put(sb, scores(a + 1))
        accumulate((sa[0], sa[1]), a)
        put(sa, scores(jnp.minimum(a + 2, qi - 1)))
        accumulate((sb[0], sb[1]), a + 1)
        return carry
    lax.fori_loop(0, npairs, pair, 0)

    @pl.when(qi % 2 == 1)
    def _():
        accumulate((sa[0], sa[1]), qi - 1)

    lq1, lk1, lq2, lk2 = (lamv_ref[i:i + 1, :] for i in range(4))
    lam = (jnp.exp(jnp.sum(lq1 * lk1, axis=-1, keepdims=True))
           - jnp.exp(jnp.sum(lq2 * lk2, axis=-1, keepdims=True)) + lam_init)
    o = acc_s[0] / l_s[0] - lam * (acc_s[1] / l_s[1])
    on = o * lax.rsqrt(jnp.mean(o * o, axis=0, keepdims=True) + NORM_EPS) * sg_ref[...] * (1.0 - lam_init)
    o_ref[...] = on.T


def _attn(slopes2, qa, ka, vt, lamv, sg_col, nb, seq, tq, lam_init):
    t = qa.shape[0]
    nq = seq // tq
    dv = 2 * DIFF_DH
    grid_spec = pltpu.PrefetchScalarGridSpec(
        num_scalar_prefetch=1,
        grid=(nb, DIFF_HEADS, nq),
        in_specs=[pl.BlockSpec((tq, 2 * LANES), lambda b, h, i, s: (b * nq + i, h)),
                  pl.BlockSpec((seq, 2 * LANES), lambda b, h, i, s: (b, h)),
                  pl.BlockSpec((1, nq, dv, tq), lambda b, h, i, s: (b, 0, h, 0)),
                  pl.BlockSpec(lamv.shape, lambda b, h, i, s: (0, 0)),
                  pl.BlockSpec(sg_col.shape, lambda b, h, i, s: (0, 0))],
        out_specs=pl.BlockSpec((tq, dv), lambda b, h, i, s: (b * nq + i, h)),
        scratch_shapes=[pltpu.VMEM((2, 1, tq), F32), pltpu.VMEM((2, 1, tq), F32),
                        pltpu.VMEM((2, dv, tq), F32),
                        pltpu.VMEM((2, tq, tq), F32), pltpu.VMEM((2, tq, tq), F32)],
    )
    return pl.pallas_call(
        functools.partial(_attn_kernel, tq=tq, lam_init=lam_init),
        out_shape=jax.ShapeDtypeStruct((t, DIFF_V), F32),
        grid_spec=grid_spec,
        compiler_params=_cparams(("arbitrary", "arbitrary", "arbitrary")),
        name="attn",
    )(slopes2, qa, ka, vt, lamv, sg_col)


def _pack_halves(x):
    n = x.shape[1] // 2
    lo = pltpu.bitcast(x[:, :n].astype(BF16).astype(F32), jnp.uint32)
    hi = pltpu.bitcast(x[:, n:].astype(BF16).astype(F32), jnp.uint32)
    return (lo >> 16) | (hi & jnp.uint32(0xFFFF0000))


def _unpack_halves(p):
    lo = pltpu.bitcast(p << 16, F32)
    hi = pltpu.bitcast(p & jnp.uint32(0xFFFF0000), F32)
    return lo, hi


def _post_kernel(x_ref, oa_ref, ob_ref, gate_ref, g1_ref, sh2_ref, sc2_ref, n2_ref,
                 woa_ref, wob_ref, wout_ref, x1_ref, h2_ref, hp_ref):
    ya = _dot(oa_ref[...].astype(BF16), woa_ref[...])
    yb = _dot(ob_ref[...].astype(BF16), wob_ref[...])
    merged = jax.nn.sigmoid(gate_ref[:, 0:D_MODEL]) * ya + jax.nn.sigmoid(gate_ref[:, D_MODEL:]) * yb
    y = _dot(merged.astype(BF16), wout_ref[...])
    x1 = x_ref[...] + g1_ref[0] * y
    x1_ref[...] = x1
    n = x1 * lax.rsqrt(jnp.mean(x1 * x1, axis=-1, keepdims=True) + NORM_EPS) * n2_ref[...]
    h2 = n * (1.0 + sc2_ref[0]) + sh2_ref[0]
    h2_ref[...] = h2
    hp_ref[...] = _pack_halves(h2)


def _post(x2, oa, ob, gates, mod3, norm2_g, woa, wob, wout, seq):
    t = x2.shape[0]
    tm = min(256, seq)
    tpb = seq // tm
    const = lambda i: (0, 0)
    row = lambda n: pl.BlockSpec((tm, n), lambda i: (i, 0))
    modspec = lambda j: pl.BlockSpec((1, 1, D_MODEL), lambda i: (i // tpb, 0, j))
    wspec = lambda w: pl.BlockSpec(w.shape, const, pipeline_mode=pl.Buffered(1))
    return pl.pallas_call(
        _post_kernel,
        out_shape=(jax.ShapeDtypeStruct((t, D_MODEL), F32), jax.ShapeDtypeStruct((t, D_MODEL), F32),
                   jax.ShapeDtypeStruct((t, D_MODEL // 2), jnp.uint32)),
        grid=(t // tm,),
        in_specs=[row(D_MODEL), row(GDN_V), row(DIFF_V), row(2 * D_MODEL),
                  modspec(2), modspec(3), modspec(4), pl.BlockSpec((1, D_MODEL), const),
                  wspec(woa), wspec(wob), wspec(wout)],
        out_specs=(row(D_MODEL), row(D_MODEL), row(D_MODEL // 2)),
        compiler_params=_cparams(("arbitrary",)),
        name="post",
    )(x2, oa, ob, gates, mod3, mod3, mod3, norm2_g, woa, wob, wout)


def _router_kernel(h_ref, wr_ref, rb_ref, idx_ref, w_ref, cnt_ref, cnt_s):
    step = pl.program_id(0)

    @pl.when(step == 0)
    def _():
        cnt_s[...] = jnp.zeros_like(cnt_s)

    tm = h_ref.shape[0]
    logits = jnp.dot(h_ref[...], wr_ref[...], precision=HIGHEST, preferred_element_type=F32)
    scores = jax.nn.sigmoid(logits)
    choice = scores + rb_ref[...]
    lane = lax.broadcasted_iota(jnp.int32, (tm, N_EXPERTS), 1)
    grp = lane // GROUP_SIZE
    neg = -jnp.inf

    def first_max(v):
        m = jnp.max(v, axis=-1, keepdims=True)
        i = jnp.min(jnp.where(v == m, lane, N_EXPERTS), axis=-1, keepdims=True)
        return m, i

    gl = lax.broadcasted_iota(jnp.int32, (tm, LANES), 1)
    gscore = jnp.full((tm, LANES), neg, F32)
    for g in range(N_GROUPS):
        vg = jnp.where(grp == g, choice, neg)
        m1, i1 = first_max(vg)
        m2 = jnp.max(jnp.where(lane == i1, neg, vg), axis=-1, keepdims=True)
        gscore = jnp.where(gl == g, m1 + m2, gscore)
    emask = jnp.zeros((tm, N_EXPERTS), jnp.bool_)
    for _ in range(TOPK_GROUPS):
        m = jnp.max(gscore, axis=-1, keepdims=True)
        gi = jnp.min(jnp.where(gscore == m, gl, LANES), axis=-1, keepdims=True)
        emask = emask | (grp == gi)
        gscore = jnp.where(gl == gi, neg, gscore)
    masked = jnp.where(emask, choice, neg)
    idxs = jnp.zeros((tm, LANES), jnp.int32)
    ws = jnp.zeros((tm, LANES), F32)
    sel = jnp.zeros((tm, N_EXPERTS), F32)
    for k in range(TOP_K):
        _, i = first_max(masked)
        hit = lane == i
        wk = jnp.sum(jnp.where(hit, scores, 0.0), axis=-1, keepdims=True)
        idxs = jnp.where(gl == k, i, idxs)
        ws = jnp.where(gl == k, wk, ws)
        sel = jnp.where(hit, 1.0, sel)
        masked = jnp.where(hit, neg, masked)
    wsum = jnp.sum(ws, axis=-1, keepdims=True)
    idx_ref[...] = idxs
    w_ref[...] = ws / wsum * ROUTED_SCALE
    cnt_s[...] = cnt_s[...] + jnp.sum(sel, axis=0, keepdims=True)
    cnt_ref[...] = cnt_s[...]


def _router(h2, w_router, rbias):
    t = h2.shape[0]
    tm = min(256, t)
    const = lambda i: (0, 0)
    return pl.pallas_call(
        _router_kernel,
        out_shape=(jax.ShapeDtypeStruct((t, LANES), jnp.int32),
                   jax.ShapeDtypeStruct((t, LANES), F32),
                   jax.ShapeDtypeStruct((1, N_EXPERTS), F32)),
        grid=(t // tm,),
        in_specs=[pl.BlockSpec((tm, D_MODEL), lambda i: (i, 0)),
                  pl.BlockSpec(w_router.shape, const), pl.BlockSpec(rbias.shape, const)],
        out_specs=(pl.BlockSpec((tm, LANES), lambda i: (i, 0)),
                   pl.BlockSpec((tm, LANES), lambda i: (i, 0)),
                   pl.BlockSpec((1, N_EXPERTS), const)),
        scratch_shapes=[pltpu.VMEM((1, N_EXPERTS), F32)],
        compiler_params=_cparams(("arbitrary",)),
        name="router",
    )(h2, w_router, rbias)


def _rank_kernel(idx_ref, ps_ref, dest_ref, run_s):
    step = pl.program_id(0)

    @pl.when(step == 0)
    def _():
        run_s[...] = jnp.zeros_like(run_s)

    tm = idx_ref.shape[0]
    idx = idx_ref[...]
    lane = lax.broadcasted_iota(jnp.int32, (tm, N_EXPERTS), 1)
    gl = lax.broadcasted_iota(jnp.int32, (tm, LANES), 1)
    hits = [lane == idx[:, k:k + 1] for k in range(TOP_K)]
    sel = jnp.zeros((tm, N_EXPERTS), F32)
    for hit in hits:
        sel = jnp.where(hit, 1.0, sel)
    ri = lax.broadcasted_iota(jnp.int32, (tm, tm), 0)
    ci = lax.broadcasted_iota(jnp.int32, (tm, tm), 1)
    before = _dot((ri > ci).astype(BF16), sel.astype(BF16))
    base = before + run_s[...] + ps_ref[...]
    dest = jnp.zeros((tm, LANES), F32)
    for k, hit in enumerate(hits):
        dk = jnp.sum(jnp.where(hit, base, 0.0), axis=-1, keepdims=True)
        dest = jnp.where(gl == k, dk, dest)
    dest_ref[...] = dest.astype(jnp.int32)
    run_s[...] = run_s[...] + jnp.sum(sel, axis=0, keepdims=True)


def _rank(idx, pstart):
    t = idx.shape[0]
    tm = min(256, t)
    const = lambda i: (0, 0)
    return pl.pallas_call(
        _rank_kernel,
        out_shape=jax.ShapeDtypeStruct((t, LANES), jnp.int32),
        grid=(t // tm,),
        in_specs=[pl.BlockSpec((tm, LANES), lambda i: (i, 0)), pl.BlockSpec(pstart.shape, const)],
        out_specs=pl.BlockSpec((tm, LANES), lambda i: (i, 0)),
        scratch_shapes=[pltpu.VMEM((1, N_EXPERTS), F32)],
        compiler_params=_cparams(("arbitrary",)),
        name="rank",
    )(idx, pstart)


def _row_copy(src, dst, sem):
    return pltpu.make_async_copy(src, dst, sem)


def _dispatch_kernel(dest_ref, h_ref, xs_in, xs_ref, sem, *, tm):
    del xs_in

    def start_row(r, c):
        for k in range(TOP_K):
            d = dest_ref[r * TOP_K + k]
            _row_copy(h_ref.at[pl.ds(r, 1), :], xs_ref.at[pl.ds(d, 1), :], sem).start(priority=k % 2)
        return c

    lax.fori_loop(0, tm, start_row, 0)
    for k in range(TOP_K):
        _row_copy(h_ref, xs_ref.at[pl.ds(0, tm), :], sem).wait()


def _dispatch(dest_flat, h2, xs0):
    t = h2.shape[0]
    tm = min(256, t)
    return pl.pallas_call(
        functools.partial(_dispatch_kernel, tm=tm),
        out_shape=jax.ShapeDtypeStruct(xs0.shape, xs0.dtype),
        grid=(t // tm,),
        in_specs=[pl.BlockSpec((tm * TOP_K,), lambda i: (i,), memory_space=pltpu.SMEM),
                  pl.BlockSpec((tm, h2.shape[1]), lambda i: (i, 0)),
                  pl.BlockSpec(memory_space=pl.ANY)],
        out_specs=pl.BlockSpec(memory_space=pl.ANY),
        scratch_shapes=[pltpu.SemaphoreType.DMA(())],
        input_output_aliases={2: 0},
        compiler_params=_cparams(("arbitrary",)),
        name="dispatch",
    )(dest_flat, h2, xs0)


SC_CORES = 2
SC_SUBCORES = 16
SC_LANES = 16
INVERT_CHUNK = 4096


def _invert(dest_flat, n_rows, n_tokens):
    n_assign = dest_flat.shape[0]
    workers = SC_CORES * SC_SUBCORES
    rpw = n_rows // workers
    assert rpw * workers == n_rows and rpw % SC_LANES == 0 and n_assign % INVERT_CHUNK == 0
    scratch_base = TOP_K * n_tokens
    log2_k = TOP_K.bit_length() - 1

    def body(dest_hbm, inv_hbm, loc, chunk_v):
        wid = lax.axis_index("s") * SC_CORES + lax.axis_index("c")
        lo = wid * rpw
        lane = lax.iota(jnp.int32, SC_LANES)

        def init(i, c):
            loc[pl.ds(i * SC_LANES, SC_LANES)] = scratch_base + lo + i * SC_LANES + lane
            return c
        lax.fori_loop(0, rpw // SC_LANES, init, 0)

        def do_chunk(ci, c):
            pltpu.sync_copy(dest_hbm.at[pl.ds(ci * INVERT_CHUNK, INVERT_CHUNK)], chunk_v)

            def inner(j, cc):
                rel = chunk_v[pl.ds(j * SC_LANES, SC_LANES)] - lo
                mine = (rel >= 0) & (rel < rpw)
                a = ci * INVERT_CHUNK + j * SC_LANES + lane
                val = (a & (TOP_K - 1)) * n_tokens + lax.shift_right_logical(a, log2_k)
                plsc.store_scatter(loc, [jnp.where(mine, rel, 0)], val, mask=mine)
                return cc
            lax.fori_loop(0, INVERT_CHUNK // SC_LANES, inner, 0)
            return c
        lax.fori_loop(0, n_assign // INVERT_CHUNK, do_chunk, 0)
        pltpu.sync_copy(loc, inv_hbm.at[pl.ds(lo, rpw)])

    mesh = plsc.VectorSubcoreMesh(core_axis_name="c", subcore_axis_name="s",
                                  num_cores=SC_CORES, num_subcores=SC_SUBCORES)
    return pl.kernel(body, out_type=jax.ShapeDtypeStruct((n_rows,), jnp.int32), mesh=mesh,
                     scratch_types=[pltpu.VMEM((rpw,), jnp.int32), pltpu.VMEM((INVERT_CHUNK,), jnp.int32)],
                     compiler_params=pltpu.CompilerParams(needs_layout_passes=False),
                     name="invert")(dest_flat)


EXPERT_RING = 4
WEIGHT_RING = 3


def _experts_kernel(start_ref, count_ref, xs_ref, inv_ref, wgu_ref, wdn_ref, g_ref,
                    wgu_f, wdn_f, wgu_s, wdn_s, xbuf, ybuf, inv_s, sem_w, sem_in, sem_inv, sem_out,
                    *, nblk_total):
    half = D_MODEL // 2
    last_e = N_EXPERTS - 1
    used = start_ref[last_e] + count_ref[last_e]

    def rows(g):
        return pl.ds(pl.multiple_of(g * EXPERT_BLOCK, EXPERT_BLOCK), EXPERT_BLOCK)

    def in_copies(g):
        slot = g % EXPERT_RING
        gc = jnp.minimum(g, nblk_total - 1)
        return (pltpu.make_async_copy(xs_ref.at[rows(gc), :], xbuf.at[slot], sem_in.at[slot]),
                pltpu.make_async_copy(inv_ref.at[pl.ds(gc, 1), :], inv_s.at[slot], sem_inv.at[slot]))

    def out_wait(slot):
        pltpu.make_async_copy(ybuf.at[slot], g_ref.at[pl.ds(0, EXPERT_BLOCK), :], sem_out.at[slot]).wait()

    def w_copies(e):
        slot = e % WEIGHT_RING
        return (pltpu.make_async_copy(wgu_ref.at[e], wgu_f.at[slot], sem_w.at[0, slot]),
                pltpu.make_async_copy(wdn_ref.at[e], wdn_f.at[slot], sem_w.at[1, slot]))

    for e0 in range(WEIGHT_RING - 1):
        for cp in w_copies(e0):
            cp.start()
    for g in range(EXPERT_RING - 1):
        for cp in in_copies(g):
            cp.start()

    def expert(e, carry):
        for cp in w_copies(e):
            cp.wait()

        @pl.when(e + WEIGHT_RING - 1 <= last_e)
        def _():
            for cp in w_copies(e + WEIGHT_RING - 1):
                cp.start()

        slot = e % WEIGHT_RING
        wgu_s[...] = wgu_f[slot].astype(BF16)
        wdn_s[...] = wdn_f[slot].astype(BF16)
        first = start_ref[e]

        def block(i, c):
            g = first + i
            for cp in in_copies(g):
                cp.wait()
            for cp in in_copies(g + EXPERT_RING - 1):
                cp.start()
            s = g % EXPERT_RING

            @pl.when(g >= EXPERT_RING)
            def _():
                out_wait(s)

            lo, hi = _unpack_halves(xbuf[s])
            gu = _dot(lo.astype(BF16), wgu_s[0:half, :]) + _dot(hi.astype(BF16), wgu_s[half:, :])
            act = _silu(gu[:, :EXPERT_FF]) * gu[:, EXPERT_FF:]
            ybuf[s] = _pack_halves(_dot(act.astype(BF16), wdn_s[...]))
            for r in range(EXPERT_BLOCK):
                d = inv_s[s, 0, r]
                _row_copy(ybuf.at[s, pl.ds(r, 1), :], g_ref.at[pl.ds(d, 1), :], sem_out.at[s]).start(priority=r % 2)
            return c

        lax.fori_loop(0, count_ref[e], block, 0)
        return carry

    lax.fori_loop(0, N_EXPERTS, expert, 0)

    for j in range(EXPERT_RING - 1):
        for cp in in_copies(used + j):
            cp.wait()
    for j in range(EXPERT_RING):
        @pl.when(used > j)
        def _():
            out_wait((used - 1 - j) % EXPERT_RING)


def _experts(blk_start, blk_count, xs, inv2, w_gu, w_dn, n_slots):
    rows, half = xs.shape
    nblk = rows // EXPERT_BLOCK
    grid_spec = pltpu.PrefetchScalarGridSpec(
        num_scalar_prefetch=2,
        grid=(1,),
        in_specs=[pl.BlockSpec(memory_space=pl.ANY), pl.BlockSpec(memory_space=pl.ANY),
                  pl.BlockSpec(memory_space=pl.ANY), pl.BlockSpec(memory_space=pl.ANY)],
        out_specs=pl.BlockSpec(memory_space=pl.ANY),
        scratch_shapes=[pltpu.VMEM((WEIGHT_RING, D_MODEL, 2 * EXPERT_FF), F32),
                        pltpu.VMEM((WEIGHT_RING, EXPERT_FF, D_MODEL), F32),
                        pltpu.VMEM((D_MODEL, 2 * EXPERT_FF), BF16), pltpu.VMEM((EXPERT_FF, D_MODEL), BF16),
                        pltpu.VMEM((EXPERT_RING, EXPERT_BLOCK, half), jnp.uint32),
                        pltpu.VMEM((EXPERT_RING, EXPERT_BLOCK, half), jnp.uint32),
                        pltpu.SMEM((EXPERT_RING, 1, EXPERT_BLOCK), jnp.int32),
                        pltpu.SemaphoreType.DMA((2, WEIGHT_RING)), pltpu.SemaphoreType.DMA((EXPERT_RING,)),
                        pltpu.SemaphoreType.DMA((EXPERT_RING,)), pltpu.SemaphoreType.DMA((EXPERT_RING,))],
    )
    return pl.pallas_call(
        functools.partial(_experts_kernel, nblk_total=nblk),
        out_shape=jax.ShapeDtypeStruct((n_slots, half), jnp.uint32),
        grid_spec=grid_spec,
        compiler_params=_cparams(("arbitrary",)),
        name="experts",
    )(blk_start, blk_count, xs, inv2, w_gu, w_dn)


def _combine_kernel(x1_ref, h_ref, w_ref, g2_ref, wsgu_ref, wsdn_ref, buf, o_ref):
    su = _dot(h_ref[...].astype(BF16), wsgu_ref[...])
    y = _dot((_silu(su[:, :SHARED_FF]) * su[:, SHARED_FF:]).astype(BF16), wsdn_ref[...])
    w = w_ref[...]
    half = D_MODEL // 2
    ylo, yhi = y[:, :half], y[:, half:]
    for k in range(TOP_K):
        lo, hi = _unpack_halves(buf[k])
        wk = w[:, k:k + 1]
        ylo = ylo + wk * lo
        yhi = yhi + wk * hi
    g2 = g2_ref[0]
    o_ref[:, :half] = x1_ref[:, :half] + g2[:, :half] * ylo
    o_ref[:, half:] = x1_ref[:, half:] + g2[:, half:] * yhi


def _combine(x1, h2, wts, mod3, wsgu, wsdn, slots3, seq):
    t = x1.shape[0]
    tm = min(256, seq)
    tpb = seq // tm
    const = lambda i: (0, 0)
    row = lambda n: pl.BlockSpec((tm, n), lambda i: (i, 0))
    return pl.pallas_call(
        _combine_kernel,
        out_shape=jax.ShapeDtypeStruct((t, D_MODEL), F32),
        grid=(t // tm,),
        in_specs=[row(D_MODEL), row(D_MODEL), row(LANES),
                  pl.BlockSpec((1, 1, D_MODEL), lambda i: (i // tpb, 0, 5)),
                  pl.BlockSpec(wsgu.shape, const), pl.BlockSpec(wsdn.shape, const),
                  pl.BlockSpec((TOP_K, tm, D_MODEL // 2), lambda i: (0, i, 0))],
        out_specs=row(D_MODEL),
        compiler_params=_cparams(("arbitrary",)),
        name="combine",
    )(x1, h2, wts, mod3, wsgu, wsdn, slots3)


def _layer(x, c, layer, w_ada, b_ada, norm1_g, w_in, conv_w, a_log, dt_bias, gdn_norm_g, w_o_gdn,
           q_norm_g, k_norm_g, lambda_q1, lambda_k1, lambda_q2, lambda_k2, subln_g, w_o_diff,
           w_out, norm2_g, w_router, router_bias, w_exp_gate_up, w_exp_down,
           w_shared_gate_up, w_shared_down):
    nb, seq, d = x.shape
    t = nb * seq
    lam_init = 0.8 - 0.6 * math.exp(-0.3 * layer)
    x2 = x.reshape(t, d)

    c_pad = jnp.pad(c, ((0, SUBLANES - nb % SUBLANES if nb % SUBLANES else 0), (0, 0)))
    mod = _ada(c_pad, w_ada, b_ada.reshape(1, -1))[:nb]
    mod3 = mod.reshape(nb, 1, 6 * d)

    o_bd = 2 * GDN_QK + 2 * GDN_V
    o_att = o_bd + 2 * GDN_HEADS
    o_gate = o_att + 2 * DIFF_QK + DIFF_V
    w_gdn = w_in[:, :o_bd].astype(BF16)
    w_bd = jnp.pad(w_in[:, o_bd:o_att], ((0, 0), (0, LANES - 2 * GDN_HEADS))).astype(BF16)
    w_att = w_in[:, o_att:o_gate].astype(BF16)
    w_gate = w_in[:, o_gate:].astype(BF16)
    gdn_in, bd, att_in, gates = _inproj(x2, mod3, norm1_g.reshape(1, d), w_gdn, w_bd, w_att, w_gate, seq)

    aparams = jnp.zeros((SUBLANES, LANES), F32)
    aparams = aparams.at[0, GDN_HEADS:2 * GDN_HEADS].set(a_log).at[1, GDN_HEADS:2 * GDN_HEADS].set(dt_bias)
    oa = _gdn(gdn_in.reshape(nb, seq, -1), bd.reshape(nb, seq, LANES), conv_w, aparams,
              gdn_norm_g.reshape(1, GDN_DV)).reshape(t, GDN_V)

    tq = min(512, seq)
    slopes = [2.0 ** (-8.0 * (h + 1) / DIFF_HEADS) for h in range(DIFF_HEADS)]
    c3 = _bf16_split3(LOG2E)
    qfeat = np.zeros((1, 2 * DIFF_HEADS * DIFF_DH), np.float32)
    for g in range(2 * DIFF_HEADS):
        for j in range(6):
            qfeat[0, g * DIFF_DH + j] = slopes[g // 2] * c3[j // 2]
    grp = np.arange(DIFF_QK) // DIFF_DH
    bdm = jnp.asarray((grp[:, None] == grp[None, :]).astype(np.float32))
    qa, ka, vt = _aprep(att_in, jnp.tile(q_norm_g, 2 * DIFF_HEADS).reshape(1, -1),
                        jnp.tile(k_norm_g, 2 * DIFF_HEADS).reshape(1, -1), jnp.asarray(qfeat), bdm, nb, seq, tq)
    slopes2 = jnp.asarray([s * (c3[0] + c3[1] + c3[2]) for s in slopes], F32)
    lamv = jnp.zeros((SUBLANES, DIFF_DH), F32)
    lamv = lamv.at[0].set(lambda_q1).at[1].set(lambda_k1).at[2].set(lambda_q2).at[3].set(lambda_k2)
    ob = _attn(slopes2, qa, ka, vt, lamv, subln_g.reshape(-1, 1), nb, seq, tq, lam_init)

    x1, h2, hp = _post(x2, oa, ob, gates, mod3, norm2_g.reshape(1, d), w_o_gdn.astype(BF16),
                       w_o_diff.astype(BF16), w_out.astype(BF16), seq)

    idx, wts, counts = _router(h2, w_router, router_bias.reshape(1, -1))
    cnt = counts[0].astype(jnp.int32)
    padded = (cnt + EXPERT_BLOCK - 1) // EXPERT_BLOCK * EXPERT_BLOCK
    pends = jnp.cumsum(padded)
    pstart = (pends - padded).astype(F32).reshape(1, -1)
    nblk = -(-(t * TOP_K) // EXPERT_BLOCK) + N_EXPERTS
    blk_start = ((pends - padded) // EXPERT_BLOCK).astype(jnp.int32)
    blk_count = (padded // EXPERT_BLOCK).astype(jnp.int32)
    dest = _rank(idx, pstart)
    dest_flat = dest[:, :TOP_K].reshape(-1)
    n_rows = nblk * EXPERT_BLOCK
    assert n_rows % t == 0
    xs = _dispatch(dest_flat, hp, jnp.zeros((n_rows, d // 2), jnp.uint32))
    inv = _invert(dest_flat, n_rows, t)
    slots = _experts(blk_start, blk_count, xs, inv.reshape(nblk, EXPERT_BLOCK), w_exp_gate_up, w_exp_down,
                     TOP_K * t + n_rows)
    out = _combine(x1, h2, wts, mod3, w_shared_gate_up.astype(BF16), w_shared_down.astype(BF16),
                   slots.reshape(-1, t, d // 2), seq)
    return out.reshape(nb, seq, d)


def kernel(x, c, w_ada, b_ada, norm1_g, w_in, conv_w, a_log, dt_bias, gdn_norm_g, w_o_gdn, q_norm_g, k_norm_g, lambda_q1, lambda_k1, lambda_q2, lambda_k2, subln_g, w_o_diff, w_out, norm2_g, w_router, router_bias, w_exp_gate_up, w_exp_down, w_shared_gate_up, w_shared_down):
    params = (w_ada, b_ada, norm1_g, w_in, conv_w, a_log, dt_bias, gdn_norm_g, w_o_gdn, q_norm_g,
              k_norm_g, lambda_q1, lambda_k1, lambda_q2, lambda_k2, subln_g, w_o_diff, w_out, norm2_g,
              w_router, router_bias, w_exp_gate_up, w_exp_down, w_shared_gate_up, w_shared_down)
    for layer in range(w_ada.shape[0]):
        x = _layer(x, c, layer, *(p[layer] for p in params))
    return x
```

```python
import functools
import math

import jax
import jax.numpy as jnp
import numpy as np
from jax import lax
from jax.experimental import pallas as pl
from jax.experimental.pallas import tpu as pltpu
from jax.experimental.pallas import tpu_sc as plsc

F32 = jnp.float32
BF16 = jnp.bfloat16
HIGHEST = lax.Precision.HIGHEST

D_MODEL = 1024
CHUNK = 64
GDN_HEADS = 4
GDN_DK = 128
GDN_DV = 128
GDN_CONV = 4
DIFF_HEADS = 4
DIFF_DH = 64
N_EXPERTS = 256
TOP_K = 8
N_GROUPS = 8
TOPK_GROUPS = 4
EXPERT_FF = 256
SHARED_FF = 256
ROUTED_SCALE = 2.5
NORM_EPS = 1e-6
GROUP_SIZE = N_EXPERTS // N_GROUPS

GDN_QK = GDN_HEADS * GDN_DK
GDN_V = GDN_HEADS * GDN_DV
CONV_CH = 2 * GDN_QK + GDN_V
DIFF_QK = DIFF_HEADS * 2 * DIFF_DH
DIFF_V = DIFF_HEADS * 2 * DIFF_DH

LANES = 128
SUBLANES = 8
EXPERT_BLOCK = 128
INV_BASE = 8
LOG2E = math.log2(math.e)
VMEM_LIMIT = 56 * 1024 * 1024


def _cparams(sem):
    return pltpu.CompilerParams(dimension_semantics=sem, vmem_limit_bytes=VMEM_LIMIT)


def _dot(a, b):
    return jnp.dot(a, b, preferred_element_type=F32)


def _dot_nt(a, b):
    return lax.dot_general(a, b, (((1,), (1,)), ((), ())), preferred_element_type=F32)


def _dot_tn(a, b):
    return lax.dot_general(a, b, (((0,), (0,)), ((), ())), preferred_element_type=F32)


def _silu(x):
    return x * jax.nn.sigmoid(x)


def _bf16_split3(x):
    rnd = lambda v: float(np.float32(v).astype(BF16).astype(np.float32))
    a = rnd(x)
    b = rnd(x - a)
    c = rnd(x - a - b)
    return a, b, c


def _ada_kernel(c_ref, w_ref, b_ref, o_ref):
    s = _silu(c_ref[...])
    o_ref[...] = jnp.dot(s, w_ref[...], precision=HIGHEST, preferred_element_type=F32) + b_ref[...]


def _ada(c_pad, w_ada, b_ada):
    n = w_ada.shape[1]
    tn = 1024
    return pl.pallas_call(
        _ada_kernel,
        out_shape=jax.ShapeDtypeStruct((c_pad.shape[0], n), F32),
        grid=(n // tn,),
        in_specs=[pl.BlockSpec(c_pad.shape, lambda j: (0, 0)),
                  pl.BlockSpec((D_MODEL, tn), lambda j: (0, j)),
                  pl.BlockSpec((1, tn), lambda j: (0, j))],
        out_specs=pl.BlockSpec((c_pad.shape[0], tn), lambda j: (0, j)),
        compiler_params=_cparams(("arbitrary",)),
        name="ada",
    )(c_pad, w_ada, b_ada)


def _inproj_kernel(x_ref, shift_ref, scale_ref, g_ref, wg_ref, wbd_ref, wa_ref, wgt_ref,
                   gdn_ref, bd_ref, att_ref, gate_ref):
    x = x_ref[...]
    y = x * lax.rsqrt(jnp.mean(x * x, axis=-1, keepdims=True) + NORM_EPS) * g_ref[...]
    h = (y * (1.0 + scale_ref[0]) + shift_ref[0]).astype(BF16)
    gdn_ref[...] = _dot(h, wg_ref[...])
    bd_ref[...] = _dot(h, wbd_ref[...])
    att_ref[...] = _dot(h, wa_ref[...])
    gate_ref[...] = _dot(h, wgt_ref[...])


def _inproj(x2, mod3, norm1_g, w_gdn, w_bd, w_att, w_gate, seq):
    t = x2.shape[0]
    tm = min(256, seq)
    tiles_per_b = seq // tm
    const = lambda i: (0, 0)
    wspec = lambda w: pl.BlockSpec(w.shape, const, pipeline_mode=pl.Buffered(1))
    row = lambda n: pl.BlockSpec((tm, n), lambda i: (i, 0))
    return pl.pallas_call(
        _inproj_kernel,
        out_shape=(jax.ShapeDtypeStruct((t, w_gdn.shape[1]), F32),
                   jax.ShapeDtypeStruct((t, LANES), F32),
                   jax.ShapeDtypeStruct((t, w_att.shape[1]), F32),
                   jax.ShapeDtypeStruct((t, w_gate.shape[1]), F32)),
        grid=(t // tm,),
        in_specs=[row(D_MODEL),
                  pl.BlockSpec((1, 1, D_MODEL), lambda i: (i // tiles_per_b, 0, 0)),
                  pl.BlockSpec((1, 1, D_MODEL), lambda i: (i // tiles_per_b, 0, 1)),
                  pl.BlockSpec((1, D_MODEL), const),
                  wspec(w_gdn), wspec(w_bd), wspec(w_att), wspec(w_gate)],
        out_specs=(row(w_gdn.shape[1]), row(LANES), row(w_att.shape[1]), row(w_gate.shape[1])),
        compiler_params=_cparams(("arbitrary",)),
        name="inproj",
    )(x2, mod3, mod3, norm1_g, w_gdn, w_bd, w_att, w_gate)


def _gdn_kernel(x_ref, bd_ref, cw_ref, ap_ref, ng_ref, o_ref, cbuf, state, *, nb, lg):
    step = pl.program_id(0)
    nc = lg // CHUNK

    @pl.when(step == 0)
    def _():
        cbuf[...] = jnp.zeros_like(cbuf)
        state[...] = jnp.zeros_like(state)

    ri = lax.broadcasted_iota(jnp.int32, (CHUNK, CHUNK), 0)
    ci = lax.broadcasted_iota(jnp.int32, (CHUNK, CHUNK), 1)
    causal = ri >= ci
    strict = ri > ci
    eye = (ri == ci).astype(F32)
    rl = lax.broadcasted_iota(jnp.int32, (lg, lg), 0)
    cl = lax.broadcasted_iota(jnp.int32, (lg, lg), 1)
    blocktri = ((rl >= cl) & ((rl // CHUNK) == (cl // CHUNK))).astype(F32)
    cw = cw_ref[...]
    a_row = ap_ref[0:1, :]
    dt_row = ap_ref[1:2, :]
    ng = ng_ref[...]

    units = []
    for b in range(nb):
        cbuf[b, SUBLANES:SUBLANES + lg, :] = x_ref[b, :, 0:CONV_CH]
        acc = cw[GDN_CONV - 1:GDN_CONV, :] * cbuf[b, SUBLANES:SUBLANES + lg, :]
        for j in range(GDN_CONV - 1):
            off = SUBLANES - (GDN_CONV - 1) + j
            acc = acc + cw[j:j + 1, :] * cbuf[b, off:off + lg, :]
        cbuf[b, 0:SUBLANES, :] = cbuf[b, lg:lg + SUBLANES, :]
        qkv = _silu(acc)

        bd = bd_ref[b]
        beta_t = jax.nn.sigmoid(bd)
        g_t = -jnp.exp(a_row) * jax.nn.softplus(bd + dt_row)
        gcum = jnp.dot(blocktri, g_t, precision=HIGHEST, preferred_element_type=F32)
        gcum_t = gcum.T
        egcum = jnp.exp(gcum)

        for h in range(GDN_HEADS):
            qh = qkv[:, h * GDN_DK:(h + 1) * GDN_DK]
            kh = qkv[:, GDN_QK + h * GDN_DK:GDN_QK + (h + 1) * GDN_DK]
            vh = qkv[:, 2 * GDN_QK + h * GDN_DV:2 * GDN_QK + (h + 1) * GDN_DV]
            qh = qh * lax.rsqrt(jnp.sum(qh * qh, axis=-1, keepdims=True) + NORM_EPS) * (GDN_DK ** -0.5)
            kh = kh * lax.rsqrt(jnp.sum(kh * kh, axis=-1, keepdims=True) + NORM_EPS)
            for c in range(nc):
                r0, r1 = c * CHUNK, (c + 1) * CHUNK
                q, k, v = qh[r0:r1], kh[r0:r1], vh[r0:r1]
                gc = gcum[r0:r1, 4 + h:5 + h]
                gr = gcum_t[4 + h:5 + h, r0:r1]
                egc = egcum[r0:r1, 4 + h:5 + h]
                bcol = beta_t[r0:r1, h:h + 1]
                gl = gcum[r1 - 1:r1, 4 + h:5 + h]
                kb = k.astype(BF16)
                units.append(dict(
                    b=b, h=h, c=c, kb=kb, bcol=bcol,
                    qkb=jnp.concatenate([q.astype(BF16), kb], axis=0),
                    decay=jnp.exp(jnp.where(causal, gc - gr, -jnp.inf)),
                    rhs=jnp.concatenate([v * bcol, k * (bcol * egc)], axis=-1).astype(BF16),
                    qg=(q * egc).astype(BF16), egl=jnp.exp(gl),
                    kdec=(k * jnp.exp(gl - gc)).astype(BF16)))

    for u in units:
        u["kq"] = _dot_nt(u["qkb"], u["kb"])
    def same_block(size):
        return (ri // size) == (ci // size)
    for u in units:
        lower = jnp.where(strict, u["bcol"] * u["kq"][CHUNK:] * u["decay"], 0.0)
        u["lower"] = lower
        diag = jnp.where(same_block(INV_BASE), lower, 0.0)
        u["xinv"] = eye - diag
        u["pw"] = diag.astype(BF16)
    for r in range(INV_BASE.bit_length() - 2):
        for u in units:
            u["pw"] = _dot(u["pw"], u["pw"]).astype(BF16)
        for u in units:
            u["xinv"] = u["xinv"] + _dot(u["xinv"].astype(BF16), u["pw"])
    size = INV_BASE
    while size < CHUNK:
        pair_off = same_block(2 * size) & jnp.logical_not(same_block(size))
        for u in units:
            u["xb"] = u["xinv"].astype(BF16)
            u["cx"] = _dot(jnp.where(pair_off, u["lower"], 0.0).astype(BF16), u["xb"]).astype(BF16)
        for u in units:
            u["xinv"] = u["xinv"] - _dot(u["xb"], u["cx"])
        size *= 2
    for u in units:
        u["sol"] = _dot(u["xinv"].astype(BF16), u["rhs"])
        u["a_qk"] = (u["kq"][:CHUNK] * u["decay"]).astype(BF16)

    st = {(b, h): state[b, h] for b in range(nb) for h in range(GDN_HEADS)}
    for c in range(nc):
        cu = [u for u in units if u["c"] == c]
        for u in cu:
            stb = st[(u["b"], u["h"])].astype(BF16)
            u["ws"] = _dot(jnp.concatenate([u["sol"][:, GDN_DV:].astype(BF16), u["qg"]], axis=0), stb)
        for u in cu:
            u["vnb"] = (u["sol"][:, :GDN_DV] - u["ws"][:CHUNK]).astype(BF16)
        for u in cu:
            u["out"] = u["ws"][CHUNK:] + _dot(u["a_qk"], u["vnb"])
        for u in cu:
            key = (u["b"], u["h"])
            st[key] = st[key] * u["egl"] + _dot_tn(u["kdec"], u["vnb"])
        for u in cu:
            b, h = u["b"], u["h"]
            r0, r1 = c * CHUNK, (c + 1) * CHUNK
            out = u["out"]
            on = out * lax.rsqrt(jnp.mean(out * out, axis=-1, keepdims=True) + NORM_EPS) * ng
            zh = x_ref[b, r0:r1, CONV_CH + h * GDN_DV:CONV_CH + (h + 1) * GDN_DV]
            o_ref[b, r0:r1, h * GDN_DV:(h + 1) * GDN_DV] = on * _silu(zh)
    for (b, h), s in st.items():
        state[b, h] = s


def _gdn(gdn_in3, bd3, conv_w, aparams, gdn_norm_g):
    nb, seq, _ = gdn_in3.shape
    lg = min(128, seq)
    const = lambda i: (0, 0)
    return pl.pallas_call(
        functools.partial(_gdn_kernel, nb=nb, lg=lg),
        out_shape=jax.ShapeDtypeStruct((nb, seq, GDN_V), F32),
        grid=(seq // lg,),
        in_specs=[pl.BlockSpec((nb, lg, gdn_in3.shape[2]), lambda i: (0, i, 0)),
                  pl.BlockSpec((nb, lg, LANES), lambda i: (0, i, 0)),
                  pl.BlockSpec(conv_w.shape, const),
                  pl.BlockSpec(aparams.shape, const),
                  pl.BlockSpec(gdn_norm_g.shape, const)],
        out_specs=pl.BlockSpec((nb, lg, GDN_V), lambda i: (0, i, 0)),
        scratch_shapes=[pltpu.VMEM((nb, lg + SUBLANES, CONV_CH), F32),
                        pltpu.VMEM((nb, GDN_HEADS, GDN_DK, GDN_DV), F32)],
        compiler_params=_cparams(("arbitrary",)),
        name="gdn",
    )(gdn_in3, bd3, conv_w, aparams, gdn_norm_g)


def _aprep_kernel(x_ref, gq_ref, gk_ref, qf_ref, bdm_ref, q_ref, k_ref, vt_ref, *, tk):
    x = x_ref[...]
    tm = x.shape[0]
    bdm = bdm_ref[...]

    def qknorm(v, gain):
        ms = jnp.dot(v * v, bdm, precision=HIGHEST, preferred_element_type=F32) * (1.0 / DIFF_DH)
        return v * lax.rsqrt(ms + NORM_EPS) * gain

    qn = qknorm(x[:, 0:DIFF_QK], gq_ref[...]) * (DIFF_DH ** -0.5 * LOG2E)
    kn = qknorm(x[:, DIFF_QK:2 * DIFF_QK], gk_ref[...])
    pos = pl.program_id(0) * tm + lax.broadcasted_iota(jnp.int32, (tm, DIFF_DH), 0)
    krel = pos % tk
    lane = lax.broadcasted_iota(jnp.int32, (tm, DIFF_DH), 1)
    hi = ((krel // 256) * 256).astype(F32)
    lo = (krel % 256).astype(F32)
    kfeat = jnp.where(lane < 6, jnp.where(lane % 2 == 0, hi, lo), 0.0)
    qfeat = qf_ref[...]
    qparts, kparts = [], []
    for g in range(2 * DIFF_HEADS):
        qparts += [qn[:, g * DIFF_DH:(g + 1) * DIFF_DH],
                   jnp.broadcast_to(qfeat[:, g * DIFF_DH:(g + 1) * DIFF_DH], (tm, DIFF_DH))]
        kparts += [kn[:, g * DIFF_DH:(g + 1) * DIFF_DH], kfeat]
    q_ref[...] = jnp.concatenate(qparts, axis=-1).astype(BF16)
    k_ref[...] = jnp.concatenate(kparts, axis=-1).astype(BF16)
    vt_ref[0, 0] = x[:, 2 * DIFF_QK:].T.astype(BF16)


def _aprep(att_in, gq, gk, qfeat, bdm, nb, seq, tk):
    t = att_in.shape[0]
    tm = tk
    nk = seq // tk
    const = lambda i: (0, 0)
    return pl.pallas_call(
        functools.partial(_aprep_kernel, tk=tk),
        out_shape=(jax.ShapeDtypeStruct((t, 2 * DIFF_QK), BF16),
                   jax.ShapeDtypeStruct((t, 2 * DIFF_QK), BF16),
                   jax.ShapeDtypeStruct((nb, nk, DIFF_V, tk), BF16)),
        grid=(t // tm,),
        in_specs=[pl.BlockSpec((tm, att_in.shape[1]), lambda i: (i, 0)),
                  pl.BlockSpec(gq.shape, const), pl.BlockSpec(gk.shape, const),
                  pl.BlockSpec(qfeat.shape, const), pl.BlockSpec(bdm.shape, const)],
        out_specs=(pl.BlockSpec((tm, 2 * DIFF_QK), lambda i: (i, 0)),
                   pl.BlockSpec((tm, 2 * DIFF_QK), lambda i: (i, 0)),
                   pl.BlockSpec((1, 1, DIFF_V, tk), lambda i: (i // nk, i % nk, 0, 0))),
        compiler_params=_cparams(("arbitrary",)),
        name="aprep",
    )(att_in, gq, gk, qfeat, bdm)


def _attn_kernel(slope_ref, q_ref, k_ref, vt_ref, lamv_ref, sg_ref, o_ref, m_s, l_s, acc_s, sa, sb,
                 *, tq, lam_init):
    h = pl.program_id(1)
    qi = pl.program_id(2)
    slope2 = slope_ref[h]
    dv = 2 * DIFF_DH
    qs = [q_ref[:, 0:LANES], q_ref[:, LANES:2 * LANES]]

    m_s[...] = jnp.full_like(m_s, -jnp.inf)
    l_s[...] = jnp.zeros_like(l_s)
    acc_s[...] = jnp.zeros_like(acc_s)

    def scores(kj):
        k0 = pl.multiple_of(kj * tq, tq)
        return tuple(_dot_nt(k_ref[pl.ds(k0, tq), c * LANES:(c + 1) * LANES], qs[c]) for c in range(2))

    def accumulate(ss, kj):
        vt = vt_ref[0, kj]
        off = slope2 * ((kj - qi) * tq).astype(F32)
        for c in range(2):
            s = ss[c]
            m_old = m_s[c]
            m_new = jnp.maximum(m_old, jnp.max(s, axis=0, keepdims=True) + off)
            alpha = jnp.exp2(m_old - m_new)
            p = jnp.exp2(s - (m_new - off))
            l_s[c] = alpha * l_s[c] + jnp.sum(p, axis=0, keepdims=True)
            acc_s[c] = alpha * acc_s[c] + _dot(vt, p.astype(BF16))
            m_s[c] = m_new

    kpos = lax.broadcasted_iota(jnp.int32, (tq, tq), 0)
    qpos = lax.broadcasted_iota(jnp.int32, (tq, tq), 1)
    allowed = (kpos // CHUNK) <= (qpos // CHUNK)
    fut = jnp.maximum(kpos - qpos, 0).astype(F32)
    add = jnp.where(allowed, (-2.0 * slope2) * fut, -jnp.inf)
    accumulate(tuple(s + add for s in scores(qi)), qi)

    def put(slot, ss):
        for c in range(2):
            slot[c] = ss[c]

    npairs = qi // 2
    put(sa, scores(0))

    def pair(j, carry):
        a = 2 * j
        put(sb, scores(a + 1))
        accumulate((sa[0], sa[1]), a)
        put(sa, scores(jnp.minimum(a + 2, qi - 1)))
        accumulate((sb[0], sb[1]), a + 1)
        return carry
    lax.fori_loop(0, npairs, pair, 0)

    @pl.when(qi % 2 == 1)
    def _():
        accumulate((sa[0], sa[1]), qi - 1)

    lq1, lk1, lq2, lk2 = (lamv_ref[i:i + 1, :] for i in range(4))
    lam = (jnp.exp(jnp.sum(lq1 * lk1, axis=-1, keepdims=True))
           - jnp.exp(jnp.sum(lq2 * lk2, axis=-1, keepdims=True)) + lam_init)
    o = acc_s[0] / l_s[0] - lam * (acc_s[1] / l_s[1])
    on = o * lax.rsqrt(jnp.mean(o * o, axis=0, keepdims=True) + NORM_EPS) * sg_ref[...] * (1.0 - lam_init)
    o_ref[...] = on.T


def _attn(slopes2, qa, ka, vt, lamv, sg_col, nb, seq, tq, lam_init):
    t = qa.shape[0]
    nq = seq // tq
    dv = 2 * DIFF_DH
    grid_spec = pltpu.PrefetchScalarGridSpec(
        num_scalar_prefetch=1,
        grid=(nb, DIFF_HEADS, nq),
        in_specs=[pl.BlockSpec((tq, 2 * LANES), lambda b, h, i, s: (b * nq + i, h)),
                  pl.BlockSpec((seq, 2 * LANES), lambda b, h, i, s: (b, h)),
                  pl.BlockSpec((1, nq, dv, tq), lambda b, h, i, s: (b, 0, h, 0)),
                  pl.BlockSpec(lamv.shape, lambda b, h, i, s: (0, 0)),
                  pl.BlockSpec(sg_col.shape, lambda b, h, i, s: (0, 0))],
        out_specs=pl.BlockSpec((tq, dv), lambda b, h, i, s: (b * nq + i, h)),
        scratch_shapes=[pltpu.VMEM((2, 1, tq), F32), pltpu.VMEM((2, 1, tq), F32),
                        pltpu.VMEM((2, dv, tq), F32),
                        pltpu.VMEM((2, tq, tq), F32), pltpu.VMEM((2, tq, tq), F32)],
    )
    return pl.pallas_call(
        functools.partial(_attn_kernel, tq=tq, lam_init=lam_init),
        out_shape=jax.ShapeDtypeStruct((t, DIFF_V), F32),
        grid_spec=grid_spec,
        compiler_params=_cparams(("arbitrary", "arbitrary", "arbitrary")),
        name="attn",
    )(slopes2, qa, ka, vt, lamv, sg_col)


def _pack_halves(x):
    n = x.shape[1] // 2
    lo = pltpu.bitcast(x[:, :n].astype(BF16).astype(F32), jnp.uint32)
    hi = pltpu.bitcast(x[:, n:].astype(BF16).astype(F32), jnp.uint32)
    return (lo >> 16) | (hi & jnp.uint32(0xFFFF0000))


def _unpack_halves(p):
    lo = pltpu.bitcast(p << 16, F32)
    hi = pltpu.bitcast(p & jnp.uint32(0xFFFF0000), F32)
    return lo, hi


def _post_kernel(x_ref, oa_ref, ob_ref, gate_ref, g1_ref, sh2_ref, sc2_ref, n2_ref,
                 woa_ref, wob_ref, wout_ref, x1_ref, h2_ref, hp_ref):
    ya = _dot(oa_ref[...].astype(BF16), woa_ref[...])
    yb = _dot(ob_ref[...].astype(BF16), wob_ref[...])
    merged = jax.nn.sigmoid(gate_ref[:, 0:D_MODEL]) * ya + jax.nn.sigmoid(gate_ref[:, D_MODEL:]) * yb
    y = _dot(merged.astype(BF16), wout_ref[...])
    x1 = x_ref[...] + g1_ref[0] * y
    x1_ref[...] = x1
    n = x1 * lax.rsqrt(jnp.mean(x1 * x1, axis=-1, keepdims=True) + NORM_EPS) * n2_ref[...]
    h2 = n * (1.0 + sc2_ref[0]) + sh2_ref[0]
    h2_ref[...] = h2
    hp_ref[...] = _pack_halves(h2)


def _post(x2, oa, ob, gates, mod3, norm2_g, woa, wob, wout, seq):
    t = x2.shape[0]
    tm = min(256, seq)
    tpb = seq // tm
    const = lambda i: (0, 0)
    row = lambda n: pl.BlockSpec((tm, n), lambda i: (i, 0))
    modspec = lambda j: pl.BlockSpec((1, 1, D_MODEL), lambda i: (i // tpb, 0, j))
    wspec = lambda w: pl.BlockSpec(w.shape, const, pipeline_mode=pl.Buffered(1))
    return pl.pallas_call(
        _post_kernel,
        out_shape=(jax.ShapeDtypeStruct((t, D_MODEL), F32), jax.ShapeDtypeStruct((t, D_MODEL), F32),
                   jax.ShapeDtypeStruct((t, D_MODEL // 2), jnp.uint32)),
        grid=(t // tm,),
        in_specs=[row(D_MODEL), row(GDN_V), row(DIFF_V), row(2 * D_MODEL),
                  modspec(2), modspec(3), modspec(4), pl.BlockSpec((1, D_MODEL), const),
                  wspec(woa), wspec(wob), wspec(wout)],
        out_specs=(row(D_MODEL), row(D_MODEL), row(D_MODEL // 2)),
        compiler_params=_cparams(("arbitrary",)),
        name="post",
    )(x2, oa, ob, gates, mod3, mod3, mod3, norm2_g, woa, wob, wout)


def _router_kernel(h_ref, wrt_ref, rb_ref, idx_ref, w_ref, cnt_ref, cnt_s):
    step = pl.program_id(0)

    @pl.when(step == 0)
    def _():
        cnt_s[...] = jnp.zeros_like(cnt_s)

    tm = h_ref.shape[0]
    neg = -jnp.inf
    logits = lax.dot_general(wrt_ref[...], h_ref[...], (((1,), (1,)), ((), ())), precision=HIGHEST,
                             preferred_element_type=F32)
    scores = jax.nn.sigmoid(logits)
    choice = scores + rb_ref[...]
    eid = lax.broadcasted_iota(jnp.int32, (N_EXPERTS, tm), 0).astype(F32)
    c3 = choice.reshape(N_GROUPS, GROUP_SIZE, tm)
    e3 = eid.reshape(N_GROUPS, GROUP_SIZE, tm)
    m1 = jnp.max(c3, axis=1, keepdims=True)
    i1 = jnp.min(jnp.where(c3 == m1, e3, float(N_EXPERTS)), axis=1, keepdims=True)
    m2 = jnp.max(jnp.where(e3 == i1, neg, c3), axis=1, keepdims=True)
    gscore = (m1 + m2).reshape(N_GROUPS, tm)
    gid = lax.broadcasted_iota(jnp.int32, (N_GROUPS, tm), 0).astype(F32)
    gsel = jnp.zeros((N_GROUPS, tm), F32)
    for _ in range(TOPK_GROUPS):
        m = jnp.max(gscore, axis=0, keepdims=True)
        g = jnp.min(jnp.where(gscore == m, gid, float(N_GROUPS)), axis=0, keepdims=True)
        hit = gid == g
        gsel = jnp.where(hit, 1.0, gsel)
        gscore = jnp.where(hit, neg, gscore)
    masked = jnp.where(gsel.reshape(N_GROUPS, 1, tm) > 0.0, c3, neg).reshape(N_EXPERTS, tm)
    idx_rows, w_rows = [], []
    sel = jnp.zeros((N_EXPERTS, tm), F32)
    for k in range(TOP_K):
        m = jnp.max(masked, axis=0, keepdims=True)
        i = jnp.min(jnp.where(masked == m, eid, float(N_EXPERTS)), axis=0, keepdims=True)
        hit = eid == i
        idx_rows.append(i)
        w_rows.append(jnp.sum(jnp.where(hit, scores, 0.0), axis=0, keepdims=True))
        sel = jnp.where(hit, 1.0, sel)
        masked = jnp.where(hit, neg, masked)
    ws = jnp.concatenate(w_rows, axis=0)
    idx_ref[...] = jnp.concatenate(idx_rows, axis=0).astype(jnp.int32)
    w_ref[...] = ws / jnp.sum(ws, axis=0, keepdims=True) * ROUTED_SCALE
    ones = jnp.ones((SUBLANES, tm), BF16)
    cnt_s[...] = cnt_s[...] + _dot_nt(ones, sel.astype(BF16))[0:1, :]
    cnt_ref[...] = cnt_s[...]


def _router(h2, w_router_t, rbias_col):
    t = h2.shape[0]
    tm = min(256, t)
    const = lambda i: (0, 0)
    return pl.pallas_call(
        _router_kernel,
        out_shape=(jax.ShapeDtypeStruct((TOP_K, t), jnp.int32),
                   jax.ShapeDtypeStruct((TOP_K, t), F32),
                   jax.ShapeDtypeStruct((1, N_EXPERTS), F32)),
        grid=(t // tm,),
        in_specs=[pl.BlockSpec((tm, D_MODEL), lambda i: (i, 0)),
                  pl.BlockSpec(w_router_t.shape, const), pl.BlockSpec(rbias_col.shape, const)],
        out_specs=(pl.BlockSpec((TOP_K, tm), lambda i: (0, i)),
                   pl.BlockSpec((TOP_K, tm), lambda i: (0, i)),
                   pl.BlockSpec((1, N_EXPERTS), const)),
        scratch_shapes=[pltpu.VMEM((1, N_EXPERTS), F32)],
        compiler_params=_cparams(("arbitrary",)),
        name="router",
    )(h2, w_router_t, rbias_col)


def _rank_kernel(idx_ref, ps_ref, dest_ref, run_s):
    step = pl.program_id(0)

    @pl.when(step == 0)
    def _():
        run_s[...] = jnp.zeros_like(run_s)

    tm = idx_ref.shape[0]
    idx = idx_ref[...]
    lane = lax.broadcasted_iota(jnp.int32, (tm, N_EXPERTS), 1)
    gl = lax.broadcasted_iota(jnp.int32, (tm, LANES), 1)
    hits = [lane == idx[:, k:k + 1] for k in range(TOP_K)]
    sel = jnp.zeros((tm, N_EXPERTS), F32)
    for hit in hits:
        sel = jnp.where(hit, 1.0, sel)
    ri = lax.broadcasted_iota(jnp.int32, (tm, tm), 0)
    ci = lax.broadcasted_iota(jnp.int32, (tm, tm), 1)
    before = _dot((ri > ci).astype(BF16), sel.astype(BF16))
    base = before + run_s[...] + ps_ref[...]
    dest = jnp.zeros((tm, LANES), F32)
    for k, hit in enumerate(hits):
        dk = jnp.sum(jnp.where(hit, base, 0.0), axis=-1, keepdims=True)
        dest = jnp.where(gl == k, dk, dest)
    dest_ref[...] = dest.astype(jnp.int32)
    run_s[...] = run_s[...] + jnp.sum(sel, axis=0, keepdims=True)


def _rank(idx, pstart):
    t = idx.shape[0]
    tm = min(256, t)
    const = lambda i: (0, 0)
    return pl.pallas_call(
        _rank_kernel,
        out_shape=jax.ShapeDtypeStruct((t, LANES), jnp.int32),
        grid=(t // tm,),
        in_specs=[pl.BlockSpec((tm, LANES), lambda i: (i, 0)), pl.BlockSpec(pstart.shape, const)],
        out_specs=pl.BlockSpec((tm, LANES), lambda i: (i, 0)),
        scratch_shapes=[pltpu.VMEM((1, N_EXPERTS), F32)],
        compiler_params=_cparams(("arbitrary",)),
        name="rank",
    )(idx, pstart)


def _row_copy(src, dst, sem):
    return pltpu.make_async_copy(src, dst, sem)


def _dispatch_kernel(dest_ref, h_ref, xs_in, xs_ref, sem, *, tm):
    del xs_in

    def start_row(r, c):
        for k in range(TOP_K):
            d = dest_ref[r * TOP_K + k]
            _row_copy(h_ref.at[pl.ds(r, 1), :], xs_ref.at[pl.ds(d, 1), :], sem).start(priority=k % 2)
        return c

    lax.fori_loop(0, tm, start_row, 0)
    for k in range(TOP_K):
        _row_copy(h_ref, xs_ref.at[pl.ds(0, tm), :], sem).wait()


def _dispatch(dest_flat, h2, xs0):
    t = h2.shape[0]
    tm = min(256, t)
    return pl.pallas_call(
        functools.partial(_dispatch_kernel, tm=tm),
        out_shape=jax.ShapeDtypeStruct(xs0.shape, xs0.dtype),
        grid=(t // tm,),
        in_specs=[pl.BlockSpec((tm * TOP_K,), lambda i: (i,), memory_space=pltpu.SMEM),
                  pl.BlockSpec((tm, h2.shape[1]), lambda i: (i, 0)),
                  pl.BlockSpec(memory_space=pl.ANY)],
        out_specs=pl.BlockSpec(memory_space=pl.ANY),
        scratch_shapes=[pltpu.SemaphoreType.DMA(())],
        input_output_aliases={2: 0},
        compiler_params=_cparams(("arbitrary",)),
        name="dispatch",
    )(dest_flat, h2, xs0)


SC_CORES = 2
SC_SUBCORES = 16
SC_LANES = 16
INVERT_CHUNK = 4096


def _invert(dest_flat, n_rows, n_tokens):
    n_assign = dest_flat.shape[0]
    workers = SC_CORES * SC_SUBCORES
    rpw = n_rows // workers
    assert rpw * workers == n_rows and rpw % SC_LANES == 0 and n_assign % INVERT_CHUNK == 0
    scratch_base = TOP_K * n_tokens
    log2_k = TOP_K.bit_length() - 1

    def body(dest_hbm, inv_hbm, loc, chunk_v):
        wid = lax.axis_index("s") * SC_CORES + lax.axis_index("c")
        lo = wid * rpw
        lane = lax.iota(jnp.int32, SC_LANES)

        def init(i, c):
            loc[pl.ds(i * SC_LANES, SC_LANES)] = scratch_base + lo + i * SC_LANES + lane
            return c
        lax.fori_loop(0, rpw // SC_LANES, init, 0)

        def do_chunk(ci, c):
            pltpu.sync_copy(dest_hbm.at[pl.ds(ci * INVERT_CHUNK, INVERT_CHUNK)], chunk_v)

            def inner(j, cc):
                rel = chunk_v[pl.ds(j * SC_LANES, SC_LANES)] - lo
                mine = (rel >= 0) & (rel < rpw)
                a = ci * INVERT_CHUNK + j * SC_LANES + lane
                val = (a & (TOP_K - 1)) * n_tokens + lax.shift_right_logical(a, log2_k)
                plsc.store_scatter(loc, [jnp.where(mine, rel, 0)], val, mask=mine)
                return cc
            lax.fori_loop(0, INVERT_CHUNK // SC_LANES, inner, 0)
            return c
        lax.fori_loop(0, n_assign // INVERT_CHUNK, do_chunk, 0)
        pltpu.sync_copy(loc, inv_hbm.at[pl.ds(lo, rpw)])

    mesh = plsc.VectorSubcoreMesh(core_axis_name="c", subcore_axis_name="s",
                                  num_cores=SC_CORES, num_subcores=SC_SUBCORES)
    return pl.kernel(body, out_type=jax.ShapeDtypeStruct((n_rows,), jnp.int32), mesh=mesh,
                     scratch_types=[pltpu.VMEM((rpw,), jnp.int32), pltpu.VMEM((INVERT_CHUNK,), jnp.int32)],
                     compiler_params=pltpu.CompilerParams(needs_layout_passes=False),
                     name="invert")(dest_flat)


EXPERT_RING = 5
EXPERT_AHEAD = EXPERT_RING - 2
WEIGHT_RING = 3


def _experts_kernel(start_ref, count_ref, xs_ref, inv_ref, wgu_ref, wdn_ref, g_ref,
                    wgu_f, wdn_f, wgu_s, wdn_s, xbuf, ybuf, inv_s, sem_w, sem_in, sem_inv, sem_out,
                    *, nblk_total, spare_row0):
    half = D_MODEL // 2
    last_e = N_EXPERTS - 1
    used = start_ref[last_e] + count_ref[last_e]

    def rows(g):
        return pl.ds(pl.multiple_of(g * EXPERT_BLOCK, EXPERT_BLOCK), EXPERT_BLOCK)

    def in_copies(g):
        slot = g % EXPERT_RING
        gc = jnp.minimum(g, nblk_total - 1)
        return (pltpu.make_async_copy(xs_ref.at[rows(gc), :], xbuf.at[slot], sem_in.at[slot]),
                pltpu.make_async_copy(inv_ref.at[pl.ds(gc, 1), :], inv_s.at[slot], sem_inv.at[slot]))

    def out_wait(slot):
        pltpu.make_async_copy(ybuf.at[slot], g_ref.at[pl.ds(0, EXPERT_BLOCK), :], sem_out.at[slot]).wait()

    def w_copies(e):
        slot = e % WEIGHT_RING
        return (pltpu.make_async_copy(wgu_ref.at[e], wgu_f.at[slot], sem_w.at[0, slot]),
                pltpu.make_async_copy(wdn_ref.at[e], wdn_f.at[slot], sem_w.at[1, slot]))

    def issue_rows(slot):
        for r in range(EXPERT_BLOCK):
            d = inv_s[slot, 0, r]
            _row_copy(ybuf.at[slot, pl.ds(r, 1), :], g_ref.at[pl.ds(d, 1), :], sem_out.at[slot]).start(priority=r % 2)

    for e0 in range(WEIGHT_RING - 1):
        for cp in w_copies(e0):
            cp.start()
    for g in range(EXPERT_AHEAD):
        for cp in in_copies(g):
            cp.start()
    ybuf[EXPERT_RING - 1] = jnp.zeros((EXPERT_BLOCK, half), jnp.uint32)
    for r in range(EXPERT_BLOCK):
        inv_s[EXPERT_RING - 1, 0, r] = spare_row0 + r

    def expert(e, carry):
        for cp in w_copies(e):
            cp.wait()

        @pl.when(e + WEIGHT_RING - 1 <= last_e)
        def _():
            for cp in w_copies(e + WEIGHT_RING - 1):
                cp.start()

        slot = e % WEIGHT_RING
        wgu_s[...] = wgu_f[slot].astype(BF16)
        wdn_s[...] = wdn_f[slot].astype(BF16)
        first = start_ref[e]

        def block(i, c):
            g = first + i
            for cp in in_copies(g):
                cp.wait()
            for cp in in_copies(g + EXPERT_AHEAD):
                cp.start()
            s = g % EXPERT_RING

            @pl.when(g >= EXPERT_RING - 1)
            def _():
                out_wait(s)

            issue_rows((g + EXPERT_RING - 1) % EXPERT_RING)
            lo, hi = _unpack_halves(xbuf[s])
            gu = _dot(lo.astype(BF16), wgu_s[0:half, :]) + _dot(hi.astype(BF16), wgu_s[half:, :])
            act = _silu(gu[:, :EXPERT_FF]) * gu[:, EXPERT_FF:]
            ybuf[s] = _pack_halves(_dot(act.astype(BF16), wdn_s[...]))
            return c

        lax.fori_loop(0, count_ref[e], block, 0)
        return carry

    lax.fori_loop(0, N_EXPERTS, expert, 0)
    issue_rows((used + EXPERT_RING - 1) % EXPERT_RING)

    for j in range(EXPERT_AHEAD):
        for cp in in_copies(used + j):
            cp.wait()
    for j in range(EXPERT_RING):
        @pl.when(used > j)
        def _():
            out_wait((used - 1 - j) % EXPERT_RING)

    @pl.when(used < EXPERT_RING)
    def _():
        out_wait(EXPERT_RING - 1)


def _experts(blk_start, blk_count, xs, inv2, w_gu, w_dn, n_slots):
    rows, half = xs.shape
    nblk = rows // EXPERT_BLOCK
    grid_spec = pltpu.PrefetchScalarGridSpec(
        num_scalar_prefetch=2,
        grid=(1,),
        in_specs=[pl.BlockSpec(memory_space=pl.ANY), pl.BlockSpec(memory_space=pl.ANY),
                  pl.BlockSpec(memory_space=pl.ANY), pl.BlockSpec(memory_space=pl.ANY)],
        out_specs=pl.BlockSpec(memory_space=pl.ANY),
        scratch_shapes=[pltpu.VMEM((WEIGHT_RING, D_MODEL, 2 * EXPERT_FF), F32),
                        pltpu.VMEM((WEIGHT_RING, EXPERT_FF, D_MODEL), F32),
                        pltpu.VMEM((D_MODEL, 2 * EXPERT_FF), BF16), pltpu.VMEM((EXPERT_FF, D_MODEL), BF16),
                        pltpu.VMEM((EXPERT_RING, EXPERT_BLOCK, half), jnp.uint32),
                        pltpu.VMEM((EXPERT_RING, EXPERT_BLOCK, half), jnp.uint32),
                        pltpu.SMEM((EXPERT_RING, 1, EXPERT_BLOCK), jnp.int32),
                        pltpu.SemaphoreType.DMA((2, WEIGHT_RING)), pltpu.SemaphoreType.DMA((EXPERT_RING,)),
                        pltpu.SemaphoreType.DMA((EXPERT_RING,)), pltpu.SemaphoreType.DMA((EXPERT_RING,))],
    )
    return pl.pallas_call(
        functools.partial(_experts_kernel, nblk_total=nblk, spare_row0=n_slots - EXPERT_BLOCK),
        out_shape=jax.ShapeDtypeStruct((n_slots, half), jnp.uint32),
        grid_spec=grid_spec,
        compiler_params=_cparams(("arbitrary",)),
        name="experts",
    )(blk_start, blk_count, xs, inv2, w_gu, w_dn)


def _combine_kernel(x1_ref, h_ref, w_ref, g2_ref, wsgu_ref, wsdn_ref, buf, o_ref):
    su = _dot(h_ref[...].astype(BF16), wsgu_ref[...])
    y = _dot((_silu(su[:, :SHARED_FF]) * su[:, SHARED_FF:]).astype(BF16), wsdn_ref[...])
    w = w_ref[...]
    half = D_MODEL // 2
    ylo, yhi = y[:, :half], y[:, half:]
    for k in range(TOP_K):
        lo, hi = _unpack_halves(buf[k])
        wk = w[:, k:k + 1]
        ylo = ylo + wk * lo
        yhi = yhi + wk * hi
    g2 = g2_ref[0]
    o_ref[:, :half] = x1_ref[:, :half] + g2[:, :half] * ylo
    o_ref[:, half:] = x1_ref[:, half:] + g2[:, half:] * yhi


def _combine(x1, h2, wts, mod3, wsgu, wsdn, slots3, seq):
    t = x1.shape[0]
    tm = min(256, seq)
    tpb = seq // tm
    const = lambda i: (0, 0)
    row = lambda n: pl.BlockSpec((tm, n), lambda i: (i, 0))
    return pl.pallas_call(
        _combine_kernel,
        out_shape=jax.ShapeDtypeStruct((t, D_MODEL), F32),
        grid=(t // tm,),
        in_specs=[row(D_MODEL), row(D_MODEL), row(LANES),
                  pl.BlockSpec((1, 1, D_MODEL), lambda i: (i // tpb, 0, 5)),
                  pl.BlockSpec(wsgu.shape, const), pl.BlockSpec(wsdn.shape, const),
                  pl.BlockSpec((TOP_K, tm, D_MODEL // 2), lambda i: (0, i, 0))],
        out_specs=row(D_MODEL),
        compiler_params=_cparams(("arbitrary",)),
        name="combine",
    )(x1, h2, wts, mod3, wsgu, wsdn, slots3)


def _layer(x, c, layer, w_ada, b_ada, norm1_g, w_in, conv_w, a_log, dt_bias, gdn_norm_g, w_o_gdn,
           q_norm_g, k_norm_g, lambda_q1, lambda_k1, lambda_q2, lambda_k2, subln_g, w_o_diff,
           w_out, norm2_g, w_router, router_bias, w_exp_gate_up, w_exp_down,
           w_shared_gate_up, w_shared_down):
    nb, seq, d = x.shape
    t = nb * seq
    lam_init = 0.8 - 0.6 * math.exp(-0.3 * layer)
    x2 = x.reshape(t, d)

    c_pad = jnp.pad(c, ((0, SUBLANES - nb % SUBLANES if nb % SUBLANES else 0), (0, 0)))
    mod = _ada(c_pad, w_ada, b_ada.reshape(1, -1))[:nb]
    mod3 = mod.reshape(nb, 1, 6 * d)

    o_bd = 2 * GDN_QK + 2 * GDN_V
    o_att = o_bd + 2 * GDN_HEADS
    o_gate = o_att + 2 * DIFF_QK + DIFF_V
    w_gdn = w_in[:, :o_bd].astype(BF16)
    w_bd = jnp.pad(w_in[:, o_bd:o_att], ((0, 0), (0, LANES - 2 * GDN_HEADS))).astype(BF16)
    w_att = w_in[:, o_att:o_gate].astype(BF16)
    w_gate = w_in[:, o_gate:].astype(BF16)
    gdn_in, bd, att_in, gates = _inproj(x2, mod3, norm1_g.reshape(1, d), w_gdn, w_bd, w_att, w_gate, seq)

    aparams = jnp.zeros((SUBLANES, LANES), F32)
    aparams = aparams.at[0, GDN_HEADS:2 * GDN_HEADS].set(a_log).at[1, GDN_HEADS:2 * GDN_HEADS].set(dt_bias)
    oa = _gdn(gdn_in.reshape(nb, seq, -1), bd.reshape(nb, seq, LANES), conv_w, aparams,
              gdn_norm_g.reshape(1, GDN_DV)).reshape(t, GDN_V)

    tq = min(512, seq)
    slopes = [2.0 ** (-8.0 * (h + 1) / DIFF_HEADS) for h in range(DIFF_HEADS)]
    c3 = _bf16_split3(LOG2E)
    qfeat = np.zeros((1, 2 * DIFF_HEADS * DIFF_DH), np.float32)
    for g in range(2 * DIFF_HEADS):
        for j in range(6):
            qfeat[0, g * DIFF_DH + j] = slopes[g // 2] * c3[j // 2]
    grp = np.arange(DIFF_QK) // DIFF_DH
    bdm = jnp.asarray((grp[:, None] == grp[None, :]).astype(np.float32))
    qa, ka, vt = _aprep(att_in, jnp.tile(q_norm_g, 2 * DIFF_HEADS).reshape(1, -1),
                        jnp.tile(k_norm_g, 2 * DIFF_HEADS).reshape(1, -1), jnp.asarray(qfeat), bdm, nb, seq, tq)
    slopes2 = jnp.asarray([s * (c3[0] + c3[1] + c3[2]) for s in slopes], F32)
    lamv = jnp.zeros((SUBLANES, DIFF_DH), F32)
    lamv = lamv.at[0].set(lambda_q1).at[1].set(lambda_k1).at[2].set(lambda_q2).at[3].set(lambda_k2)
    ob = _attn(slopes2, qa, ka, vt, lamv, subln_g.reshape(-1, 1), nb, seq, tq, lam_init)

    x1, h2, hp = _post(x2, oa, ob, gates, mod3, norm2_g.reshape(1, d), w_o_gdn.astype(BF16),
                       w_o_diff.astype(BF16), w_out.astype(BF16), seq)

    idx_t, wts_t, counts = _router(h2, w_router.T, router_bias.reshape(-1, 1))
    idx = jnp.pad(idx_t.T, ((0, 0), (0, LANES - TOP_K)))
    wts = jnp.pad(wts_t.T, ((0, 0), (0, LANES - TOP_K)))
    cnt = counts[0].astype(jnp.int32)
    padded = (cnt + EXPERT_BLOCK - 1) // EXPERT_BLOCK * EXPERT_BLOCK
    pends = jnp.cumsum(padded)
    pstart = (pends - padded).astype(F32).reshape(1, -1)
    nblk = -(-(t * TOP_K) // EXPERT_BLOCK) + N_EXPERTS
    blk_start = ((pends - padded) // EXPERT_BLOCK).astype(jnp.int32)
    blk_count = (padded // EXPERT_BLOCK).astype(jnp.int32)
    dest = _rank(idx, pstart)
    dest_flat = dest[:, :TOP_K].reshape(-1)
    n_rows = nblk * EXPERT_BLOCK
    assert n_rows % t == 0
    xs = _dispatch(dest_flat, hp, jnp.zeros((n_rows, d // 2), jnp.uint32))
    inv = _invert(dest_flat, n_rows, t)
    assert t >= EXPERT_BLOCK
    slots = _experts(blk_start, blk_count, xs, inv.reshape(nblk, EXPERT_BLOCK), w_exp_gate_up, w_exp_down,
                     TOP_K * t + n_rows + t)
    out = _combine(x1, h2, wts, mod3, w_shared_gate_up.astype(BF16), w_shared_down.astype(BF16),
                   slots.reshape(-1, t, d // 2), seq)
    return out.reshape(nb, seq, d)


def kernel(x, c, w_ada, b_ada, norm1_g, w_in, conv_w, a_log, dt_bias, gdn_norm_g, w_o_gdn, q_norm_g, k_norm_g, lambda_q1, lambda_k1, lambda_q2, lambda_k2, subln_g, w_o_diff, w_out, norm2_g, w_router, router_bias, w_exp_gate_up, w_exp_down, w_shared_gate_up, w_shared_down):
    params = (w_ada, b_ada, norm1_g, w_in, conv_w, a_log, dt_bias, gdn_norm_g, w_o_gdn, q_norm_g,
              k_norm_g, lambda_q1, lambda_k1, lambda_q2, lambda_k2, subln_g, w_o_diff, w_out, norm2_g,
              w_router, router_bias, w_exp_gate_up, w_exp_down, w_shared_gate_up, w_shared_down)
    for layer in range(w_ada.shape[0]):
        x = _layer(x, c, layer, *(p[layer] for p in params))
    return x
```

```python
import functools
import math

import jax
import jax.numpy as jnp
import numpy as np
from jax import lax
from jax.experimental import pallas as pl
from jax.experimental.pallas import tpu as pltpu
from jax.experimental.pallas import tpu_sc as plsc

F32 = jnp.float32
BF16 = jnp.bfloat16
HIGHEST = lax.Precision.HIGHEST

D_MODEL = 1024
CHUNK = 64
GDN_HEADS = 4
GDN_DK = 128
GDN_DV = 128
GDN_CONV = 4
DIFF_HEADS = 4
DIFF_DH = 64
N_EXPERTS = 256
TOP_K = 8
N_GROUPS = 8
TOPK_GROUPS = 4
EXPERT_FF = 256
SHARED_FF = 256
ROUTED_SCALE = 2.5
NORM_EPS = 1e-6
GROUP_SIZE = N_EXPERTS // N_GROUPS

GDN_QK = GDN_HEADS * GDN_DK
GDN_V = GDN_HEADS * GDN_DV
CONV_CH = 2 * GDN_QK + GDN_V
DIFF_QK = DIFF_HEADS * 2 * DIFF_DH
DIFF_V = DIFF_HEADS * 2 * DIFF_DH

LANES = 128
SUBLANES = 8
EXPERT_BLOCK = 128
INV_BASE = 8
ATTN_HEADS_PER_STEP = 2
LOG2E = math.log2(math.e)
VMEM_LIMIT = 56 * 1024 * 1024


def _cparams(sem):
    return pltpu.CompilerParams(dimension_semantics=sem, vmem_limit_bytes=VMEM_LIMIT)


def _dot(a, b):
    return jnp.dot(a, b, preferred_element_type=F32)


def _dot_nt(a, b):
    return lax.dot_general(a, b, (((1,), (1,)), ((), ())), preferred_element_type=F32)


def _dot_tn(a, b):
    return lax.dot_general(a, b, (((0,), (0,)), ((), ())), preferred_element_type=F32)


def _silu(x):
    return x * jax.nn.sigmoid(x)


def _bf16_split3(x):
    rnd = lambda v: float(np.float32(v).astype(BF16).astype(np.float32))
    a = rnd(x)
    b = rnd(x - a)
    c = rnd(x - a - b)
    return a, b, c


def _ada_kernel(c_ref, w_ref, b_ref, o_ref):
    s = _silu(c_ref[...])
    o_ref[...] = jnp.dot(s, w_ref[...], precision=HIGHEST, preferred_element_type=F32) + b_ref[...]


def _ada(c_pad, w_ada, b_ada):
    n = w_ada.shape[1]
    tn = 1024
    return pl.pallas_call(
        _ada_kernel,
        out_shape=jax.ShapeDtypeStruct((c_pad.shape[0], n), F32),
        grid=(n // tn,),
        in_specs=[pl.BlockSpec(c_pad.shape, lambda j: (0, 0)),
                  pl.BlockSpec((D_MODEL, tn), lambda j: (0, j)),
                  pl.BlockSpec((1, tn), lambda j: (0, j))],
        out_specs=pl.BlockSpec((c_pad.shape[0], tn), lambda j: (0, j)),
        compiler_params=_cparams(("arbitrary",)),
        name="ada",
    )(c_pad, w_ada, b_ada)


def _inproj_kernel(x_ref, shift_ref, scale_ref, g_ref, wg_ref, wbd_ref, wa_ref, wgt_ref,
                   gdn_ref, bd_ref, att_ref, gate_ref):
    x = x_ref[...]
    y = x * lax.rsqrt(jnp.mean(x * x, axis=-1, keepdims=True) + NORM_EPS) * g_ref[...]
    h = (y * (1.0 + scale_ref[0]) + shift_ref[0]).astype(BF16)
    gdn_ref[...] = _dot(h, wg_ref[...]).astype(gdn_ref.dtype)
    bd_ref[...] = _dot(h, wbd_ref[...])
    att_ref[...] = _dot(h, wa_ref[...]).astype(att_ref.dtype)
    gate_ref[...] = _dot(h, wgt_ref[...]).astype(gate_ref.dtype)


def _inproj(x2, mod3, norm1_g, w_gdn, w_bd, w_att, w_gate, seq):
    t = x2.shape[0]
    tm = min(256, seq)
    tiles_per_b = seq // tm
    const = lambda i: (0, 0)
    wspec = lambda w: pl.BlockSpec(w.shape, const, pipeline_mode=pl.Buffered(1))
    row = lambda n: pl.BlockSpec((tm, n), lambda i: (i, 0))
    return pl.pallas_call(
        _inproj_kernel,
        out_shape=(jax.ShapeDtypeStruct((t, w_gdn.shape[1]), BF16),
                   jax.ShapeDtypeStruct((t, LANES), F32),
                   jax.ShapeDtypeStruct((t, w_att.shape[1]), BF16),
                   jax.ShapeDtypeStruct((t, w_gate.shape[1]), BF16)),
        grid=(t // tm,),
        in_specs=[row(D_MODEL),
                  pl.BlockSpec((1, 1, D_MODEL), lambda i: (i // tiles_per_b, 0, 0)),
                  pl.BlockSpec((1, 1, D_MODEL), lambda i: (i // tiles_per_b, 0, 1)),
                  pl.BlockSpec((1, D_MODEL), const),
                  wspec(w_gdn), wspec(w_bd), wspec(w_att), wspec(w_gate)],
        out_specs=(row(w_gdn.shape[1]), row(LANES), row(w_att.shape[1]), row(w_gate.shape[1])),
        compiler_params=_cparams(("arbitrary",)),
        name="inproj",
    )(x2, mod3, mod3, norm1_g, w_gdn, w_bd, w_att, w_gate)


def _gdn_kernel(x_ref, bd_ref, cw_ref, ap_ref, ng_ref, o_ref, cbuf, state, *, nb, lg):
    step = pl.program_id(0)
    nc = lg // CHUNK

    @pl.when(step == 0)
    def _():
        cbuf[...] = jnp.zeros_like(cbuf)
        state[...] = jnp.zeros_like(state)

    ri = lax.broadcasted_iota(jnp.int32, (CHUNK, CHUNK), 0)
    ci = lax.broadcasted_iota(jnp.int32, (CHUNK, CHUNK), 1)
    causal = ri >= ci
    strict = ri > ci
    eye = (ri == ci).astype(F32)
    rl = lax.broadcasted_iota(jnp.int32, (lg, lg), 0)
    cl = lax.broadcasted_iota(jnp.int32, (lg, lg), 1)
    blocktri = ((rl >= cl) & ((rl // CHUNK) == (cl // CHUNK))).astype(F32)
    cw = cw_ref[...]
    a_row = ap_ref[0:1, :]
    dt_row = ap_ref[1:2, :]
    ng = ng_ref[...]

    units = []
    for b in range(nb):
        cbuf[b, SUBLANES:SUBLANES + lg, :] = x_ref[b, :, 0:CONV_CH].astype(F32)
        acc = cw[GDN_CONV - 1:GDN_CONV, :] * cbuf[b, SUBLANES:SUBLANES + lg, :]
        for j in range(GDN_CONV - 1):
            off = SUBLANES - (GDN_CONV - 1) + j
            acc = acc + cw[j:j + 1, :] * cbuf[b, off:off + lg, :]
        cbuf[b, 0:SUBLANES, :] = cbuf[b, lg:lg + SUBLANES, :]
        qkv = _silu(acc)

        bd = bd_ref[b]
        beta_t = jax.nn.sigmoid(bd)
        g_t = -jnp.exp(a_row) * jax.nn.softplus(bd + dt_row)
        gcum = jnp.dot(blocktri, g_t, precision=HIGHEST, preferred_element_type=F32)
        gcum_t = gcum.T
        egcum = jnp.exp(gcum)

        for h in range(GDN_HEADS):
            qh = qkv[:, h * GDN_DK:(h + 1) * GDN_DK]
            kh = qkv[:, GDN_QK + h * GDN_DK:GDN_QK + (h + 1) * GDN_DK]
            vh = qkv[:, 2 * GDN_QK + h * GDN_DV:2 * GDN_QK + (h + 1) * GDN_DV]
            qh = qh * lax.rsqrt(jnp.sum(qh * qh, axis=-1, keepdims=True) + NORM_EPS) * (GDN_DK ** -0.5)
            kh = kh * lax.rsqrt(jnp.sum(kh * kh, axis=-1, keepdims=True) + NORM_EPS)
            for c in range(nc):
                r0, r1 = c * CHUNK, (c + 1) * CHUNK
                q, k, v = qh[r0:r1], kh[r0:r1], vh[r0:r1]
                gc = gcum[r0:r1, 4 + h:5 + h]
                gr = gcum_t[4 + h:5 + h, r0:r1]
                egc = egcum[r0:r1, 4 + h:5 + h]
                bcol = beta_t[r0:r1, h:h + 1]
                gl = gcum[r1 - 1:r1, 4 + h:5 + h]
                kb = k.astype(BF16)
                units.append(dict(
                    b=b, h=h, c=c, kb=kb, bcol=bcol,
                    qkb=jnp.concatenate([q.astype(BF16), kb], axis=0),
                    decay=jnp.exp(jnp.where(causal, gc - gr, -jnp.inf)),
                    rhs=jnp.concatenate([v * bcol, k * (bcol * egc)], axis=-1).astype(BF16),
                    qg=(q * egc).astype(BF16), egl=jnp.exp(gl),
                    kdec=(k * jnp.exp(gl - gc)).astype(BF16)))

    for u in units:
        u["kq"] = _dot_nt(u["qkb"], u["kb"])
    def same_block(size):
        return (ri // size) == (ci // size)
    for u in units:
        lower = jnp.where(strict, u["bcol"] * u["kq"][CHUNK:] * u["decay"], 0.0)
        u["lower"] = lower
        diag = jnp.where(same_block(INV_BASE), lower, 0.0)
        u["xinv"] = eye - diag
        u["pw"] = diag.astype(BF16)
    for r in range(INV_BASE.bit_length() - 2):
        for u in units:
            u["pw"] = _dot(u["pw"], u["pw"]).astype(BF16)
        for u in units:
            u["xinv"] = u["xinv"] + _dot(u["xinv"].astype(BF16), u["pw"])
    size = INV_BASE
    while size < CHUNK:
        pair_off = same_block(2 * size) & jnp.logical_not(same_block(size))
        for u in units:
            u["xb"] = u["xinv"].astype(BF16)
            u["cx"] = _dot(jnp.where(pair_off, u["lower"], 0.0).astype(BF16), u["xb"]).astype(BF16)
        for u in units:
            u["xinv"] = u["xinv"] - _dot(u["xb"], u["cx"])
        size *= 2
    for u in units:
        u["sol"] = _dot(u["xinv"].astype(BF16), u["rhs"])
        u["a_qk"] = (u["kq"][:CHUNK] * u["decay"]).astype(BF16)

    st = {(b, h): state[b, h] for b in range(nb) for h in range(GDN_HEADS)}
    for c in range(nc):
        cu = [u for u in units if u["c"] == c]
        for u in cu:
            stb = st[(u["b"], u["h"])].astype(BF16)
            u["ws"] = _dot(jnp.concatenate([u["sol"][:, GDN_DV:].astype(BF16), u["qg"]], axis=0), stb)
        for u in cu:
            u["vnb"] = (u["sol"][:, :GDN_DV] - u["ws"][:CHUNK]).astype(BF16)
        for u in cu:
            u["out"] = u["ws"][CHUNK:] + _dot(u["a_qk"], u["vnb"])
        for u in cu:
            key = (u["b"], u["h"])
            st[key] = st[key] * u["egl"] + _dot_tn(u["kdec"], u["vnb"])
        for u in cu:
            b, h = u["b"], u["h"]
            r0, r1 = c * CHUNK, (c + 1) * CHUNK
            out = u["out"]
            on = out * lax.rsqrt(jnp.mean(out * out, axis=-1, keepdims=True) + NORM_EPS) * ng
            zh = x_ref[b, r0:r1, CONV_CH + h * GDN_DV:CONV_CH + (h + 1) * GDN_DV].astype(F32)
            o_ref[b, r0:r1, h * GDN_DV:(h + 1) * GDN_DV] = on * _silu(zh)
    for (b, h), s in st.items():
        state[b, h] = s


def _gdn(gdn_in3, bd3, conv_w, aparams, gdn_norm_g):
    nb, seq, _ = gdn_in3.shape
    lg = min(128, seq)
    const = lambda i: (0, 0)
    return pl.pallas_call(
        functools.partial(_gdn_kernel, nb=nb, lg=lg),
        out_shape=jax.ShapeDtypeStruct((nb, seq, GDN_V), F32),
        grid=(seq // lg,),
        in_specs=[pl.BlockSpec((nb, lg, gdn_in3.shape[2]), lambda i: (0, i, 0)),
                  pl.BlockSpec((nb, lg, LANES), lambda i: (0, i, 0)),
                  pl.BlockSpec(conv_w.shape, const),
                  pl.BlockSpec(aparams.shape, const),
                  pl.BlockSpec(gdn_norm_g.shape, const)],
        out_specs=pl.BlockSpec((nb, lg, GDN_V), lambda i: (0, i, 0)),
        scratch_shapes=[pltpu.VMEM((nb, lg + SUBLANES, CONV_CH), F32),
                        pltpu.VMEM((nb, GDN_HEADS, GDN_DK, GDN_DV), F32)],
        compiler_params=_cparams(("arbitrary",)),
        name="gdn",
    )(gdn_in3, bd3, conv_w, aparams, gdn_norm_g)


def _aprep_kernel(x_ref, gq_ref, gk_ref, qf_ref, bdm_ref, q_ref, k_ref, vt_ref, *, tk):
    x = x_ref[...].astype(F32)
    tm = x.shape[0]
    bdm = bdm_ref[...]

    def qknorm(v, gain):
        ms = jnp.dot(v * v, bdm, precision=HIGHEST, preferred_element_type=F32) * (1.0 / DIFF_DH)
        return v * lax.rsqrt(ms + NORM_EPS) * gain

    qn = qknorm(x[:, 0:DIFF_QK], gq_ref[...]) * (DIFF_DH ** -0.5 * LOG2E)
    kn = qknorm(x[:, DIFF_QK:2 * DIFF_QK], gk_ref[...])
    pos = pl.program_id(0) * tm + lax.broadcasted_iota(jnp.int32, (tm, DIFF_DH), 0)
    krel = pos % tk
    lane = lax.broadcasted_iota(jnp.int32, (tm, DIFF_DH), 1)
    hi = ((krel // 256) * 256).astype(F32)
    lo = (krel % 256).astype(F32)
    kfeat = jnp.where(lane < 6, jnp.where(lane % 2 == 0, hi, lo), 0.0)
    qfeat = qf_ref[...]
    qparts, kparts = [], []
    for g in range(2 * DIFF_HEADS):
        qparts += [qn[:, g * DIFF_DH:(g + 1) * DIFF_DH],
                   jnp.broadcast_to(qfeat[:, g * DIFF_DH:(g + 1) * DIFF_DH], (tm, DIFF_DH))]
        kparts += [kn[:, g * DIFF_DH:(g + 1) * DIFF_DH], kfeat]
    q_ref[...] = jnp.concatenate(qparts, axis=-1).astype(BF16)
    k_ref[...] = jnp.concatenate(kparts, axis=-1).astype(BF16)
    vt_ref[0, 0] = x[:, 2 * DIFF_QK:].T.astype(BF16)


def _aprep(att_in, gq, gk, qfeat, bdm, nb, seq, tk):
    t = att_in.shape[0]
    tm = tk
    nk = seq // tk
    const = lambda i: (0, 0)
    return pl.pallas_call(
        functools.partial(_aprep_kernel, tk=tk),
        out_shape=(jax.ShapeDtypeStruct((t, 2 * DIFF_QK), BF16),
                   jax.ShapeDtypeStruct((t, 2 * DIFF_QK), BF16),
                   jax.ShapeDtypeStruct((nb, nk, DIFF_V, tk), BF16)),
        grid=(t // tm,),
        in_specs=[pl.BlockSpec((tm, att_in.shape[1]), lambda i: (i, 0)),
                  pl.BlockSpec(gq.shape, const), pl.BlockSpec(gk.shape, const),
                  pl.BlockSpec(qfeat.shape, const), pl.BlockSpec(bdm.shape, const)],
        out_specs=(pl.BlockSpec((tm, 2 * DIFF_QK), lambda i: (i, 0)),
                   pl.BlockSpec((tm, 2 * DIFF_QK), lambda i: (i, 0)),
                   pl.BlockSpec((1, 1, DIFF_V, tk), lambda i: (i // nk, i % nk, 0, 0))),
        compiler_params=_cparams(("arbitrary",)),
        name="aprep",
    )(att_in, gq, gk, qfeat, bdm)


def _attn_kernel(slope_ref, q_ref, k_ref, vt_ref, lamv_ref, sg_ref, o_ref, m_s, l_s, acc_s, sa, sb,
                 *, tq, lam_init):
    hp = pl.program_id(1)
    qi = pl.program_id(2)
    dv = 2 * DIFF_DH
    nstream = 2 * ATTN_HEADS_PER_STEP
    slope2 = [slope_ref[hp * ATTN_HEADS_PER_STEP + c // 2] for c in range(nstream)]
    qs = [q_ref[:, c * LANES:(c + 1) * LANES] for c in range(nstream)]

    m_s[...] = jnp.full_like(m_s, -jnp.inf)
    l_s[...] = jnp.zeros_like(l_s)
    acc_s[...] = jnp.zeros_like(acc_s)

    def scores(kj):
        k0 = pl.multiple_of(kj * tq, tq)
        return tuple(_dot_nt(k_ref[pl.ds(k0, tq), c * LANES:(c + 1) * LANES], qs[c]) for c in range(nstream))

    def accumulate(ss, kj):
        rel = ((kj - qi) * tq).astype(F32)
        for c in range(nstream):
            hl = c // 2
            vt = vt_ref[0, kj, hl * dv:(hl + 1) * dv, :]
            off = slope2[c] * rel
            s = ss[c]
            m_old = m_s[c]
            m_new = jnp.maximum(m_old, jnp.max(s, axis=0, keepdims=True) + off)
            alpha = jnp.exp2(m_old - m_new)
            p = jnp.exp2(s - (m_new - off))
            l_s[c] = alpha * l_s[c] + jnp.sum(p, axis=0, keepdims=True)
            acc_s[c] = alpha * acc_s[c] + _dot(vt, p.astype(BF16))
            m_s[c] = m_new

    kpos = lax.broadcasted_iota(jnp.int32, (tq, tq), 0)
    qpos = lax.broadcasted_iota(jnp.int32, (tq, tq), 1)
    allowed = (kpos // CHUNK) <= (qpos // CHUNK)
    fut = jnp.maximum(kpos - qpos, 0).astype(F32)
    adds = [jnp.where(allowed, (-2.0 * slope2[2 * hl]) * fut, -jnp.inf) for hl in range(ATTN_HEADS_PER_STEP)]
    accumulate(tuple(s + adds[c // 2] for c, s in enumerate(scores(qi))), qi)

    def put(slot, ss):
        for c in range(nstream):
            slot[c] = ss[c]

    def get(slot):
        return tuple(slot[c] for c in range(nstream))

    npairs = qi // 2
    put(sa, scores(0))

    def pair(j, carry):
        a = 2 * j
        put(sb, scores(a + 1))
        accumulate(get(sa), a)
        put(sa, scores(jnp.minimum(a + 2, qi - 1)))
        accumulate(get(sb), a + 1)
        return carry
    lax.fori_loop(0, npairs, pair, 0)

    @pl.when(qi % 2 == 1)
    def _():
        accumulate(get(sa), qi - 1)

    lq1, lk1, lq2, lk2 = (lamv_ref[i:i + 1, :] for i in range(4))
    lam = (jnp.exp(jnp.sum(lq1 * lk1, axis=-1, keepdims=True))
           - jnp.exp(jnp.sum(lq2 * lk2, axis=-1, keepdims=True)) + lam_init)
    for hl in range(ATTN_HEADS_PER_STEP):
        o = acc_s[2 * hl] / l_s[2 * hl] - lam * (acc_s[2 * hl + 1] / l_s[2 * hl + 1])
        on = o * lax.rsqrt(jnp.mean(o * o, axis=0, keepdims=True) + NORM_EPS) * sg_ref[...] * (1.0 - lam_init)
        o_ref[:, hl * dv:(hl + 1) * dv] = on.T


def _attn(slopes2, qa, ka, vt, lamv, sg_col, nb, seq, tq, lam_init):
    t = qa.shape[0]
    nq = seq // tq
    dv = 2 * DIFF_DH
    hps = ATTN_HEADS_PER_STEP
    nstream = 2 * hps
    grid_spec = pltpu.PrefetchScalarGridSpec(
        num_scalar_prefetch=1,
        grid=(nb, DIFF_HEADS // hps, nq),
        in_specs=[pl.BlockSpec((tq, nstream * LANES), lambda b, h, i, s: (b * nq + i, h)),
                  pl.BlockSpec((seq, nstream * LANES), lambda b, h, i, s: (b, h)),
                  pl.BlockSpec((1, nq, hps * dv, tq), lambda b, h, i, s: (b, 0, h, 0)),
                  pl.BlockSpec(lamv.shape, lambda b, h, i, s: (0, 0)),
                  pl.BlockSpec(sg_col.shape, lambda b, h, i, s: (0, 0))],
        out_specs=pl.BlockSpec((tq, hps * dv), lambda b, h, i, s: (b * nq + i, h)),
        scratch_shapes=[pltpu.VMEM((nstream, 1, tq), F32), pltpu.VMEM((nstream, 1, tq), F32),
                        pltpu.VMEM((nstream, dv, tq), F32),
                        pltpu.VMEM((nstream, tq, tq), F32), pltpu.VMEM((nstream, tq, tq), F32)],
    )
    return pl.pallas_call(
        functools.partial(_attn_kernel, tq=tq, lam_init=lam_init),
        out_shape=jax.ShapeDtypeStruct((t, DIFF_V), F32),
        grid_spec=grid_spec,
        compiler_params=_cparams(("arbitrary", "arbitrary", "arbitrary")),
        name="attn",
    )(slopes2, qa, ka, vt, lamv, sg_col)


def _pack_halves(x):
    n = x.shape[1] // 2
    lo = pltpu.bitcast(x[:, :n].astype(BF16).astype(F32), jnp.uint32)
    hi = pltpu.bitcast(x[:, n:].astype(BF16).astype(F32), jnp.uint32)
    return (lo >> 16) | (hi & jnp.uint32(0xFFFF0000))


def _unpack_halves(p):
    lo = pltpu.bitcast(p << 16, F32)
    hi = pltpu.bitcast(p & jnp.uint32(0xFFFF0000), F32)
    return lo, hi


def _post_kernel(x_ref, oa_ref, ob_ref, gate_ref, g1_ref, sh2_ref, sc2_ref, n2_ref,
                 woa_ref, wob_ref, wout_ref, x1_ref, h2_ref, hp_ref):
    ya = _dot(oa_ref[...].astype(BF16), woa_ref[...])
    yb = _dot(ob_ref[...].astype(BF16), wob_ref[...])
    merged = (jax.nn.sigmoid(gate_ref[:, 0:D_MODEL].astype(F32)) * ya
              + jax.nn.sigmoid(gate_ref[:, D_MODEL:].astype(F32)) * yb)
    y = _dot(merged.astype(BF16), wout_ref[...])
    x1 = x_ref[...] + g1_ref[0] * y
    x1_ref[...] = x1
    n = x1 * lax.rsqrt(jnp.mean(x1 * x1, axis=-1, keepdims=True) + NORM_EPS) * n2_ref[...]
    h2 = n * (1.0 + sc2_ref[0]) + sh2_ref[0]
    h2_ref[...] = h2
    hp_ref[...] = _pack_halves(h2)


def _post(x2, oa, ob, gates, mod3, norm2_g, woa, wob, wout, seq):
    t = x2.shape[0]
    tm = min(256, seq)
    tpb = seq // tm
    const = lambda i: (0, 0)
    row = lambda n: pl.BlockSpec((tm, n), lambda i: (i, 0))
    modspec = lambda j: pl.BlockSpec((1, 1, D_MODEL), lambda i: (i // tpb, 0, j))
    wspec = lambda w: pl.BlockSpec(w.shape, const, pipeline_mode=pl.Buffered(1))
    return pl.pallas_call(
        _post_kernel,
        out_shape=(jax.ShapeDtypeStruct((t, D_MODEL), F32), jax.ShapeDtypeStruct((t, D_MODEL), F32),
                   jax.ShapeDtypeStruct((t, D_MODEL // 2), jnp.uint32)),
        grid=(t // tm,),
        in_specs=[row(D_MODEL), row(GDN_V), row(DIFF_V), row(2 * D_MODEL),
                  modspec(2), modspec(3), modspec(4), pl.BlockSpec((1, D_MODEL), const),
                  wspec(woa), wspec(wob), wspec(wout)],
        out_specs=(row(D_MODEL), row(D_MODEL), row(D_MODEL // 2)),
        compiler_params=_cparams(("arbitrary",)),
        name="post",
    )(x2, oa, ob, gates, mod3, mod3, mod3, norm2_g, woa, wob, wout)


def _router_kernel(h_ref, wrt_ref, rb_ref, idx_ref, w_ref, cnt_ref, cnt_s):
    step = pl.program_id(0)

    @pl.when(step == 0)
    def _():
        cnt_s[...] = jnp.zeros_like(cnt_s)

    tm = h_ref.shape[0]
    neg = -jnp.inf
    logits = lax.dot_general(wrt_ref[...], h_ref[...], (((1,), (1,)), ((), ())), precision=HIGHEST,
                             preferred_element_type=F32)
    scores = jax.nn.sigmoid(logits)
    choice = scores + rb_ref[...]
    eid = lax.broadcasted_iota(jnp.int32, (N_EXPERTS, tm), 0).astype(F32)
    c3 = choice.reshape(N_GROUPS, GROUP_SIZE, tm)
    e3 = eid.reshape(N_GROUPS, GROUP_SIZE, tm)
    m1 = jnp.max(c3, axis=1, keepdims=True)
    i1 = jnp.min(jnp.where(c3 == m1, e3, float(N_EXPERTS)), axis=1, keepdims=True)
    m2 = jnp.max(jnp.where(e3 == i1, neg, c3), axis=1, keepdims=True)
    gscore = (m1 + m2).reshape(N_GROUPS, tm)
    gid = lax.broadcasted_iota(jnp.int32, (N_GROUPS, tm), 0).astype(F32)
    gsel = jnp.zeros((N_GROUPS, tm), F32)
    for _ in range(TOPK_GROUPS):
        m = jnp.max(gscore, axis=0, keepdims=True)
        g = jnp.min(jnp.where(gscore == m, gid, float(N_GROUPS)), axis=0, keepdims=True)
        hit = gid == g
        gsel = jnp.where(hit, 1.0, gsel)
        gscore = jnp.where(hit, neg, gscore)
    masked = jnp.where(gsel.reshape(N_GROUPS, 1, tm) > 0.0, c3, neg).reshape(N_EXPERTS, tm)
    idx_rows, w_rows = [], []
    sel = jnp.zeros((N_EXPERTS, tm), F32)
    for k in range(TOP_K):
        m = jnp.max(masked, axis=0, keepdims=True)
        i = jnp.min(jnp.where(masked == m, eid, float(N_EXPERTS)), axis=0, keepdims=True)
        hit = eid == i
        idx_rows.append(i)
        w_rows.append(jnp.sum(jnp.where(hit, scores, 0.0), axis=0, keepdims=True))
        sel = jnp.where(hit, 1.0, sel)
        masked = jnp.where(hit, neg, masked)
    ws = jnp.concatenate(w_rows, axis=0)
    idx_ref[...] = jnp.concatenate(idx_rows, axis=0).astype(jnp.int32)
    w_ref[...] = ws / jnp.sum(ws, axis=0, keepdims=True) * ROUTED_SCALE
    ones = jnp.ones((SUBLANES, tm), BF16)
    cnt_s[...] = cnt_s[...] + _dot_nt(ones, sel.astype(BF16))[0:1, :]
    cnt_ref[...] = cnt_s[...]


def _router(h2, w_router_t, rbias_col):
    t = h2.shape[0]
    tm = min(256, t)
    const = lambda i: (0, 0)
    return pl.pallas_call(
        _router_kernel,
        out_shape=(jax.ShapeDtypeStruct((TOP_K, t), jnp.int32),
                   jax.ShapeDtypeStruct((TOP_K, t), F32),
                   jax.ShapeDtypeStruct((1, N_EXPERTS), F32)),
        grid=(t // tm,),
        in_specs=[pl.BlockSpec((tm, D_MODEL), lambda i: (i, 0)),
                  pl.BlockSpec(w_router_t.shape, const), pl.BlockSpec(rbias_col.shape, const)],
        out_specs=(pl.BlockSpec((TOP_K, tm), lambda i: (0, i)),
                   pl.BlockSpec((TOP_K, tm), lambda i: (0, i)),
                   pl.BlockSpec((1, N_EXPERTS), const)),
        scratch_shapes=[pltpu.VMEM((1, N_EXPERTS), F32)],
        compiler_params=_cparams(("arbitrary",)),
        name="router",
    )(h2, w_router_t, rbias_col)


def _rank_kernel(idx_ref, ps_ref, dest_ref, run_s):
    step = pl.program_id(0)

    @pl.when(step == 0)
    def _():
        run_s[...] = jnp.zeros_like(run_s)

    tm = idx_ref.shape[0]
    idx = idx_ref[...]
    lane = lax.broadcasted_iota(jnp.int32, (tm, N_EXPERTS), 1)
    gl = lax.broadcasted_iota(jnp.int32, (tm, LANES), 1)
    hits = [lane == idx[:, k:k + 1] for k in range(TOP_K)]
    sel = jnp.zeros((tm, N_EXPERTS), F32)
    for hit in hits:
        sel = jnp.where(hit, 1.0, sel)
    ri = lax.broadcasted_iota(jnp.int32, (tm, tm), 0)
    ci = lax.broadcasted_iota(jnp.int32, (tm, tm), 1)
    before = _dot((ri > ci).astype(BF16), sel.astype(BF16))
    base = before + run_s[...] + ps_ref[...]
    dest = jnp.zeros((tm, LANES), F32)
    for k, hit in enumerate(hits):
        dk = jnp.sum(jnp.where(hit, base, 0.0), axis=-1, keepdims=True)
        dest = jnp.where(gl == k, dk, dest)
    dest_ref[...] = dest.astype(jnp.int32)
    run_s[...] = run_s[...] + jnp.sum(sel, axis=0, keepdims=True)


def _rank(idx, pstart):
    t = idx.shape[0]
    tm = min(256, t)
    const = lambda i: (0, 0)
    return pl.pallas_call(
        _rank_kernel,
        out_shape=jax.ShapeDtypeStruct((t, LANES), jnp.int32),
        grid=(t // tm,),
        in_specs=[pl.BlockSpec((tm, LANES), lambda i: (i, 0)), pl.BlockSpec(pstart.shape, const)],
        out_specs=pl.BlockSpec((tm, LANES), lambda i: (i, 0)),
        scratch_shapes=[pltpu.VMEM((1, N_EXPERTS), F32)],
        compiler_params=_cparams(("arbitrary",)),
        name="rank",
    )(idx, pstart)


def _row_copy(src, dst, sem):
    return pltpu.make_async_copy(src, dst, sem)


def _dispatch_kernel(dest_ref, h_ref, xs_in, xs_ref, sem, *, tm):
    del xs_in

    def start_row(r, c):
        for k in range(TOP_K):
            d = dest_ref[r * TOP_K + k]
            _row_copy(h_ref.at[pl.ds(r, 1), :], xs_ref.at[pl.ds(d, 1), :], sem).start(priority=k % 2)
        return c

    lax.fori_loop(0, tm, start_row, 0)
    for k in range(TOP_K):
        _row_copy(h_ref, xs_ref.at[pl.ds(0, tm), :], sem).wait()


def _dispatch(dest_flat, h2, xs0):
    t = h2.shape[0]
    tm = min(256, t)
    return pl.pallas_call(
        functools.partial(_dispatch_kernel, tm=tm),
        out_shape=jax.ShapeDtypeStruct(xs0.shape, xs0.dtype),
        grid=(t // tm,),
        in_specs=[pl.BlockSpec((tm * TOP_K,), lambda i: (i,), memory_space=pltpu.SMEM),
                  pl.BlockSpec((tm, h2.shape[1]), lambda i: (i, 0)),
                  pl.BlockSpec(memory_space=pl.ANY)],
        out_specs=pl.BlockSpec(memory_space=pl.ANY),
        scratch_shapes=[pltpu.SemaphoreType.DMA(())],
        input_output_aliases={2: 0},
        compiler_params=_cparams(("arbitrary",)),
        name="dispatch",
    )(dest_flat, h2, xs0)


SC_CORES = 2
SC_SUBCORES = 16
SC_LANES = 16
INVERT_CHUNK = 4096


def _invert(dest_flat, n_rows, n_tokens):
    n_assign = dest_flat.shape[0]
    workers = SC_CORES * SC_SUBCORES
    rpw = n_rows // workers
    assert rpw * workers == n_rows and rpw % SC_LANES == 0 and n_assign % INVERT_CHUNK == 0
    scratch_base = TOP_K * n_tokens
    log2_k = TOP_K.bit_length() - 1

    def body(dest_hbm, inv_hbm, loc, chunk_v):
        wid = lax.axis_index("s") * SC_CORES + lax.axis_index("c")
        lo = wid * rpw
        lane = lax.iota(jnp.int32, SC_LANES)

        def init(i, c):
            loc[pl.ds(i * SC_LANES, SC_LANES)] = scratch_base + lo + i * SC_LANES + lane
            return c
        lax.fori_loop(0, rpw // SC_LANES, init, 0)

        def do_chunk(ci, c):
            pltpu.sync_copy(dest_hbm.at[pl.ds(ci * INVERT_CHUNK, INVERT_CHUNK)], chunk_v)

            def inner(j, cc):
                rel = chunk_v[pl.ds(j * SC_LANES, SC_LANES)] - lo
                mine = (rel >= 0) & (rel < rpw)
                a = ci * INVERT_CHUNK + j * SC_LANES + lane
                val = (a & (TOP_K - 1)) * n_tokens + lax.shift_right_logical(a, log2_k)
                plsc.store_scatter(loc, [jnp.where(mine, rel, 0)], val, mask=mine)
                return cc
            lax.fori_loop(0, INVERT_CHUNK // SC_LANES, inner, 0)
            return c
        lax.fori_loop(0, n_assign // INVERT_CHUNK, do_chunk, 0)
        pltpu.sync_copy(loc, inv_hbm.at[pl.ds(lo, rpw)])

    mesh = plsc.VectorSubcoreMesh(core_axis_name="c", subcore_axis_name="s",
                                  num_cores=SC_CORES, num_subcores=SC_SUBCORES)
    return pl.kernel(body, out_type=jax.ShapeDtypeStruct((n_rows,), jnp.int32), mesh=mesh,
                     scratch_types=[pltpu.VMEM((rpw,), jnp.int32), pltpu.VMEM((INVERT_CHUNK,), jnp.int32)],
                     compiler_params=pltpu.CompilerParams(needs_layout_passes=False),
                     name="invert")(dest_flat)


EXPERT_RING = 5
EXPERT_AHEAD = EXPERT_RING - 2
WEIGHT_RING = 3


def _experts_kernel(start_ref, count_ref, xs_ref, inv_ref, wgu_ref, wdn_ref, g_ref,
                    wgu_f, wdn_f, wgu_s, wdn_s, xbuf, ybuf, inv_s, sem_w, sem_in, sem_inv, sem_out,
                    *, nblk_total, spare_row0):
    half = D_MODEL // 2
    last_e = N_EXPERTS - 1
    used = start_ref[last_e] + count_ref[last_e]

    def rows(g):
        return pl.ds(pl.multiple_of(g * EXPERT_BLOCK, EXPERT_BLOCK), EXPERT_BLOCK)

    def in_copies(g):
        slot = g % EXPERT_RING
        gc = jnp.minimum(g, nblk_total - 1)
        return (pltpu.make_async_copy(xs_ref.at[rows(gc), :], xbuf.at[slot], sem_in.at[slot]),
                pltpu.make_async_copy(inv_ref.at[pl.ds(gc, 1), :], inv_s.at[slot], sem_inv.at[slot]))

    def out_wait(slot):
        pltpu.make_async_copy(ybuf.at[slot], g_ref.at[pl.ds(0, EXPERT_BLOCK), :], sem_out.at[slot]).wait()

    def w_copies(e):
        slot = e % WEIGHT_RING
        return (pltpu.make_async_copy(wgu_ref.at[e], wgu_f.at[slot], sem_w.at[0, slot]),
                pltpu.make_async_copy(wdn_ref.at[e], wdn_f.at[slot], sem_w.at[1, slot]))

    def issue_rows(slot):
        for r in range(EXPERT_BLOCK):
            d = inv_s[slot, 0, r]
            _row_copy(ybuf.at[slot, pl.ds(r, 1), :], g_ref.at[pl.ds(d, 1), :], sem_out.at[slot]).start(priority=r % 2)

    for e0 in range(WEIGHT_RING - 1):
        for cp in w_copies(e0):
            cp.start()
    for g in range(EXPERT_AHEAD):
        for cp in in_copies(g):
            cp.start()
    ybuf[EXPERT_RING - 1] = jnp.zeros((EXPERT_BLOCK, half), jnp.uint32)
    for r in range(EXPERT_BLOCK):
        inv_s[EXPERT_RING - 1, 0, r] = spare_row0 + r

    def expert(e, carry):
        for cp in w_copies(e):
            cp.wait()

        @pl.when(e + WEIGHT_RING - 1 <= last_e)
        def _():
            for cp in w_copies(e + WEIGHT_RING - 1):
                cp.start()

        slot = e % WEIGHT_RING
        wgu_s[...] = wgu_f[slot].astype(BF16)
        wdn_s[...] = wdn_f[slot].astype(BF16)
        first = start_ref[e]

        def block(i, c):
            g = first + i
            for cp in in_copies(g):
                cp.wait()
            for cp in in_copies(g + EXPERT_AHEAD):
                cp.start()
            s = g % EXPERT_RING

            @pl.when(g >= EXPERT_RING - 1)
            def _():
                out_wait(s)

            issue_rows((g + EXPERT_RING - 1) % EXPERT_RING)
            lo, hi = _unpack_halves(xbuf[s])
            gu = _dot(lo.astype(BF16), wgu_s[0:half, :]) + _dot(hi.astype(BF16), wgu_s[half:, :])
            act = _silu(gu[:, :EXPERT_FF]) * gu[:, EXPERT_FF:]
            ybuf[s] = _pack_halves(_dot(act.astype(BF16), wdn_s[...]))
            return c

        lax.fori_loop(0, count_ref[e], block, 0)
        return carry

    lax.fori_loop(0, N_EXPERTS, expert, 0)
    issue_rows((used + EXPERT_RING - 1) % EXPERT_RING)

    for j in range(EXPERT_AHEAD):
        for cp in in_copies(used + j):
            cp.wait()
    for j in range(EXPERT_RING):
        @pl.when(used > j)
        def _():
            out_wait((used - 1 - j) % EXPERT_RING)

    @pl.when(used < EXPERT_RING)
    def _():
        out_wait(EXPERT_RING - 1)


def _experts(blk_start, blk_count, xs, inv2, w_gu, w_dn, n_slots):
    rows, half = xs.shape
    nblk = rows // EXPERT_BLOCK
    grid_spec = pltpu.PrefetchScalarGridSpec(
        num_scalar_prefetch=2,
        grid=(1,),
        in_specs=[pl.BlockSpec(memory_space=pl.ANY), pl.BlockSpec(memory_space=pl.ANY),
                  pl.BlockSpec(memory_space=pl.ANY), pl.BlockSpec(memory_space=pl.ANY)],
        out_specs=pl.BlockSpec(memory_space=pl.ANY),
        scratch_shapes=[pltpu.VMEM((WEIGHT_RING, D_MODEL, 2 * EXPERT_FF), F32),
                        pltpu.VMEM((WEIGHT_RING, EXPERT_FF, D_MODEL), F32),
                        pltpu.VMEM((D_MODEL, 2 * EXPERT_FF), BF16), pltpu.VMEM((EXPERT_FF, D_MODEL), BF16),
                        pltpu.VMEM((EXPERT_RING, EXPERT_BLOCK, half), jnp.uint32),
                        pltpu.VMEM((EXPERT_RING, EXPERT_BLOCK, half), jnp.uint32),
                        pltpu.SMEM((EXPERT_RING, 1, EXPERT_BLOCK), jnp.int32),
                        pltpu.SemaphoreType.DMA((2, WEIGHT_RING)), pltpu.SemaphoreType.DMA((EXPERT_RING,)),
                        pltpu.SemaphoreType.DMA((EXPERT_RING,)), pltpu.SemaphoreType.DMA((EXPERT_RING,))],
    )
    return pl.pallas_call(
        functools.partial(_experts_kernel, nblk_total=nblk, spare_row0=n_slots - EXPERT_BLOCK),
        out_shape=jax.ShapeDtypeStruct((n_slots, half), jnp.uint32),
        grid_spec=grid_spec,
        compiler_params=_cparams(("arbitrary",)),
        name="experts",
    )(blk_start, blk_count, xs, inv2, w_gu, w_dn)


def _combine_kernel(x1_ref, h_ref, w_ref, g2_ref, wsgu_ref, wsdn_ref, buf, o_ref):
    su = _dot(h_ref[...].astype(BF16), wsgu_ref[...])
    y = _dot((_silu(su[:, :SHARED_FF]) * su[:, SHARED_FF:]).astype(BF16), wsdn_ref[...])
    w = w_ref[...]
    half = D_MODEL // 2
    ylo, yhi = y[:, :half], y[:, half:]
    for k in range(TOP_K):
        lo, hi = _unpack_halves(buf[k])
        wk = w[:, k:k + 1]
        ylo = ylo + wk * lo
        yhi = yhi + wk * hi
    g2 = g2_ref[0]
    o_ref[:, :half] = x1_ref[:, :half] + g2[:, :half] * ylo
    o_ref[:, half:] = x1_ref[:, half:] + g2[:, half:] * yhi


def _combine(x1, h2, wts, mod3, wsgu, wsdn, slots3, seq):
    t = x1.shape[0]
    tm = min(256, seq)
    tpb = seq // tm
    const = lambda i: (0, 0)
    row = lambda n: pl.BlockSpec((tm, n), lambda i: (i, 0))
    return pl.pallas_call(
        _combine_kernel,
        out_shape=jax.ShapeDtypeStruct((t, D_MODEL), F32),
        grid=(t // tm,),
        in_specs=[row(D_MODEL), row(D_MODEL), row(LANES),
                  pl.BlockSpec((1, 1, D_MODEL), lambda i: (i // tpb, 0, 5)),
                  pl.BlockSpec(wsgu.shape, const), pl.BlockSpec(wsdn.shape, const),
                  pl.BlockSpec((TOP_K, tm, D_MODEL // 2), lambda i: (0, i, 0))],
        out_specs=row(D_MODEL),
        compiler_params=_cparams(("arbitrary",)),
        name="combine",
    )(x1, h2, wts, mod3, wsgu, wsdn, slots3)


def _layer(x, c, layer, w_ada, b_ada, norm1_g, w_in, conv_w, a_log, dt_bias, gdn_norm_g, w_o_gdn,
           q_norm_g, k_norm_g, lambda_q1, lambda_k1, lambda_q2, lambda_k2, subln_g, w_o_diff,
           w_out, norm2_g, w_router, router_bias, w_exp_gate_up, w_exp_down,
           w_shared_gate_up, w_shared_down):
    nb, seq, d = x.shape
    t = nb * seq
    lam_init = 0.8 - 0.6 * math.exp(-0.3 * layer)
    x2 = x.reshape(t, d)

    c_pad = jnp.pad(c, ((0, SUBLANES - nb % SUBLANES if nb % SUBLANES else 0), (0, 0)))
    mod = _ada(c_pad, w_ada, b_ada.reshape(1, -1))[:nb]
    mod3 = mod.reshape(nb, 1, 6 * d)

    o_bd = 2 * GDN_QK + 2 * GDN_V
    o_att = o_bd + 2 * GDN_HEADS
    o_gate = o_att + 2 * DIFF_QK + DIFF_V
    w_gdn = w_in[:, :o_bd].astype(BF16)
    w_bd = jnp.pad(w_in[:, o_bd:o_att], ((0, 0), (0, LANES - 2 * GDN_HEADS))).astype(BF16)
    w_att = w_in[:, o_att:o_gate].astype(BF16)
    w_gate = w_in[:, o_gate:].astype(BF16)
    gdn_in, bd, att_in, gates = _inproj(x2, mod3, norm1_g.reshape(1, d), w_gdn, w_bd, w_att, w_gate, seq)

    aparams = jnp.zeros((SUBLANES, LANES), F32)
    aparams = aparams.at[0, GDN_HEADS:2 * GDN_HEADS].set(a_log).at[1, GDN_HEADS:2 * GDN_HEADS].set(dt_bias)
    oa = _gdn(gdn_in.reshape(nb, seq, -1), bd.reshape(nb, seq, LANES), conv_w, aparams,
              gdn_norm_g.reshape(1, GDN_DV)).reshape(t, GDN_V)

    tq = min(512, seq)
    slopes = [2.0 ** (-8.0 * (h + 1) / DIFF_HEADS) for h in range(DIFF_HEADS)]
    c3 = _bf16_split3(LOG2E)
    qfeat = np.zeros((1, 2 * DIFF_HEADS * DIFF_DH), np.float32)
    for g in range(2 * DIFF_HEADS):
        for j in range(6):
            qfeat[0, g * DIFF_DH + j] = slopes[g // 2] * c3[j // 2]
    grp = np.arange(DIFF_QK) // DIFF_DH
    bdm = jnp.asarray((grp[:, None] == grp[None, :]).astype(np.float32))
    qa, ka, vt = _aprep(att_in, jnp.tile(q_norm_g, 2 * DIFF_HEADS).reshape(1, -1),
                        jnp.tile(k_norm_g, 2 * DIFF_HEADS).reshape(1, -1), jnp.asarray(qfeat), bdm, nb, seq, tq)
    slopes2 = jnp.asarray([s * (c3[0] + c3[1] + c3[2]) for s in slopes], F32)
    lamv = jnp.zeros((SUBLANES, DIFF_DH), F32)
    lamv = lamv.at[0].set(lambda_q1).at[1].set(lambda_k1).at[2].set(lambda_q2).at[3].set(lambda_k2)
    ob = _attn(slopes2, qa, ka, vt, lamv, subln_g.reshape(-1, 1), nb, seq, tq, lam_init)

    x1, h2, hp = _post(x2, oa, ob, gates, mod3, norm2_g.reshape(1, d), w_o_gdn.astype(BF16),
                       w_o_diff.astype(BF16), w_out.astype(BF16), seq)

    idx_t, wts_t, counts = _router(h2, w_router.T, router_bias.reshape(-1, 1))
    idx = jnp.pad(idx_t.T, ((0, 0), (0, LANES - TOP_K)))
    wts = jnp.pad(wts_t.T, ((0, 0), (0, LANES - TOP_K)))
    cnt = counts[0].astype(jnp.int32)
    padded = (cnt + EXPERT_BLOCK - 1) // EXPERT_BLOCK * EXPERT_BLOCK
    pends = jnp.cumsum(padded)
    pstart = (pends - padded).astype(F32).reshape(1, -1)
    nblk = -(-(t * TOP_K) // EXPERT_BLOCK) + N_EXPERTS
    blk_start = ((pends - padded) // EXPERT_BLOCK).astype(jnp.int32)
    blk_count = (padded // EXPERT_BLOCK).astype(jnp.int32)
    dest = _rank(idx, pstart)
    dest_flat = dest[:, :TOP_K].reshape(-1)
    n_rows = nblk * EXPERT_BLOCK
    assert n_rows % t == 0
    xs = _dispatch(dest_flat, hp, jnp.zeros((n_rows, d // 2), jnp.uint32))
    inv = _invert(dest_flat, n_rows, t)
    assert t >= EXPERT_BLOCK
    slots = _experts(blk_start, blk_count, xs, inv.reshape(nblk, EXPERT_BLOCK), w_exp_gate_up, w_exp_down,
                     TOP_K * t + n_rows + t)
    out = _combine(x1, h2, wts, mod3, w_shared_gate_up.astype(BF16), w_shared_down.astype(BF16),
                   slots.reshape(-1, t, d // 2), seq)
    return out.reshape(nb, seq, d)


def kernel(x, c, w_ada, b_ada, norm1_g, w_in, conv_w, a_log, dt_bias, gdn_norm_g, w_o_gdn, q_norm_g, k_norm_g, lambda_q1, lambda_k1, lambda_q2, lambda_k2, subln_g, w_o_diff, w_out, norm2_g, w_router, router_bias, w_exp_gate_up, w_exp_down, w_shared_gate_up, w_shared_down):
    params = (w_ada, b_ada, norm1_g, w_in, conv_w, a_log, dt_bias, gdn_norm_g, w_o_gdn, q_norm_g,
              k_norm_g, lambda_q1, lambda_k1, lambda_q2, lambda_k2, subln_g, w_o_diff, w_out, norm2_g,
              w_router, router_bias, w_exp_gate_up, w_exp_down, w_shared_gate_up, w_shared_down)
    for layer in range(w_ada.shape[0]):
        x = _layer(x, c, layer, *(p[layer] for p in params))
    return x
```

```python
import functools
import math

import jax
import jax.numpy as jnp
import numpy as np
from jax import lax
from jax.experimental import pallas as pl
from jax.experimental.pallas import tpu as pltpu
from jax.experimental.pallas import tpu_sc as plsc

F32 = jnp.float32
BF16 = jnp.bfloat16
HIGHEST = lax.Precision.HIGHEST

D_MODEL = 1024
CHUNK = 64
GDN_HEADS = 4
GDN_DK = 128
GDN_DV = 128
GDN_CONV = 4
DIFF_HEADS = 4
DIFF_DH = 64
N_EXPERTS = 256
TOP_K = 8
N_GROUPS = 8
TOPK_GROUPS = 4
EXPERT_FF = 256
SHARED_FF = 256
ROUTED_SCALE = 2.5
NORM_EPS = 1e-6
GROUP_SIZE = N_EXPERTS // N_GROUPS

GDN_QK = GDN_HEADS * GDN_DK
GDN_V = GDN_HEADS * GDN_DV
CONV_CH = 2 * GDN_QK + GDN_V
DIFF_QK = DIFF_HEADS * 2 * DIFF_DH
DIFF_V = DIFF_HEADS * 2 * DIFF_DH

LANES = 128
SUBLANES = 8
EXPERT_BLOCK = 128
INV_BASE = 8
ATTN_HEADS_PER_STEP = 2
LOG2E = math.log2(math.e)
VMEM_LIMIT = 56 * 1024 * 1024


def _cparams(sem):
    return pltpu.CompilerParams(dimension_semantics=sem, vmem_limit_bytes=VMEM_LIMIT)


def _dot(a, b):
    return jnp.dot(a, b, preferred_element_type=F32)


def _dot_nt(a, b):
    return lax.dot_general(a, b, (((1,), (1,)), ((), ())), preferred_element_type=F32)


def _dot_tn(a, b):
    return lax.dot_general(a, b, (((0,), (0,)), ((), ())), preferred_element_type=F32)


def _silu(x):
    return x * jax.nn.sigmoid(x)


def _bf16_split3(x):
    rnd = lambda v: float(np.float32(v).astype(BF16).astype(np.float32))
    a = rnd(x)
    b = rnd(x - a)
    c = rnd(x - a - b)
    return a, b, c


def _ada_kernel(c_ref, w_ref, b_ref, o_ref):
    s = _silu(c_ref[...])
    o_ref[...] = jnp.dot(s, w_ref[...], precision=HIGHEST, preferred_element_type=F32) + b_ref[...]


def _ada(c_pad, w_ada, b_ada):
    n = w_ada.shape[1]
    tn = 1024
    return pl.pallas_call(
        _ada_kernel,
        out_shape=jax.ShapeDtypeStruct((c_pad.shape[0], n), F32),
        grid=(n // tn,),
        in_specs=[pl.BlockSpec(c_pad.shape, lambda j: (0, 0)),
                  pl.BlockSpec((D_MODEL, tn), lambda j: (0, j)),
                  pl.BlockSpec((1, tn), lambda j: (0, j))],
        out_specs=pl.BlockSpec((c_pad.shape[0], tn), lambda j: (0, j)),
        compiler_params=_cparams(("arbitrary",)),
        name="ada",
    )(c_pad, w_ada, b_ada)


def _inproj_kernel(x_ref, shift_ref, scale_ref, g_ref, wg_ref, wbd_ref, wa_ref, wgt_ref,
                   gdn_ref, bd_ref, att_ref, gate_ref):
    x = x_ref[...]
    y = x * lax.rsqrt(jnp.mean(x * x, axis=-1, keepdims=True) + NORM_EPS) * g_ref[...]
    h = (y * (1.0 + scale_ref[0]) + shift_ref[0]).astype(BF16)
    gdn_ref[...] = _dot(h, wg_ref[...]).astype(gdn_ref.dtype)
    bd_ref[...] = _dot(h, wbd_ref[...])
    att_ref[...] = _dot(h, wa_ref[...]).astype(att_ref.dtype)
    gate_ref[...] = _dot(h, wgt_ref[...]).astype(gate_ref.dtype)


def _inproj(x2, mod3, norm1_g, w_gdn, w_bd, w_att, w_gate, seq):
    t = x2.shape[0]
    tm = min(256, seq)
    tiles_per_b = seq // tm
    const = lambda i: (0, 0)
    wspec = lambda w: pl.BlockSpec(w.shape, const, pipeline_mode=pl.Buffered(1))
    row = lambda n: pl.BlockSpec((tm, n), lambda i: (i, 0))
    return pl.pallas_call(
        _inproj_kernel,
        out_shape=(jax.ShapeDtypeStruct((t, w_gdn.shape[1]), BF16),
                   jax.ShapeDtypeStruct((t, LANES), F32),
                   jax.ShapeDtypeStruct((t, w_att.shape[1]), BF16),
                   jax.ShapeDtypeStruct((t, w_gate.shape[1]), BF16)),
        grid=(t // tm,),
        in_specs=[row(D_MODEL),
                  pl.BlockSpec((1, 1, D_MODEL), lambda i: (i // tiles_per_b, 0, 0)),
                  pl.BlockSpec((1, 1, D_MODEL), lambda i: (i // tiles_per_b, 0, 1)),
                  pl.BlockSpec((1, D_MODEL), const),
                  wspec(w_gdn), wspec(w_bd), wspec(w_att), wspec(w_gate)],
        out_specs=(row(w_gdn.shape[1]), row(LANES), row(w_att.shape[1]), row(w_gate.shape[1])),
        compiler_params=_cparams(("arbitrary",)),
        name="inproj",
    )(x2, mod3, mod3, norm1_g, w_gdn, w_bd, w_att, w_gate)


def _gdn_kernel(x_ref, bd_ref, cw_ref, ap_ref, ng_ref, o_ref, cbuf, state, *, nb, lg):
    step = pl.program_id(0)
    nc = lg // CHUNK

    @pl.when(step == 0)
    def _():
        cbuf[...] = jnp.zeros_like(cbuf)
        state[...] = jnp.zeros_like(state)

    ri = lax.broadcasted_iota(jnp.int32, (CHUNK, CHUNK), 0)
    ci = lax.broadcasted_iota(jnp.int32, (CHUNK, CHUNK), 1)
    causal = ri >= ci
    strict = ri > ci
    eye = (ri == ci).astype(F32)
    rl = lax.broadcasted_iota(jnp.int32, (lg, lg), 0)
    cl = lax.broadcasted_iota(jnp.int32, (lg, lg), 1)
    blocktri = ((rl >= cl) & ((rl // CHUNK) == (cl // CHUNK))).astype(F32)
    cw = cw_ref[...]
    a_row = ap_ref[0:1, :]
    dt_row = ap_ref[1:2, :]
    ng = ng_ref[...]

    units = []
    for b in range(nb):
        cbuf[b, SUBLANES:SUBLANES + lg, :] = x_ref[b, :, 0:CONV_CH].astype(F32)
        acc = cw[GDN_CONV - 1:GDN_CONV, :] * cbuf[b, SUBLANES:SUBLANES + lg, :]
        for j in range(GDN_CONV - 1):
            off = SUBLANES - (GDN_CONV - 1) + j
            acc = acc + cw[j:j + 1, :] * cbuf[b, off:off + lg, :]
        cbuf[b, 0:SUBLANES, :] = cbuf[b, lg:lg + SUBLANES, :]
        qkv = _silu(acc)

        bd = bd_ref[b]
        beta_t = jax.nn.sigmoid(bd)
        g_t = -jnp.exp(a_row) * jax.nn.softplus(bd + dt_row)
        gcum = jnp.dot(blocktri, g_t, precision=HIGHEST, preferred_element_type=F32)
        gcum_t = gcum.T
        egcum = jnp.exp(gcum)

        for h in range(GDN_HEADS):
            qh = qkv[:, h * GDN_DK:(h + 1) * GDN_DK]
            kh = qkv[:, GDN_QK + h * GDN_DK:GDN_QK + (h + 1) * GDN_DK]
            vh = qkv[:, 2 * GDN_QK + h * GDN_DV:2 * GDN_QK + (h + 1) * GDN_DV]
            qh = qh * lax.rsqrt(jnp.sum(qh * qh, axis=-1, keepdims=True) + NORM_EPS) * (GDN_DK ** -0.5)
            kh = kh * lax.rsqrt(jnp.sum(kh * kh, axis=-1, keepdims=True) + NORM_EPS)
            for c in range(nc):
                r0, r1 = c * CHUNK, (c + 1) * CHUNK
                q, k, v = qh[r0:r1], kh[r0:r1], vh[r0:r1]
                gc = gcum[r0:r1, 4 + h:5 + h]
                gr = gcum_t[4 + h:5 + h, r0:r1]
                egc = egcum[r0:r1, 4 + h:5 + h]
                bcol = beta_t[r0:r1, h:h + 1]
                gl = gcum[r1 - 1:r1, 4 + h:5 + h]
                kb = k.astype(BF16)
                units.append(dict(
                    b=b, h=h, c=c, kb=kb, bcol=bcol,
                    qkb=jnp.concatenate([q.astype(BF16), kb], axis=0),
                    decay=jnp.exp(jnp.where(causal, gc - gr, -jnp.inf)),
                    rhs=jnp.concatenate([v * bcol, k * (bcol * egc)], axis=-1).astype(BF16),
                    qg=(q * egc).astype(BF16), egl=jnp.exp(gl),
                    kdec=(k * jnp.exp(gl - gc)).astype(BF16)))

    for u in units:
        u["kq"] = _dot_nt(u["qkb"], u["kb"])
    def same_block(size):
        return (ri // size) == (ci // size)
    for u in units:
        lower = jnp.where(strict, u["bcol"] * u["kq"][CHUNK:] * u["decay"], 0.0)
        u["lower"] = lower
        diag = jnp.where(same_block(INV_BASE), lower, 0.0)
        u["xinv"] = eye - diag
        u["pw"] = diag.astype(BF16)
    for r in range(INV_BASE.bit_length() - 2):
        for u in units:
            u["pw"] = _dot(u["pw"], u["pw"]).astype(BF16)
        for u in units:
            u["xinv"] = u["xinv"] + _dot(u["xinv"].astype(BF16), u["pw"])
    size = INV_BASE
    while size < CHUNK:
        pair_off = same_block(2 * size) & jnp.logical_not(same_block(size))
        for u in units:
            u["xb"] = u["xinv"].astype(BF16)
            u["cx"] = _dot(jnp.where(pair_off, u["lower"], 0.0).astype(BF16), u["xb"]).astype(BF16)
        for u in units:
            u["xinv"] = u["xinv"] - _dot(u["xb"], u["cx"])
        size *= 2
    for u in units:
        u["sol"] = _dot(u["xinv"].astype(BF16), u["rhs"])
        u["a_qk"] = (u["kq"][:CHUNK] * u["decay"]).astype(BF16)

    st = {(b, h): state[b, h] for b in range(nb) for h in range(GDN_HEADS)}
    for c in range(nc):
        cu = [u for u in units if u["c"] == c]
        for u in cu:
            stb = st[(u["b"], u["h"])].astype(BF16)
            u["ws"] = _dot(jnp.concatenate([u["sol"][:, GDN_DV:].astype(BF16), u["qg"]], axis=0), stb)
        for u in cu:
            u["vnb"] = (u["sol"][:, :GDN_DV] - u["ws"][:CHUNK]).astype(BF16)
        for u in cu:
            u["out"] = u["ws"][CHUNK:] + _dot(u["a_qk"], u["vnb"])
        for u in cu:
            key = (u["b"], u["h"])
            st[key] = st[key] * u["egl"] + _dot_tn(u["kdec"], u["vnb"])
        for u in cu:
            b, h = u["b"], u["h"]
            r0, r1 = c * CHUNK, (c + 1) * CHUNK
            out = u["out"]
            on = out * lax.rsqrt(jnp.mean(out * out, axis=-1, keepdims=True) + NORM_EPS) * ng
            zh = x_ref[b, r0:r1, CONV_CH + h * GDN_DV:CONV_CH + (h + 1) * GDN_DV].astype(F32)
            o_ref[b, r0:r1, h * GDN_DV:(h + 1) * GDN_DV] = on * _silu(zh)
    for (b, h), s in st.items():
        state[b, h] = s


def _gdn(gdn_in3, bd3, conv_w, aparams, gdn_norm_g):
    nb, seq, _ = gdn_in3.shape
    lg = min(128, seq)
    const = lambda i: (0, 0)
    return pl.pallas_call(
        functools.partial(_gdn_kernel, nb=nb, lg=lg),
        out_shape=jax.ShapeDtypeStruct((nb, seq, GDN_V), F32),
        grid=(seq // lg,),
        in_specs=[pl.BlockSpec((nb, lg, gdn_in3.shape[2]), lambda i: (0, i, 0)),
                  pl.BlockSpec((nb, lg, LANES), lambda i: (0, i, 0)),
                  pl.BlockSpec(conv_w.shape, const),
                  pl.BlockSpec(aparams.shape, const),
                  pl.BlockSpec(gdn_norm_g.shape, const)],
        out_specs=pl.BlockSpec((nb, lg, GDN_V), lambda i: (0, i, 0)),
        scratch_shapes=[pltpu.VMEM((nb, lg + SUBLANES, CONV_CH), F32),
                        pltpu.VMEM((nb, GDN_HEADS, GDN_DK, GDN_DV), F32)],
        compiler_params=_cparams(("arbitrary",)),
        name="gdn",
    )(gdn_in3, bd3, conv_w, aparams, gdn_norm_g)


def _aprep_kernel(x_ref, gq_ref, gk_ref, qf_ref, bdm_ref, q_ref, k_ref, vt_ref, *, tk):
    x = x_ref[...].astype(F32)
    tm = x.shape[0]
    bdm = bdm_ref[...]

    def qknorm(v, gain):
        sq = v * v
        hi = sq.astype(BF16)
        lo = (sq - hi.astype(F32)).astype(BF16)
        ms = (_dot(hi, bdm) + _dot(lo, bdm)) * (1.0 / DIFF_DH)
        return v * lax.rsqrt(ms + NORM_EPS) * gain

    qn = qknorm(x[:, 0:DIFF_QK], gq_ref[...]) * (DIFF_DH ** -0.5 * LOG2E)
    kn = qknorm(x[:, DIFF_QK:2 * DIFF_QK], gk_ref[...])
    pos = pl.program_id(0) * tm + lax.broadcasted_iota(jnp.int32, (tm, DIFF_DH), 0)
    krel = pos % tk
    lane = lax.broadcasted_iota(jnp.int32, (tm, DIFF_DH), 1)
    hi = ((krel // 256) * 256).astype(F32)
    lo = (krel % 256).astype(F32)
    kfeat = jnp.where(lane < 6, jnp.where(lane % 2 == 0, hi, lo), 0.0)
    qfeat = qf_ref[...]
    qparts, kparts = [], []
    for g in range(2 * DIFF_HEADS):
        qparts += [qn[:, g * DIFF_DH:(g + 1) * DIFF_DH],
                   jnp.broadcast_to(qfeat[:, g * DIFF_DH:(g + 1) * DIFF_DH], (tm, DIFF_DH))]
        kparts += [kn[:, g * DIFF_DH:(g + 1) * DIFF_DH], kfeat]
    q_ref[...] = jnp.concatenate(qparts, axis=-1).astype(BF16)
    k_ref[...] = jnp.concatenate(kparts, axis=-1).astype(BF16)
    vt_ref[0, 0] = x[:, 2 * DIFF_QK:].T.astype(BF16)


def _aprep(att_in, gq, gk, qfeat, bdm, nb, seq, tk):
    t = att_in.shape[0]
    tm = tk
    nk = seq // tk
    const = lambda i: (0, 0)
    return pl.pallas_call(
        functools.partial(_aprep_kernel, tk=tk),
        out_shape=(jax.ShapeDtypeStruct((t, 2 * DIFF_QK), BF16),
                   jax.ShapeDtypeStruct((t, 2 * DIFF_QK), BF16),
                   jax.ShapeDtypeStruct((nb, nk, DIFF_V, tk), BF16)),
        grid=(t // tm,),
        in_specs=[pl.BlockSpec((tm, att_in.shape[1]), lambda i: (i, 0)),
                  pl.BlockSpec(gq.shape, const), pl.BlockSpec(gk.shape, const),
                  pl.BlockSpec(qfeat.shape, const), pl.BlockSpec(bdm.shape, const)],
        out_specs=(pl.BlockSpec((tm, 2 * DIFF_QK), lambda i: (i, 0)),
                   pl.BlockSpec((tm, 2 * DIFF_QK), lambda i: (i, 0)),
                   pl.BlockSpec((1, 1, DIFF_V, tk), lambda i: (i // nk, i % nk, 0, 0))),
        compiler_params=_cparams(("arbitrary",)),
        name="aprep",
    )(att_in, gq, gk, qfeat, bdm)


def _attn_kernel(slope_ref, q_ref, k_ref, vt_ref, lamv_ref, sg_ref, o_ref, m_s, l_s, acc_s, sa, sb,
                 *, tq, lam_init):
    hp = pl.program_id(1)
    qi = pl.program_id(2)
    dv = 2 * DIFF_DH
    nstream = 2 * ATTN_HEADS_PER_STEP
    slope2 = [slope_ref[hp * ATTN_HEADS_PER_STEP + c // 2] for c in range(nstream)]
    qs = [q_ref[:, c * LANES:(c + 1) * LANES] for c in range(nstream)]

    m_s[...] = jnp.full_like(m_s, -jnp.inf)
    l_s[...] = jnp.zeros_like(l_s)
    acc_s[...] = jnp.zeros_like(acc_s)

    def scores(kj):
        k0 = pl.multiple_of(kj * tq, tq)
        return tuple(_dot_nt(k_ref[pl.ds(k0, tq), c * LANES:(c + 1) * LANES], qs[c]) for c in range(nstream))

    def accumulate(ss, kj):
        rel = ((kj - qi) * tq).astype(F32)
        for c in range(nstream):
            hl = c // 2
            vt = vt_ref[0, kj, hl * dv:(hl + 1) * dv, :]
            off = slope2[c] * rel
            s = ss[c]
            m_old = m_s[c]
            m_new = jnp.maximum(m_old, jnp.max(s, axis=0, keepdims=True) + off)
            alpha = jnp.exp2(m_old - m_new)
            p = jnp.exp2(s - (m_new - off))
            l_s[c] = alpha * l_s[c] + jnp.sum(p, axis=0, keepdims=True)
            acc_s[c] = alpha * acc_s[c] + _dot(vt, p.astype(BF16))
            m_s[c] = m_new

    kpos = lax.broadcasted_iota(jnp.int32, (tq, tq), 0)
    qpos = lax.broadcasted_iota(jnp.int32, (tq, tq), 1)
    allowed = (kpos // CHUNK) <= (qpos // CHUNK)
    fut = jnp.maximum(kpos - qpos, 0).astype(F32)
    adds = [jnp.where(allowed, (-2.0 * slope2[2 * hl]) * fut, -jnp.inf) for hl in range(ATTN_HEADS_PER_STEP)]
    accumulate(tuple(s + adds[c // 2] for c, s in enumerate(scores(qi))), qi)

    def put(slot, ss):
        for c in range(nstream):
            slot[c] = ss[c]

    def get(slot):
        return tuple(slot[c] for c in range(nstream))

    npairs = qi // 2
    put(sa, scores(0))

    def pair(j, carry):
        a = 2 * j
        put(sb, scores(a + 1))
        accumulate(get(sa), a)
        put(sa, scores(jnp.minimum(a + 2, qi - 1)))
        accumulate(get(sb), a + 1)
        return carry
    lax.fori_loop(0, npairs, pair, 0)

    @pl.when(qi % 2 == 1)
    def _():
        accumulate(get(sa), qi - 1)

    lq1, lk1, lq2, lk2 = (lamv_ref[i:i + 1, :] for i in range(4))
    lam = (jnp.exp(jnp.sum(lq1 * lk1, axis=-1, keepdims=True))
           - jnp.exp(jnp.sum(lq2 * lk2, axis=-1, keepdims=True)) + lam_init)
    for hl in range(ATTN_HEADS_PER_STEP):
        o = acc_s[2 * hl] / l_s[2 * hl] - lam * (acc_s[2 * hl + 1] / l_s[2 * hl + 1])
        on = o * lax.rsqrt(jnp.mean(o * o, axis=0, keepdims=True) + NORM_EPS) * sg_ref[...] * (1.0 - lam_init)
        o_ref[:, hl * dv:(hl + 1) * dv] = on.T


def _attn(slopes2, qa, ka, vt, lamv, sg_col, nb, seq, tq, lam_init):
    t = qa.shape[0]
    nq = seq // tq
    dv = 2 * DIFF_DH
    hps = ATTN_HEADS_PER_STEP
    nstream = 2 * hps
    grid_spec = pltpu.PrefetchScalarGridSpec(
        num_scalar_prefetch=1,
        grid=(nb, DIFF_HEADS // hps, nq),
        in_specs=[pl.BlockSpec((tq, nstream * LANES), lambda b, h, i, s: (b * nq + i, h)),
                  pl.BlockSpec((seq, nstream * LANES), lambda b, h, i, s: (b, h)),
                  pl.BlockSpec((1, nq, hps * dv, tq), lambda b, h, i, s: (b, 0, h, 0)),
                  pl.BlockSpec(lamv.shape, lambda b, h, i, s: (0, 0)),
                  pl.BlockSpec(sg_col.shape, lambda b, h, i, s: (0, 0))],
        out_specs=pl.BlockSpec((tq, hps * dv), lambda b, h, i, s: (b * nq + i, h)),
        scratch_shapes=[pltpu.VMEM((nstream, 1, tq), F32), pltpu.VMEM((nstream, 1, tq), F32),
                        pltpu.VMEM((nstream, dv, tq), F32),
                        pltpu.VMEM((nstream, tq, tq), F32), pltpu.VMEM((nstream, tq, tq), F32)],
    )
    return pl.pallas_call(
        functools.partial(_attn_kernel, tq=tq, lam_init=lam_init),
        out_shape=jax.ShapeDtypeStruct((t, DIFF_V), F32),
        grid_spec=grid_spec,
        compiler_params=_cparams(("arbitrary", "arbitrary", "arbitrary")),
        name="attn",
    )(slopes2, qa, ka, vt, lamv, sg_col)


def _pack_halves(x):
    n = x.shape[1] // 2
    lo = pltpu.bitcast(x[:, :n].astype(BF16).astype(F32), jnp.uint32)
    hi = pltpu.bitcast(x[:, n:].astype(BF16).astype(F32), jnp.uint32)
    return (lo >> 16) | (hi & jnp.uint32(0xFFFF0000))


def _unpack_halves(p):
    lo = pltpu.bitcast(p << 16, F32)
    hi = pltpu.bitcast(p & jnp.uint32(0xFFFF0000), F32)
    return lo, hi


def _post_kernel(x_ref, oa_ref, ob_ref, gate_ref, g1_ref, sh2_ref, sc2_ref, n2_ref,
                 woa_ref, wob_ref, wout_ref, x1_ref, h2_ref, hp_ref):
    ya = _dot(oa_ref[...].astype(BF16), woa_ref[...])
    yb = _dot(ob_ref[...].astype(BF16), wob_ref[...])
    merged = (jax.nn.sigmoid(gate_ref[:, 0:D_MODEL].astype(F32)) * ya
              + jax.nn.sigmoid(gate_ref[:, D_MODEL:].astype(F32)) * yb)
    y = _dot(merged.astype(BF16), wout_ref[...])
    x1 = x_ref[...] + g1_ref[0] * y
    x1_ref[...] = x1
    n = x1 * lax.rsqrt(jnp.mean(x1 * x1, axis=-1, keepdims=True) + NORM_EPS) * n2_ref[...]
    h2 = n * (1.0 + sc2_ref[0]) + sh2_ref[0]
    h2_ref[...] = h2
    hp_ref[...] = _pack_halves(h2)


def _post(x2, oa, ob, gates, mod3, norm2_g, woa, wob, wout, seq):
    t = x2.shape[0]
    tm = min(256, seq)
    tpb = seq // tm
    const = lambda i: (0, 0)
    row = lambda n: pl.BlockSpec((tm, n), lambda i: (i, 0))
    modspec = lambda j: pl.BlockSpec((1, 1, D_MODEL), lambda i: (i // tpb, 0, j))
    wspec = lambda w: pl.BlockSpec(w.shape, const, pipeline_mode=pl.Buffered(1))
    return pl.pallas_call(
        _post_kernel,
        out_shape=(jax.ShapeDtypeStruct((t, D_MODEL), F32), jax.ShapeDtypeStruct((t, D_MODEL), F32),
                   jax.ShapeDtypeStruct((t, D_MODEL // 2), jnp.uint32)),
        grid=(t // tm,),
        in_specs=[row(D_MODEL), row(GDN_V), row(DIFF_V), row(2 * D_MODEL),
                  modspec(2), modspec(3), modspec(4), pl.BlockSpec((1, D_MODEL), const),
                  wspec(woa), wspec(wob), wspec(wout)],
        out_specs=(row(D_MODEL), row(D_MODEL), row(D_MODEL // 2)),
        compiler_params=_cparams(("arbitrary",)),
        name="post",
    )(x2, oa, ob, gates, mod3, mod3, mod3, norm2_g, woa, wob, wout)


def _router_kernel(h_ref, wrt_ref, rb_ref, idx_ref, w_ref, cnt_ref, cnt_s):
    step = pl.program_id(0)

    @pl.when(step == 0)
    def _():
        cnt_s[...] = jnp.zeros_like(cnt_s)

    tm = h_ref.shape[0]
    neg = -jnp.inf
    logits = lax.dot_general(wrt_ref[...], h_ref[...], (((1,), (1,)), ((), ())), precision=HIGHEST,
                             preferred_element_type=F32)
    scores = jax.nn.sigmoid(logits)
    choice = scores + rb_ref[...]
    eid = lax.broadcasted_iota(jnp.int32, (N_EXPERTS, tm), 0).astype(F32)
    c3 = choice.reshape(N_GROUPS, GROUP_SIZE, tm)
    e3 = eid.reshape(N_GROUPS, GROUP_SIZE, tm)
    m1 = jnp.max(c3, axis=1, keepdims=True)
    i1 = jnp.min(jnp.where(c3 == m1, e3, float(N_EXPERTS)), axis=1, keepdims=True)
    m2 = jnp.max(jnp.where(e3 == i1, neg, c3), axis=1, keepdims=True)
    gscore = (m1 + m2).reshape(N_GROUPS, tm)
    gid = lax.broadcasted_iota(jnp.int32, (N_GROUPS, tm), 0).astype(F32)
    gsel = jnp.zeros((N_GROUPS, tm), F32)
    for _ in range(TOPK_GROUPS):
        m = jnp.max(gscore, axis=0, keepdims=True)
        g = jnp.min(jnp.where(gscore == m, gid, float(N_GROUPS)), axis=0, keepdims=True)
        hit = gid == g
        gsel = jnp.where(hit, 1.0, gsel)
        gscore = jnp.where(hit, neg, gscore)
    masked = jnp.where(gsel.reshape(N_GROUPS, 1, tm) > 0.0, c3, neg).reshape(N_EXPERTS, tm)
    idx_rows, w_rows = [], []
    sel = jnp.zeros((N_EXPERTS, tm), F32)
    for k in range(TOP_K):
        m = jnp.max(masked, axis=0, keepdims=True)
        i = jnp.min(jnp.where(masked == m, eid, float(N_EXPERTS)), axis=0, keepdims=True)
        hit = eid == i
        idx_rows.append(i)
        w_rows.append(jnp.sum(jnp.where(hit, scores, 0.0), axis=0, keepdims=True))
        sel = jnp.where(hit, 1.0, sel)
        masked = jnp.where(hit, neg, masked)
    ws = jnp.concatenate(w_rows, axis=0)
    idx_ref[...] = jnp.concatenate(idx_rows, axis=0).astype(jnp.int32)
    w_ref[...] = ws / jnp.sum(ws, axis=0, keepdims=True) * ROUTED_SCALE
    ones = jnp.ones((SUBLANES, tm), BF16)
    cnt_s[...] = cnt_s[...] + _dot_nt(ones, sel.astype(BF16))[0:1, :]
    cnt_ref[...] = cnt_s[...]


def _router(h2, w_router_t, rbias_col):
    t = h2.shape[0]
    tm = min(256, t)
    const = lambda i: (0, 0)
    return pl.pallas_call(
        _router_kernel,
        out_shape=(jax.ShapeDtypeStruct((TOP_K, t), jnp.int32),
                   jax.ShapeDtypeStruct((TOP_K, t), F32),
                   jax.ShapeDtypeStruct((1, N_EXPERTS), F32)),
        grid=(t // tm,),
        in_specs=[pl.BlockSpec((tm, D_MODEL), lambda i: (i, 0)),
                  pl.BlockSpec(w_router_t.shape, const), pl.BlockSpec(rbias_col.shape, const)],
        out_specs=(pl.BlockSpec((TOP_K, tm), lambda i: (0, i)),
                   pl.BlockSpec((TOP_K, tm), lambda i: (0, i)),
                   pl.BlockSpec((1, N_EXPERTS), const)),
        scratch_shapes=[pltpu.VMEM((1, N_EXPERTS), F32)],
        compiler_params=_cparams(("arbitrary",)),
        name="router",
    )(h2, w_router_t, rbias_col)


def _rank_kernel(idx_ref, ps_ref, dest_ref, run_s):
    step = pl.program_id(0)

    @pl.when(step == 0)
    def _():
        run_s[...] = jnp.zeros_like(run_s)

    tm = idx_ref.shape[0]
    idx = idx_ref[...]
    lane = lax.broadcasted_iota(jnp.int32, (tm, N_EXPERTS), 1)
    gl = lax.broadcasted_iota(jnp.int32, (tm, LANES), 1)
    hits = [lane == idx[:, k:k + 1] for k in range(TOP_K)]
    sel = jnp.zeros((tm, N_EXPERTS), F32)
    for hit in hits:
        sel = jnp.where(hit, 1.0, sel)
    ri = lax.broadcasted_iota(jnp.int32, (tm, tm), 0)
    ci = lax.broadcasted_iota(jnp.int32, (tm, tm), 1)
    before = _dot((ri > ci).astype(BF16), sel.astype(BF16))
    base = before + run_s[...] + ps_ref[...]
    dest = jnp.zeros((tm, LANES), F32)
    for k, hit in enumerate(hits):
        dk = jnp.sum(jnp.where(hit, base, 0.0), axis=-1, keepdims=True)
        dest = jnp.where(gl == k, dk, dest)
    dest_ref[...] = dest.astype(jnp.int32)
    run_s[...] = run_s[...] + jnp.sum(sel, axis=0, keepdims=True)


def _rank(idx, pstart):
    t = idx.shape[0]
    tm = min(256, t)
    const = lambda i: (0, 0)
    return pl.pallas_call(
        _rank_kernel,
        out_shape=jax.ShapeDtypeStruct((t, LANES), jnp.int32),
        grid=(t // tm,),
        in_specs=[pl.BlockSpec((tm, LANES), lambda i: (i, 0)), pl.BlockSpec(pstart.shape, const)],
        out_specs=pl.BlockSpec((tm, LANES), lambda i: (i, 0)),
        scratch_shapes=[pltpu.VMEM((1, N_EXPERTS), F32)],
        compiler_params=_cparams(("arbitrary",)),
        name="rank",
    )(idx, pstart)


def _row_copy(src, dst, sem):
    return pltpu.make_async_copy(src, dst, sem)


def _dispatch_kernel(dest_ref, h_ref, xs_in, xs_ref, sem, *, tm):
    del xs_in

    def start_row(r, c):
        for k in range(TOP_K):
            d = dest_ref[r * TOP_K + k]
            _row_copy(h_ref.at[pl.ds(r, 1), :], xs_ref.at[pl.ds(d, 1), :], sem).start(priority=k % 2)
        return c

    lax.fori_loop(0, tm, start_row, 0)
    for k in range(TOP_K):
        _row_copy(h_ref, xs_ref.at[pl.ds(0, tm), :], sem).wait()


def _dispatch(dest_flat, h2, xs0):
    t = h2.shape[0]
    tm = min(256, t)
    return pl.pallas_call(
        functools.partial(_dispatch_kernel, tm=tm),
        out_shape=jax.ShapeDtypeStruct(xs0.shape, xs0.dtype),
        grid=(t // tm,),
        in_specs=[pl.BlockSpec((tm * TOP_K,), lambda i: (i,), memory_space=pltpu.SMEM),
                  pl.BlockSpec((tm, h2.shape[1]), lambda i: (i, 0)),
                  pl.BlockSpec(memory_space=pl.ANY)],
        out_specs=pl.BlockSpec(memory_space=pl.ANY),
        scratch_shapes=[pltpu.SemaphoreType.DMA(())],
        input_output_aliases={2: 0},
        compiler_params=_cparams(("arbitrary",)),
        name="dispatch",
    )(dest_flat, h2, xs0)


SC_CORES = 2
SC_SUBCORES = 16
SC_LANES = 16
INVERT_CHUNK = 4096


def _invert(dest_flat, n_rows, n_tokens):
    n_assign = dest_flat.shape[0]
    workers = SC_CORES * SC_SUBCORES
    rpw = n_rows // workers
    assert rpw * workers == n_rows and rpw % SC_LANES == 0 and n_assign % INVERT_CHUNK == 0
    scratch_base = TOP_K * n_tokens
    log2_k = TOP_K.bit_length() - 1

    def body(dest_hbm, inv_hbm, loc, chunk_v):
        wid = lax.axis_index("s") * SC_CORES + lax.axis_index("c")
        lo = wid * rpw
        lane = lax.iota(jnp.int32, SC_LANES)

        def init(i, c):
            loc[pl.ds(i * SC_LANES, SC_LANES)] = scratch_base + lo + i * SC_LANES + lane
            return c
        lax.fori_loop(0, rpw // SC_LANES, init, 0)

        def do_chunk(ci, c):
            pltpu.sync_copy(dest_hbm.at[pl.ds(ci * INVERT_CHUNK, INVERT_CHUNK)], chunk_v)

            def inner(j, cc):
                rel = chunk_v[pl.ds(j * SC_LANES, SC_LANES)] - lo
                mine = (rel >= 0) & (rel < rpw)
                a = ci * INVERT_CHUNK + j * SC_LANES + lane
                val = (a & (TOP_K - 1)) * n_tokens + lax.shift_right_logical(a, log2_k)
                plsc.store_scatter(loc, [jnp.where(mine, rel, 0)], val, mask=mine)
                return cc
            lax.fori_loop(0, INVERT_CHUNK // SC_LANES, inner, 0)
            return c
        lax.fori_loop(0, n_assign // INVERT_CHUNK, do_chunk, 0)
        pltpu.sync_copy(loc, inv_hbm.at[pl.ds(lo, rpw)])

    mesh = plsc.VectorSubcoreMesh(core_axis_name="c", subcore_axis_name="s",
                                  num_cores=SC_CORES, num_subcores=SC_SUBCORES)
    return pl.kernel(body, out_type=jax.ShapeDtypeStruct((n_rows,), jnp.int32), mesh=mesh,
                     scratch_types=[pltpu.VMEM((rpw,), jnp.int32), pltpu.VMEM((INVERT_CHUNK,), jnp.int32)],
                     compiler_params=pltpu.CompilerParams(needs_layout_passes=False),
                     name="invert")(dest_flat)


GATHER_WIN = 64


def _sc_dispatch(hp, inv, n_tokens):
    n_rows = inv.shape[0]
    half = hp.shape[1]
    workers = SC_CORES * SC_SUBCORES
    rpw = n_rows // workers
    nwin = rpw // GATHER_WIN
    assert rpw * workers == n_rows and nwin * GATHER_WIN == rpw and nwin % 2 == 0
    assert n_tokens & (n_tokens - 1) == 0
    limit = TOP_K * n_tokens

    def body(hp_hbm, inv_hbm, xs_hbm, idx0, idx1, rows0, rows1, gsem0, gsem1, wsem0, wsem1):
        wid = lax.axis_index("s") * SC_CORES + lax.axis_index("c")
        base = wid * rpw
        idx, rows, gsem, wsem = (idx0, idx1), (rows0, rows1), (gsem0, gsem1), (wsem0, wsem1)

        def load_idx(w, s):
            pltpu.sync_copy(inv_hbm.at[pl.ds(base + w * GATHER_WIN, GATHER_WIN)], idx[s])
            for j in range(GATHER_WIN // SC_LANES):
                v = idx[s][pl.ds(j * SC_LANES, SC_LANES)]
                idx[s][pl.ds(j * SC_LANES, SC_LANES)] = jnp.where(v < limit, v & (n_tokens - 1), 0)

        def gather(s):
            return pltpu.make_async_copy(hp_hbm.at[idx[s]], rows[s], gsem[s])

        def write(w, s):
            return pltpu.make_async_copy(rows[s], xs_hbm.at[pl.ds(base + w * GATHER_WIN, GATHER_WIN)], wsem[s])

        load_idx(0, 0)
        gather(0).start()

        def pair(p, c):
            w0 = 2 * p
            load_idx(w0 + 1, 1)

            @pl.when(p > 0)
            def _():
                write(w0 - 1, 1).wait()

            gather(1).start()
            gather(0).wait()
            write(w0, 0).start()

            @pl.when(p + 1 < nwin // 2)
            def _():
                load_idx(w0 + 2, 0)
                write(w0, 0).wait()
                gather(0).start()

            gather(1).wait()
            write(w0 + 1, 1).start()
            return c

        lax.fori_loop(0, nwin // 2, pair, 0)
        write(nwin - 2, 0).wait()
        write(nwin - 1, 1).wait()

    mesh = plsc.VectorSubcoreMesh(core_axis_name="c", subcore_axis_name="s",
                                  num_cores=SC_CORES, num_subcores=SC_SUBCORES)
    return pl.kernel(body, out_type=jax.ShapeDtypeStruct((n_rows, half), hp.dtype), mesh=mesh,
                     scratch_types=[pltpu.VMEM((GATHER_WIN,), jnp.int32), pltpu.VMEM((GATHER_WIN,), jnp.int32),
                                    pltpu.VMEM((GATHER_WIN, half), hp.dtype), pltpu.VMEM((GATHER_WIN, half), hp.dtype),
                                    pltpu.SemaphoreType.DMA, pltpu.SemaphoreType.DMA,
                                    pltpu.SemaphoreType.DMA, pltpu.SemaphoreType.DMA],
                     compiler_params=pltpu.CompilerParams(needs_layout_passes=False),
                     name="scdispatch")(hp, inv)


EXPERT_RING = 5
EXPERT_AHEAD = EXPERT_RING - 2
WEIGHT_RING = 3


def _experts_kernel(start_ref, count_ref, xs_ref, inv_ref, wgu_ref, wdn_ref, g_ref,
                    wgu_f, wdn_f, wgu_s, wdn_s, xbuf, ybuf, inv_s, sem_w, sem_in, sem_inv, sem_out,
                    *, nblk_total, spare_row0):
    half = D_MODEL // 2
    last_e = N_EXPERTS - 1
    used = start_ref[last_e] + count_ref[last_e]

    def rows(g):
        return pl.ds(pl.multiple_of(g * EXPERT_BLOCK, EXPERT_BLOCK), EXPERT_BLOCK)

    def in_copies(g):
        slot = g % EXPERT_RING
        gc = jnp.minimum(g, nblk_total - 1)
        return (pltpu.make_async_copy(xs_ref.at[rows(gc), :], xbuf.at[slot], sem_in.at[slot]),
                pltpu.make_async_copy(inv_ref.at[pl.ds(gc, 1), :], inv_s.at[slot], sem_inv.at[slot]))

    def out_wait(slot):
        pltpu.make_async_copy(ybuf.at[slot], g_ref.at[pl.ds(0, EXPERT_BLOCK), :], sem_out.at[slot]).wait()

    def w_copies(e):
        slot = e % WEIGHT_RING
        return (pltpu.make_async_copy(wgu_ref.at[e], wgu_f.at[slot], sem_w.at[0, slot]),
                pltpu.make_async_copy(wdn_ref.at[e], wdn_f.at[slot], sem_w.at[1, slot]))

    def issue_rows(slot):
        for r in range(EXPERT_BLOCK):
            d = inv_s[slot, 0, r]
            _row_copy(ybuf.at[slot, pl.ds(r, 1), :], g_ref.at[pl.ds(d, 1), :], sem_out.at[slot]).start(priority=r % 2)

    for e0 in range(WEIGHT_RING - 1):
        for cp in w_copies(e0):
            cp.start()
    for g in range(EXPERT_AHEAD):
        for cp in in_copies(g):
            cp.start()
    ybuf[EXPERT_RING - 1] = jnp.zeros((EXPERT_BLOCK, half), jnp.uint32)
    for r in range(EXPERT_BLOCK):
        inv_s[EXPERT_RING - 1, 0, r] = spare_row0 + r

    def expert(e, carry):
        for cp in w_copies(e):
            cp.wait()

        @pl.when(e + WEIGHT_RING - 1 <= last_e)
        def _():
            for cp in w_copies(e + WEIGHT_RING - 1):
                cp.start()

        slot = e % WEIGHT_RING
        wgu_s[...] = wgu_f[slot].astype(BF16)
        wdn_s[...] = wdn_f[slot].astype(BF16)
        first = start_ref[e]

        def block(i, c):
            g = first + i
            for cp in in_copies(g):
                cp.wait()
            for cp in in_copies(g + EXPERT_AHEAD):
                cp.start()
            s = g % EXPERT_RING

            @pl.when(g >= EXPERT_RING - 1)
            def _():
                out_wait(s)

            issue_rows((g + EXPERT_RING - 1) % EXPERT_RING)
            lo, hi = _unpack_halves(xbuf[s])
            gu = _dot(lo.astype(BF16), wgu_s[0:half, :]) + _dot(hi.astype(BF16), wgu_s[half:, :])
            act = _silu(gu[:, :EXPERT_FF]) * gu[:, EXPERT_FF:]
            ybuf[s] = _pack_halves(_dot(act.astype(BF16), wdn_s[...]))
            return c

        lax.fori_loop(0, count_ref[e], block, 0)
        return carry

    lax.fori_loop(0, N_EXPERTS, expert, 0)
    issue_rows((used + EXPERT_RING - 1) % EXPERT_RING)

    for j in range(EXPERT_AHEAD):
        for cp in in_copies(used + j):
            cp.wait()
    for j in range(EXPERT_RING):
        @pl.when(used > j)
        def _():
            out_wait((used - 1 - j) % EXPERT_RING)

    @pl.when(used < EXPERT_RING)
    def _():
        out_wait(EXPERT_RING - 1)


def _experts(blk_start, blk_count, xs, inv2, w_gu, w_dn, n_slots):
    rows, half = xs.shape
    nblk = rows // EXPERT_BLOCK
    grid_spec = pltpu.PrefetchScalarGridSpec(
        num_scalar_prefetch=2,
        grid=(1,),
        in_specs=[pl.BlockSpec(memory_space=pl.ANY), pl.BlockSpec(memory_space=pl.ANY),
                  pl.BlockSpec(memory_space=pl.ANY), pl.BlockSpec(memory_space=pl.ANY)],
        out_specs=pl.BlockSpec(memory_space=pl.ANY),
        scratch_shapes=[pltpu.VMEM((WEIGHT_RING, D_MODEL, 2 * EXPERT_FF), F32),
                        pltpu.VMEM((WEIGHT_RING, EXPERT_FF, D_MODEL), F32),
                        pltpu.VMEM((D_MODEL, 2 * EXPERT_FF), BF16), pltpu.VMEM((EXPERT_FF, D_MODEL), BF16),
                        pltpu.VMEM((EXPERT_RING, EXPERT_BLOCK, half), jnp.uint32),
                        pltpu.VMEM((EXPERT_RING, EXPERT_BLOCK, half), jnp.uint32),
                        pltpu.SMEM((EXPERT_RING, 1, EXPERT_BLOCK), jnp.int32),
                        pltpu.SemaphoreType.DMA((2, WEIGHT_RING)), pltpu.SemaphoreType.DMA((EXPERT_RING,)),
                        pltpu.SemaphoreType.DMA((EXPERT_RING,)), pltpu.SemaphoreType.DMA((EXPERT_RING,))],
    )
    return pl.pallas_call(
        functools.partial(_experts_kernel, nblk_total=nblk, spare_row0=n_slots - EXPERT_BLOCK),
        out_shape=jax.ShapeDtypeStruct((n_slots, half), jnp.uint32),
        grid_spec=grid_spec,
        compiler_params=_cparams(("arbitrary",)),
        name="experts",
    )(blk_start, blk_count, xs, inv2, w_gu, w_dn)


def _combine_kernel(x1_ref, h_ref, w_ref, g2_ref, wsgu_ref, wsdn_ref, buf, o_ref):
    su = _dot(h_ref[...].astype(BF16), wsgu_ref[...])
    y = _dot((_silu(su[:, :SHARED_FF]) * su[:, SHARED_FF:]).astype(BF16), wsdn_ref[...])
    w = w_ref[...]
    half = D_MODEL // 2
    ylo, yhi = y[:, :half], y[:, half:]
    for k in range(TOP_K):
        lo, hi = _unpack_halves(buf[k])
        wk = w[:, k:k + 1]
        ylo = ylo + wk * lo
        yhi = yhi + wk * hi
    g2 = g2_ref[0]
    o_ref[:, :half] = x1_ref[:, :half] + g2[:, :half] * ylo
    o_ref[:, half:] = x1_ref[:, half:] + g2[:, half:] * yhi


def _combine(x1, h2, wts, mod3, wsgu, wsdn, slots3, seq):
    t = x1.shape[0]
    tm = min(256, seq)
    tpb = seq // tm
    const = lambda i: (0, 0)
    row = lambda n: pl.BlockSpec((tm, n), lambda i: (i, 0))
    return pl.pallas_call(
        _combine_kernel,
        out_shape=jax.ShapeDtypeStruct((t, D_MODEL), F32),
        grid=(t // tm,),
        in_specs=[row(D_MODEL), row(D_MODEL), row(LANES),
                  pl.BlockSpec((1, 1, D_MODEL), lambda i: (i // tpb, 0, 5)),
                  pl.BlockSpec(wsgu.shape, const), pl.BlockSpec(wsdn.shape, const),
                  pl.BlockSpec((TOP_K, tm, D_MODEL // 2), lambda i: (0, i, 0))],
        out_specs=row(D_MODEL),
        compiler_params=_cparams(("arbitrary",)),
        name="combine",
    )(x1, h2, wts, mod3, wsgu, wsdn, slots3)


def _layer(x, c, layer, w_ada, b_ada, norm1_g, w_in, conv_w, a_log, dt_bias, gdn_norm_g, w_o_gdn,
           q_norm_g, k_norm_g, lambda_q1, lambda_k1, lambda_q2, lambda_k2, subln_g, w_o_diff,
           w_out, norm2_g, w_router, router_bias, w_exp_gate_up, w_exp_down,
           w_shared_gate_up, w_shared_down):
    nb, seq, d = x.shape
    t = nb * seq
    lam_init = 0.8 - 0.6 * math.exp(-0.3 * layer)
    x2 = x.reshape(t, d)

    c_pad = jnp.pad(c, ((0, SUBLANES - nb % SUBLANES if nb % SUBLANES else 0), (0, 0)))
    mod = _ada(c_pad, w_ada, b_ada.reshape(1, -1))[:nb]
    mod3 = mod.reshape(nb, 1, 6 * d)

    o_bd = 2 * GDN_QK + 2 * GDN_V
    o_att = o_bd + 2 * GDN_HEADS
    o_gate = o_att + 2 * DIFF_QK + DIFF_V
    w_gdn = w_in[:, :o_bd].astype(BF16)
    w_bd = jnp.pad(w_in[:, o_bd:o_att], ((0, 0), (0, LANES - 2 * GDN_HEADS))).astype(BF16)
    w_att = w_in[:, o_att:o_gate].astype(BF16)
    w_gate = w_in[:, o_gate:].astype(BF16)
    gdn_in, bd, att_in, gates = _inproj(x2, mod3, norm1_g.reshape(1, d), w_gdn, w_bd, w_att, w_gate, seq)

    aparams = jnp.zeros((SUBLANES, LANES), F32)
    aparams = aparams.at[0, GDN_HEADS:2 * GDN_HEADS].set(a_log).at[1, GDN_HEADS:2 * GDN_HEADS].set(dt_bias)
    oa = _gdn(gdn_in.reshape(nb, seq, -1), bd.reshape(nb, seq, LANES), conv_w, aparams,
              gdn_norm_g.reshape(1, GDN_DV)).reshape(t, GDN_V)

    tq = min(512, seq)
    slopes = [2.0 ** (-8.0 * (h + 1) / DIFF_HEADS) for h in range(DIFF_HEADS)]
    c3 = _bf16_split3(LOG2E)
    qfeat = np.zeros((1, 2 * DIFF_HEADS * DIFF_DH), np.float32)
    for g in range(2 * DIFF_HEADS):
        for j in range(6):
            qfeat[0, g * DIFF_DH + j] = slopes[g // 2] * c3[j // 2]
    grp = np.arange(DIFF_QK) // DIFF_DH
    bdm = jnp.asarray((grp[:, None] == grp[None, :]).astype(np.float32)).astype(BF16)
    qa, ka, vt = _aprep(att_in, jnp.tile(q_norm_g, 2 * DIFF_HEADS).reshape(1, -1),
                        jnp.tile(k_norm_g, 2 * DIFF_HEADS).reshape(1, -1), jnp.asarray(qfeat), bdm, nb, seq, tq)
    slopes2 = jnp.asarray([s * (c3[0] + c3[1] + c3[2]) for s in slopes], F32)
    lamv = jnp.zeros((SUBLANES, DIFF_DH), F32)
    lamv = lamv.at[0].set(lambda_q1).at[1].set(lambda_k1).at[2].set(lambda_q2).at[3].set(lambda_k2)
    ob = _attn(slopes2, qa, ka, vt, lamv, subln_g.reshape(-1, 1), nb, seq, tq, lam_init)

    x1, h2, hp = _post(x2, oa, ob, gates, mod3, norm2_g.reshape(1, d), w_o_gdn.astype(BF16),
                       w_o_diff.astype(BF16), w_out.astype(BF16), seq)

    idx_t, wts_t, counts = _router(h2, w_router.T, router_bias.reshape(-1, 1))
    idx = jnp.pad(idx_t.T, ((0, 0), (0, LANES - TOP_K)))
    wts = jnp.pad(wts_t.T, ((0, 0), (0, LANES - TOP_K)))
    cnt = counts[0].astype(jnp.int32)
    padded = (cnt + EXPERT_BLOCK - 1) // EXPERT_BLOCK * EXPERT_BLOCK
    pends = jnp.cumsum(padded)
    pstart = (pends - padded).astype(F32).reshape(1, -1)
    nblk = -(-(t * TOP_K) // EXPERT_BLOCK) + N_EXPERTS
    blk_start = ((pends - padded) // EXPERT_BLOCK).astype(jnp.int32)
    blk_count = (padded // EXPERT_BLOCK).astype(jnp.int32)
    dest = _rank(idx, pstart)
    dest_flat = dest[:, :TOP_K].reshape(-1)
    n_rows = nblk * EXPERT_BLOCK
    assert n_rows % t == 0
    inv = _invert(dest_flat, n_rows, t)
    xs = _sc_dispatch(hp, inv, t)
    assert t >= EXPERT_BLOCK
    slots = _experts(blk_start, blk_count, xs, inv.reshape(nblk, EXPERT_BLOCK), w_exp_gate_up, w_exp_down,
                     TOP_K * t + n_rows + t)
    out = _combine(x1, h2, wts, mod3, w_shared_gate_up.astype(BF16), w_shared_down.astype(BF16),
                   slots.reshape(-1, t, d // 2), seq)
    return out.reshape(nb, seq, d)


def kernel(x, c, w_ada, b_ada, norm1_g, w_in, conv_w, a_log, dt_bias, gdn_norm_g, w_o_gdn, q_norm_g, k_norm_g, lambda_q1, lambda_k1, lambda_q2, lambda_k2, subln_g, w_o_diff, w_out, norm2_g, w_router, router_bias, w_exp_gate_up, w_exp_down, w_shared_gate_up, w_shared_down):
    params = (w_ada, b_ada, norm1_g, w_in, conv_w, a_log, dt_bias, gdn_norm_g, w_o_gdn, q_norm_g,
              k_norm_g, lambda_q1, lambda_k1, lambda_q2, lambda_k2, subln_g, w_o_diff, w_out, norm2_g,
              w_router, router_bias, w_exp_gate_up, w_exp_down, w_shared_gate_up, w_shared_down)
    for layer in range(w_ada.shape[0]):
        x = _layer(x, c, layer, *(p[layer] for p in params))
    return x
```

```python
import functools
import math

import jax
import jax.numpy as jnp
import numpy as np
from jax import lax
from jax.experimental import pallas as pl
from jax.experimental.pallas import tpu as pltpu
from jax.experimental.pallas import tpu_sc as plsc

F32 = jnp.float32
BF16 = jnp.bfloat16
HIGHEST = lax.Precision.HIGHEST

D_MODEL = 1024
CHUNK = 64
GDN_HEADS = 4
GDN_DK = 128
GDN_DV = 128
GDN_CONV = 4
DIFF_HEADS = 4
DIFF_DH = 64
N_EXPERTS = 256
TOP_K = 8
N_GROUPS = 8
TOPK_GROUPS = 4
EXPERT_FF = 256
SHARED_FF = 256
ROUTED_SCALE = 2.5
NORM_EPS = 1e-6
GROUP_SIZE = N_EXPERTS // N_GROUPS

GDN_QK = GDN_HEADS * GDN_DK
GDN_V = GDN_HEADS * GDN_DV
CONV_CH = 2 * GDN_QK + GDN_V
DIFF_QK = DIFF_HEADS * 2 * DIFF_DH
DIFF_V = DIFF_HEADS * 2 * DIFF_DH

LANES = 128
SUBLANES = 8
EXPERT_BLOCK = 128
INV_BASE = 8
ATTN_HEADS_PER_STEP = 2
LOG2E = math.log2(math.e)
VMEM_LIMIT = 56 * 1024 * 1024


def _cparams(sem):
    return pltpu.CompilerParams(dimension_semantics=sem, vmem_limit_bytes=VMEM_LIMIT)


def _dot(a, b):
    return jnp.dot(a, b, preferred_element_type=F32)


def _dot_nt(a, b):
    return lax.dot_general(a, b, (((1,), (1,)), ((), ())), preferred_element_type=F32)


def _dot_tn(a, b):
    return lax.dot_general(a, b, (((0,), (0,)), ((), ())), preferred_element_type=F32)


def _silu(x):
    return x * jax.nn.sigmoid(x)


def _bf16_split3(x):
    rnd = lambda v: float(np.float32(v).astype(BF16).astype(np.float32))
    a = rnd(x)
    b = rnd(x - a)
    c = rnd(x - a - b)
    return a, b, c


def _ada_kernel(c_ref, w_ref, b_ref, o_ref):
    s = _silu(c_ref[...])
    o_ref[...] = jnp.dot(s, w_ref[...], precision=HIGHEST, preferred_element_type=F32) + b_ref[...]


def _ada(c_pad, w_ada, b_ada):
    n = w_ada.shape[1]
    tn = 1024
    return pl.pallas_call(
        _ada_kernel,
        out_shape=jax.ShapeDtypeStruct((c_pad.shape[0], n), F32),
        grid=(n // tn,),
        in_specs=[pl.BlockSpec(c_pad.shape, lambda j: (0, 0)),
                  pl.BlockSpec((D_MODEL, tn), lambda j: (0, j)),
                  pl.BlockSpec((1, tn), lambda j: (0, j))],
        out_specs=pl.BlockSpec((c_pad.shape[0], tn), lambda j: (0, j)),
        compiler_params=_cparams(("arbitrary",)),
        name="ada",
    )(c_pad, w_ada, b_ada)


def _inproj_kernel(x_ref, shift_ref, scale_ref, g_ref, wg_ref, wbd_ref, wa_ref, wgt_ref,
                   gdn_ref, bd_ref, att_ref, gate_ref):
    x = x_ref[...]
    y = x * lax.rsqrt(jnp.mean(x * x, axis=-1, keepdims=True) + NORM_EPS) * g_ref[...]
    h = (y * (1.0 + scale_ref[0]) + shift_ref[0]).astype(BF16)
    gdn_ref[...] = _dot(h, wg_ref[...]).astype(gdn_ref.dtype)
    bd_ref[...] = _dot(h, wbd_ref[...])
    att_ref[...] = _dot(h, wa_ref[...]).astype(att_ref.dtype)
    gate_ref[...] = _dot(h, wgt_ref[...]).astype(gate_ref.dtype)


def _inproj(x2, mod3, norm1_g, w_gdn, w_bd, w_att, w_gate, seq):
    t = x2.shape[0]
    tm = min(256, seq)
    tiles_per_b = seq // tm
    const = lambda i: (0, 0)
    wspec = lambda w: pl.BlockSpec(w.shape, const, pipeline_mode=pl.Buffered(1))
    row = lambda n: pl.BlockSpec((tm, n), lambda i: (i, 0))
    return pl.pallas_call(
        _inproj_kernel,
        out_shape=(jax.ShapeDtypeStruct((t, w_gdn.shape[1]), BF16),
                   jax.ShapeDtypeStruct((t, LANES), F32),
                   jax.ShapeDtypeStruct((t, w_att.shape[1]), BF16),
                   jax.ShapeDtypeStruct((t, w_gate.shape[1]), BF16)),
        grid=(t // tm,),
        in_specs=[row(D_MODEL),
                  pl.BlockSpec((1, 1, D_MODEL), lambda i: (i // tiles_per_b, 0, 0)),
                  pl.BlockSpec((1, 1, D_MODEL), lambda i: (i // tiles_per_b, 0, 1)),
                  pl.BlockSpec((1, D_MODEL), const),
                  wspec(w_gdn), wspec(w_bd), wspec(w_att), wspec(w_gate)],
        out_specs=(row(w_gdn.shape[1]), row(LANES), row(w_att.shape[1]), row(w_gate.shape[1])),
        compiler_params=_cparams(("arbitrary",)),
        name="inproj",
    )(x2, mod3, mod3, norm1_g, w_gdn, w_bd, w_att, w_gate)


def _gdn_kernel(x_ref, bd_ref, cw_ref, ap_ref, ng_ref, o_ref, cbuf, state, *, nb, lg):
    step = pl.program_id(0)
    nc = lg // CHUNK

    @pl.when(step == 0)
    def _():
        cbuf[...] = jnp.zeros_like(cbuf)
        state[...] = jnp.zeros_like(state)

    ri = lax.broadcasted_iota(jnp.int32, (CHUNK, CHUNK), 0)
    ci = lax.broadcasted_iota(jnp.int32, (CHUNK, CHUNK), 1)
    causal = ri >= ci
    strict = ri > ci
    eye = (ri == ci).astype(F32)
    rl = lax.broadcasted_iota(jnp.int32, (lg, lg), 0)
    cl = lax.broadcasted_iota(jnp.int32, (lg, lg), 1)
    blocktri = ((rl >= cl) & ((rl // CHUNK) == (cl // CHUNK))).astype(F32)
    cw = cw_ref[...]
    a_row = ap_ref[0:1, :]
    dt_row = ap_ref[1:2, :]
    ng = ng_ref[...]

    units = []
    for b in range(nb):
        cbuf[b, SUBLANES:SUBLANES + lg, :] = x_ref[b, :, 0:CONV_CH].astype(F32)
        acc = cw[GDN_CONV - 1:GDN_CONV, :] * cbuf[b, SUBLANES:SUBLANES + lg, :]
        for j in range(GDN_CONV - 1):
            off = SUBLANES - (GDN_CONV - 1) + j
            acc = acc + cw[j:j + 1, :] * cbuf[b, off:off + lg, :]
        cbuf[b, 0:SUBLANES, :] = cbuf[b, lg:lg + SUBLANES, :]
        qkv = _silu(acc)

        bd = bd_ref[b]
        beta_t = jax.nn.sigmoid(bd)
        g_t = -jnp.exp(a_row) * jax.nn.softplus(bd + dt_row)
        gcum = jnp.dot(blocktri, g_t, precision=HIGHEST, preferred_element_type=F32)
        gcum_t = gcum.T
        egcum = jnp.exp(gcum)

        for h in range(GDN_HEADS):
            qh = qkv[:, h * GDN_DK:(h + 1) * GDN_DK]
            kh = qkv[:, GDN_QK + h * GDN_DK:GDN_QK + (h + 1) * GDN_DK]
            vh = qkv[:, 2 * GDN_QK + h * GDN_DV:2 * GDN_QK + (h + 1) * GDN_DV]
            qh = qh * lax.rsqrt(jnp.sum(qh * qh, axis=-1, keepdims=True) + NORM_EPS) * (GDN_DK ** -0.5)
            kh = kh * lax.rsqrt(jnp.sum(kh * kh, axis=-1, keepdims=True) + NORM_EPS)
            for c in range(nc):
                r0, r1 = c * CHUNK, (c + 1) * CHUNK
                q, k, v = qh[r0:r1], kh[r0:r1], vh[r0:r1]
                gc = gcum[r0:r1, 4 + h:5 + h]
                gr = gcum_t[4 + h:5 + h, r0:r1]
                egc = egcum[r0:r1, 4 + h:5 + h]
                bcol = beta_t[r0:r1, h:h + 1]
                gl = gcum[r1 - 1:r1, 4 + h:5 + h]
                kb = k.astype(BF16)
                units.append(dict(
                    b=b, h=h, c=c, kb=kb, bcol=bcol,
                    qkb=jnp.concatenate([q.astype(BF16), kb], axis=0),
                    decay=jnp.exp(jnp.where(causal, gc - gr, -jnp.inf)),
                    rhs=jnp.concatenate([v * bcol, k * (bcol * egc)], axis=-1).astype(BF16),
                    qg=(q * egc).astype(BF16), egl=jnp.exp(gl),
                    kdec=(k * jnp.exp(gl - gc)).astype(BF16)))

    for u in units:
        u["kq"] = _dot_nt(u["qkb"], u["kb"])
    def same_block(size):
        return (ri // size) == (ci // size)
    for u in units:
        lower = jnp.where(strict, u["bcol"] * u["kq"][CHUNK:] * u["decay"], 0.0)
        u["lower"] = lower
        diag = jnp.where(same_block(INV_BASE), lower, 0.0)
        u["xinv"] = eye - diag
        u["pw"] = diag.astype(BF16)
    for r in range(INV_BASE.bit_length() - 2):
        for u in units:
            u["pw"] = _dot(u["pw"], u["pw"]).astype(BF16)
        for u in units:
            u["xinv"] = u["xinv"] + _dot(u["xinv"].astype(BF16), u["pw"])
    size = INV_BASE
    while size < CHUNK:
        pair_off = same_block(2 * size) & jnp.logical_not(same_block(size))
        for u in units:
            u["xb"] = u["xinv"].astype(BF16)
            u["cx"] = _dot(jnp.where(pair_off, u["lower"], 0.0).astype(BF16), u["xb"]).astype(BF16)
        for u in units:
            u["xinv"] = u["xinv"] - _dot(u["xb"], u["cx"])
        size *= 2
    for u in units:
        u["sol"] = _dot(u["xinv"].astype(BF16), u["rhs"])
        u["a_qk"] = (u["kq"][:CHUNK] * u["decay"]).astype(BF16)

    st = {(b, h): state[b, h] for b in range(nb) for h in range(GDN_HEADS)}
    for c in range(nc):
        cu = [u for u in units if u["c"] == c]
        for u in cu:
            stb = st[(u["b"], u["h"])].astype(BF16)
            u["ws"] = _dot(jnp.concatenate([u["sol"][:, GDN_DV:].astype(BF16), u["qg"]], axis=0), stb)
        for u in cu:
            u["vnb"] = (u["sol"][:, :GDN_DV] - u["ws"][:CHUNK]).astype(BF16)
        for u in cu:
            u["out"] = u["ws"][CHUNK:] + _dot(u["a_qk"], u["vnb"])
        for u in cu:
            key = (u["b"], u["h"])
            st[key] = st[key] * u["egl"] + _dot_tn(u["kdec"], u["vnb"])
        for u in cu:
            b, h = u["b"], u["h"]
            r0, r1 = c * CHUNK, (c + 1) * CHUNK
            out = u["out"]
            on = out * lax.rsqrt(jnp.mean(out * out, axis=-1, keepdims=True) + NORM_EPS) * ng
            zh = x_ref[b, r0:r1, CONV_CH + h * GDN_DV:CONV_CH + (h + 1) * GDN_DV].astype(F32)
            o_ref[b, r0:r1, h * GDN_DV:(h + 1) * GDN_DV] = on * _silu(zh)
    for (b, h), s in st.items():
        state[b, h] = s


def _gdn(gdn_in3, bd3, conv_w, aparams, gdn_norm_g):
    nb, seq, _ = gdn_in3.shape
    lg = min(128, seq)
    const = lambda i: (0, 0)
    return pl.pallas_call(
        functools.partial(_gdn_kernel, nb=nb, lg=lg),
        out_shape=jax.ShapeDtypeStruct((nb, seq, GDN_V), F32),
        grid=(seq // lg,),
        in_specs=[pl.BlockSpec((nb, lg, gdn_in3.shape[2]), lambda i: (0, i, 0)),
                  pl.BlockSpec((nb, lg, LANES), lambda i: (0, i, 0)),
                  pl.BlockSpec(conv_w.shape, const),
                  pl.BlockSpec(aparams.shape, const),
                  pl.BlockSpec(gdn_norm_g.shape, const)],
        out_specs=pl.BlockSpec((nb, lg, GDN_V), lambda i: (0, i, 0)),
        scratch_shapes=[pltpu.VMEM((nb, lg + SUBLANES, CONV_CH), F32),
                        pltpu.VMEM((nb, GDN_HEADS, GDN_DK, GDN_DV), F32)],
        compiler_params=_cparams(("arbitrary",)),
        name="gdn",
    )(gdn_in3, bd3, conv_w, aparams, gdn_norm_g)


def _aprep_kernel(x_ref, gq_ref, gk_ref, qf_ref, bdm_ref, q_ref, k_ref, vt_ref, *, tk):
    x = x_ref[...].astype(F32)
    tm = x.shape[0]
    bdm = bdm_ref[...]

    def qknorm(v, gain):
        sq = v * v
        hi = sq.astype(BF16)
        lo = (sq - hi.astype(F32)).astype(BF16)
        ms = (_dot(hi, bdm) + _dot(lo, bdm)) * (1.0 / DIFF_DH)
        return v * lax.rsqrt(ms + NORM_EPS) * gain

    qn = qknorm(x[:, 0:DIFF_QK], gq_ref[...]) * (DIFF_DH ** -0.5 * LOG2E)
    kn = qknorm(x[:, DIFF_QK:2 * DIFF_QK], gk_ref[...])
    pos = pl.program_id(0) * tm + lax.broadcasted_iota(jnp.int32, (tm, DIFF_DH), 0)
    krel = pos % tk
    lane = lax.broadcasted_iota(jnp.int32, (tm, DIFF_DH), 1)
    hi = ((krel // 256) * 256).astype(F32)
    lo = (krel % 256).astype(F32)
    kfeat = jnp.where(lane < 6, jnp.where(lane % 2 == 0, hi, lo), 0.0)
    qfeat = qf_ref[...]
    qparts, kparts = [], []
    for g in range(2 * DIFF_HEADS):
        qparts += [qn[:, g * DIFF_DH:(g + 1) * DIFF_DH],
                   jnp.broadcast_to(qfeat[:, g * DIFF_DH:(g + 1) * DIFF_DH], (tm, DIFF_DH))]
        kparts += [kn[:, g * DIFF_DH:(g + 1) * DIFF_DH], kfeat]
    q_ref[...] = jnp.concatenate(qparts, axis=-1).astype(BF16)
    k_ref[...] = jnp.concatenate(kparts, axis=-1).astype(BF16)
    vt_ref[0, 0] = x[:, 2 * DIFF_QK:].T.astype(BF16)


def _aprep(att_in, gq, gk, qfeat, bdm, nb, seq, tk):
    t = att_in.shape[0]
    tm = tk
    nk = seq // tk
    const = lambda i: (0, 0)
    return pl.pallas_call(
        functools.partial(_aprep_kernel, tk=tk),
        out_shape=(jax.ShapeDtypeStruct((t, 2 * DIFF_QK), BF16),
                   jax.ShapeDtypeStruct((t, 2 * DIFF_QK), BF16),
                   jax.ShapeDtypeStruct((nb, nk, DIFF_V, tk), BF16)),
        grid=(t // tm,),
        in_specs=[pl.BlockSpec((tm, att_in.shape[1]), lambda i: (i, 0)),
                  pl.BlockSpec(gq.shape, const), pl.BlockSpec(gk.shape, const),
                  pl.BlockSpec(qfeat.shape, const), pl.BlockSpec(bdm.shape, const)],
        out_specs=(pl.BlockSpec((tm, 2 * DIFF_QK), lambda i: (i, 0)),
                   pl.BlockSpec((tm, 2 * DIFF_QK), lambda i: (i, 0)),
                   pl.BlockSpec((1, 1, DIFF_V, tk), lambda i: (i // nk, i % nk, 0, 0))),
        compiler_params=_cparams(("arbitrary",)),
        name="aprep",
    )(att_in, gq, gk, qfeat, bdm)


def _attn_kernel(slope_ref, q_ref, k_ref, vt_ref, lamv_ref, sg_ref, o_ref, m_s, l_s, acc_s, sa, sb,
                 *, tq, lam_init):
    hp = pl.program_id(1)
    qi = pl.program_id(2)
    dv = 2 * DIFF_DH
    nstream = 2 * ATTN_HEADS_PER_STEP
    slope2 = [slope_ref[hp * ATTN_HEADS_PER_STEP + c // 2] for c in range(nstream)]
    qs = [q_ref[:, c * LANES:(c + 1) * LANES] for c in range(nstream)]

    m_s[...] = jnp.full_like(m_s, -jnp.inf)
    l_s[...] = jnp.zeros_like(l_s)
    acc_s[...] = jnp.zeros_like(acc_s)

    def scores(kj):
        k0 = pl.multiple_of(kj * tq, tq)
        return tuple(_dot_nt(k_ref[pl.ds(k0, tq), c * LANES:(c + 1) * LANES], qs[c]) for c in range(nstream))

    def accumulate(ss, kj):
        rel = ((kj - qi) * tq).astype(F32)
        for c in range(nstream):
            hl = c // 2
            vt = vt_ref[0, kj, hl * dv:(hl + 1) * dv, :]
            off = slope2[c] * rel
            s = ss[c]
            m_old = m_s[c]
            m_new = jnp.maximum(m_old, jnp.max(s, axis=0, keepdims=True) + off)
            alpha = jnp.exp2(m_old - m_new)
            p = jnp.exp2(s - (m_new - off))
            l_s[c] = alpha * l_s[c] + jnp.sum(p, axis=0, keepdims=True)
            acc_s[c] = alpha * acc_s[c] + _dot(vt, p.astype(BF16))
            m_s[c] = m_new

    kpos = lax.broadcasted_iota(jnp.int32, (tq, tq), 0)
    qpos = lax.broadcasted_iota(jnp.int32, (tq, tq), 1)
    allowed = (kpos // CHUNK) <= (qpos // CHUNK)
    fut = jnp.maximum(kpos - qpos, 0).astype(F32)
    adds = [jnp.where(allowed, (-2.0 * slope2[2 * hl]) * fut, -jnp.inf) for hl in range(ATTN_HEADS_PER_STEP)]
    accumulate(tuple(s + adds[c // 2] for c, s in enumerate(scores(qi))), qi)

    def put(slot, ss):
        for c in range(nstream):
            slot[c] = ss[c]

    def get(slot):
        return tuple(slot[c] for c in range(nstream))

    npairs = qi // 2
    put(sa, scores(0))

    def pair(j, carry):
        a = 2 * j
        put(sb, scores(a + 1))
        accumulate(get(sa), a)
        put(sa, scores(jnp.minimum(a + 2, qi - 1)))
        accumulate(get(sb), a + 1)
        return carry
    lax.fori_loop(0, npairs, pair, 0)

    @pl.when(qi % 2 == 1)
    def _():
        accumulate(get(sa), qi - 1)

    lq1, lk1, lq2, lk2 = (lamv_ref[i:i + 1, :] for i in range(4))
    lam = (jnp.exp(jnp.sum(lq1 * lk1, axis=-1, keepdims=True))
           - jnp.exp(jnp.sum(lq2 * lk2, axis=-1, keepdims=True)) + lam_init)
    for hl in range(ATTN_HEADS_PER_STEP):
        o = acc_s[2 * hl] / l_s[2 * hl] - lam * (acc_s[2 * hl + 1] / l_s[2 * hl + 1])
        on = o * lax.rsqrt(jnp.mean(o * o, axis=0, keepdims=True) + NORM_EPS) * sg_ref[...] * (1.0 - lam_init)
        o_ref[:, hl * dv:(hl + 1) * dv] = on.T


def _attn(slopes2, qa, ka, vt, lamv, sg_col, nb, seq, tq, lam_init):
    t = qa.shape[0]
    nq = seq // tq
    dv = 2 * DIFF_DH
    hps = ATTN_HEADS_PER_STEP
    nstream = 2 * hps
    grid_spec = pltpu.PrefetchScalarGridSpec(
        num_scalar_prefetch=1,
        grid=(nb, DIFF_HEADS // hps, nq),
        in_specs=[pl.BlockSpec((tq, nstream * LANES), lambda b, h, i, s: (b * nq + i, h)),
                  pl.BlockSpec((seq, nstream * LANES), lambda b, h, i, s: (b, h)),
                  pl.BlockSpec((1, nq, hps * dv, tq), lambda b, h, i, s: (b, 0, h, 0)),
                  pl.BlockSpec(lamv.shape, lambda b, h, i, s: (0, 0)),
                  pl.BlockSpec(sg_col.shape, lambda b, h, i, s: (0, 0))],
        out_specs=pl.BlockSpec((tq, hps * dv), lambda b, h, i, s: (b * nq + i, h)),
        scratch_shapes=[pltpu.VMEM((nstream, 1, tq), F32), pltpu.VMEM((nstream, 1, tq), F32),
                        pltpu.VMEM((nstream, dv, tq), F32),
                        pltpu.VMEM((nstream, tq, tq), F32), pltpu.VMEM((nstream, tq, tq), F32)],
    )
    return pl.pallas_call(
        functools.partial(_attn_kernel, tq=tq, lam_init=lam_init),
        out_shape=jax.ShapeDtypeStruct((t, DIFF_V), F32),
        grid_spec=grid_spec,
        compiler_params=_cparams(("arbitrary", "arbitrary", "arbitrary")),
        name="attn",
    )(slopes2, qa, ka, vt, lamv, sg_col)


def _pack_halves(x):
    n = x.shape[1] // 2
    lo = pltpu.bitcast(x[:, :n].astype(BF16).astype(F32), jnp.uint32)
    hi = pltpu.bitcast(x[:, n:].astype(BF16).astype(F32), jnp.uint32)
    return (lo >> 16) | (hi & jnp.uint32(0xFFFF0000))


def _unpack_halves(p):
    lo = pltpu.bitcast(p << 16, F32)
    hi = pltpu.bitcast(p & jnp.uint32(0xFFFF0000), F32)
    return lo, hi


def _post_kernel(x_ref, oa_ref, ob_ref, gate_ref, g1_ref, sh2_ref, sc2_ref, n2_ref,
                 woa_ref, wob_ref, wout_ref, x1_ref, h2_ref, hp_ref):
    ya = _dot(oa_ref[...].astype(BF16), woa_ref[...])
    yb = _dot(ob_ref[...].astype(BF16), wob_ref[...])
    merged = (jax.nn.sigmoid(gate_ref[:, 0:D_MODEL].astype(F32)) * ya
              + jax.nn.sigmoid(gate_ref[:, D_MODEL:].astype(F32)) * yb)
    y = _dot(merged.astype(BF16), wout_ref[...])
    x1 = x_ref[...] + g1_ref[0] * y
    x1_ref[...] = x1
    n = x1 * lax.rsqrt(jnp.mean(x1 * x1, axis=-1, keepdims=True) + NORM_EPS) * n2_ref[...]
    h2 = n * (1.0 + sc2_ref[0]) + sh2_ref[0]
    h2_ref[...] = h2
    hp_ref[...] = _pack_halves(h2)


def _post(x2, oa, ob, gates, mod3, norm2_g, woa, wob, wout, seq):
    t = x2.shape[0]
    tm = min(256, seq)
    tpb = seq // tm
    const = lambda i: (0, 0)
    row = lambda n: pl.BlockSpec((tm, n), lambda i: (i, 0))
    modspec = lambda j: pl.BlockSpec((1, 1, D_MODEL), lambda i: (i // tpb, 0, j))
    wspec = lambda w: pl.BlockSpec(w.shape, const, pipeline_mode=pl.Buffered(1))
    return pl.pallas_call(
        _post_kernel,
        out_shape=(jax.ShapeDtypeStruct((t, D_MODEL), F32), jax.ShapeDtypeStruct((t, D_MODEL), F32),
                   jax.ShapeDtypeStruct((t, D_MODEL // 2), jnp.uint32)),
        grid=(t // tm,),
        in_specs=[row(D_MODEL), row(GDN_V), row(DIFF_V), row(2 * D_MODEL),
                  modspec(2), modspec(3), modspec(4), pl.BlockSpec((1, D_MODEL), const),
                  wspec(woa), wspec(wob), wspec(wout)],
        out_specs=(row(D_MODEL), row(D_MODEL), row(D_MODEL // 2)),
        compiler_params=_cparams(("arbitrary",)),
        name="post",
    )(x2, oa, ob, gates, mod3, mod3, mod3, norm2_g, woa, wob, wout)


def _router_kernel(h_ref, wrt_ref, rb_ref, idx_ref, w_ref, cnt_ref, cnt_s):
    step = pl.program_id(0)

    @pl.when(step == 0)
    def _():
        cnt_s[...] = jnp.zeros_like(cnt_s)

    tm = h_ref.shape[0]
    neg = -jnp.inf
    logits = lax.dot_general(wrt_ref[...], h_ref[...], (((1,), (1,)), ((), ())), precision=HIGHEST,
                             preferred_element_type=F32)
    scores = jax.nn.sigmoid(logits)
    choice = scores + rb_ref[...]
    eid = lax.broadcasted_iota(jnp.int32, (N_EXPERTS, tm), 0).astype(F32)
    c3 = choice.reshape(N_GROUPS, GROUP_SIZE, tm)
    e3 = eid.reshape(N_GROUPS, GROUP_SIZE, tm)
    m1 = jnp.max(c3, axis=1, keepdims=True)
    i1 = jnp.min(jnp.where(c3 == m1, e3, float(N_EXPERTS)), axis=1, keepdims=True)
    m2 = jnp.max(jnp.where(e3 == i1, neg, c3), axis=1, keepdims=True)
    gscore = (m1 + m2).reshape(N_GROUPS, tm)
    gid = lax.broadcasted_iota(jnp.int32, (N_GROUPS, tm), 0).astype(F32)
    gsel = jnp.zeros((N_GROUPS, tm), F32)
    for _ in range(TOPK_GROUPS):
        m = jnp.max(gscore, axis=0, keepdims=True)
        g = jnp.min(jnp.where(gscore == m, gid, float(N_GROUPS)), axis=0, keepdims=True)
        hit = gid == g
        gsel = jnp.where(hit, 1.0, gsel)
        gscore = jnp.where(hit, neg, gscore)
    masked = jnp.where(gsel.reshape(N_GROUPS, 1, tm) > 0.0, c3, neg).reshape(N_EXPERTS, tm)
    idx_rows, w_rows = [], []
    sel = jnp.zeros((N_EXPERTS, tm), F32)
    for k in range(TOP_K):
        m = jnp.max(masked, axis=0, keepdims=True)
        i = jnp.min(jnp.where(masked == m, eid, float(N_EXPERTS)), axis=0, keepdims=True)
        hit = eid == i
        idx_rows.append(i)
        w_rows.append(jnp.sum(jnp.where(hit, scores, 0.0), axis=0, keepdims=True))
        sel = jnp.where(hit, 1.0, sel)
        masked = jnp.where(hit, neg, masked)
    ws = jnp.concatenate(w_rows, axis=0)
    idx_ref[...] = jnp.concatenate(idx_rows, axis=0).astype(jnp.int32)
    w_ref[...] = ws / jnp.sum(ws, axis=0, keepdims=True) * ROUTED_SCALE
    ones = jnp.ones((SUBLANES, tm), BF16)
    cnt_s[...] = cnt_s[...] + _dot_nt(ones, sel.astype(BF16))[0:1, :]
    cnt_ref[...] = cnt_s[...]


def _router(h2, w_router_t, rbias_col):
    t = h2.shape[0]
    tm = min(256, t)
    const = lambda i: (0, 0)
    return pl.pallas_call(
        _router_kernel,
        out_shape=(jax.ShapeDtypeStruct((TOP_K, t), jnp.int32),
                   jax.ShapeDtypeStruct((TOP_K, t), F32),
                   jax.ShapeDtypeStruct((1, N_EXPERTS), F32)),
        grid=(t // tm,),
        in_specs=[pl.BlockSpec((tm, D_MODEL), lambda i: (i, 0)),
                  pl.BlockSpec(w_router_t.shape, const), pl.BlockSpec(rbias_col.shape, const)],
        out_specs=(pl.BlockSpec((TOP_K, tm), lambda i: (0, i)),
                   pl.BlockSpec((TOP_K, tm), lambda i: (0, i)),
                   pl.BlockSpec((1, N_EXPERTS), const)),
        scratch_shapes=[pltpu.VMEM((1, N_EXPERTS), F32)],
        compiler_params=_cparams(("arbitrary",)),
        name="router",
    )(h2, w_router_t, rbias_col)


def _rank_kernel(idx_ref, ps_ref, dest_ref, run_s):
    step = pl.program_id(0)

    @pl.when(step == 0)
    def _():
        run_s[...] = jnp.zeros_like(run_s)

    tm = idx_ref.shape[0]
    idx = idx_ref[...]
    lane = lax.broadcasted_iota(jnp.int32, (tm, N_EXPERTS), 1)
    gl = lax.broadcasted_iota(jnp.int32, (tm, LANES), 1)
    hits = [lane == idx[:, k:k + 1] for k in range(TOP_K)]
    sel = jnp.zeros((tm, N_EXPERTS), F32)
    for hit in hits:
        sel = jnp.where(hit, 1.0, sel)
    ri = lax.broadcasted_iota(jnp.int32, (tm, tm), 0)
    ci = lax.broadcasted_iota(jnp.int32, (tm, tm), 1)
    before = _dot((ri > ci).astype(BF16), sel.astype(BF16))
    base = before + run_s[...] + ps_ref[...]
    dest = jnp.zeros((tm, LANES), F32)
    for k, hit in enumerate(hits):
        dk = jnp.sum(jnp.where(hit, base, 0.0), axis=-1, keepdims=True)
        dest = jnp.where(gl == k, dk, dest)
    dest_ref[...] = dest.astype(jnp.int32)
    run_s[...] = run_s[...] + jnp.sum(sel, axis=0, keepdims=True)


def _rank(idx, pstart):
    t = idx.shape[0]
    tm = min(256, t)
    const = lambda i: (0, 0)
    return pl.pallas_call(
        _rank_kernel,
        out_shape=jax.ShapeDtypeStruct((t, LANES), jnp.int32),
        grid=(t // tm,),
        in_specs=[pl.BlockSpec((tm, LANES), lambda i: (i, 0)), pl.BlockSpec(pstart.shape, const)],
        out_specs=pl.BlockSpec((tm, LANES), lambda i: (i, 0)),
        scratch_shapes=[pltpu.VMEM((1, N_EXPERTS), F32)],
        compiler_params=_cparams(("arbitrary",)),
        name="rank",
    )(idx, pstart)


def _row_copy(src, dst, sem):
    return pltpu.make_async_copy(src, dst, sem)


def _dispatch_kernel(dest_ref, h_ref, xs_in, xs_ref, sem, *, tm):
    del xs_in

    def start_row(r, c):
        for k in range(TOP_K):
            d = dest_ref[r * TOP_K + k]
            _row_copy(h_ref.at[pl.ds(r, 1), :], xs_ref.at[pl.ds(d, 1), :], sem).start(priority=k % 2)
        return c

    lax.fori_loop(0, tm, start_row, 0)
    for k in range(TOP_K):
        _row_copy(h_ref, xs_ref.at[pl.ds(0, tm), :], sem).wait()


def _dispatch(dest_flat, h2, xs0):
    t = h2.shape[0]
    tm = min(256, t)
    return pl.pallas_call(
        functools.partial(_dispatch_kernel, tm=tm),
        out_shape=jax.ShapeDtypeStruct(xs0.shape, xs0.dtype),
        grid=(t // tm,),
        in_specs=[pl.BlockSpec((tm * TOP_K,), lambda i: (i,), memory_space=pltpu.SMEM),
                  pl.BlockSpec((tm, h2.shape[1]), lambda i: (i, 0)),
                  pl.BlockSpec(memory_space=pl.ANY)],
        out_specs=pl.BlockSpec(memory_space=pl.ANY),
        scratch_shapes=[pltpu.SemaphoreType.DMA(())],
        input_output_aliases={2: 0},
        compiler_params=_cparams(("arbitrary",)),
        name="dispatch",
    )(dest_flat, h2, xs0)


SC_CORES = 2
SC_SUBCORES = 16
SC_LANES = 16
INVERT_CHUNK = 4096


def _invert(dest_flat, n_rows, n_tokens):
    n_assign = dest_flat.shape[0]
    workers = SC_CORES * SC_SUBCORES
    rpw = n_rows // workers
    assert rpw * workers == n_rows and rpw % SC_LANES == 0 and n_assign % INVERT_CHUNK == 0
    scratch_base = TOP_K * n_tokens
    log2_k = TOP_K.bit_length() - 1

    def body(dest_hbm, inv_hbm, loc, chunk_v):
        wid = lax.axis_index("s") * SC_CORES + lax.axis_index("c")
        lo = wid * rpw
        lane = lax.iota(jnp.int32, SC_LANES)

        def init(i, c):
            loc[pl.ds(i * SC_LANES, SC_LANES)] = scratch_base + lo + i * SC_LANES + lane
            return c
        lax.fori_loop(0, rpw // SC_LANES, init, 0)

        def do_chunk(ci, c):
            pltpu.sync_copy(dest_hbm.at[pl.ds(ci * INVERT_CHUNK, INVERT_CHUNK)], chunk_v)

            def inner(j, cc):
                rel = chunk_v[pl.ds(j * SC_LANES, SC_LANES)] - lo
                mine = (rel >= 0) & (rel < rpw)
                a = ci * INVERT_CHUNK + j * SC_LANES + lane
                val = (a & (TOP_K - 1)) * n_tokens + lax.shift_right_logical(a, log2_k)
                plsc.store_scatter(loc, [jnp.where(mine, rel, 0)], val, mask=mine)
                return cc
            lax.fori_loop(0, INVERT_CHUNK // SC_LANES, inner, 0)
            return c
        lax.fori_loop(0, n_assign // INVERT_CHUNK, do_chunk, 0)
        pltpu.sync_copy(loc, inv_hbm.at[pl.ds(lo, rpw)])

    mesh = plsc.VectorSubcoreMesh(core_axis_name="c", subcore_axis_name="s",
                                  num_cores=SC_CORES, num_subcores=SC_SUBCORES)
    return pl.kernel(body, out_type=jax.ShapeDtypeStruct((n_rows,), jnp.int32), mesh=mesh,
                     scratch_types=[pltpu.VMEM((rpw,), jnp.int32), pltpu.VMEM((INVERT_CHUNK,), jnp.int32)],
                     compiler_params=pltpu.CompilerParams(needs_layout_passes=False),
                     name="invert")(dest_flat)


EXPERT_RING = 5
EXPERT_AHEAD = EXPERT_RING - 2
WEIGHT_RING = 3


def _experts_kernel(start_ref, count_ref, xs_ref, inv_ref, wgu_ref, wdn_ref, g_ref,
                    wgu_f, wdn_f, wgu_s, wdn_s, xbuf, ybuf, inv_s, sem_w, sem_in, sem_inv, sem_out,
                    *, nblk_total, spare_row0):
    half = D_MODEL // 2
    last_e = N_EXPERTS - 1
    used = start_ref[last_e] + count_ref[last_e]

    def rows(g):
        return pl.ds(pl.multiple_of(g * EXPERT_BLOCK, EXPERT_BLOCK), EXPERT_BLOCK)

    def in_copies(g):
        slot = g % EXPERT_RING
        gc = jnp.minimum(g, nblk_total - 1)
        return (pltpu.make_async_copy(xs_ref.at[rows(gc), :], xbuf.at[slot], sem_in.at[slot]),
                pltpu.make_async_copy(inv_ref.at[pl.ds(gc, 1), :], inv_s.at[slot], sem_inv.at[slot]))

    def out_wait(slot):
        pltpu.make_async_copy(ybuf.at[slot], g_ref.at[pl.ds(0, EXPERT_BLOCK), :], sem_out.at[slot]).wait()

    def w_copies(e):
        slot = e % WEIGHT_RING
        return (pltpu.make_async_copy(wgu_ref.at[e], wgu_f.at[slot], sem_w.at[0, slot]),
                pltpu.make_async_copy(wdn_ref.at[e], wdn_f.at[slot], sem_w.at[1, slot]))

    def issue_rows(slot):
        for r in range(EXPERT_BLOCK):
            d = inv_s[slot, 0, r]
            _row_copy(ybuf.at[slot, pl.ds(r, 1), :], g_ref.at[pl.ds(d, 1), :], sem_out.at[slot]).start(priority=r % 2)

    for e0 in range(WEIGHT_RING - 1):
        for cp in w_copies(e0):
            cp.start()
    for g in range(EXPERT_AHEAD):
        for cp in in_copies(g):
            cp.start()
    ybuf[EXPERT_RING - 1] = jnp.zeros((EXPERT_BLOCK, half), jnp.uint32)
    for r in range(EXPERT_BLOCK):
        inv_s[EXPERT_RING - 1, 0, r] = spare_row0 + r

    def expert(e, carry):
        for cp in w_copies(e):
            cp.wait()

        @pl.when(e + WEIGHT_RING - 1 <= last_e)
        def _():
            for cp in w_copies(e + WEIGHT_RING - 1):
                cp.start()

        slot = e % WEIGHT_RING
        wgu_s[...] = wgu_f[slot].astype(BF16)
        wdn_s[...] = wdn_f[slot].astype(BF16)
        first = start_ref[e]

        def block(i, c):
            g = first + i
            for cp in in_copies(g):
                cp.wait()
            for cp in in_copies(g + EXPERT_AHEAD):
                cp.start()
            s = g % EXPERT_RING

            @pl.when(g >= EXPERT_RING - 1)
            def _():
                out_wait(s)

            issue_rows((g + EXPERT_RING - 1) % EXPERT_RING)
            lo, hi = _unpack_halves(xbuf[s])
            gu = _dot(lo.astype(BF16), wgu_s[0:half, :]) + _dot(hi.astype(BF16), wgu_s[half:, :])
            act = _silu(gu[:, :EXPERT_FF]) * gu[:, EXPERT_FF:]
            ybuf[s] = _pack_halves(_dot(act.astype(BF16), wdn_s[...]))
            return c

        lax.fori_loop(0, count_ref[e], block, 0)
        return carry

    lax.fori_loop(0, N_EXPERTS, expert, 0)
    issue_rows((used + EXPERT_RING - 1) % EXPERT_RING)

    for j in range(EXPERT_AHEAD):
        for cp in in_copies(used + j):
            cp.wait()
    for j in range(EXPERT_RING):
        @pl.when(used > j)
        def _():
            out_wait((used - 1 - j) % EXPERT_RING)

    @pl.when(used < EXPERT_RING)
    def _():
        out_wait(EXPERT_RING - 1)


def _experts(blk_start, blk_count, xs, inv2, w_gu, w_dn, n_slots):
    rows, half = xs.shape
    nblk = rows // EXPERT_BLOCK
    grid_spec = pltpu.PrefetchScalarGridSpec(
        num_scalar_prefetch=2,
        grid=(1,),
        in_specs=[pl.BlockSpec(memory_space=pl.ANY), pl.BlockSpec(memory_space=pl.ANY),
                  pl.BlockSpec(memory_space=pl.ANY), pl.BlockSpec(memory_space=pl.ANY)],
        out_specs=pl.BlockSpec(memory_space=pl.ANY),
        scratch_shapes=[pltpu.VMEM((WEIGHT_RING, D_MODEL, 2 * EXPERT_FF), F32),
                        pltpu.VMEM((WEIGHT_RING, EXPERT_FF, D_MODEL), F32),
                        pltpu.VMEM((D_MODEL, 2 * EXPERT_FF), BF16), pltpu.VMEM((EXPERT_FF, D_MODEL), BF16),
                        pltpu.VMEM((EXPERT_RING, EXPERT_BLOCK, half), jnp.uint32),
                        pltpu.VMEM((EXPERT_RING, EXPERT_BLOCK, half), jnp.uint32),
                        pltpu.SMEM((EXPERT_RING, 1, EXPERT_BLOCK), jnp.int32),
                        pltpu.SemaphoreType.DMA((2, WEIGHT_RING)), pltpu.SemaphoreType.DMA((EXPERT_RING,)),
                        pltpu.SemaphoreType.DMA((EXPERT_RING,)), pltpu.SemaphoreType.DMA((EXPERT_RING,))],
    )
    return pl.pallas_call(
        functools.partial(_experts_kernel, nblk_total=nblk, spare_row0=n_slots - EXPERT_BLOCK),
        out_shape=jax.ShapeDtypeStruct((n_slots, half), jnp.uint32),
        grid_spec=grid_spec,
        compiler_params=_cparams(("arbitrary",)),
        name="experts",
    )(blk_start, blk_count, xs, inv2, w_gu, w_dn)


def _combine_kernel(x1_ref, h_ref, w_ref, g2_ref, wsgu_ref, wsdn_ref, buf, o_ref):
    su = _dot(h_ref[...].astype(BF16), wsgu_ref[...])
    y = _dot((_silu(su[:, :SHARED_FF]) * su[:, SHARED_FF:]).astype(BF16), wsdn_ref[...])
    w = w_ref[...]
    half = D_MODEL // 2
    ylo, yhi = y[:, :half], y[:, half:]
    for k in range(TOP_K):
        lo, hi = _unpack_halves(buf[k])
        wk = w[:, k:k + 1]
        ylo = ylo + wk * lo
        yhi = yhi + wk * hi
    g2 = g2_ref[0]
    o_ref[:, :half] = x1_ref[:, :half] + g2[:, :half] * ylo
    o_ref[:, half:] = x1_ref[:, half:] + g2[:, half:] * yhi


def _combine(x1, h2, wts, mod3, wsgu, wsdn, slots3, seq):
    t = x1.shape[0]
    tm = min(256, seq)
    tpb = seq // tm
    const = lambda i: (0, 0)
    row = lambda n: pl.BlockSpec((tm, n), lambda i: (i, 0))
    return pl.pallas_call(
        _combine_kernel,
        out_shape=jax.ShapeDtypeStruct((t, D_MODEL), F32),
        grid=(t // tm,),
        in_specs=[row(D_MODEL), row(D_MODEL), row(LANES),
                  pl.BlockSpec((1, 1, D_MODEL), lambda i: (i // tpb, 0, 5)),
                  pl.BlockSpec(wsgu.shape, const), pl.BlockSpec(wsdn.shape, const),
                  pl.BlockSpec((TOP_K, tm, D_MODEL // 2), lambda i: (0, i, 0))],
        out_specs=row(D_MODEL),
        compiler_params=_cparams(("arbitrary",)),
        name="combine",
    )(x1, h2, wts, mod3, wsgu, wsdn, slots3)


def _layer(x, c, layer, w_ada, b_ada, norm1_g, w_in, conv_w, a_log, dt_bias, gdn_norm_g, w_o_gdn,
           q_norm_g, k_norm_g, lambda_q1, lambda_k1, lambda_q2, lambda_k2, subln_g, w_o_diff,
           w_out, norm2_g, w_router, router_bias, w_exp_gate_up, w_exp_down,
           w_shared_gate_up, w_shared_down):
    nb, seq, d = x.shape
    t = nb * seq
    lam_init = 0.8 - 0.6 * math.exp(-0.3 * layer)
    x2 = x.reshape(t, d)

    c_pad = jnp.pad(c, ((0, SUBLANES - nb % SUBLANES if nb % SUBLANES else 0), (0, 0)))
    mod = _ada(c_pad, w_ada, b_ada.reshape(1, -1))[:nb]
    mod3 = mod.reshape(nb, 1, 6 * d)

    o_bd = 2 * GDN_QK + 2 * GDN_V
    o_att = o_bd + 2 * GDN_HEADS
    o_gate = o_att + 2 * DIFF_QK + DIFF_V
    w_gdn = w_in[:, :o_bd].astype(BF16)
    w_bd = jnp.pad(w_in[:, o_bd:o_att], ((0, 0), (0, LANES - 2 * GDN_HEADS))).astype(BF16)
    w_att = w_in[:, o_att:o_gate].astype(BF16)
    w_gate = w_in[:, o_gate:].astype(BF16)
    gdn_in, bd, att_in, gates = _inproj(x2, mod3, norm1_g.reshape(1, d), w_gdn, w_bd, w_att, w_gate, seq)

    aparams = jnp.zeros((SUBLANES, LANES), F32)
    aparams = aparams.at[0, GDN_HEADS:2 * GDN_HEADS].set(a_log).at[1, GDN_HEADS:2 * GDN_HEADS].set(dt_bias)
    oa = _gdn(gdn_in.reshape(nb, seq, -1), bd.reshape(nb, seq, LANES), conv_w, aparams,
              gdn_norm_g.reshape(1, GDN_DV)).reshape(t, GDN_V)

    tq = min(512, seq)
    slopes = [2.0 ** (-8.0 * (h + 1) / DIFF_HEADS) for h in range(DIFF_HEADS)]
    c3 = _bf16_split3(LOG2E)
    qfeat = np.zeros((1, 2 * DIFF_HEADS * DIFF_DH), np.float32)
    for g in range(2 * DIFF_HEADS):
        for j in range(6):
            qfeat[0, g * DIFF_DH + j] = slopes[g // 2] * c3[j // 2]
    grp = np.arange(DIFF_QK) // DIFF_DH
    bdm = jnp.asarray((grp[:, None] == grp[None, :]).astype(np.float32)).astype(BF16)
    qa, ka, vt = _aprep(att_in, jnp.tile(q_norm_g, 2 * DIFF_HEADS).reshape(1, -1),
                        jnp.tile(k_norm_g, 2 * DIFF_HEADS).reshape(1, -1), jnp.asarray(qfeat), bdm, nb, seq, tq)
    slopes2 = jnp.asarray([s * (c3[0] + c3[1] + c3[2]) for s in slopes], F32)
    lamv = jnp.zeros((SUBLANES, DIFF_DH), F32)
    lamv = lamv.at[0].set(lambda_q1).at[1].set(lambda_k1).at[2].set(lambda_q2).at[3].set(lambda_k2)
    ob = _attn(slopes2, qa, ka, vt, lamv, subln_g.reshape(-1, 1), nb, seq, tq, lam_init)

    x1, h2, hp = _post(x2, oa, ob, gates, mod3, norm2_g.reshape(1, d), w_o_gdn.astype(BF16),
                       w_o_diff.astype(BF16), w_out.astype(BF16), seq)

    idx_t, wts_t, counts = _router(h2, w_router.T, router_bias.reshape(-1, 1))
    idx = jnp.pad(idx_t.T, ((0, 0), (0, LANES - TOP_K)))
    wts = jnp.pad(wts_t.T, ((0, 0), (0, LANES - TOP_K)))
    cnt = counts[0].astype(jnp.int32)
    padded = (cnt + EXPERT_BLOCK - 1) // EXPERT_BLOCK * EXPERT_BLOCK
    pends = jnp.cumsum(padded)
    pstart = (pends - padded).astype(F32).reshape(1, -1)
    nblk = -(-(t * TOP_K) // EXPERT_BLOCK) + N_EXPERTS
    blk_start = ((pends - padded) // EXPERT_BLOCK).astype(jnp.int32)
    blk_count = (padded // EXPERT_BLOCK).astype(jnp.int32)
    dest = _rank(idx, pstart)
    dest_flat = dest[:, :TOP_K].reshape(-1)
    n_rows = nblk * EXPERT_BLOCK
    assert n_rows % t == 0
    xs = _dispatch(dest_flat, hp, jnp.zeros((n_rows, d // 2), jnp.uint32))
    inv = _invert(dest_flat, n_rows, t)
    assert t >= EXPERT_BLOCK
    slots = _experts(blk_start, blk_count, xs, inv.reshape(nblk, EXPERT_BLOCK), w_exp_gate_up, w_exp_down,
                     TOP_K * t + n_rows + t)
    out = _combine(x1, h2, wts, mod3, w_shared_gate_up.astype(BF16), w_shared_down.astype(BF16),
                   slots.reshape(-1, t, d // 2), seq)
    return out.reshape(nb, seq, d)


def kernel(x, c, w_ada, b_ada, norm1_g, w_in, conv_w, a_log, dt_bias, gdn_norm_g, w_o_gdn, q_norm_g, k_norm_g, lambda_q1, lambda_k1, lambda_q2, lambda_k2, subln_g, w_o_diff, w_out, norm2_g, w_router, router_bias, w_exp_gate_up, w_exp_down, w_shared_gate_up, w_shared_down):
    params = (w_ada, b_ada, norm1_g, w_in, conv_w, a_log, dt_bias, gdn_norm_g, w_o_gdn, q_norm_g,
              k_norm_g, lambda_q1, lambda_k1, lambda_q2, lambda_k2, subln_g, w_o_diff, w_out, norm2_g,
              w_router, router_bias, w_exp_gate_up, w_exp_down, w_shared_gate_up, w_shared_down)
    for layer in range(w_ada.shape[0]):
        x = _layer(x, c, layer, *(p[layer] for p in params))
    return x
```

```python
import functools
import math

import jax
import jax.numpy as jnp
import numpy as np
from jax import lax
from jax.experimental import pallas as pl
from jax.experimental.pallas import tpu as pltpu
from jax.experimental.pallas import tpu_sc as plsc

F32 = jnp.float32
BF16 = jnp.bfloat16
HIGHEST = lax.Precision.HIGHEST

D_MODEL = 1024
CHUNK = 64
GDN_HEADS = 4
GDN_DK = 128
GDN_DV = 128
GDN_CONV = 4
DIFF_HEADS = 4
DIFF_DH = 64
N_EXPERTS = 256
TOP_K = 8
N_GROUPS = 8
TOPK_GROUPS = 4
EXPERT_FF = 256
SHARED_FF = 256
ROUTED_SCALE = 2.5
NORM_EPS = 1e-6
GROUP_SIZE = N_EXPERTS // N_GROUPS

GDN_QK = GDN_HEADS * GDN_DK
GDN_V = GDN_HEADS * GDN_DV
CONV_CH = 2 * GDN_QK + GDN_V
DIFF_QK = DIFF_HEADS * 2 * DIFF_DH
DIFF_V = DIFF_HEADS * 2 * DIFF_DH

LANES = 128
SUBLANES = 8
EXPERT_BLOCK = 128
INV_BASE = 8
ATTN_HEADS_PER_STEP = 2
LOG2E = math.log2(math.e)
VMEM_LIMIT = 56 * 1024 * 1024


def _cparams(sem):
    return pltpu.CompilerParams(dimension_semantics=sem, vmem_limit_bytes=VMEM_LIMIT)


def _dot(a, b):
    return jnp.dot(a, b, preferred_element_type=F32)


def _dot_nt(a, b):
    return lax.dot_general(a, b, (((1,), (1,)), ((), ())), preferred_element_type=F32)


def _dot_tn(a, b):
    return lax.dot_general(a, b, (((0,), (0,)), ((), ())), preferred_element_type=F32)


def _silu(x):
    return x * jax.nn.sigmoid(x)


def _bf16_split3(x):
    rnd = lambda v: float(np.float32(v).astype(BF16).astype(np.float32))
    a = rnd(x)
    b = rnd(x - a)
    c = rnd(x - a - b)
    return a, b, c


def _ada_kernel(c_ref, w_ref, b_ref, o_ref):
    s = _silu(c_ref[...])
    o_ref[...] = jnp.dot(s, w_ref[...], precision=HIGHEST, preferred_element_type=F32) + b_ref[...]


def _ada(c_pad, w_ada, b_ada):
    n = w_ada.shape[1]
    tn = 1024
    return pl.pallas_call(
        _ada_kernel,
        out_shape=jax.ShapeDtypeStruct((c_pad.shape[0], n), F32),
        grid=(n // tn,),
        in_specs=[pl.BlockSpec(c_pad.shape, lambda j: (0, 0)),
                  pl.BlockSpec((D_MODEL, tn), lambda j: (0, j)),
                  pl.BlockSpec((1, tn), lambda j: (0, j))],
        out_specs=pl.BlockSpec((c_pad.shape[0], tn), lambda j: (0, j)),
        compiler_params=_cparams(("arbitrary",)),
        name="ada",
    )(c_pad, w_ada, b_ada)


def _inproj_kernel(x_ref, shift_ref, scale_ref, g_ref, wg_ref, wbd_ref, wa_ref, wgt_ref,
                   gdn_ref, bd_ref, att_ref, gate_ref):
    x = x_ref[...]
    y = x * lax.rsqrt(jnp.mean(x * x, axis=-1, keepdims=True) + NORM_EPS) * g_ref[...]
    h = (y * (1.0 + scale_ref[0]) + shift_ref[0]).astype(BF16)
    gdn_ref[...] = _dot(h, wg_ref[...]).astype(gdn_ref.dtype)
    bd_ref[...] = _dot(h, wbd_ref[...])
    att_ref[...] = _dot(h, wa_ref[...]).astype(att_ref.dtype)
    gate_ref[...] = _dot(h, wgt_ref[...]).astype(gate_ref.dtype)


def _inproj(x2, mod3, norm1_g, w_gdn, w_bd, w_att, w_gate, seq):
    t = x2.shape[0]
    tm = min(256, seq)
    tiles_per_b = seq // tm
    const = lambda i: (0, 0)
    wspec = lambda w: pl.BlockSpec(w.shape, const, pipeline_mode=pl.Buffered(1))
    row = lambda n: pl.BlockSpec((tm, n), lambda i: (i, 0))
    return pl.pallas_call(
        _inproj_kernel,
        out_shape=(jax.ShapeDtypeStruct((t, w_gdn.shape[1]), BF16),
                   jax.ShapeDtypeStruct((t, LANES), F32),
                   jax.ShapeDtypeStruct((t, w_att.shape[1]), BF16),
                   jax.ShapeDtypeStruct((t, w_gate.shape[1]), BF16)),
        grid=(t // tm,),
        in_specs=[row(D_MODEL),
                  pl.BlockSpec((1, 1, D_MODEL), lambda i: (i // tiles_per_b, 0, 0)),
                  pl.BlockSpec((1, 1, D_MODEL), lambda i: (i // tiles_per_b, 0, 1)),
                  pl.BlockSpec((1, D_MODEL), const),
                  wspec(w_gdn), wspec(w_bd), wspec(w_att), wspec(w_gate)],
        out_specs=(row(w_gdn.shape[1]), row(LANES), row(w_att.shape[1]), row(w_gate.shape[1])),
        compiler_params=_cparams(("arbitrary",)),
        name="inproj",
    )(x2, mod3, mod3, norm1_g, w_gdn, w_bd, w_att, w_gate)


def _gdn_kernel(x_ref, bd_ref, cw_ref, ap_ref, ng_ref, o_ref, cbuf, state, *, nb, lg):
    step = pl.program_id(0)
    nc = lg // CHUNK

    @pl.when(step == 0)
    def _():
        cbuf[...] = jnp.zeros_like(cbuf)
        state[...] = jnp.zeros_like(state)

    ri = lax.broadcasted_iota(jnp.int32, (CHUNK, CHUNK), 0)
    ci = lax.broadcasted_iota(jnp.int32, (CHUNK, CHUNK), 1)
    causal = ri >= ci
    strict = ri > ci
    eye = (ri == ci).astype(F32)
    rl = lax.broadcasted_iota(jnp.int32, (lg, lg), 0)
    cl = lax.broadcasted_iota(jnp.int32, (lg, lg), 1)
    blocktri = ((rl >= cl) & ((rl // CHUNK) == (cl // CHUNK))).astype(F32)
    cw = cw_ref[...]
    a_row = ap_ref[0:1, :]
    dt_row = ap_ref[1:2, :]
    ng = ng_ref[...]

    units = []
    for b in range(nb):
        cbuf[b, SUBLANES:SUBLANES + lg, :] = x_ref[b, :, 0:CONV_CH].astype(F32)
        acc = cw[GDN_CONV - 1:GDN_CONV, :] * cbuf[b, SUBLANES:SUBLANES + lg, :]
        for j in range(GDN_CONV - 1):
            off = SUBLANES - (GDN_CONV - 1) + j
            acc = acc + cw[j:j + 1, :] * cbuf[b, off:off + lg, :]
        cbuf[b, 0:SUBLANES, :] = cbuf[b, lg:lg + SUBLANES, :]
        qkv = _silu(acc)

        bd = bd_ref[b]
        beta_t = jax.nn.sigmoid(bd)
        g_t = -jnp.exp(a_row) * jax.nn.softplus(bd + dt_row)
        gcum = jnp.dot(blocktri, g_t, precision=HIGHEST, preferred_element_type=F32)
        gcum_t = gcum.T
        egcum = jnp.exp(gcum)

        for h in range(GDN_HEADS):
            qh = qkv[:, h * GDN_DK:(h + 1) * GDN_DK]
            kh = qkv[:, GDN_QK + h * GDN_DK:GDN_QK + (h + 1) * GDN_DK]
            vh = qkv[:, 2 * GDN_QK + h * GDN_DV:2 * GDN_QK + (h + 1) * GDN_DV]
            qh = qh * lax.rsqrt(jnp.sum(qh * qh, axis=-1, keepdims=True) + NORM_EPS) * (GDN_DK ** -0.5)
            kh = kh * lax.rsqrt(jnp.sum(kh * kh, axis=-1, keepdims=True) + NORM_EPS)
            for c in range(nc):
                r0, r1 = c * CHUNK, (c + 1) * CHUNK
                q, k, v = qh[r0:r1], kh[r0:r1], vh[r0:r1]
                gc = gcum[r0:r1, 4 + h:5 + h]
                gr = gcum_t[4 + h:5 + h, r0:r1]
                egc = egcum[r0:r1, 4 + h:5 + h]
                bcol = beta_t[r0:r1, h:h + 1]
                gl = gcum[r1 - 1:r1, 4 + h:5 + h]
                kb = k.astype(BF16)
                units.append(dict(
                    b=b, h=h, c=c, kb=kb, bcol=bcol,
                    qkb=jnp.concatenate([q.astype(BF16), kb], axis=0),
                    decay=jnp.exp(jnp.where(causal, gc - gr, -jnp.inf)),
                    rhs=jnp.concatenate([v * bcol, k * (bcol * egc)], axis=-1).astype(BF16),
                    qg=(q * egc).astype(BF16), egl=jnp.exp(gl),
                    kdec=(k * jnp.exp(gl - gc)).astype(BF16)))

    for u in units:
        u["kq"] = _dot_nt(u["qkb"], u["kb"])
    def same_block(size):
        return (ri // size) == (ci // size)
    for u in units:
        lower = jnp.where(strict, u["bcol"] * u["kq"][CHUNK:] * u["decay"], 0.0)
        u["lower"] = lower
        diag = jnp.where(same_block(INV_BASE), lower, 0.0)
        u["xinv"] = eye - diag
        u["pw"] = diag.astype(BF16)
    for r in range(INV_BASE.bit_length() - 2):
        for u in units:
            u["pw"] = _dot(u["pw"], u["pw"]).astype(BF16)
        for u in units:
            u["xinv"] = u["xinv"] + _dot(u["xinv"].astype(BF16), u["pw"])
    size = INV_BASE
    while size < CHUNK:
        pair_off = same_block(2 * size) & jnp.logical_not(same_block(size))
        for u in units:
            u["xb"] = u["xinv"].astype(BF16)
            u["cx"] = _dot(jnp.where(pair_off, u["lower"], 0.0).astype(BF16), u["xb"]).astype(BF16)
        for u in units:
            u["xinv"] = u["xinv"] - _dot(u["xb"], u["cx"])
        size *= 2
    for u in units:
        u["sol"] = _dot(u["xinv"].astype(BF16), u["rhs"])
        u["a_qk"] = (u["kq"][:CHUNK] * u["decay"]).astype(BF16)

    st = {(b, h): state[b, h] for b in range(nb) for h in range(GDN_HEADS)}
    for c in range(nc):
        cu = [u for u in units if u["c"] == c]
        for u in cu:
            stb = st[(u["b"], u["h"])].astype(BF16)
            u["ws"] = _dot(jnp.concatenate([u["sol"][:, GDN_DV:].astype(BF16), u["qg"]], axis=0), stb)
        for u in cu:
            u["vnb"] = (u["sol"][:, :GDN_DV] - u["ws"][:CHUNK]).astype(BF16)
        for u in cu:
            u["out"] = u["ws"][CHUNK:] + _dot(u["a_qk"], u["vnb"])
        for u in cu:
            key = (u["b"], u["h"])
            st[key] = st[key] * u["egl"] + _dot_tn(u["kdec"], u["vnb"])
        for u in cu:
            b, h = u["b"], u["h"]
            r0, r1 = c * CHUNK, (c + 1) * CHUNK
            out = u["out"]
            on = out * lax.rsqrt(jnp.mean(out * out, axis=-1, keepdims=True) + NORM_EPS) * ng
            zh = x_ref[b, r0:r1, CONV_CH + h * GDN_DV:CONV_CH + (h + 1) * GDN_DV].astype(F32)
            o_ref[b, r0:r1, h * GDN_DV:(h + 1) * GDN_DV] = on * _silu(zh)
    for (b, h), s in st.items():
        state[b, h] = s


def _gdn(gdn_in3, bd3, conv_w, aparams, gdn_norm_g):
    nb, seq, _ = gdn_in3.shape
    lg = min(128, seq)
    const = lambda i: (0, 0)
    return pl.pallas_call(
        functools.partial(_gdn_kernel, nb=nb, lg=lg),
        out_shape=jax.ShapeDtypeStruct((nb, seq, GDN_V), F32),
        grid=(seq // lg,),
        in_specs=[pl.BlockSpec((nb, lg, gdn_in3.shape[2]), lambda i: (0, i, 0)),
                  pl.BlockSpec((nb, lg, LANES), lambda i: (0, i, 0)),
                  pl.BlockSpec(conv_w.shape, const),
                  pl.BlockSpec(aparams.shape, const),
                  pl.BlockSpec(gdn_norm_g.shape, const)],
        out_specs=pl.BlockSpec((nb, lg, GDN_V), lambda i: (0, i, 0)),
        scratch_shapes=[pltpu.VMEM((nb, lg + SUBLANES, CONV_CH), F32),
                        pltpu.VMEM((nb, GDN_HEADS, GDN_DK, GDN_DV), F32)],
        compiler_params=_cparams(("arbitrary",)),
        name="gdn",
    )(gdn_in3, bd3, conv_w, aparams, gdn_norm_g)


def _aprep_kernel(x_ref, gq_ref, gk_ref, qf_ref, bdm_ref, q_ref, k_ref, vt_ref, *, tk):
    x = x_ref[...].astype(F32)
    tm = x.shape[0]
    bdm = bdm_ref[...]

    def qknorm(v, gain):
        sq = v * v
        hi = sq.astype(BF16)
        lo = (sq - hi.astype(F32)).astype(BF16)
        ms = (_dot(hi, bdm) + _dot(lo, bdm)) * (1.0 / DIFF_DH)
        return v * lax.rsqrt(ms + NORM_EPS) * gain

    qn = qknorm(x[:, 0:DIFF_QK], gq_ref[...]) * (DIFF_DH ** -0.5 * LOG2E)
    kn = qknorm(x[:, DIFF_QK:2 * DIFF_QK], gk_ref[...])
    pos = pl.program_id(0) * tm + lax.broadcasted_iota(jnp.int32, (tm, DIFF_DH), 0)
    krel = pos % tk
    lane = lax.broadcasted_iota(jnp.int32, (tm, DIFF_DH), 1)
    hi = ((krel // 256) * 256).astype(F32)
    lo = (krel % 256).astype(F32)
    kfeat = jnp.where(lane < 6, jnp.where(lane % 2 == 0, hi, lo), 0.0)
    qfeat = qf_ref[...]
    qparts, kparts = [], []
    for g in range(2 * DIFF_HEADS):
        qparts += [qn[:, g * DIFF_DH:(g + 1) * DIFF_DH],
                   jnp.broadcast_to(qfeat[:, g * DIFF_DH:(g + 1) * DIFF_DH], (tm, DIFF_DH))]
        kparts += [kn[:, g * DIFF_DH:(g + 1) * DIFF_DH], kfeat]
    q_ref[...] = jnp.concatenate(qparts, axis=-1).astype(BF16)
    k_ref[...] = jnp.concatenate(kparts, axis=-1).astype(BF16)
    vt_ref[0, 0] = x[:, 2 * DIFF_QK:].T.astype(BF16)


def _aprep(att_in, gq, gk, qfeat, bdm, nb, seq, tk):
    t = att_in.shape[0]
    tm = tk
    nk = seq // tk
    const = lambda i: (0, 0)
    return pl.pallas_call(
        functools.partial(_aprep_kernel, tk=tk),
        out_shape=(jax.ShapeDtypeStruct((t, 2 * DIFF_QK), BF16),
                   jax.ShapeDtypeStruct((t, 2 * DIFF_QK), BF16),
                   jax.ShapeDtypeStruct((nb, nk, DIFF_V, tk), BF16)),
        grid=(t // tm,),
        in_specs=[pl.BlockSpec((tm, att_in.shape[1]), lambda i: (i, 0)),
                  pl.BlockSpec(gq.shape, const), pl.BlockSpec(gk.shape, const),
                  pl.BlockSpec(qfeat.shape, const), pl.BlockSpec(bdm.shape, const)],
        out_specs=(pl.BlockSpec((tm, 2 * DIFF_QK), lambda i: (i, 0)),
                   pl.BlockSpec((tm, 2 * DIFF_QK), lambda i: (i, 0)),
                   pl.BlockSpec((1, 1, DIFF_V, tk), lambda i: (i // nk, i % nk, 0, 0))),
        compiler_params=_cparams(("arbitrary",)),
        name="aprep",
    )(att_in, gq, gk, qfeat, bdm)


def _attn_kernel(slope_ref, q_ref, k_ref, vt_ref, lamv_ref, sg_ref, o_ref, m_s, l_s, acc_s, sa, sb,
                 *, tq, lam_init):
    hp = pl.program_id(1)
    qi = pl.program_id(2)
    dv = 2 * DIFF_DH
    nstream = 2 * ATTN_HEADS_PER_STEP
    slope2 = [slope_ref[hp * ATTN_HEADS_PER_STEP + c // 2] for c in range(nstream)]
    qs = [q_ref[:, c * LANES:(c + 1) * LANES] for c in range(nstream)]

    m_s[...] = jnp.full_like(m_s, -jnp.inf)
    l_s[...] = jnp.zeros_like(l_s)
    acc_s[...] = jnp.zeros_like(acc_s)

    def scores(kj):
        k0 = pl.multiple_of(kj * tq, tq)
        return tuple(_dot_nt(k_ref[pl.ds(k0, tq), c * LANES:(c + 1) * LANES], qs[c]) for c in range(nstream))

    def accumulate(ss, kj):
        rel = ((kj - qi) * tq).astype(F32)
        for c in range(nstream):
            hl = c // 2
            vt = vt_ref[0, kj, hl * dv:(hl + 1) * dv, :]
            off = slope2[c] * rel
            s = ss[c]
            m_old = m_s[c]
            m_new = jnp.maximum(m_old, jnp.max(s, axis=0, keepdims=True) + off)
            alpha = jnp.exp2(m_old - m_new)
            p = jnp.exp2(s - (m_new - off))
            l_s[c] = alpha * l_s[c] + jnp.sum(p, axis=0, keepdims=True)
            acc_s[c] = alpha * acc_s[c] + _dot(vt, p.astype(BF16))
            m_s[c] = m_new

    kpos = lax.broadcasted_iota(jnp.int32, (tq, tq), 0)
    qpos = lax.broadcasted_iota(jnp.int32, (tq, tq), 1)
    allowed = (kpos // CHUNK) <= (qpos // CHUNK)
    fut = jnp.maximum(kpos - qpos, 0).astype(F32)
    adds = [jnp.where(allowed, (-2.0 * slope2[2 * hl]) * fut, -jnp.inf) for hl in range(ATTN_HEADS_PER_STEP)]
    def put(slot, ss):
        for c in range(nstream):
            slot[c] = ss[c]

    def get(slot):
        return tuple(slot[c] for c in range(nstream))

    diag = tuple(s + adds[c // 2] for c, s in enumerate(scores(qi)))
    npairs = qi // 2
    put(sa, scores(0))
    accumulate(diag, qi)

    def pair(j, carry):
        a = 2 * j
        put(sb, scores(a + 1))
        accumulate(get(sa), a)
        put(sa, scores(jnp.minimum(a + 2, qi - 1)))
        accumulate(get(sb), a + 1)
        return carry
    lax.fori_loop(0, npairs, pair, 0)

    @pl.when(qi % 2 == 1)
    def _():
        accumulate(get(sa), qi - 1)

    lq1, lk1, lq2, lk2 = (lamv_ref[i:i + 1, :] for i in range(4))
    lam = (jnp.exp(jnp.sum(lq1 * lk1, axis=-1, keepdims=True))
           - jnp.exp(jnp.sum(lq2 * lk2, axis=-1, keepdims=True)) + lam_init)
    for hl in range(ATTN_HEADS_PER_STEP):
        o = acc_s[2 * hl] / l_s[2 * hl] - lam * (acc_s[2 * hl + 1] / l_s[2 * hl + 1])
        on = o * lax.rsqrt(jnp.mean(o * o, axis=0, keepdims=True) + NORM_EPS) * sg_ref[...] * (1.0 - lam_init)
        o_ref[:, hl * dv:(hl + 1) * dv] = on.T


def _attn(slopes2, qa, ka, vt, lamv, sg_col, nb, seq, tq, lam_init):
    t = qa.shape[0]
    nq = seq // tq
    dv = 2 * DIFF_DH
    hps = ATTN_HEADS_PER_STEP
    nstream = 2 * hps
    grid_spec = pltpu.PrefetchScalarGridSpec(
        num_scalar_prefetch=1,
        grid=(nb, DIFF_HEADS // hps, nq),
        in_specs=[pl.BlockSpec((tq, nstream * LANES), lambda b, h, i, s: (b * nq + i, h)),
                  pl.BlockSpec((seq, nstream * LANES), lambda b, h, i, s: (b, h)),
                  pl.BlockSpec((1, nq, hps * dv, tq), lambda b, h, i, s: (b, 0, h, 0)),
                  pl.BlockSpec(lamv.shape, lambda b, h, i, s: (0, 0)),
                  pl.BlockSpec(sg_col.shape, lambda b, h, i, s: (0, 0))],
        out_specs=pl.BlockSpec((tq, hps * dv), lambda b, h, i, s: (b * nq + i, h)),
        scratch_shapes=[pltpu.VMEM((nstream, 1, tq), F32), pltpu.VMEM((nstream, 1, tq), F32),
                        pltpu.VMEM((nstream, dv, tq), F32),
                        pltpu.VMEM((nstream, tq, tq), F32), pltpu.VMEM((nstream, tq, tq), F32)],
    )
    return pl.pallas_call(
        functools.partial(_attn_kernel, tq=tq, lam_init=lam_init),
        out_shape=jax.ShapeDtypeStruct((t, DIFF_V), F32),
        grid_spec=grid_spec,
        compiler_params=_cparams(("arbitrary", "arbitrary", "arbitrary")),
        name="attn",
    )(slopes2, qa, ka, vt, lamv, sg_col)


def _pack_halves(x):
    n = x.shape[1] // 2
    lo = pltpu.bitcast(x[:, :n].astype(BF16).astype(F32), jnp.uint32)
    hi = pltpu.bitcast(x[:, n:].astype(BF16).astype(F32), jnp.uint32)
    return (lo >> 16) | (hi & jnp.uint32(0xFFFF0000))


def _unpack_halves(p):
    lo = pltpu.bitcast(p << 16, F32)
    hi = pltpu.bitcast(p & jnp.uint32(0xFFFF0000), F32)
    return lo, hi


def _post_kernel(x_ref, oa_ref, ob_ref, gate_ref, g1_ref, sh2_ref, sc2_ref, n2_ref,
                 woa_ref, wob_ref, wout_ref, x1_ref, h2_ref, hp_ref):
    ya = _dot(oa_ref[...].astype(BF16), woa_ref[...])
    yb = _dot(ob_ref[...].astype(BF16), wob_ref[...])
    merged = (jax.nn.sigmoid(gate_ref[:, 0:D_MODEL].astype(F32)) * ya
              + jax.nn.sigmoid(gate_ref[:, D_MODEL:].astype(F32)) * yb)
    y = _dot(merged.astype(BF16), wout_ref[...])
    x1 = x_ref[...] + g1_ref[0] * y
    x1_ref[...] = x1
    n = x1 * lax.rsqrt(jnp.mean(x1 * x1, axis=-1, keepdims=True) + NORM_EPS) * n2_ref[...]
    h2 = n * (1.0 + sc2_ref[0]) + sh2_ref[0]
    h2_ref[...] = h2
    hp_ref[...] = _pack_halves(h2)


def _post(x2, oa, ob, gates, mod3, norm2_g, woa, wob, wout, seq):
    t = x2.shape[0]
    tm = min(256, seq)
    tpb = seq // tm
    const = lambda i: (0, 0)
    row = lambda n: pl.BlockSpec((tm, n), lambda i: (i, 0))
    modspec = lambda j: pl.BlockSpec((1, 1, D_MODEL), lambda i: (i // tpb, 0, j))
    wspec = lambda w: pl.BlockSpec(w.shape, const, pipeline_mode=pl.Buffered(1))
    return pl.pallas_call(
        _post_kernel,
        out_shape=(jax.ShapeDtypeStruct((t, D_MODEL), F32), jax.ShapeDtypeStruct((t, D_MODEL), F32),
                   jax.ShapeDtypeStruct((t, D_MODEL // 2), jnp.uint32)),
        grid=(t // tm,),
        in_specs=[row(D_MODEL), row(GDN_V), row(DIFF_V), row(2 * D_MODEL),
                  modspec(2), modspec(3), modspec(4), pl.BlockSpec((1, D_MODEL), const),
                  wspec(woa), wspec(wob), wspec(wout)],
        out_specs=(row(D_MODEL), row(D_MODEL), row(D_MODEL // 2)),
        compiler_params=_cparams(("arbitrary",)),
        name="post",
    )(x2, oa, ob, gates, mod3, mod3, mod3, norm2_g, woa, wob, wout)


def _router_kernel(h_ref, wrt_ref, rb_ref, idx_ref, w_ref, cnt_ref, cnt_s):
    step = pl.program_id(0)

    @pl.when(step == 0)
    def _():
        cnt_s[...] = jnp.zeros_like(cnt_s)

    tm = h_ref.shape[0]
    neg = -jnp.inf
    logits = lax.dot_general(wrt_ref[...], h_ref[...], (((1,), (1,)), ((), ())), precision=HIGHEST,
                             preferred_element_type=F32)
    scores = jax.nn.sigmoid(logits)
    choice = scores + rb_ref[...]
    eid = lax.broadcasted_iota(jnp.int32, (N_EXPERTS, tm), 0).astype(F32)
    c3 = choice.reshape(N_GROUPS, GROUP_SIZE, tm)
    e3 = eid.reshape(N_GROUPS, GROUP_SIZE, tm)
    m1 = jnp.max(c3, axis=1, keepdims=True)
    i1 = jnp.min(jnp.where(c3 == m1, e3, float(N_EXPERTS)), axis=1, keepdims=True)
    m2 = jnp.max(jnp.where(e3 == i1, neg, c3), axis=1, keepdims=True)
    gscore = (m1 + m2).reshape(N_GROUPS, tm)
    gid = lax.broadcasted_iota(jnp.int32, (N_GROUPS, tm), 0).astype(F32)
    gsel = jnp.zeros((N_GROUPS, tm), F32)
    for _ in range(TOPK_GROUPS):
        m = jnp.max(gscore, axis=0, keepdims=True)
        g = jnp.min(jnp.where(gscore == m, gid, float(N_GROUPS)), axis=0, keepdims=True)
        hit = gid == g
        gsel = jnp.where(hit, 1.0, gsel)
        gscore = jnp.where(hit, neg, gscore)
    masked = jnp.where(gsel.reshape(N_GROUPS, 1, tm) > 0.0, c3, neg).reshape(N_EXPERTS, tm)
    idx_rows, w_rows = [], []
    sel = jnp.zeros((N_EXPERTS, tm), F32)
    for k in range(TOP_K):
        m = jnp.max(masked, axis=0, keepdims=True)
        i = jnp.min(jnp.where(masked == m, eid, float(N_EXPERTS)), axis=0, keepdims=True)
        hit = eid == i
        idx_rows.append(i)
        w_rows.append(jnp.sum(jnp.where(hit, scores, 0.0), axis=0, keepdims=True))
        sel = jnp.where(hit, 1.0, sel)
        masked = jnp.where(hit, neg, masked)
    ws = jnp.concatenate(w_rows, axis=0)
    idx_ref[...] = jnp.concatenate(idx_rows, axis=0).astype(jnp.int32)
    w_ref[...] = ws / jnp.sum(ws, axis=0, keepdims=True) * ROUTED_SCALE
    ones = jnp.ones((SUBLANES, tm), BF16)
    cnt_s[...] = cnt_s[...] + _dot_nt(ones, sel.astype(BF16))[0:1, :]
    cnt_ref[...] = cnt_s[...]


def _router(h2, w_router_t, rbias_col):
    t = h2.shape[0]
    tm = min(256, t)
    const = lambda i: (0, 0)
    return pl.pallas_call(
        _router_kernel,
        out_shape=(jax.ShapeDtypeStruct((TOP_K, t), jnp.int32),
                   jax.ShapeDtypeStruct((TOP_K, t), F32),
                   jax.ShapeDtypeStruct((1, N_EXPERTS), F32)),
        grid=(t // tm,),
        in_specs=[pl.BlockSpec((tm, D_MODEL), lambda i: (i, 0)),
                  pl.BlockSpec(w_router_t.shape, const), pl.BlockSpec(rbias_col.shape, const)],
        out_specs=(pl.BlockSpec((TOP_K, tm), lambda i: (0, i)),
                   pl.BlockSpec((TOP_K, tm), lambda i: (0, i)),
                   pl.BlockSpec((1, N_EXPERTS), const)),
        scratch_shapes=[pltpu.VMEM((1, N_EXPERTS), F32)],
        compiler_params=_cparams(("arbitrary",)),
        name="router",
    )(h2, w_router_t, rbias_col)


def _rank_kernel(idx_ref, ps_ref, dest_ref, run_s):
    step = pl.program_id(0)

    @pl.when(step == 0)
    def _():
        run_s[...] = jnp.zeros_like(run_s)

    tm = idx_ref.shape[0]
    idx = idx_ref[...]
    lane = lax.broadcasted_iota(jnp.int32, (tm, N_EXPERTS), 1)
    gl = lax.broadcasted_iota(jnp.int32, (tm, LANES), 1)
    hits = [lane == idx[:, k:k + 1] for k in range(TOP_K)]
    sel = jnp.zeros((tm, N_EXPERTS), F32)
    for hit in hits:
        sel = jnp.where(hit, 1.0, sel)
    ri = lax.broadcasted_iota(jnp.int32, (tm, tm), 0)
    ci = lax.broadcasted_iota(jnp.int32, (tm, tm), 1)
    before = _dot((ri > ci).astype(BF16), sel.astype(BF16))
    base = before + run_s[...] + ps_ref[...]
    dest = jnp.zeros((tm, LANES), F32)
    for k, hit in enumerate(hits):
        dk = jnp.sum(jnp.where(hit, base, 0.0), axis=-1, keepdims=True)
        dest = jnp.where(gl == k, dk, dest)
    dest_ref[...] = dest.astype(jnp.int32)
    run_s[...] = run_s[...] + jnp.sum(sel, axis=0, keepdims=True)


def _rank(idx, pstart):
    t = idx.shape[0]
    tm = min(256, t)
    const = lambda i: (0, 0)
    return pl.pallas_call(
        _rank_kernel,
        out_shape=jax.ShapeDtypeStruct((t, LANES), jnp.int32),
        grid=(t // tm,),
        in_specs=[pl.BlockSpec((tm, LANES), lambda i: (i, 0)), pl.BlockSpec(pstart.shape, const)],
        out_specs=pl.BlockSpec((tm, LANES), lambda i: (i, 0)),
        scratch_shapes=[pltpu.VMEM((1, N_EXPERTS), F32)],
        compiler_params=_cparams(("arbitrary",)),
        name="rank",
    )(idx, pstart)


def _row_copy(src, dst, sem):
    return pltpu.make_async_copy(src, dst, sem)


def _dispatch_kernel(dest_ref, lastblk_ref, h_ref, xs_ref, zbuf, sem, zsem, *, tm):
    @pl.when(pl.program_id(0) == 0)
    def _():
        zbuf[...] = jnp.zeros_like(zbuf)

        def zcopy(e):
            r0 = pl.multiple_of(lastblk_ref[e] * EXPERT_BLOCK, EXPERT_BLOCK)
            return pltpu.make_async_copy(zbuf, xs_ref.at[pl.ds(r0, EXPERT_BLOCK), :], zsem)

        def zstart(e, c):
            zcopy(e).start()
            return c

        def zwait(e, c):
            zcopy(e).wait()
            return c

        lax.fori_loop(0, N_EXPERTS, zstart, 0)
        lax.fori_loop(0, N_EXPERTS, zwait, 0)

    def start_row(r, c):
        for k in range(TOP_K):
            d = dest_ref[r * TOP_K + k]
            _row_copy(h_ref.at[pl.ds(r, 1), :], xs_ref.at[pl.ds(d, 1), :], sem).start(priority=k % 2)
        return c

    lax.fori_loop(0, tm, start_row, 0)
    for k in range(TOP_K):
        _row_copy(h_ref, xs_ref.at[pl.ds(0, tm), :], sem).wait()


def _dispatch(dest_flat, last_blk, h2, n_rows):
    t, half = h2.shape
    tm = min(256, t)
    return pl.pallas_call(
        functools.partial(_dispatch_kernel, tm=tm),
        out_shape=jax.ShapeDtypeStruct((n_rows, half), h2.dtype),
        grid=(t // tm,),
        in_specs=[pl.BlockSpec((tm * TOP_K,), lambda i: (i,), memory_space=pltpu.SMEM),
                  pl.BlockSpec(last_blk.shape, lambda i: (0,), memory_space=pltpu.SMEM),
                  pl.BlockSpec((tm, half), lambda i: (i, 0))],
        out_specs=pl.BlockSpec(memory_space=pl.ANY),
        scratch_shapes=[pltpu.VMEM((EXPERT_BLOCK, half), h2.dtype), pltpu.SemaphoreType.DMA(()),
                        pltpu.SemaphoreType.DMA(())],
        compiler_params=_cparams(("arbitrary",)),
        name="dispatch",
    )(dest_flat, last_blk, h2)


SC_CORES = 2
SC_SUBCORES = 16
SC_LANES = 16
INVERT_CHUNK = 4096


def _invert(dest_flat, n_rows, n_tokens):
    n_assign = dest_flat.shape[0]
    workers = SC_CORES * SC_SUBCORES
    rpw = n_rows // workers
    assert rpw * workers == n_rows and rpw % SC_LANES == 0 and n_assign % INVERT_CHUNK == 0
    scratch_base = TOP_K * n_tokens
    log2_k = TOP_K.bit_length() - 1

    def body(dest_hbm, inv_hbm, loc, chunk_v):
        wid = lax.axis_index("s") * SC_CORES + lax.axis_index("c")
        lo = wid * rpw
        lane = lax.iota(jnp.int32, SC_LANES)

        def init(i, c):
            loc[pl.ds(i * SC_LANES, SC_LANES)] = scratch_base + lo + i * SC_LANES + lane
            return c
        lax.fori_loop(0, rpw // SC_LANES, init, 0)

        def do_chunk(ci, c):
            pltpu.sync_copy(dest_hbm.at[pl.ds(ci * INVERT_CHUNK, INVERT_CHUNK)], chunk_v)

            def inner(j, cc):
                rel = chunk_v[pl.ds(j * SC_LANES, SC_LANES)] - lo
                mine = (rel >= 0) & (rel < rpw)
                a = ci * INVERT_CHUNK + j * SC_LANES + lane
                val = (a & (TOP_K - 1)) * n_tokens + lax.shift_right_logical(a, log2_k)
                plsc.store_scatter(loc, [jnp.where(mine, rel, 0)], val, mask=mine)
                return cc
            lax.fori_loop(0, INVERT_CHUNK // SC_LANES, inner, 0)
            return c
        lax.fori_loop(0, n_assign // INVERT_CHUNK, do_chunk, 0)
        pltpu.sync_copy(loc, inv_hbm.at[pl.ds(lo, rpw)])

    mesh = plsc.VectorSubcoreMesh(core_axis_name="c", subcore_axis_name="s",
                                  num_cores=SC_CORES, num_subcores=SC_SUBCORES)
    return pl.kernel(body, out_type=jax.ShapeDtypeStruct((n_rows,), jnp.int32), mesh=mesh,
                     scratch_types=[pltpu.VMEM((rpw,), jnp.int32), pltpu.VMEM((INVERT_CHUNK,), jnp.int32)],
                     compiler_params=pltpu.CompilerParams(needs_layout_passes=False),
                     name="invert")(dest_flat)


EXPERT_RING = 5
EXPERT_AHEAD = EXPERT_RING - 2
WEIGHT_RING = 3


def _experts_kernel(start_ref, count_ref, xs_ref, inv_ref, wgu_ref, wdn_ref, g_ref,
                    wgu_f, wdn_f, wgu_s, wdn_s, xbuf, ybuf, inv_s, sem_w, sem_in, sem_inv, sem_out,
                    *, spare_row0):
    half = D_MODEL // 2
    last_e = N_EXPERTS - 1
    used = start_ref[last_e] + count_ref[last_e]

    def rows(g):
        return pl.ds(pl.multiple_of(g * EXPERT_BLOCK, EXPERT_BLOCK), EXPERT_BLOCK)

    def in_copies(g):
        slot = g % EXPERT_RING
        gc = jnp.minimum(g, used - 1)
        return (pltpu.make_async_copy(xs_ref.at[rows(gc), :], xbuf.at[slot], sem_in.at[slot]),
                pltpu.make_async_copy(inv_ref.at[pl.ds(gc, 1), :], inv_s.at[slot], sem_inv.at[slot]))

    def out_wait(slot):
        pltpu.make_async_copy(ybuf.at[slot], g_ref.at[pl.ds(0, EXPERT_BLOCK), :], sem_out.at[slot]).wait()

    def w_copies(e):
        slot = e % WEIGHT_RING
        return (pltpu.make_async_copy(wgu_ref.at[e], wgu_f.at[slot], sem_w.at[0, slot]),
                pltpu.make_async_copy(wdn_ref.at[e], wdn_f.at[slot], sem_w.at[1, slot]))

    def issue_rows(slot):
        for r in range(EXPERT_BLOCK):
            d = inv_s[slot, 0, r]
            _row_copy(ybuf.at[slot, pl.ds(r, 1), :], g_ref.at[pl.ds(d, 1), :], sem_out.at[slot]).start(priority=r % 2)

    for e0 in range(WEIGHT_RING - 1):
        for cp in w_copies(e0):
            cp.start()
    for g in range(EXPERT_AHEAD):
        for cp in in_copies(g):
            cp.start()
    ybuf[EXPERT_RING - 1] = jnp.zeros((EXPERT_BLOCK, half), jnp.uint32)
    for r in range(EXPERT_BLOCK):
        inv_s[EXPERT_RING - 1, 0, r] = spare_row0 + r

    def expert(e, carry):
        for cp in w_copies(e):
            cp.wait()

        @pl.when(e + WEIGHT_RING - 1 <= last_e)
        def _():
            for cp in w_copies(e + WEIGHT_RING - 1):
                cp.start()

        slot = e % WEIGHT_RING
        wgu_s[...] = wgu_f[slot].astype(BF16)
        wdn_s[...] = wdn_f[slot].astype(BF16)
        first = start_ref[e]

        def block(i, c):
            g = first + i
            for cp in in_copies(g):
                cp.wait()
            for cp in in_copies(g + EXPERT_AHEAD):
                cp.start()
            s = g % EXPERT_RING

            @pl.when(g >= EXPERT_RING - 1)
            def _():
                out_wait(s)

            issue_rows((g + EXPERT_RING - 1) % EXPERT_RING)
            lo, hi = _unpack_halves(xbuf[s])
            gu = _dot(lo.astype(BF16), wgu_s[0:half, :]) + _dot(hi.astype(BF16), wgu_s[half:, :])
            act = _silu(gu[:, :EXPERT_FF]) * gu[:, EXPERT_FF:]
            ybuf[s] = _pack_halves(_dot(act.astype(BF16), wdn_s[...]))
            return c

        lax.fori_loop(0, count_ref[e], block, 0)
        return carry

    lax.fori_loop(0, N_EXPERTS, expert, 0)
    issue_rows((used + EXPERT_RING - 1) % EXPERT_RING)

    for j in range(EXPERT_AHEAD):
        for cp in in_copies(used + j):
            cp.wait()
    for j in range(EXPERT_RING):
        @pl.when(used > j)
        def _():
            out_wait((used - 1 - j) % EXPERT_RING)

    @pl.when(used < EXPERT_RING)
    def _():
        out_wait(EXPERT_RING - 1)


def _experts(blk_start, blk_count, xs, inv2, w_gu, w_dn, n_slots):
    half = xs.shape[1]
    grid_spec = pltpu.PrefetchScalarGridSpec(
        num_scalar_prefetch=2,
        grid=(1,),
        in_specs=[pl.BlockSpec(memory_space=pl.ANY), pl.BlockSpec(memory_space=pl.ANY),
                  pl.BlockSpec(memory_space=pl.ANY), pl.BlockSpec(memory_space=pl.ANY)],
        out_specs=pl.BlockSpec(memory_space=pl.ANY),
        scratch_shapes=[pltpu.VMEM((WEIGHT_RING, D_MODEL, 2 * EXPERT_FF), F32),
                        pltpu.VMEM((WEIGHT_RING, EXPERT_FF, D_MODEL), F32),
                        pltpu.VMEM((D_MODEL, 2 * EXPERT_FF), BF16), pltpu.VMEM((EXPERT_FF, D_MODEL), BF16),
                        pltpu.VMEM((EXPERT_RING, EXPERT_BLOCK, half), jnp.uint32),
                        pltpu.VMEM((EXPERT_RING, EXPERT_BLOCK, half), jnp.uint32),
                        pltpu.SMEM((EXPERT_RING, 1, EXPERT_BLOCK), jnp.int32),
                        pltpu.SemaphoreType.DMA((2, WEIGHT_RING)), pltpu.SemaphoreType.DMA((EXPERT_RING,)),
                        pltpu.SemaphoreType.DMA((EXPERT_RING,)), pltpu.SemaphoreType.DMA((EXPERT_RING,))],
    )
    return pl.pallas_call(
        functools.partial(_experts_kernel, spare_row0=n_slots - EXPERT_BLOCK),
        out_shape=jax.ShapeDtypeStruct((n_slots, half), jnp.uint32),
        grid_spec=grid_spec,
        compiler_params=_cparams(("arbitrary",)),
        name="experts",
    )(blk_start, blk_count, xs, inv2, w_gu, w_dn)


def _combine_kernel(x1_ref, h_ref, w_ref, g2_ref, wsgu_ref, wsdn_ref, buf, o_ref):
    su = _dot(h_ref[...].astype(BF16), wsgu_ref[...])
    y = _dot((_silu(su[:, :SHARED_FF]) * su[:, SHARED_FF:]).astype(BF16), wsdn_ref[...])
    w = w_ref[...]
    half = D_MODEL // 2
    ylo, yhi = y[:, :half], y[:, half:]
    for k in range(TOP_K):
        lo, hi = _unpack_halves(buf[k])
        wk = w[:, k:k + 1]
        ylo = ylo + wk * lo
        yhi = yhi + wk * hi
    g2 = g2_ref[0]
    o_ref[:, :half] = x1_ref[:, :half] + g2[:, :half] * ylo
    o_ref[:, half:] = x1_ref[:, half:] + g2[:, half:] * yhi


def _combine(x1, h2, wts, mod3, wsgu, wsdn, slots3, seq):
    t = x1.shape[0]
    tm = min(256, seq)
    tpb = seq // tm
    const = lambda i: (0, 0)
    row = lambda n: pl.BlockSpec((tm, n), lambda i: (i, 0))
    return pl.pallas_call(
        _combine_kernel,
        out_shape=jax.ShapeDtypeStruct((t, D_MODEL), F32),
        grid=(t // tm,),
        in_specs=[row(D_MODEL), row(D_MODEL), row(LANES),
                  pl.BlockSpec((1, 1, D_MODEL), lambda i: (i // tpb, 0, 5)),
                  pl.BlockSpec(wsgu.shape, const), pl.BlockSpec(wsdn.shape, const),
                  pl.BlockSpec((TOP_K, tm, D_MODEL // 2), lambda i: (0, i, 0))],
        out_specs=row(D_MODEL),
        compiler_params=_cparams(("arbitrary",)),
        name="combine",
    )(x1, h2, wts, mod3, wsgu, wsdn, slots3)


def _layer(x, c, layer, w_ada, b_ada, norm1_g, w_in, conv_w, a_log, dt_bias, gdn_norm_g, w_o_gdn,
           q_norm_g, k_norm_g, lambda_q1, lambda_k1, lambda_q2, lambda_k2, subln_g, w_o_diff,
           w_out, norm2_g, w_router, router_bias, w_exp_gate_up, w_exp_down,
           w_shared_gate_up, w_shared_down):
    nb, seq, d = x.shape
    t = nb * seq
    lam_init = 0.8 - 0.6 * math.exp(-0.3 * layer)
    x2 = x.reshape(t, d)

    c_pad = jnp.pad(c, ((0, SUBLANES - nb % SUBLANES if nb % SUBLANES else 0), (0, 0)))
    mod = _ada(c_pad, w_ada, b_ada.reshape(1, -1))[:nb]
    mod3 = mod.reshape(nb, 1, 6 * d)

    o_bd = 2 * GDN_QK + 2 * GDN_V
    o_att = o_bd + 2 * GDN_HEADS
    o_gate = o_att + 2 * DIFF_QK + DIFF_V
    w_gdn = w_in[:, :o_bd].astype(BF16)
    w_bd = jnp.pad(w_in[:, o_bd:o_att], ((0, 0), (0, LANES - 2 * GDN_HEADS))).astype(BF16)
    w_att = w_in[:, o_att:o_gate].astype(BF16)
    w_gate = w_in[:, o_gate:].astype(BF16)
    gdn_in, bd, att_in, gates = _inproj(x2, mod3, norm1_g.reshape(1, d), w_gdn, w_bd, w_att, w_gate, seq)

    aparams = jnp.zeros((SUBLANES, LANES), F32)
    aparams = aparams.at[0, GDN_HEADS:2 * GDN_HEADS].set(a_log).at[1, GDN_HEADS:2 * GDN_HEADS].set(dt_bias)
    oa = _gdn(gdn_in.reshape(nb, seq, -1), bd.reshape(nb, seq, LANES), conv_w, aparams,
              gdn_norm_g.reshape(1, GDN_DV)).reshape(t, GDN_V)

    tq = min(512, seq)
    slopes = [2.0 ** (-8.0 * (h + 1) / DIFF_HEADS) for h in range(DIFF_HEADS)]
    c3 = _bf16_split3(LOG2E)
    qfeat = np.zeros((1, 2 * DIFF_HEADS * DIFF_DH), np.float32)
    for g in range(2 * DIFF_HEADS):
        for j in range(6):
            qfeat[0, g * DIFF_DH + j] = slopes[g // 2] * c3[j // 2]
    grp = np.arange(DIFF_QK) // DIFF_DH
    bdm = jnp.asarray((grp[:, None] == grp[None, :]).astype(np.float32)).astype(BF16)
    qa, ka, vt = _aprep(att_in, jnp.tile(q_norm_g, 2 * DIFF_HEADS).reshape(1, -1),
                        jnp.tile(k_norm_g, 2 * DIFF_HEADS).reshape(1, -1), jnp.asarray(qfeat), bdm, nb, seq, tq)
    slopes2 = jnp.asarray([s * (c3[0] + c3[1] + c3[2]) for s in slopes], F32)
    lamv = jnp.zeros((SUBLANES, DIFF_DH), F32)
    lamv = lamv.at[0].set(lambda_q1).at[1].set(lambda_k1).at[2].set(lambda_q2).at[3].set(lambda_k2)
    ob = _attn(slopes2, qa, ka, vt, lamv, subln_g.reshape(-1, 1), nb, seq, tq, lam_init)

    x1, h2, hp = _post(x2, oa, ob, gates, mod3, norm2_g.reshape(1, d), w_o_gdn.astype(BF16),
                       w_o_diff.astype(BF16), w_out.astype(BF16), seq)

    idx_t, wts_t, counts = _router(h2, w_router.T, router_bias.reshape(-1, 1))
    idx = jnp.pad(idx_t.T, ((0, 0), (0, LANES - TOP_K)))
    wts = jnp.pad(wts_t.T, ((0, 0), (0, LANES - TOP_K)))
    cnt = counts[0].astype(jnp.int32)
    padded = (cnt + EXPERT_BLOCK - 1) // EXPERT_BLOCK * EXPERT_BLOCK
    pends = jnp.cumsum(padded)
    pstart = (pends - padded).astype(F32).reshape(1, -1)
    nblk = -(-(t * TOP_K) // EXPERT_BLOCK) + N_EXPERTS
    blk_start = ((pends - padded) // EXPERT_BLOCK).astype(jnp.int32)
    blk_count = (padded // EXPERT_BLOCK).astype(jnp.int32)
    dest = _rank(idx, pstart)
    dest_flat = dest[:, :TOP_K].reshape(-1)
    n_rows = nblk * EXPERT_BLOCK
    assert n_rows % t == 0
    last_blk = jnp.maximum(blk_start + blk_count - 1, 0)
    xs = _dispatch(dest_flat, last_blk, hp, n_rows)
    inv = _invert(dest_flat, n_rows, t)
    assert t >= EXPERT_BLOCK
    slots = _experts(blk_start, blk_count, xs, inv.reshape(nblk, EXPERT_BLOCK), w_exp_gate_up, w_exp_down,
                     TOP_K * t + n_rows + t)
    out = _combine(x1, h2, wts, mod3, w_shared_gate_up.astype(BF16), w_shared_down.astype(BF16),
                   slots.reshape(-1, t, d // 2), seq)
    return out.reshape(nb, seq, d)


def kernel(x, c, w_ada, b_ada, norm1_g, w_in, conv_w, a_log, dt_bias, gdn_norm_g, w_o_gdn, q_norm_g, k_norm_g, lambda_q1, lambda_k1, lambda_q2, lambda_k2, subln_g, w_o_diff, w_out, norm2_g, w_router, router_bias, w_exp_gate_up, w_exp_down, w_shared_gate_up, w_shared_down):
    params = (w_ada, b_ada, norm1_g, w_in, conv_w, a_log, dt_bias, gdn_norm_g, w_o_gdn, q_norm_g,
              k_norm_g, lambda_q1, lambda_k1, lambda_q2, lambda_k2, subln_g, w_o_diff, w_out, norm2_g,
              w_router, router_bias, w_exp_gate_up, w_exp_down, w_shared_gate_up, w_shared_down)
    for layer in range(w_ada.shape[0]):
        x = _layer(x, c, layer, *(p[layer] for p in params))
    return x
```

```python
import functools
import math

import jax
import jax.numpy as jnp
import numpy as np
from jax import lax
from jax.experimental import pallas as pl
from jax.experimental.pallas import tpu as pltpu
from jax.experimental.pallas import tpu_sc as plsc

F32 = jnp.float32
BF16 = jnp.bfloat16
HIGHEST = lax.Precision.HIGHEST

D_MODEL = 1024
CHUNK = 64
GDN_HEADS = 4
GDN_DK = 128
GDN_DV = 128
GDN_CONV = 4
DIFF_HEADS = 4
DIFF_DH = 64
N_EXPERTS = 256
TOP_K = 8
N_GROUPS = 8
TOPK_GROUPS = 4
EXPERT_FF = 256
SHARED_FF = 256
ROUTED_SCALE = 2.5
NORM_EPS = 1e-6
GROUP_SIZE = N_EXPERTS // N_GROUPS

GDN_QK = GDN_HEADS * GDN_DK
GDN_V = GDN_HEADS * GDN_DV
CONV_CH = 2 * GDN_QK + GDN_V
DIFF_QK = DIFF_HEADS * 2 * DIFF_DH
DIFF_V = DIFF_HEADS * 2 * DIFF_DH

LANES = 128
SUBLANES = 8
EXPERT_BLOCK = 128
INV_BASE = 8
ATTN_HEADS_PER_STEP = 2
LOG2E = math.log2(math.e)
VMEM_LIMIT = 56 * 1024 * 1024


def _cparams(sem):
    return pltpu.CompilerParams(dimension_semantics=sem, vmem_limit_bytes=VMEM_LIMIT)


def _dot(a, b):
    return jnp.dot(a, b, preferred_element_type=F32)


def _dot_nt(a, b):
    return lax.dot_general(a, b, (((1,), (1,)), ((), ())), preferred_element_type=F32)


def _dot_tn(a, b):
    return lax.dot_general(a, b, (((0,), (0,)), ((), ())), preferred_element_type=F32)


def _silu(x):
    return x * jax.nn.sigmoid(x)


def _bf16_split3(x):
    rnd = lambda v: float(np.float32(v).astype(BF16).astype(np.float32))
    a = rnd(x)
    b = rnd(x - a)
    c = rnd(x - a - b)
    return a, b, c


def _ada_kernel(c_ref, w_ref, b_ref, o_ref):
    s = _silu(c_ref[...])
    o_ref[...] = jnp.dot(s, w_ref[...], precision=HIGHEST, preferred_element_type=F32) + b_ref[...]


def _ada(c_pad, w_ada, b_ada):
    n = w_ada.shape[1]
    tn = 1024
    return pl.pallas_call(
        _ada_kernel,
        out_shape=jax.ShapeDtypeStruct((c_pad.shape[0], n), F32),
        grid=(n // tn,),
        in_specs=[pl.BlockSpec(c_pad.shape, lambda j: (0, 0)),
                  pl.BlockSpec((D_MODEL, tn), lambda j: (0, j)),
                  pl.BlockSpec((1, tn), lambda j: (0, j))],
        out_specs=pl.BlockSpec((c_pad.shape[0], tn), lambda j: (0, j)),
        compiler_params=_cparams(("arbitrary",)),
        name="ada",
    )(c_pad, w_ada, b_ada)


def _inproj_kernel(x_ref, shift_ref, scale_ref, g_ref, wg_ref, wbd_ref, wa_ref, wgt_ref,
                   gdn_ref, bd_ref, att_ref, gate_ref):
    x = x_ref[...]
    y = x * lax.rsqrt(jnp.mean(x * x, axis=-1, keepdims=True) + NORM_EPS) * g_ref[...]
    h = (y * (1.0 + scale_ref[0]) + shift_ref[0]).astype(BF16)
    gdn_ref[...] = _dot(h, wg_ref[...]).astype(gdn_ref.dtype)
    bd_ref[...] = _dot(h, wbd_ref[...])
    att_ref[...] = _dot(h, wa_ref[...]).astype(att_ref.dtype)
    gate_ref[...] = _dot(h, wgt_ref[...]).astype(gate_ref.dtype)


def _inproj(x2, mod3, norm1_g, w_gdn, w_bd, w_att, w_gate, seq):
    t = x2.shape[0]
    tm = min(512, seq)
    tiles_per_b = seq // tm
    const = lambda i: (0, 0)
    wspec = lambda w: pl.BlockSpec(w.shape, const, pipeline_mode=pl.Buffered(1))
    row = lambda n: pl.BlockSpec((tm, n), lambda i: (i, 0))
    return pl.pallas_call(
        _inproj_kernel,
        out_shape=(jax.ShapeDtypeStruct((t, w_gdn.shape[1]), BF16),
                   jax.ShapeDtypeStruct((t, LANES), F32),
                   jax.ShapeDtypeStruct((t, w_att.shape[1]), BF16),
                   jax.ShapeDtypeStruct((t, w_gate.shape[1]), BF16)),
        grid=(t // tm,),
        in_specs=[row(D_MODEL),
                  pl.BlockSpec((1, 1, D_MODEL), lambda i: (i // tiles_per_b, 0, 0)),
                  pl.BlockSpec((1, 1, D_MODEL), lambda i: (i // tiles_per_b, 0, 1)),
                  pl.BlockSpec((1, D_MODEL), const),
                  wspec(w_gdn), wspec(w_bd), wspec(w_att), wspec(w_gate)],
        out_specs=(row(w_gdn.shape[1]), row(LANES), row(w_att.shape[1]), row(w_gate.shape[1])),
        compiler_params=_cparams(("arbitrary",)),
        name="inproj",
    )(x2, mod3, mod3, norm1_g, w_gdn, w_bd, w_att, w_gate)


def _gdn_kernel(x_ref, bd_ref, cw_ref, ap_ref, ng_ref, o_ref, cbuf, state, *, nb, lg):
    step = pl.program_id(0)
    nc = lg // CHUNK

    @pl.when(step == 0)
    def _():
        cbuf[...] = jnp.zeros_like(cbuf)
        state[...] = jnp.zeros_like(state)

    ri = lax.broadcasted_iota(jnp.int32, (CHUNK, CHUNK), 0)
    ci = lax.broadcasted_iota(jnp.int32, (CHUNK, CHUNK), 1)
    causal = ri >= ci
    strict = ri > ci
    eye = (ri == ci).astype(F32)
    rl = lax.broadcasted_iota(jnp.int32, (lg, lg), 0)
    cl = lax.broadcasted_iota(jnp.int32, (lg, lg), 1)
    blocktri = ((rl >= cl) & ((rl // CHUNK) == (cl // CHUNK))).astype(F32)
    cw = cw_ref[...]
    a_row = ap_ref[0:1, :]
    dt_row = ap_ref[1:2, :]
    ng = ng_ref[...]

    units = []
    for b in range(nb):
        cbuf[b, SUBLANES:SUBLANES + lg, :] = x_ref[b, :, 0:CONV_CH].astype(F32)
        acc = cw[GDN_CONV - 1:GDN_CONV, :] * cbuf[b, SUBLANES:SUBLANES + lg, :]
        for j in range(GDN_CONV - 1):
            off = SUBLANES - (GDN_CONV - 1) + j
            acc = acc + cw[j:j + 1, :] * cbuf[b, off:off + lg, :]
        cbuf[b, 0:SUBLANES, :] = cbuf[b, lg:lg + SUBLANES, :]
        qkv = _silu(acc)

        bd = bd_ref[b]
        beta_t = jax.nn.sigmoid(bd)
        g_t = -jnp.exp(a_row) * jax.nn.softplus(bd + dt_row)
        gcum = jnp.dot(blocktri, g_t, precision=HIGHEST, preferred_element_type=F32)
        gcum_t = gcum.T
        egcum = jnp.exp(gcum)

        for h in range(GDN_HEADS):
            qh = qkv[:, h * GDN_DK:(h + 1) * GDN_DK]
            kh = qkv[:, GDN_QK + h * GDN_DK:GDN_QK + (h + 1) * GDN_DK]
            vh = qkv[:, 2 * GDN_QK + h * GDN_DV:2 * GDN_QK + (h + 1) * GDN_DV]
            qh = qh * lax.rsqrt(jnp.sum(qh * qh, axis=-1, keepdims=True) + NORM_EPS) * (GDN_DK ** -0.5)
            kh = kh * lax.rsqrt(jnp.sum(kh * kh, axis=-1, keepdims=True) + NORM_EPS)
            for c in range(nc):
                r0, r1 = c * CHUNK, (c + 1) * CHUNK
                q, k, v = qh[r0:r1], kh[r0:r1], vh[r0:r1]
                gc = gcum[r0:r1, 4 + h:5 + h]
                gr = gcum_t[4 + h:5 + h, r0:r1]
                egc = egcum[r0:r1, 4 + h:5 + h]
                bcol = beta_t[r0:r1, h:h + 1]
                gl = gcum[r1 - 1:r1, 4 + h:5 + h]
                kb = k.astype(BF16)
                units.append(dict(
                    b=b, h=h, c=c, kb=kb, bcol=bcol,
                    qkb=jnp.concatenate([q.astype(BF16), kb], axis=0),
                    decay=jnp.exp(jnp.where(causal, gc - gr, -jnp.inf)),
                    rhs=jnp.concatenate([v * bcol, k * (bcol * egc)], axis=-1).astype(BF16),
                    qg=(q * egc).astype(BF16), egl=jnp.exp(gl),
                    kdec=(k * jnp.exp(gl - gc)).astype(BF16)))

    for u in units:
        u["kq"] = _dot_nt(u["qkb"], u["kb"])
    def same_block(size):
        return (ri // size) == (ci // size)
    for u in units:
        lower = jnp.where(strict, u["bcol"] * u["kq"][CHUNK:] * u["decay"], 0.0)
        u["lower"] = lower
        diag = jnp.where(same_block(INV_BASE), lower, 0.0)
        u["xinv"] = eye - diag
        u["pw"] = diag.astype(BF16)
    for r in range(INV_BASE.bit_length() - 2):
        for u in units:
            u["pw"] = _dot(u["pw"], u["pw"]).astype(BF16)
        for u in units:
            u["xinv"] = u["xinv"] + _dot(u["xinv"].astype(BF16), u["pw"])
    size = INV_BASE
    while size < CHUNK:
        pair_off = same_block(2 * size) & jnp.logical_not(same_block(size))
        for u in units:
            u["xb"] = u["xinv"].astype(BF16)
            u["cx"] = _dot(jnp.where(pair_off, u["lower"], 0.0).astype(BF16), u["xb"]).astype(BF16)
        for u in units:
            u["xinv"] = u["xinv"] - _dot(u["xb"], u["cx"])
        size *= 2
    for u in units:
        u["sol"] = _dot(u["xinv"].astype(BF16), u["rhs"])
        u["a_qk"] = (u["kq"][:CHUNK] * u["decay"]).astype(BF16)

    st = {(b, h): state[b, h] for b in range(nb) for h in range(GDN_HEADS)}
    for c in range(nc):
        cu = [u for u in units if u["c"] == c]
        for u in cu:
            stb = st[(u["b"], u["h"])].astype(BF16)
            u["ws"] = _dot(jnp.concatenate([u["sol"][:, GDN_DV:].astype(BF16), u["qg"]], axis=0), stb)
        for u in cu:
            u["vnb"] = (u["sol"][:, :GDN_DV] - u["ws"][:CHUNK]).astype(BF16)
        for u in cu:
            u["out"] = u["ws"][CHUNK:] + _dot(u["a_qk"], u["vnb"])
        for u in cu:
            key = (u["b"], u["h"])
            st[key] = st[key] * u["egl"] + _dot_tn(u["kdec"], u["vnb"])
        for u in cu:
            b, h = u["b"], u["h"]
            r0, r1 = c * CHUNK, (c + 1) * CHUNK
            out = u["out"]
            on = out * lax.rsqrt(jnp.mean(out * out, axis=-1, keepdims=True) + NORM_EPS) * ng
            zh = x_ref[b, r0:r1, CONV_CH + h * GDN_DV:CONV_CH + (h + 1) * GDN_DV].astype(F32)
            o_ref[b, r0:r1, h * GDN_DV:(h + 1) * GDN_DV] = on * _silu(zh)
    for (b, h), s in st.items():
        state[b, h] = s


def _gdn(gdn_in3, bd3, conv_w, aparams, gdn_norm_g):
    nb, seq, _ = gdn_in3.shape
    lg = min(128, seq)
    const = lambda i: (0, 0)
    return pl.pallas_call(
        functools.partial(_gdn_kernel, nb=nb, lg=lg),
        out_shape=jax.ShapeDtypeStruct((nb, seq, GDN_V), F32),
        grid=(seq // lg,),
        in_specs=[pl.BlockSpec((nb, lg, gdn_in3.shape[2]), lambda i: (0, i, 0)),
                  pl.BlockSpec((nb, lg, LANES), lambda i: (0, i, 0)),
                  pl.BlockSpec(conv_w.shape, const),
                  pl.BlockSpec(aparams.shape, const),
                  pl.BlockSpec(gdn_norm_g.shape, const)],
        out_specs=pl.BlockSpec((nb, lg, GDN_V), lambda i: (0, i, 0)),
        scratch_shapes=[pltpu.VMEM((nb, lg + SUBLANES, CONV_CH), F32),
                        pltpu.VMEM((nb, GDN_HEADS, GDN_DK, GDN_DV), F32)],
        compiler_params=_cparams(("arbitrary",)),
        name="gdn",
    )(gdn_in3, bd3, conv_w, aparams, gdn_norm_g)


def _aprep_kernel(x_ref, gq_ref, gk_ref, qf_ref, bdm_ref, q_ref, k_ref, vt_ref, *, tk):
    x = x_ref[...].astype(F32)
    tm = x.shape[0]
    bdm = bdm_ref[...]

    def qknorm(v, gain):
        sq = v * v
        hi = sq.astype(BF16)
        lo = (sq - hi.astype(F32)).astype(BF16)
        ms = (_dot(hi, bdm) + _dot(lo, bdm)) * (1.0 / DIFF_DH)
        return v * lax.rsqrt(ms + NORM_EPS) * gain

    qn = qknorm(x[:, 0:DIFF_QK], gq_ref[...]) * (DIFF_DH ** -0.5 * LOG2E)
    kn = qknorm(x[:, DIFF_QK:2 * DIFF_QK], gk_ref[...])
    pos = pl.program_id(0) * tm + lax.broadcasted_iota(jnp.int32, (tm, DIFF_DH), 0)
    krel = pos % tk
    lane = lax.broadcasted_iota(jnp.int32, (tm, DIFF_DH), 1)
    hi = ((krel // 256) * 256).astype(F32)
    lo = (krel % 256).astype(F32)
    kfeat = jnp.where(lane < 6, jnp.where(lane % 2 == 0, hi, lo), 0.0)
    qfeat = qf_ref[...]
    qparts, kparts = [], []
    for g in range(2 * DIFF_HEADS):
        qparts += [qn[:, g * DIFF_DH:(g + 1) * DIFF_DH],
                   jnp.broadcast_to(qfeat[:, g * DIFF_DH:(g + 1) * DIFF_DH], (tm, DIFF_DH))]
        kparts += [kn[:, g * DIFF_DH:(g + 1) * DIFF_DH], kfeat]
    q_ref[...] = jnp.concatenate(qparts, axis=-1).astype(BF16)
    k_ref[...] = jnp.concatenate(kparts, axis=-1).astype(BF16)
    vt_ref[0, 0] = x[:, 2 * DIFF_QK:].T.astype(BF16)


def _aprep(att_in, gq, gk, qfeat, bdm, nb, seq, tk):
    t = att_in.shape[0]
    tm = tk
    nk = seq // tk
    const = lambda i: (0, 0)
    return pl.pallas_call(
        functools.partial(_aprep_kernel, tk=tk),
        out_shape=(jax.ShapeDtypeStruct((t, 2 * DIFF_QK), BF16),
                   jax.ShapeDtypeStruct((t, 2 * DIFF_QK), BF16),
                   jax.ShapeDtypeStruct((nb, nk, DIFF_V, tk), BF16)),
        grid=(t // tm,),
        in_specs=[pl.BlockSpec((tm, att_in.shape[1]), lambda i: (i, 0)),
                  pl.BlockSpec(gq.shape, const), pl.BlockSpec(gk.shape, const),
                  pl.BlockSpec(qfeat.shape, const), pl.BlockSpec(bdm.shape, const)],
        out_specs=(pl.BlockSpec((tm, 2 * DIFF_QK), lambda i: (i, 0)),
                   pl.BlockSpec((tm, 2 * DIFF_QK), lambda i: (i, 0)),
                   pl.BlockSpec((1, 1, DIFF_V, tk), lambda i: (i // nk, i % nk, 0, 0))),
        compiler_params=_cparams(("arbitrary",)),
        name="aprep",
    )(att_in, gq, gk, qfeat, bdm)


def _attn_kernel(slope_ref, q_ref, k_ref, vt_ref, lamv_ref, sg_ref, o_ref, m_s, l_s, acc_s, sa, sb,
                 *, tq, lam_init):
    hp = pl.program_id(1)
    qi = pl.program_id(2)
    dv = 2 * DIFF_DH
    nstream = 2 * ATTN_HEADS_PER_STEP
    slope2 = [slope_ref[hp * ATTN_HEADS_PER_STEP + c // 2] for c in range(nstream)]
    qs = [q_ref[:, c * LANES:(c + 1) * LANES] for c in range(nstream)]

    m_s[...] = jnp.full_like(m_s, -jnp.inf)
    l_s[...] = jnp.zeros_like(l_s)
    acc_s[...] = jnp.zeros_like(acc_s)

    def scores(kj):
        k0 = pl.multiple_of(kj * tq, tq)
        return tuple(_dot_nt(k_ref[pl.ds(k0, tq), c * LANES:(c + 1) * LANES], qs[c]) for c in range(nstream))

    def accumulate(ss, kj):
        rel = ((kj - qi) * tq).astype(F32)
        for c in range(nstream):
            hl = c // 2
            vt = vt_ref[0, kj, hl * dv:(hl + 1) * dv, :]
            off = slope2[c] * rel
            s = ss[c]
            m_old = m_s[c]
            m_new = jnp.maximum(m_old, jnp.max(s, axis=0, keepdims=True) + off)
            alpha = jnp.exp2(m_old - m_new)
            p = jnp.exp2(s - (m_new - off))
            l_s[c] = alpha * l_s[c] + jnp.sum(p, axis=0, keepdims=True)
            acc_s[c] = alpha * acc_s[c] + _dot(vt, p.astype(BF16))
            m_s[c] = m_new

    kpos = lax.broadcasted_iota(jnp.int32, (tq, tq), 0)
    qpos = lax.broadcasted_iota(jnp.int32, (tq, tq), 1)
    allowed = (kpos // CHUNK) <= (qpos // CHUNK)
    fut = jnp.maximum(kpos - qpos, 0).astype(F32)
    adds = [jnp.where(allowed, (-2.0 * slope2[2 * hl]) * fut, -jnp.inf) for hl in range(ATTN_HEADS_PER_STEP)]
    def put(slot, ss):
        for c in range(nstream):
            slot[c] = ss[c]

    def get(slot):
        return tuple(slot[c] for c in range(nstream))

    diag = tuple(s + adds[c // 2] for c, s in enumerate(scores(qi)))
    npairs = qi // 2
    put(sa, scores(0))
    accumulate(diag, qi)

    def pair(j, carry):
        a = 2 * j
        put(sb, scores(a + 1))
        accumulate(get(sa), a)
        put(sa, scores(jnp.minimum(a + 2, qi - 1)))
        accumulate(get(sb), a + 1)
        return carry
    lax.fori_loop(0, npairs, pair, 0)

    @pl.when(qi % 2 == 1)
    def _():
        accumulate(get(sa), qi - 1)

    lq1, lk1, lq2, lk2 = (lamv_ref[i:i + 1, :] for i in range(4))
    lam = (jnp.exp(jnp.sum(lq1 * lk1, axis=-1, keepdims=True))
           - jnp.exp(jnp.sum(lq2 * lk2, axis=-1, keepdims=True)) + lam_init)
    for hl in range(ATTN_HEADS_PER_STEP):
        o = acc_s[2 * hl] / l_s[2 * hl] - lam * (acc_s[2 * hl + 1] / l_s[2 * hl + 1])
        on = o * lax.rsqrt(jnp.mean(o * o, axis=0, keepdims=True) + NORM_EPS) * sg_ref[...] * (1.0 - lam_init)
        o_ref[:, hl * dv:(hl + 1) * dv] = on.T


def _attn(slopes2, qa, ka, vt, lamv, sg_col, nb, seq, tq, lam_init):
    t = qa.shape[0]
    nq = seq // tq
    dv = 2 * DIFF_DH
    hps = ATTN_HEADS_PER_STEP
    nstream = 2 * hps
    grid_spec = pltpu.PrefetchScalarGridSpec(
        num_scalar_prefetch=1,
        grid=(nb, DIFF_HEADS // hps, nq),
        in_specs=[pl.BlockSpec((tq, nstream * LANES), lambda b, h, i, s: (b * nq + i, h)),
                  pl.BlockSpec((seq, nstream * LANES), lambda b, h, i, s: (b, h)),
                  pl.BlockSpec((1, nq, hps * dv, tq), lambda b, h, i, s: (b, 0, h, 0)),
                  pl.BlockSpec(lamv.shape, lambda b, h, i, s: (0, 0)),
                  pl.BlockSpec(sg_col.shape, lambda b, h, i, s: (0, 0))],
        out_specs=pl.BlockSpec((tq, hps * dv), lambda b, h, i, s: (b * nq + i, h)),
        scratch_shapes=[pltpu.VMEM((nstream, 1, tq), F32), pltpu.VMEM((nstream, 1, tq), F32),
                        pltpu.VMEM((nstream, dv, tq), F32),
                        pltpu.VMEM((nstream, tq, tq), F32), pltpu.VMEM((nstream, tq, tq), F32)],
    )
    return pl.pallas_call(
        functools.partial(_attn_kernel, tq=tq, lam_init=lam_init),
        out_shape=jax.ShapeDtypeStruct((t, DIFF_V), F32),
        grid_spec=grid_spec,
        compiler_params=_cparams(("arbitrary", "arbitrary", "arbitrary")),
        name="attn",
    )(slopes2, qa, ka, vt, lamv, sg_col)


def _pack_halves(x):
    n = x.shape[1] // 2
    lo = pltpu.bitcast(x[:, :n].astype(BF16).astype(F32), jnp.uint32)
    hi = pltpu.bitcast(x[:, n:].astype(BF16).astype(F32), jnp.uint32)
    return (lo >> 16) | (hi & jnp.uint32(0xFFFF0000))


def _unpack_halves(p):
    lo = pltpu.bitcast(p << 16, F32)
    hi = pltpu.bitcast(p & jnp.uint32(0xFFFF0000), F32)
    return lo, hi


def _post_kernel(x_ref, oa_ref, ob_ref, gate_ref, g1_ref, sh2_ref, sc2_ref, n2_ref,
                 woa_ref, wob_ref, wout_ref, x1_ref, h2_ref, hp_ref):
    ya = _dot(oa_ref[...].astype(BF16), woa_ref[...])
    yb = _dot(ob_ref[...].astype(BF16), wob_ref[...])
    merged = (jax.nn.sigmoid(gate_ref[:, 0:D_MODEL].astype(F32)) * ya
              + jax.nn.sigmoid(gate_ref[:, D_MODEL:].astype(F32)) * yb)
    y = _dot(merged.astype(BF16), wout_ref[...])
    x1 = x_ref[...] + g1_ref[0] * y
    x1_ref[...] = x1
    n = x1 * lax.rsqrt(jnp.mean(x1 * x1, axis=-1, keepdims=True) + NORM_EPS) * n2_ref[...]
    h2 = n * (1.0 + sc2_ref[0]) + sh2_ref[0]
    h2_ref[...] = h2
    hp_ref[...] = _pack_halves(h2)


def _post(x2, oa, ob, gates, mod3, norm2_g, woa, wob, wout, seq):
    t = x2.shape[0]
    tm = min(512, seq)
    tpb = seq // tm
    const = lambda i: (0, 0)
    row = lambda n: pl.BlockSpec((tm, n), lambda i: (i, 0))
    modspec = lambda j: pl.BlockSpec((1, 1, D_MODEL), lambda i: (i // tpb, 0, j))
    wspec = lambda w: pl.BlockSpec(w.shape, const, pipeline_mode=pl.Buffered(1))
    return pl.pallas_call(
        _post_kernel,
        out_shape=(jax.ShapeDtypeStruct((t, D_MODEL), F32), jax.ShapeDtypeStruct((t, D_MODEL), F32),
                   jax.ShapeDtypeStruct((t, D_MODEL // 2), jnp.uint32)),
        grid=(t // tm,),
        in_specs=[row(D_MODEL), row(GDN_V), row(DIFF_V), row(2 * D_MODEL),
                  modspec(2), modspec(3), modspec(4), pl.BlockSpec((1, D_MODEL), const),
                  wspec(woa), wspec(wob), wspec(wout)],
        out_specs=(row(D_MODEL), row(D_MODEL), row(D_MODEL // 2)),
        compiler_params=_cparams(("arbitrary",)),
        name="post",
    )(x2, oa, ob, gates, mod3, mod3, mod3, norm2_g, woa, wob, wout)


def _router_kernel(h_ref, wrt_ref, rb_ref, idx_ref, w_ref, cnt_ref, cnt_s):
    step = pl.program_id(0)

    @pl.when(step == 0)
    def _():
        cnt_s[...] = jnp.zeros_like(cnt_s)

    tm = h_ref.shape[0]
    neg = -jnp.inf
    logits = lax.dot_general(wrt_ref[...], h_ref[...], (((1,), (1,)), ((), ())), precision=HIGHEST,
                             preferred_element_type=F32)
    scores = jax.nn.sigmoid(logits)
    choice = scores + rb_ref[...]
    eid = lax.broadcasted_iota(jnp.int32, (N_EXPERTS, tm), 0).astype(F32)
    c3 = choice.reshape(N_GROUPS, GROUP_SIZE, tm)
    e3 = eid.reshape(N_GROUPS, GROUP_SIZE, tm)
    m1 = jnp.max(c3, axis=1, keepdims=True)
    i1 = jnp.min(jnp.where(c3 == m1, e3, float(N_EXPERTS)), axis=1, keepdims=True)
    m2 = jnp.max(jnp.where(e3 == i1, neg, c3), axis=1, keepdims=True)
    gscore = (m1 + m2).reshape(N_GROUPS, tm)
    gid = lax.broadcasted_iota(jnp.int32, (N_GROUPS, tm), 0).astype(F32)
    gsel = jnp.zeros((N_GROUPS, tm), F32)
    for _ in range(TOPK_GROUPS):
        m = jnp.max(gscore, axis=0, keepdims=True)
        g = jnp.min(jnp.where(gscore == m, gid, float(N_GROUPS)), axis=0, keepdims=True)
        hit = gid == g
        gsel = jnp.where(hit, 1.0, gsel)
        gscore = jnp.where(hit, neg, gscore)
    masked = jnp.where(gsel.reshape(N_GROUPS, 1, tm) > 0.0, c3, neg).reshape(N_EXPERTS, tm)
    idx_rows, w_rows = [], []
    sel = jnp.zeros((N_EXPERTS, tm), F32)
    for k in range(TOP_K):
        m = jnp.max(masked, axis=0, keepdims=True)
        i = jnp.min(jnp.where(masked == m, eid, float(N_EXPERTS)), axis=0, keepdims=True)
        hit = eid == i
        idx_rows.append(i)
        w_rows.append(jnp.sum(jnp.where(hit, scores, 0.0), axis=0, keepdims=True))
        sel = jnp.where(hit, 1.0, sel)
        masked = jnp.where(hit, neg, masked)
    ws = jnp.concatenate(w_rows, axis=0)
    idx_ref[...] = jnp.concatenate(idx_rows, axis=0).astype(jnp.int32)
    w_ref[...] = ws / jnp.sum(ws, axis=0, keepdims=True) * ROUTED_SCALE
    ones = jnp.ones((SUBLANES, tm), BF16)
    cnt_s[...] = cnt_s[...] + _dot_nt(ones, sel.astype(BF16))[0:1, :]
    cnt_ref[...] = cnt_s[...]


def _router(h2, w_router_t, rbias_col):
    t = h2.shape[0]
    tm = min(256, t)
    const = lambda i: (0, 0)
    return pl.pallas_call(
        _router_kernel,
        out_shape=(jax.ShapeDtypeStruct((TOP_K, t), jnp.int32),
                   jax.ShapeDtypeStruct((TOP_K, t), F32),
                   jax.ShapeDtypeStruct((1, N_EXPERTS), F32)),
        grid=(t // tm,),
        in_specs=[pl.BlockSpec((tm, D_MODEL), lambda i: (i, 0)),
                  pl.BlockSpec(w_router_t.shape, const), pl.BlockSpec(rbias_col.shape, const)],
        out_specs=(pl.BlockSpec((TOP_K, tm), lambda i: (0, i)),
                   pl.BlockSpec((TOP_K, tm), lambda i: (0, i)),
                   pl.BlockSpec((1, N_EXPERTS), const)),
        scratch_shapes=[pltpu.VMEM((1, N_EXPERTS), F32)],
        compiler_params=_cparams(("arbitrary",)),
        name="router",
    )(h2, w_router_t, rbias_col)


def _rank_kernel(idx_ref, ps_ref, dest_ref, run_s):
    step = pl.program_id(0)

    @pl.when(step == 0)
    def _():
        run_s[...] = jnp.zeros_like(run_s)

    tm = idx_ref.shape[0]
    idx = idx_ref[...]
    lane = lax.broadcasted_iota(jnp.int32, (tm, N_EXPERTS), 1)
    gl = lax.broadcasted_iota(jnp.int32, (tm, LANES), 1)
    hits = [lane == idx[:, k:k + 1] for k in range(TOP_K)]
    sel = jnp.zeros((tm, N_EXPERTS), F32)
    for hit in hits:
        sel = jnp.where(hit, 1.0, sel)
    ri = lax.broadcasted_iota(jnp.int32, (tm, tm), 0)
    ci = lax.broadcasted_iota(jnp.int32, (tm, tm), 1)
    before = _dot((ri > ci).astype(BF16), sel.astype(BF16))
    base = before + run_s[...] + ps_ref[...]
    dest = jnp.zeros((tm, LANES), F32)
    for k, hit in enumerate(hits):
        dk = jnp.sum(jnp.where(hit, base, 0.0), axis=-1, keepdims=True)
        dest = jnp.where(gl == k, dk, dest)
    dest_ref[...] = dest.astype(jnp.int32)
    run_s[...] = run_s[...] + jnp.sum(sel, axis=0, keepdims=True)


def _rank(idx, pstart):
    t = idx.shape[0]
    tm = min(256, t)
    const = lambda i: (0, 0)
    return pl.pallas_call(
        _rank_kernel,
        out_shape=jax.ShapeDtypeStruct((t, LANES), jnp.int32),
        grid=(t // tm,),
        in_specs=[pl.BlockSpec((tm, LANES), lambda i: (i, 0)), pl.BlockSpec(pstart.shape, const)],
        out_specs=pl.BlockSpec((tm, LANES), lambda i: (i, 0)),
        scratch_shapes=[pltpu.VMEM((1, N_EXPERTS), F32)],
        compiler_params=_cparams(("arbitrary",)),
        name="rank",
    )(idx, pstart)


def _row_copy(src, dst, sem):
    return pltpu.make_async_copy(src, dst, sem)


def _dispatch_kernel(dest_ref, lastblk_ref, h_ref, xs_ref, zbuf, sem, zsem, *, tm):
    @pl.when(pl.program_id(0) == 0)
    def _():
        zbuf[...] = jnp.zeros_like(zbuf)

        def zcopy(e):
            r0 = pl.multiple_of(lastblk_ref[e] * EXPERT_BLOCK, EXPERT_BLOCK)
            return pltpu.make_async_copy(zbuf, xs_ref.at[pl.ds(r0, EXPERT_BLOCK), :], zsem)

        def zstart(e, c):
            zcopy(e).start()
            return c

        def zwait(e, c):
            zcopy(e).wait()
            return c

        lax.fori_loop(0, N_EXPERTS, zstart, 0)
        lax.fori_loop(0, N_EXPERTS, zwait, 0)

    def start_row(r, c):
        for k in range(TOP_K):
            d = dest_ref[r * TOP_K + k]
            _row_copy(h_ref.at[pl.ds(r, 1), :], xs_ref.at[pl.ds(d, 1), :], sem).start(priority=k % 2)
        return c

    lax.fori_loop(0, tm, start_row, 0)
    for k in range(TOP_K):
        _row_copy(h_ref, xs_ref.at[pl.ds(0, tm), :], sem).wait()


def _dispatch(dest_flat, last_blk, h2, n_rows):
    t, half = h2.shape
    tm = min(256, t)
    return pl.pallas_call(
        functools.partial(_dispatch_kernel, tm=tm),
        out_shape=jax.ShapeDtypeStruct((n_rows, half), h2.dtype),
        grid=(t // tm,),
        in_specs=[pl.BlockSpec((tm * TOP_K,), lambda i: (i,), memory_space=pltpu.SMEM),
                  pl.BlockSpec(last_blk.shape, lambda i: (0,), memory_space=pltpu.SMEM),
                  pl.BlockSpec((tm, half), lambda i: (i, 0))],
        out_specs=pl.BlockSpec(memory_space=pl.ANY),
        scratch_shapes=[pltpu.VMEM((EXPERT_BLOCK, half), h2.dtype), pltpu.SemaphoreType.DMA(()),
                        pltpu.SemaphoreType.DMA(())],
        compiler_params=_cparams(("arbitrary",)),
        name="dispatch",
    )(dest_flat, last_blk, h2)


SC_CORES = 2
SC_SUBCORES = 16
SC_LANES = 16
INVERT_CHUNK = 4096


def _invert(dest_flat, n_rows, n_tokens):
    n_assign = dest_flat.shape[0]
    workers = SC_CORES * SC_SUBCORES
    rpw = n_rows // workers
    assert rpw * workers == n_rows and rpw % SC_LANES == 0 and n_assign % INVERT_CHUNK == 0
    scratch_base = TOP_K * n_tokens
    log2_k = TOP_K.bit_length() - 1

    def body(dest_hbm, inv_hbm, loc, chunk_v):
        wid = lax.axis_index("s") * SC_CORES + lax.axis_index("c")
        lo = wid * rpw
        lane = lax.iota(jnp.int32, SC_LANES)

        def init(i, c):
            loc[pl.ds(i * SC_LANES, SC_LANES)] = scratch_base + lo + i * SC_LANES + lane
            return c
        lax.fori_loop(0, rpw // SC_LANES, init, 0)

        def do_chunk(ci, c):
            pltpu.sync_copy(dest_hbm.at[pl.ds(ci * INVERT_CHUNK, INVERT_CHUNK)], chunk_v)

            def inner(j, cc):
                rel = chunk_v[pl.ds(j * SC_LANES, SC_LANES)] - lo
                mine = (rel >= 0) & (rel < rpw)
                a = ci * INVERT_CHUNK + j * SC_LANES + lane
                val = (a & (TOP_K - 1)) * n_tokens + lax.shift_right_logical(a, log2_k)
                plsc.store_scatter(loc, [jnp.where(mine, rel, 0)], val, mask=mine)
                return cc
            lax.fori_loop(0, INVERT_CHUNK // SC_LANES, inner, 0)
            return c
        lax.fori_loop(0, n_assign // INVERT_CHUNK, do_chunk, 0)
        pltpu.sync_copy(loc, inv_hbm.at[pl.ds(lo, rpw)])

    mesh = plsc.VectorSubcoreMesh(core_axis_name="c", subcore_axis_name="s",
                                  num_cores=SC_CORES, num_subcores=SC_SUBCORES)
    return pl.kernel(body, out_type=jax.ShapeDtypeStruct((n_rows,), jnp.int32), mesh=mesh,
                     scratch_types=[pltpu.VMEM((rpw,), jnp.int32), pltpu.VMEM((INVERT_CHUNK,), jnp.int32)],
                     compiler_params=pltpu.CompilerParams(needs_layout_passes=False),
                     name="invert")(dest_flat)


EXPERT_RING = 5
EXPERT_AHEAD = EXPERT_RING - 2
WEIGHT_RING = 3


def _experts_kernel(start_ref, count_ref, xs_ref, inv_ref, wgu_ref, wdn_ref, g_ref,
                    wgu_f, wdn_f, wgu_s, wdn_s, xbuf, ybuf, inv_s, sem_w, sem_in, sem_inv, sem_out,
                    *, spare_row0):
    half = D_MODEL // 2
    last_e = N_EXPERTS - 1
    used = start_ref[last_e] + count_ref[last_e]

    def rows(g):
        return pl.ds(pl.multiple_of(g * EXPERT_BLOCK, EXPERT_BLOCK), EXPERT_BLOCK)

    def in_copies(g):
        slot = g % EXPERT_RING
        gc = jnp.minimum(g, used - 1)
        return (pltpu.make_async_copy(xs_ref.at[rows(gc), :], xbuf.at[slot], sem_in.at[slot]),
                pltpu.make_async_copy(inv_ref.at[pl.ds(gc, 1), :], inv_s.at[slot], sem_inv.at[slot]))

    def out_wait(slot):
        pltpu.make_async_copy(ybuf.at[slot], g_ref.at[pl.ds(0, EXPERT_BLOCK), :], sem_out.at[slot]).wait()

    def w_copies(e):
        slot = e % WEIGHT_RING
        return (pltpu.make_async_copy(wgu_ref.at[e], wgu_f.at[slot], sem_w.at[0, slot]),
                pltpu.make_async_copy(wdn_ref.at[e], wdn_f.at[slot], sem_w.at[1, slot]))

    def issue_rows(slot):
        for r in range(EXPERT_BLOCK):
            d = inv_s[slot, 0, r]
            _row_copy(ybuf.at[slot, pl.ds(r, 1), :], g_ref.at[pl.ds(d, 1), :], sem_out.at[slot]).start(priority=r % 2)

    for e0 in range(WEIGHT_RING - 1):
        for cp in w_copies(e0):
            cp.start()
    for g in range(EXPERT_AHEAD):
        for cp in in_copies(g):
            cp.start()
    ybuf[EXPERT_RING - 1] = jnp.zeros((EXPERT_BLOCK, half), jnp.uint32)
    for r in range(EXPERT_BLOCK):
        inv_s[EXPERT_RING - 1, 0, r] = spare_row0 + r

    def expert(e, carry):
        for cp in w_copies(e):
            cp.wait()

        @pl.when(e + WEIGHT_RING - 1 <= last_e)
        def _():
            for cp in w_copies(e + WEIGHT_RING - 1):
                cp.start()

        slot = e % WEIGHT_RING
        wgu_s[...] = wgu_f[slot].astype(BF16)
        wdn_s[...] = wdn_f[slot].astype(BF16)
        first = start_ref[e]

        def block(i, c):
            g = first + i
            for cp in in_copies(g):
                cp.wait()
            for cp in in_copies(g + EXPERT_AHEAD):
                cp.start()
            s = g % EXPERT_RING

            @pl.when(g >= EXPERT_RING - 1)
            def _():
                out_wait(s)

            issue_rows((g + EXPERT_RING - 1) % EXPERT_RING)
            lo, hi = _unpack_halves(xbuf[s])
            gu = _dot(lo.astype(BF16), wgu_s[0:half, :]) + _dot(hi.astype(BF16), wgu_s[half:, :])
            act = _silu(gu[:, :EXPERT_FF]) * gu[:, EXPERT_FF:]
            ybuf[s] = _pack_halves(_dot(act.astype(BF16), wdn_s[...]))
            return c

        lax.fori_loop(0, count_ref[e], block, 0)
        return carry

    lax.fori_loop(0, N_EXPERTS, expert, 0)
    issue_rows((used + EXPERT_RING - 1) % EXPERT_RING)

    for j in range(EXPERT_AHEAD):
        for cp in in_copies(used + j):
            cp.wait()
    for j in range(EXPERT_RING):
        @pl.when(used > j)
        def _():
            out_wait((used - 1 - j) % EXPERT_RING)

    @pl.when(used < EXPERT_RING)
    def _():
        out_wait(EXPERT_RING - 1)


def _experts(blk_start, blk_count, xs, inv2, w_gu, w_dn, n_slots):
    half = xs.shape[1]
    grid_spec = pltpu.PrefetchScalarGridSpec(
        num_scalar_prefetch=2,
        grid=(1,),
        in_specs=[pl.BlockSpec(memory_space=pl.ANY), pl.BlockSpec(memory_space=pl.ANY),
                  pl.BlockSpec(memory_space=pl.ANY), pl.BlockSpec(memory_space=pl.ANY)],
        out_specs=pl.BlockSpec(memory_space=pl.ANY),
        scratch_shapes=[pltpu.VMEM((WEIGHT_RING, D_MODEL, 2 * EXPERT_FF), F32),
                        pltpu.VMEM((WEIGHT_RING, EXPERT_FF, D_MODEL), F32),
                        pltpu.VMEM((D_MODEL, 2 * EXPERT_FF), BF16), pltpu.VMEM((EXPERT_FF, D_MODEL), BF16),
                        pltpu.VMEM((EXPERT_RING, EXPERT_BLOCK, half), jnp.uint32),
                        pltpu.VMEM((EXPERT_RING, EXPERT_BLOCK, half), jnp.uint32),
                        pltpu.SMEM((EXPERT_RING, 1, EXPERT_BLOCK), jnp.int32),
                        pltpu.SemaphoreType.DMA((2, WEIGHT_RING)), pltpu.SemaphoreType.DMA((EXPERT_RING,)),
                        pltpu.SemaphoreType.DMA((EXPERT_RING,)), pltpu.SemaphoreType.DMA((EXPERT_RING,))],
    )
    return pl.pallas_call(
        functools.partial(_experts_kernel, spare_row0=n_slots - EXPERT_BLOCK),
        out_shape=jax.ShapeDtypeStruct((n_slots, half), jnp.uint32),
        grid_spec=grid_spec,
        compiler_params=_cparams(("arbitrary",)),
        name="experts",
    )(blk_start, blk_count, xs, inv2, w_gu, w_dn)


def _combine_kernel(x1_ref, h_ref, w_ref, g2_ref, wsgu_ref, wsdn_ref, buf, o_ref):
    su = _dot(h_ref[...].astype(BF16), wsgu_ref[...])
    y = _dot((_silu(su[:, :SHARED_FF]) * su[:, SHARED_FF:]).astype(BF16), wsdn_ref[...])
    w = w_ref[...]
    half = D_MODEL // 2
    ylo, yhi = y[:, :half], y[:, half:]
    for k in range(TOP_K):
        lo, hi = _unpack_halves(buf[k])
        wk = w[:, k:k + 1]
        ylo = ylo + wk * lo
        yhi = yhi + wk * hi
    g2 = g2_ref[0]
    o_ref[:, :half] = x1_ref[:, :half] + g2[:, :half] * ylo
    o_ref[:, half:] = x1_ref[:, half:] + g2[:, half:] * yhi


def _combine(x1, h2, wts, mod3, wsgu, wsdn, slots3, seq):
    t = x1.shape[0]
    tm = min(512, seq)
    tpb = seq // tm
    const = lambda i: (0, 0)
    row = lambda n: pl.BlockSpec((tm, n), lambda i: (i, 0))
    return pl.pallas_call(
        _combine_kernel,
        out_shape=jax.ShapeDtypeStruct((t, D_MODEL), F32),
        grid=(t // tm,),
        in_specs=[row(D_MODEL), row(D_MODEL), row(LANES),
                  pl.BlockSpec((1, 1, D_MODEL), lambda i: (i // tpb, 0, 5)),
                  pl.BlockSpec(wsgu.shape, const), pl.BlockSpec(wsdn.shape, const),
                  pl.BlockSpec((TOP_K, tm, D_MODEL // 2), lambda i: (0, i, 0))],
        out_specs=row(D_MODEL),
        compiler_params=_cparams(("arbitrary",)),
        name="combine",
    )(x1, h2, wts, mod3, wsgu, wsdn, slots3)


def _layer(x, c, layer, w_ada, b_ada, norm1_g, w_in, conv_w, a_log, dt_bias, gdn_norm_g, w_o_gdn,
           q_norm_g, k_norm_g, lambda_q1, lambda_k1, lambda_q2, lambda_k2, subln_g, w_o_diff,
           w_out, norm2_g, w_router, router_bias, w_exp_gate_up, w_exp_down,
           w_shared_gate_up, w_shared_down):
    nb, seq, d = x.shape
    t = nb * seq
    lam_init = 0.8 - 0.6 * math.exp(-0.3 * layer)
    x2 = x.reshape(t, d)

    c_pad = jnp.pad(c, ((0, SUBLANES - nb % SUBLANES if nb % SUBLANES else 0), (0, 0)))
    mod = _ada(c_pad, w_ada, b_ada.reshape(1, -1))[:nb]
    mod3 = mod.reshape(nb, 1, 6 * d)

    o_bd = 2 * GDN_QK + 2 * GDN_V
    o_att = o_bd + 2 * GDN_HEADS
    o_gate = o_att + 2 * DIFF_QK + DIFF_V
    w_gdn = w_in[:, :o_bd].astype(BF16)
    w_bd = jnp.pad(w_in[:, o_bd:o_att], ((0, 0), (0, LANES - 2 * GDN_HEADS))).astype(BF16)
    w_att = w_in[:, o_att:o_gate].astype(BF16)
    w_gate = w_in[:, o_gate:].astype(BF16)
    gdn_in, bd, att_in, gates = _inproj(x2, mod3, norm1_g.reshape(1, d), w_gdn, w_bd, w_att, w_gate, seq)

    aparams = jnp.zeros((SUBLANES, LANES), F32)
    aparams = aparams.at[0, GDN_HEADS:2 * GDN_HEADS].set(a_log).at[1, GDN_HEADS:2 * GDN_HEADS].set(dt_bias)
    oa = _gdn(gdn_in.reshape(nb, seq, -1), bd.reshape(nb, seq, LANES), conv_w, aparams,
              gdn_norm_g.reshape(1, GDN_DV)).reshape(t, GDN_V)

    tq = min(512, seq)
    slopes = [2.0 ** (-8.0 * (h + 1) / DIFF_HEADS) for h in range(DIFF_HEADS)]
    c3 = _bf16_split3(LOG2E)
    qfeat = np.zeros((1, 2 * DIFF_HEADS * DIFF_DH), np.float32)
    for g in range(2 * DIFF_HEADS):
        for j in range(6):
            qfeat[0, g * DIFF_DH + j] = slopes[g // 2] * c3[j // 2]
    grp = np.arange(DIFF_QK) // DIFF_DH
    bdm = jnp.asarray((grp[:, None] == grp[None, :]).astype(np.float32)).astype(BF16)
    qa, ka, vt = _aprep(att_in, jnp.tile(q_norm_g, 2 * DIFF_HEADS).reshape(1, -1),
                        jnp.tile(k_norm_g, 2 * DIFF_HEADS).reshape(1, -1), jnp.asarray(qfeat), bdm, nb, seq, tq)
    slopes2 = jnp.asarray([s * (c3[0] + c3[1] + c3[2]) for s in slopes], F32)
    lamv = jnp.zeros((SUBLANES, DIFF_DH), F32)
    lamv = lamv.at[0].set(lambda_q1).at[1].set(lambda_k1).at[2].set(lambda_q2).at[3].set(lambda_k2)
    ob = _attn(slopes2, qa, ka, vt, lamv, subln_g.reshape(-1, 1), nb, seq, tq, lam_init)

    x1, h2, hp = _post(x2, oa, ob, gates, mod3, norm2_g.reshape(1, d), w_o_gdn.astype(BF16),
                       w_o_diff.astype(BF16), w_out.astype(BF16), seq)

    idx_t, wts_t, counts = _router(h2, w_router.T, router_bias.reshape(-1, 1))
    idx = jnp.pad(idx_t.T, ((0, 0), (0, LANES - TOP_K)))
    wts = jnp.pad(wts_t.T, ((0, 0), (0, LANES - TOP_K)))
    cnt = counts[0].astype(jnp.int32)
    padded = (cnt + EXPERT_BLOCK - 1) // EXPERT_BLOCK * EXPERT_BLOCK
    pends = jnp.cumsum(padded)
    pstart = (pends - padded).astype(F32).reshape(1, -1)
    nblk = -(-(t * TOP_K) // EXPERT_BLOCK) + N_EXPERTS
    blk_start = ((pends - padded) // EXPERT_BLOCK).astype(jnp.int32)
    blk_count = (padded // EXPERT_BLOCK).astype(jnp.int32)
    dest = _rank(idx, pstart)
    dest_flat = dest[:, :TOP_K].reshape(-1)
    n_rows = nblk * EXPERT_BLOCK
    assert n_rows % t == 0
    last_blk = jnp.maximum(blk_start + blk_count - 1, 0)
    xs = _dispatch(dest_flat, last_blk, hp, n_rows)
    inv = _invert(dest_flat, n_rows, t)
    assert t >= EXPERT_BLOCK
    slots = _experts(blk_start, blk_count, xs, inv.reshape(nblk, EXPERT_BLOCK), w_exp_gate_up, w_exp_down,
                     TOP_K * t + n_rows + t)
    out = _combine(x1, h2, wts, mod3, w_shared_gate_up.astype(BF16), w_shared_down.astype(BF16),
                   slots.reshape(-1, t, d // 2), seq)
    return out.reshape(nb, seq, d)


def kernel(x, c, w_ada, b_ada, norm1_g, w_in, conv_w, a_log, dt_bias, gdn_norm_g, w_o_gdn, q_norm_g, k_norm_g, lambda_q1, lambda_k1, lambda_q2, lambda_k2, subln_g, w_o_diff, w_out, norm2_g, w_router, router_bias, w_exp_gate_up, w_exp_down, w_shared_gate_up, w_shared_down):
    params = (w_ada, b_ada, norm1_g, w_in, conv_w, a_log, dt_bias, gdn_norm_g, w_o_gdn, q_norm_g,
              k_norm_g, lambda_q1, lambda_k1, lambda_q2, lambda_k2, subln_g, w_o_diff, w_out, norm2_g,
              w_router, router_bias, w_exp_gate_up, w_exp_down, w_shared_gate_up, w_shared_down)
    for layer in range(w_ada.shape[0]):
        x = _layer(x, c, layer, *(p[layer] for p in params))
    return x
```

```python
import functools
import math

import jax
import jax.numpy as jnp
import numpy as np
from jax import lax
from jax.experimental import pallas as pl
from jax.experimental.pallas import tpu as pltpu
from jax.experimental.pallas import tpu_sc as plsc

F32 = jnp.float32
BF16 = jnp.bfloat16
HIGHEST = lax.Precision.HIGHEST

D_MODEL = 1024
CHUNK = 64
GDN_HEADS = 4
GDN_DK = 128
GDN_DV = 128
GDN_CONV = 4
DIFF_HEADS = 4
DIFF_DH = 64
N_EXPERTS = 256
TOP_K = 8
N_GROUPS = 8
TOPK_GROUPS = 4
EXPERT_FF = 256
SHARED_FF = 256
ROUTED_SCALE = 2.5
NORM_EPS = 1e-6
GROUP_SIZE = N_EXPERTS // N_GROUPS

GDN_QK = GDN_HEADS * GDN_DK
GDN_V = GDN_HEADS * GDN_DV
CONV_CH = 2 * GDN_QK + GDN_V
DIFF_QK = DIFF_HEADS * 2 * DIFF_DH
DIFF_V = DIFF_HEADS * 2 * DIFF_DH

LANES = 128
SUBLANES = 8
EXPERT_BLOCK = 128
INV_BASE = 8
ATTN_HEADS_PER_STEP = 2
LOG2E = math.log2(math.e)
VMEM_LIMIT = 56 * 1024 * 1024


def _cparams(sem):
    return pltpu.CompilerParams(dimension_semantics=sem, vmem_limit_bytes=VMEM_LIMIT)


def _dot(a, b):
    return jnp.dot(a, b, preferred_element_type=F32)


def _dot_nt(a, b):
    return lax.dot_general(a, b, (((1,), (1,)), ((), ())), preferred_element_type=F32)


def _dot_tn(a, b):
    return lax.dot_general(a, b, (((0,), (0,)), ((), ())), preferred_element_type=F32)


def _silu(x):
    return x * jax.nn.sigmoid(x)


def _bf16_split3(x):
    rnd = lambda v: float(np.float32(v).astype(BF16).astype(np.float32))
    a = rnd(x)
    b = rnd(x - a)
    c = rnd(x - a - b)
    return a, b, c


def _ada_kernel(c_ref, w_ref, b_ref, o_ref):
    s = _silu(c_ref[...])
    o_ref[...] = jnp.dot(s, w_ref[...], precision=HIGHEST, preferred_element_type=F32) + b_ref[...]


def _ada(c_pad, w_ada, b_ada):
    n = w_ada.shape[1]
    tn = 1024
    return pl.pallas_call(
        _ada_kernel,
        out_shape=jax.ShapeDtypeStruct((c_pad.shape[0], n), F32),
        grid=(n // tn,),
        in_specs=[pl.BlockSpec(c_pad.shape, lambda j: (0, 0)),
                  pl.BlockSpec((D_MODEL, tn), lambda j: (0, j)),
                  pl.BlockSpec((1, tn), lambda j: (0, j))],
        out_specs=pl.BlockSpec((c_pad.shape[0], tn), lambda j: (0, j)),
        compiler_params=_cparams(("arbitrary",)),
        name="ada",
    )(c_pad, w_ada, b_ada)


def _inproj_kernel(x_ref, shift_ref, scale_ref, g_ref, wg_ref, wbd_ref, wa_ref, wgt_ref,
                   gdn_ref, bd_ref, att_ref, gate_ref):
    x = x_ref[...]
    y = x * lax.rsqrt(jnp.mean(x * x, axis=-1, keepdims=True) + NORM_EPS) * g_ref[...]
    h = (y * (1.0 + scale_ref[0]) + shift_ref[0]).astype(BF16)
    gdn_ref[...] = _dot(h, wg_ref[...]).astype(gdn_ref.dtype)
    bd_ref[...] = _dot(h, wbd_ref[...])
    att_ref[...] = _dot(h, wa_ref[...]).astype(att_ref.dtype)
    gate_ref[...] = _dot(h, wgt_ref[...]).astype(gate_ref.dtype)


def _inproj(x2, mod3, norm1_g, w_gdn, w_bd, w_att, w_gate, seq):
    t = x2.shape[0]
    tm = min(512, seq)
    tiles_per_b = seq // tm
    const = lambda i: (0, 0)
    wspec = lambda w: pl.BlockSpec(w.shape, const, pipeline_mode=pl.Buffered(1))
    row = lambda n: pl.BlockSpec((tm, n), lambda i: (i, 0))
    return pl.pallas_call(
        _inproj_kernel,
        out_shape=(jax.ShapeDtypeStruct((t, w_gdn.shape[1]), BF16),
                   jax.ShapeDtypeStruct((t, LANES), F32),
                   jax.ShapeDtypeStruct((t, w_att.shape[1]), BF16),
                   jax.ShapeDtypeStruct((t, w_gate.shape[1]), BF16)),
        grid=(t // tm,),
        in_specs=[row(D_MODEL),
                  pl.BlockSpec((1, 1, D_MODEL), lambda i: (i // tiles_per_b, 0, 0)),
                  pl.BlockSpec((1, 1, D_MODEL), lambda i: (i // tiles_per_b, 0, 1)),
                  pl.BlockSpec((1, D_MODEL), const),
                  wspec(w_gdn), wspec(w_bd), wspec(w_att), wspec(w_gate)],
        out_specs=(row(w_gdn.shape[1]), row(LANES), row(w_att.shape[1]), row(w_gate.shape[1])),
        compiler_params=_cparams(("arbitrary",)),
        name="inproj",
    )(x2, mod3, mod3, norm1_g, w_gdn, w_bd, w_att, w_gate)


def _gdn_kernel(x_ref, bd_ref, cw_ref, ap_ref, ng_ref, o_ref, cbuf, state, *, nb, lg):
    step = pl.program_id(0)
    nc = lg // CHUNK

    @pl.when(step == 0)
    def _():
        cbuf[...] = jnp.zeros_like(cbuf)
        state[...] = jnp.zeros_like(state)

    ri = lax.broadcasted_iota(jnp.int32, (CHUNK, CHUNK), 0)
    ci = lax.broadcasted_iota(jnp.int32, (CHUNK, CHUNK), 1)
    causal = ri >= ci
    strict = ri > ci
    eye = (ri == ci).astype(F32)
    rl = lax.broadcasted_iota(jnp.int32, (lg, lg), 0)
    cl = lax.broadcasted_iota(jnp.int32, (lg, lg), 1)
    blocktri = ((rl >= cl) & ((rl // CHUNK) == (cl // CHUNK))).astype(F32)
    cw = cw_ref[...]
    a_row = ap_ref[0:1, :]
    dt_row = ap_ref[1:2, :]
    ng = ng_ref[...]

    units = []
    for b in range(nb):
        cbuf[b, SUBLANES:SUBLANES + lg, :] = x_ref[b, :, 0:CONV_CH].astype(F32)
        acc = cw[GDN_CONV - 1:GDN_CONV, :] * cbuf[b, SUBLANES:SUBLANES + lg, :]
        for j in range(GDN_CONV - 1):
            off = SUBLANES - (GDN_CONV - 1) + j
            acc = acc + cw[j:j + 1, :] * cbuf[b, off:off + lg, :]
        cbuf[b, 0:SUBLANES, :] = cbuf[b, lg:lg + SUBLANES, :]
        qkv = _silu(acc)

        bd = bd_ref[b]
        beta_t = jax.nn.sigmoid(bd)
        g_t = -jnp.exp(a_row) * jax.nn.softplus(bd + dt_row)
        gcum = jnp.dot(blocktri, g_t, precision=HIGHEST, preferred_element_type=F32)
        gcum_t = gcum.T
        egcum = jnp.exp(gcum)

        for h in range(GDN_HEADS):
            qh = qkv[:, h * GDN_DK:(h + 1) * GDN_DK]
            kh = qkv[:, GDN_QK + h * GDN_DK:GDN_QK + (h + 1) * GDN_DK]
            vh = qkv[:, 2 * GDN_QK + h * GDN_DV:2 * GDN_QK + (h + 1) * GDN_DV]
            qh = qh * lax.rsqrt(jnp.sum(qh * qh, axis=-1, keepdims=True) + NORM_EPS) * (GDN_DK ** -0.5)
            kh = kh * lax.rsqrt(jnp.sum(kh * kh, axis=-1, keepdims=True) + NORM_EPS)
            for c in range(nc):
                r0, r1 = c * CHUNK, (c + 1) * CHUNK
                q, k, v = qh[r0:r1], kh[r0:r1], vh[r0:r1]
                gc = gcum[r0:r1, 4 + h:5 + h]
                gr = gcum_t[4 + h:5 + h, r0:r1]
                egc = egcum[r0:r1, 4 + h:5 + h]
                bcol = beta_t[r0:r1, h:h + 1]
                gl = gcum[r1 - 1:r1, 4 + h:5 + h]
                kb = k.astype(BF16)
                units.append(dict(
                    b=b, h=h, c=c, kb=kb, bcol=bcol,
                    qkb=jnp.concatenate([q.astype(BF16), kb], axis=0),
                    decay=jnp.exp(jnp.where(causal, gc - gr, -jnp.inf)),
                    rhs=jnp.concatenate([v * bcol, k * (bcol * egc)], axis=-1).astype(BF16),
                    qg=(q * egc).astype(BF16), egl=jnp.exp(gl),
                    kdec=(k * jnp.exp(gl - gc)).astype(BF16)))

    for u in units:
        u["kq"] = _dot_nt(u["qkb"], u["kb"])
    def same_block(size):
        return (ri // size) == (ci // size)
    for u in units:
        lower = jnp.where(strict, u["bcol"] * u["kq"][CHUNK:] * u["decay"], 0.0)
        u["lower"] = lower
        diag = jnp.where(same_block(INV_BASE), lower, 0.0)
        u["xinv"] = eye - diag
        u["pw"] = diag.astype(BF16)
    for r in range(INV_BASE.bit_length() - 2):
        for u in units:
            u["pw"] = _dot(u["pw"], u["pw"]).astype(BF16)
        for u in units:
            u["xinv"] = u["xinv"] + _dot(u["xinv"].astype(BF16), u["pw"])
    size = INV_BASE
    while size < CHUNK:
        pair_off = same_block(2 * size) & jnp.logical_not(same_block(size))
        for u in units:
            u["xb"] = u["xinv"].astype(BF16)
            u["cx"] = _dot(jnp.where(pair_off, u["lower"], 0.0).astype(BF16), u["xb"]).astype(BF16)
        for u in units:
            u["xinv"] = u["xinv"] - _dot(u["xb"], u["cx"])
        size *= 2
    for u in units:
        u["sol"] = _dot(u["xinv"].astype(BF16), u["rhs"])
        u["a_qk"] = (u["kq"][:CHUNK] * u["decay"]).astype(BF16)

    st = {(b, h): state[b, h] for b in range(nb) for h in range(GDN_HEADS)}
    for c in range(nc):
        cu = [u for u in units if u["c"] == c]
        for u in cu:
            stb = st[(u["b"], u["h"])].astype(BF16)
            u["ws"] = _dot(jnp.concatenate([u["sol"][:, GDN_DV:].astype(BF16), u["qg"]], axis=0), stb)
        for u in cu:
            u["vnb"] = (u["sol"][:, :GDN_DV] - u["ws"][:CHUNK]).astype(BF16)
        for u in cu:
            u["out"] = u["ws"][CHUNK:] + _dot(u["a_qk"], u["vnb"])
        for u in cu:
            key = (u["b"], u["h"])
            st[key] = st[key] * u["egl"] + _dot_tn(u["kdec"], u["vnb"])
        for u in cu:
            b, h = u["b"], u["h"]
            r0, r1 = c * CHUNK, (c + 1) * CHUNK
            out = u["out"]
            on = out * lax.rsqrt(jnp.mean(out * out, axis=-1, keepdims=True) + NORM_EPS) * ng
            zh = x_ref[b, r0:r1, CONV_CH + h * GDN_DV:CONV_CH + (h + 1) * GDN_DV].astype(F32)
            o_ref[b, r0:r1, h * GDN_DV:(h + 1) * GDN_DV] = on * _silu(zh)
    for (b, h), s in st.items():
        state[b, h] = s


def _gdn(gdn_in3, bd3, conv_w, aparams, gdn_norm_g):
    nb, seq, _ = gdn_in3.shape
    lg = min(128, seq)
    const = lambda i: (0, 0)
    return pl.pallas_call(
        functools.partial(_gdn_kernel, nb=nb, lg=lg),
        out_shape=jax.ShapeDtypeStruct((nb, seq, GDN_V), F32),
        grid=(seq // lg,),
        in_specs=[pl.BlockSpec((nb, lg, gdn_in3.shape[2]), lambda i: (0, i, 0)),
                  pl.BlockSpec((nb, lg, LANES), lambda i: (0, i, 0)),
                  pl.BlockSpec(conv_w.shape, const),
                  pl.BlockSpec(aparams.shape, const),
                  pl.BlockSpec(gdn_norm_g.shape, const)],
        out_specs=pl.BlockSpec((nb, lg, GDN_V), lambda i: (0, i, 0)),
        scratch_shapes=[pltpu.VMEM((nb, lg + SUBLANES, CONV_CH), F32),
                        pltpu.VMEM((nb, GDN_HEADS, GDN_DK, GDN_DV), F32)],
        compiler_params=_cparams(("arbitrary",)),
        name="gdn",
    )(gdn_in3, bd3, conv_w, aparams, gdn_norm_g)


def _aprep_kernel(x_ref, gq_ref, gk_ref, qf_ref, bdm_ref, q_ref, k_ref, vt_ref, *, tk):
    x = x_ref[...].astype(F32)
    tm = x.shape[0]
    bdm = bdm_ref[...]

    def qknorm(v, gain):
        sq = v * v
        hi = sq.astype(BF16)
        lo = (sq - hi.astype(F32)).astype(BF16)
        ms = (_dot(hi, bdm) + _dot(lo, bdm)) * (1.0 / DIFF_DH)
        return v * lax.rsqrt(ms + NORM_EPS) * gain

    qn = qknorm(x[:, 0:DIFF_QK], gq_ref[...]) * (DIFF_DH ** -0.5 * LOG2E)
    kn = qknorm(x[:, DIFF_QK:2 * DIFF_QK], gk_ref[...])
    pos = pl.program_id(0) * tm + lax.broadcasted_iota(jnp.int32, (tm, DIFF_DH), 0)
    krel = pos % tk
    lane = lax.broadcasted_iota(jnp.int32, (tm, DIFF_DH), 1)
    hi = ((krel // 256) * 256).astype(F32)
    lo = (krel % 256).astype(F32)
    kfeat = jnp.where(lane < 6, jnp.where(lane % 2 == 0, hi, lo), 0.0)
    qfeat = qf_ref[...]
    qparts, kparts = [], []
    for g in range(2 * DIFF_HEADS):
        qparts += [qn[:, g * DIFF_DH:(g + 1) * DIFF_DH],
                   jnp.broadcast_to(qfeat[:, g * DIFF_DH:(g + 1) * DIFF_DH], (tm, DIFF_DH))]
        kparts += [kn[:, g * DIFF_DH:(g + 1) * DIFF_DH], kfeat]
    q_ref[...] = jnp.concatenate(qparts, axis=-1).astype(BF16)
    k_ref[...] = jnp.concatenate(kparts, axis=-1).astype(BF16)
    vt_ref[0, 0] = x[:, 2 * DIFF_QK:].T.astype(BF16)


def _aprep(att_in, gq, gk, qfeat, bdm, nb, seq, tk):
    t = att_in.shape[0]
    tm = tk
    nk = seq // tk
    const = lambda i: (0, 0)
    return pl.pallas_call(
        functools.partial(_aprep_kernel, tk=tk),
        out_shape=(jax.ShapeDtypeStruct((t, 2 * DIFF_QK), BF16),
                   jax.ShapeDtypeStruct((t, 2 * DIFF_QK), BF16),
                   jax.ShapeDtypeStruct((nb, nk, DIFF_V, tk), BF16)),
        grid=(t // tm,),
        in_specs=[pl.BlockSpec((tm, att_in.shape[1]), lambda i: (i, 0)),
                  pl.BlockSpec(gq.shape, const), pl.BlockSpec(gk.shape, const),
                  pl.BlockSpec(qfeat.shape, const), pl.BlockSpec(bdm.shape, const)],
        out_specs=(pl.BlockSpec((tm, 2 * DIFF_QK), lambda i: (i, 0)),
                   pl.BlockSpec((tm, 2 * DIFF_QK), lambda i: (i, 0)),
                   pl.BlockSpec((1, 1, DIFF_V, tk), lambda i: (i // nk, i % nk, 0, 0))),
        compiler_params=_cparams(("arbitrary",)),
        name="aprep",
    )(att_in, gq, gk, qfeat, bdm)


def _attn_kernel(slope_ref, q_ref, k_ref, vt_ref, lamv_ref, sg_ref, o_ref, m_s, l_s, acc_s, sa, sb, ma, mb,
                 *, tq, lam_init):
    hp = pl.program_id(1)
    qi = pl.program_id(2)
    dv = 2 * DIFF_DH
    nstream = 2 * ATTN_HEADS_PER_STEP
    slope2 = [slope_ref[hp * ATTN_HEADS_PER_STEP + c // 2] for c in range(nstream)]
    qs = [q_ref[:, c * LANES:(c + 1) * LANES] for c in range(nstream)]

    m_s[...] = jnp.full_like(m_s, -jnp.inf)
    l_s[...] = jnp.zeros_like(l_s)
    acc_s[...] = jnp.zeros_like(acc_s)

    def scores(kj):
        k0 = pl.multiple_of(kj * tq, tq)
        return tuple(_dot_nt(k_ref[pl.ds(k0, tq), c * LANES:(c + 1) * LANES], qs[c]) for c in range(nstream))

    def accumulate(ss, mx, kj):
        rel = ((kj - qi) * tq).astype(F32)
        for c in range(nstream):
            hl = c // 2
            vt = vt_ref[0, kj, hl * dv:(hl + 1) * dv, :]
            off = slope2[c] * rel
            s = ss[c]
            m_old = m_s[c]
            m_new = jnp.maximum(m_old, mx[c] + off)
            alpha = jnp.exp2(m_old - m_new)
            p = jnp.exp2(s - (m_new - off))
            l_s[c] = alpha * l_s[c] + jnp.sum(p, axis=0, keepdims=True)
            acc_s[c] = alpha * acc_s[c] + _dot(vt, p.astype(BF16))
            m_s[c] = m_new

    kpos = lax.broadcasted_iota(jnp.int32, (tq, tq), 0)
    qpos = lax.broadcasted_iota(jnp.int32, (tq, tq), 1)
    allowed = (kpos // CHUNK) <= (qpos // CHUNK)
    fut = jnp.maximum(kpos - qpos, 0).astype(F32)
    adds = [jnp.where(allowed, (-2.0 * slope2[2 * hl]) * fut, -jnp.inf) for hl in range(ATTN_HEADS_PER_STEP)]
    def colmax(ss):
        return tuple(jnp.max(s, axis=0, keepdims=True) for s in ss)

    def put(slot, mslot, ss):
        for c in range(nstream):
            slot[c] = ss[c]
            mslot[c] = jnp.max(ss[c], axis=0, keepdims=True)

    def get(slot):
        return tuple(slot[c] for c in range(nstream))

    diag = tuple(s + adds[c // 2] for c, s in enumerate(scores(qi)))
    npairs = qi // 2
    put(sa, ma, scores(0))
    accumulate(diag, colmax(diag), qi)

    def pair(j, carry):
        a = 2 * j
        put(sb, mb, scores(a + 1))
        accumulate(get(sa), get(ma), a)
        put(sa, ma, scores(jnp.minimum(a + 2, qi - 1)))
        accumulate(get(sb), get(mb), a + 1)
        return carry
    lax.fori_loop(0, npairs, pair, 0)

    @pl.when(qi % 2 == 1)
    def _():
        accumulate(get(sa), get(ma), qi - 1)

    lq1, lk1, lq2, lk2 = (lamv_ref[i:i + 1, :] for i in range(4))
    lam = (jnp.exp(jnp.sum(lq1 * lk1, axis=-1, keepdims=True))
           - jnp.exp(jnp.sum(lq2 * lk2, axis=-1, keepdims=True)) + lam_init)
    for hl in range(ATTN_HEADS_PER_STEP):
        o = acc_s[2 * hl] / l_s[2 * hl] - lam * (acc_s[2 * hl + 1] / l_s[2 * hl + 1])
        on = o * lax.rsqrt(jnp.mean(o * o, axis=0, keepdims=True) + NORM_EPS) * sg_ref[...] * (1.0 - lam_init)
        o_ref[:, hl * dv:(hl + 1) * dv] = on.T


def _attn(slopes2, qa, ka, vt, lamv, sg_col, nb, seq, tq, lam_init):
    t = qa.shape[0]
    nq = seq // tq
    dv = 2 * DIFF_DH
    hps = ATTN_HEADS_PER_STEP
    nstream = 2 * hps
    grid_spec = pltpu.PrefetchScalarGridSpec(
        num_scalar_prefetch=1,
        grid=(nb, DIFF_HEADS // hps, nq),
        in_specs=[pl.BlockSpec((tq, nstream * LANES), lambda b, h, i, s: (b * nq + i, h)),
                  pl.BlockSpec((seq, nstream * LANES), lambda b, h, i, s: (b, h)),
                  pl.BlockSpec((1, nq, hps * dv, tq), lambda b, h, i, s: (b, 0, h, 0)),
                  pl.BlockSpec(lamv.shape, lambda b, h, i, s: (0, 0)),
                  pl.BlockSpec(sg_col.shape, lambda b, h, i, s: (0, 0))],
        out_specs=pl.BlockSpec((tq, hps * dv), lambda b, h, i, s: (b * nq + i, h)),
        scratch_shapes=[pltpu.VMEM((nstream, 1, tq), F32), pltpu.VMEM((nstream, 1, tq), F32),
                        pltpu.VMEM((nstream, dv, tq), F32),
                        pltpu.VMEM((nstream, tq, tq), F32), pltpu.VMEM((nstream, tq, tq), F32),
                        pltpu.VMEM((nstream, 1, tq), F32), pltpu.VMEM((nstream, 1, tq), F32)],
    )
    return pl.pallas_call(
        functools.partial(_attn_kernel, tq=tq, lam_init=lam_init),
        out_shape=jax.ShapeDtypeStruct((t, DIFF_V), F32),
        grid_spec=grid_spec,
        compiler_params=_cparams(("arbitrary", "arbitrary", "arbitrary")),
        name="attn",
    )(slopes2, qa, ka, vt, lamv, sg_col)


def _pack_halves(x):
    n = x.shape[1] // 2
    lo = pltpu.bitcast(x[:, :n].astype(BF16).astype(F32), jnp.uint32)
    hi = pltpu.bitcast(x[:, n:].astype(BF16).astype(F32), jnp.uint32)
    return (lo >> 16) | (hi & jnp.uint32(0xFFFF0000))


def _unpack_halves(p):
    lo = pltpu.bitcast(p << 16, F32)
    hi = pltpu.bitcast(p & jnp.uint32(0xFFFF0000), F32)
    return lo, hi


def _post_kernel(x_ref, oa_ref, ob_ref, gate_ref, g1_ref, sh2_ref, sc2_ref, n2_ref,
                 woa_ref, wob_ref, wout_ref, x1_ref, h2_ref, hp_ref):
    ya = _dot(oa_ref[...].astype(BF16), woa_ref[...])
    yb = _dot(ob_ref[...].astype(BF16), wob_ref[...])
    merged = (jax.nn.sigmoid(gate_ref[:, 0:D_MODEL].astype(F32)) * ya
              + jax.nn.sigmoid(gate_ref[:, D_MODEL:].astype(F32)) * yb)
    y = _dot(merged.astype(BF16), wout_ref[...])
    x1 = x_ref[...] + g1_ref[0] * y
    x1_ref[...] = x1
    n = x1 * lax.rsqrt(jnp.mean(x1 * x1, axis=-1, keepdims=True) + NORM_EPS) * n2_ref[...]
    h2 = n * (1.0 + sc2_ref[0]) + sh2_ref[0]
    h2_ref[...] = h2
    hp_ref[...] = _pack_halves(h2)


def _post(x2, oa, ob, gates, mod3, norm2_g, woa, wob, wout, seq):
    t = x2.shape[0]
    tm = min(512, seq)
    tpb = seq // tm
    const = lambda i: (0, 0)
    row = lambda n: pl.BlockSpec((tm, n), lambda i: (i, 0))
    modspec = lambda j: pl.BlockSpec((1, 1, D_MODEL), lambda i: (i // tpb, 0, j))
    wspec = lambda w: pl.BlockSpec(w.shape, const, pipeline_mode=pl.Buffered(1))
    return pl.pallas_call(
        _post_kernel,
        out_shape=(jax.ShapeDtypeStruct((t, D_MODEL), F32), jax.ShapeDtypeStruct((t, D_MODEL), F32),
                   jax.ShapeDtypeStruct((t, D_MODEL // 2), jnp.uint32)),
        grid=(t // tm,),
        in_specs=[row(D_MODEL), row(GDN_V), row(DIFF_V), row(2 * D_MODEL),
                  modspec(2), modspec(3), modspec(4), pl.BlockSpec((1, D_MODEL), const),
                  wspec(woa), wspec(wob), wspec(wout)],
        out_specs=(row(D_MODEL), row(D_MODEL), row(D_MODEL // 2)),
        compiler_params=_cparams(("arbitrary",)),
        name="post",
    )(x2, oa, ob, gates, mod3, mod3, mod3, norm2_g, woa, wob, wout)


def _router_kernel(h_ref, wrt_ref, rb_ref, idx_ref, w_ref, cnt_ref, cnt_s):
    step = pl.program_id(0)

    @pl.when(step == 0)
    def _():
        cnt_s[...] = jnp.zeros_like(cnt_s)

    tm = h_ref.shape[0]
    neg = -jnp.inf
    h = h_ref[...]
    h_hi = h.astype(BF16)
    h_lo = (h - h_hi.astype(F32)).astype(BF16)
    logits = _dot_nt(wrt_ref[0], h_hi) + _dot_nt(wrt_ref[0], h_lo) + _dot_nt(wrt_ref[1], h_hi)
    scores = jax.nn.sigmoid(logits)
    choice = scores + rb_ref[...]
    eid = lax.broadcasted_iota(jnp.int32, (N_EXPERTS, tm), 0).astype(F32)
    c3 = choice.reshape(N_GROUPS, GROUP_SIZE, tm)
    e3 = eid.reshape(N_GROUPS, GROUP_SIZE, tm)
    m1 = jnp.max(c3, axis=1, keepdims=True)
    i1 = jnp.min(jnp.where(c3 == m1, e3, float(N_EXPERTS)), axis=1, keepdims=True)
    m2 = jnp.max(jnp.where(e3 == i1, neg, c3), axis=1, keepdims=True)
    gscore = (m1 + m2).reshape(N_GROUPS, tm)
    gid = lax.broadcasted_iota(jnp.int32, (N_GROUPS, tm), 0).astype(F32)
    gsel = jnp.zeros((N_GROUPS, tm), F32)
    for _ in range(TOPK_GROUPS):
        m = jnp.max(gscore, axis=0, keepdims=True)
        g = jnp.min(jnp.where(gscore == m, gid, float(N_GROUPS)), axis=0, keepdims=True)
        hit = gid == g
        gsel = jnp.where(hit, 1.0, gsel)
        gscore = jnp.where(hit, neg, gscore)
    masked = jnp.where(gsel.reshape(N_GROUPS, 1, tm) > 0.0, c3, neg).reshape(N_EXPERTS, tm)
    idx_rows, w_rows = [], []
    sel = jnp.zeros((N_EXPERTS, tm), F32)
    for k in range(TOP_K):
        m = jnp.max(masked, axis=0, keepdims=True)
        i = jnp.min(jnp.where(masked == m, eid, float(N_EXPERTS)), axis=0, keepdims=True)
        hit = eid == i
        idx_rows.append(i)
        w_rows.append(jnp.sum(jnp.where(hit, scores, 0.0), axis=0, keepdims=True))
        sel = jnp.where(hit, 1.0, sel)
        masked = jnp.where(hit, neg, masked)
    ws = jnp.concatenate(w_rows, axis=0)
    idx_ref[...] = jnp.concatenate(idx_rows, axis=0).astype(jnp.int32)
    w_ref[...] = ws / jnp.sum(ws, axis=0, keepdims=True) * ROUTED_SCALE
    ones = jnp.ones((SUBLANES, tm), BF16)
    cnt_s[...] = cnt_s[...] + _dot_nt(ones, sel.astype(BF16))[0:1, :]
    cnt_ref[...] = cnt_s[...]


def _router(h2, w_router_t, rbias_col):
    t = h2.shape[0]
    tm = min(256, t)
    const = lambda i: (0, 0)
    return pl.pallas_call(
        _router_kernel,
        out_shape=(jax.ShapeDtypeStruct((TOP_K, t), jnp.int32),
                   jax.ShapeDtypeStruct((TOP_K, t), F32),
                   jax.ShapeDtypeStruct((1, N_EXPERTS), F32)),
        grid=(t // tm,),
        in_specs=[pl.BlockSpec((tm, D_MODEL), lambda i: (i, 0)),
                  pl.BlockSpec(w_router_t.shape, lambda i: (0, 0, 0)), pl.BlockSpec(rbias_col.shape, const)],
        out_specs=(pl.BlockSpec((TOP_K, tm), lambda i: (0, i)),
                   pl.BlockSpec((TOP_K, tm), lambda i: (0, i)),
                   pl.BlockSpec((1, N_EXPERTS), const)),
        scratch_shapes=[pltpu.VMEM((1, N_EXPERTS), F32)],
        compiler_params=_cparams(("arbitrary",)),
        name="router",
    )(h2, w_router_t, rbias_col)


def _rank_kernel(idx_ref, ps_ref, dest_ref, run_s):
    step = pl.program_id(0)

    @pl.when(step == 0)
    def _():
        run_s[...] = jnp.zeros_like(run_s)

    tm = idx_ref.shape[0]
    idx = idx_ref[...]
    lane = lax.broadcasted_iota(jnp.int32, (tm, N_EXPERTS), 1)
    gl = lax.broadcasted_iota(jnp.int32, (tm, LANES), 1)
    hits = [lane == idx[:, k:k + 1] for k in range(TOP_K)]
    sel = jnp.zeros((tm, N_EXPERTS), F32)
    for hit in hits:
        sel = jnp.where(hit, 1.0, sel)
    ri = lax.broadcasted_iota(jnp.int32, (tm, tm), 0)
    ci = lax.broadcasted_iota(jnp.int32, (tm, tm), 1)
    before = _dot((ri > ci).astype(BF16), sel.astype(BF16))
    base = before + run_s[...] + ps_ref[...]
    dest = jnp.zeros((tm, LANES), F32)
    for k, hit in enumerate(hits):
        dk = jnp.sum(jnp.where(hit, base, 0.0), axis=-1, keepdims=True)
        dest = jnp.where(gl == k, dk, dest)
    dest_ref[...] = dest.astype(jnp.int32)
    run_s[...] = run_s[...] + jnp.sum(sel, axis=0, keepdims=True)


def _rank(idx, pstart):
    t = idx.shape[0]
    tm = min(256, t)
    const = lambda i: (0, 0)
    return pl.pallas_call(
        _rank_kernel,
        out_shape=jax.ShapeDtypeStruct((t, LANES), jnp.int32),
        grid=(t // tm,),
        in_specs=[pl.BlockSpec((tm, LANES), lambda i: (i, 0)), pl.BlockSpec(pstart.shape, const)],
        out_specs=pl.BlockSpec((tm, LANES), lambda i: (i, 0)),
        scratch_shapes=[pltpu.VMEM((1, N_EXPERTS), F32)],
        compiler_params=_cparams(("arbitrary",)),
        name="rank",
    )(idx, pstart)


def _row_copy(src, dst, sem):
    return pltpu.make_async_copy(src, dst, sem)


def _dispatch_kernel(dest_ref, lastblk_ref, h_ref, xs_ref, zbuf, sem, zsem, *, tm):
    @pl.when(pl.program_id(0) == 0)
    def _():
        zbuf[...] = jnp.zeros_like(zbuf)

        def zcopy(e):
            r0 = pl.multiple_of(lastblk_ref[e] * EXPERT_BLOCK, EXPERT_BLOCK)
            return pltpu.make_async_copy(zbuf, xs_ref.at[pl.ds(r0, EXPERT_BLOCK), :], zsem)

        def zstart(e, c):
            zcopy(e).start()
            return c

        def zwait(e, c):
            zcopy(e).wait()
            return c

        lax.fori_loop(0, N_EXPERTS, zstart, 0)
        lax.fori_loop(0, N_EXPERTS, zwait, 0)

    def start_row(r, c):
        for k in range(TOP_K):
            d = dest_ref[r * TOP_K + k]
            _row_copy(h_ref.at[pl.ds(r, 1), :], xs_ref.at[pl.ds(d, 1), :], sem).start(priority=k % 2)
        return c

    lax.fori_loop(0, tm, start_row, 0)
    for k in range(TOP_K):
        _row_copy(h_ref, xs_ref.at[pl.ds(0, tm), :], sem).wait()


def _dispatch(dest_flat, last_blk, h2, n_rows):
    t, half = h2.shape
    tm = min(256, t)
    return pl.pallas_call(
        functools.partial(_dispatch_kernel, tm=tm),
        out_shape=jax.ShapeDtypeStruct((n_rows, half), h2.dtype),
        grid=(t // tm,),
        in_specs=[pl.BlockSpec((tm * TOP_K,), lambda i: (i,), memory_space=pltpu.SMEM),
                  pl.BlockSpec(last_blk.shape, lambda i: (0,), memory_space=pltpu.SMEM),
                  pl.BlockSpec((tm, half), lambda i: (i, 0))],
        out_specs=pl.BlockSpec(memory_space=pl.ANY),
        scratch_shapes=[pltpu.VMEM((EXPERT_BLOCK, half), h2.dtype), pltpu.SemaphoreType.DMA(()),
                        pltpu.SemaphoreType.DMA(())],
        compiler_params=_cparams(("arbitrary",)),
        name="dispatch",
    )(dest_flat, last_blk, h2)


SC_CORES = 2
SC_SUBCORES = 16
SC_LANES = 16
INVERT_CHUNK = 4096


def _invert(dest_flat, n_rows, n_tokens):
    n_assign = dest_flat.shape[0]
    workers = SC_CORES * SC_SUBCORES
    rpw = n_rows // workers
    assert rpw * workers == n_rows and rpw % SC_LANES == 0 and n_assign % INVERT_CHUNK == 0
    scratch_base = TOP_K * n_tokens
    log2_k = TOP_K.bit_length() - 1

    def body(dest_hbm, inv_hbm, loc, chunk_v):
        wid = lax.axis_index("s") * SC_CORES + lax.axis_index("c")
        lo = wid * rpw
        lane = lax.iota(jnp.int32, SC_LANES)

        def init(i, c):
            loc[pl.ds(i * SC_LANES, SC_LANES)] = scratch_base + lo + i * SC_LANES + lane
            return c
        lax.fori_loop(0, rpw // SC_LANES, init, 0)

        def do_chunk(ci, c):
            pltpu.sync_copy(dest_hbm.at[pl.ds(ci * INVERT_CHUNK, INVERT_CHUNK)], chunk_v)

            def inner(j, cc):
                rel = chunk_v[pl.ds(j * SC_LANES, SC_LANES)] - lo
                mine = (rel >= 0) & (rel < rpw)
                a = ci * INVERT_CHUNK + j * SC_LANES + lane
                val = (a & (TOP_K - 1)) * n_tokens + lax.shift_right_logical(a, log2_k)
                plsc.store_scatter(loc, [jnp.where(mine, rel, 0)], val, mask=mine)
                return cc
            lax.fori_loop(0, INVERT_CHUNK // SC_LANES, inner, 0)
            return c
        lax.fori_loop(0, n_assign // INVERT_CHUNK, do_chunk, 0)
        pltpu.sync_copy(loc, inv_hbm.at[pl.ds(lo, rpw)])

    mesh = plsc.VectorSubcoreMesh(core_axis_name="c", subcore_axis_name="s",
                                  num_cores=SC_CORES, num_subcores=SC_SUBCORES)
    return pl.kernel(body, out_type=jax.ShapeDtypeStruct((n_rows,), jnp.int32), mesh=mesh,
                     scratch_types=[pltpu.VMEM((rpw,), jnp.int32), pltpu.VMEM((INVERT_CHUNK,), jnp.int32)],
                     compiler_params=pltpu.CompilerParams(needs_layout_passes=False),
                     name="invert")(dest_flat)


EXPERT_RING = 5
EXPERT_AHEAD = EXPERT_RING - 2
WEIGHT_RING = 3


def _experts_kernel(start_ref, count_ref, xs_ref, inv_ref, wgu_ref, wdn_ref, g_ref,
                    wgu_f, wdn_f, wgu_s, wdn_s, xbuf, ybuf, inv_s, sem_w, sem_in, sem_inv, sem_out,
                    *, spare_row0):
    half = D_MODEL // 2
    last_e = N_EXPERTS - 1
    used = start_ref[last_e] + count_ref[last_e]

    def rows(g):
        return pl.ds(pl.multiple_of(g * EXPERT_BLOCK, EXPERT_BLOCK), EXPERT_BLOCK)

    def in_copies(g):
        slot = g % EXPERT_RING
        gc = jnp.minimum(g, used - 1)
        return (pltpu.make_async_copy(xs_ref.at[rows(gc), :], xbuf.at[slot], sem_in.at[slot]),
                pltpu.make_async_copy(inv_ref.at[pl.ds(gc, 1), :], inv_s.at[slot], sem_inv.at[slot]))

    def out_wait(slot):
        pltpu.make_async_copy(ybuf.at[slot], g_ref.at[pl.ds(0, EXPERT_BLOCK), :], sem_out.at[slot]).wait()

    def w_copies(e):
        slot = e % WEIGHT_RING
        return (pltpu.make_async_copy(wgu_ref.at[e], wgu_f.at[slot], sem_w.at[0, slot]),
                pltpu.make_async_copy(wdn_ref.at[e], wdn_f.at[slot], sem_w.at[1, slot]))

    def issue_rows(slot):
        for r in range(EXPERT_BLOCK):
            d = inv_s[slot, 0, r]
            _row_copy(ybuf.at[slot, pl.ds(r, 1), :], g_ref.at[pl.ds(d, 1), :], sem_out.at[slot]).start(priority=r % 2)

    for e0 in range(WEIGHT_RING - 1):
        for cp in w_copies(e0):
            cp.start()
    for g in range(EXPERT_AHEAD):
        for cp in in_copies(g):
            cp.start()
    ybuf[EXPERT_RING - 1] = jnp.zeros((EXPERT_BLOCK, half), jnp.uint32)
    for r in range(EXPERT_BLOCK):
        inv_s[EXPERT_RING - 1, 0, r] = spare_row0 + r

    def expert(e, carry):
        for cp in w_copies(e):
            cp.wait()

        @pl.when(e + WEIGHT_RING - 1 <= last_e)
        def _():
            for cp in w_copies(e + WEIGHT_RING - 1):
                cp.start()

        slot = e % WEIGHT_RING
        wgu_s[...] = wgu_f[slot].astype(BF16)
        wdn_s[...] = wdn_f[slot].astype(BF16)
        first = start_ref[e]

        def block(i, c):
            g = first + i
            for cp in in_copies(g):
                cp.wait()
            for cp in in_copies(g + EXPERT_AHEAD):
                cp.start()
            s = g % EXPERT_RING

            @pl.when(g >= EXPERT_RING - 1)
            def _():
                out_wait(s)

            issue_rows((g + EXPERT_RING - 1) % EXPERT_RING)
            lo, hi = _unpack_halves(xbuf[s])
            gu = _dot(lo.astype(BF16), wgu_s[0:half, :]) + _dot(hi.astype(BF16), wgu_s[half:, :])
            act = _silu(gu[:, :EXPERT_FF]) * gu[:, EXPERT_FF:]
            ybuf[s] = _pack_halves(_dot(act.astype(BF16), wdn_s[...]))
            return c

        lax.fori_loop(0, count_ref[e], block, 0)
        return carry

    lax.fori_loop(0, N_EXPERTS, expert, 0)
    issue_rows((used + EXPERT_RING - 1) % EXPERT_RING)

    for j in range(EXPERT_AHEAD):
        for cp in in_copies(used + j):
            cp.wait()
    for j in range(EXPERT_RING):
        @pl.when(used > j)
        def _():
            out_wait((used - 1 - j) % EXPERT_RING)

    @pl.when(used < EXPERT_RING)
    def _():
        out_wait(EXPERT_RING - 1)


def _experts(blk_start, blk_count, xs, inv2, w_gu, w_dn, n_slots):
    half = xs.shape[1]
    grid_spec = pltpu.PrefetchScalarGridSpec(
        num_scalar_prefetch=2,
        grid=(1,),
        in_specs=[pl.BlockSpec(memory_space=pl.ANY), pl.BlockSpec(memory_space=pl.ANY),
                  pl.BlockSpec(memory_space=pl.ANY), pl.BlockSpec(memory_space=pl.ANY)],
        out_specs=pl.BlockSpec(memory_space=pl.ANY),
        scratch_shapes=[pltpu.VMEM((WEIGHT_RING, D_MODEL, 2 * EXPERT_FF), F32),
                        pltpu.VMEM((WEIGHT_RING, EXPERT_FF, D_MODEL), F32),
                        pltpu.VMEM((D_MODEL, 2 * EXPERT_FF), BF16), pltpu.VMEM((EXPERT_FF, D_MODEL), BF16),
                        pltpu.VMEM((EXPERT_RING, EXPERT_BLOCK, half), jnp.uint32),
                        pltpu.VMEM((EXPERT_RING, EXPERT_BLOCK, half), jnp.uint32),
                        pltpu.SMEM((EXPERT_RING, 1, EXPERT_BLOCK), jnp.int32),
                        pltpu.SemaphoreType.DMA((2, WEIGHT_RING)), pltpu.SemaphoreType.DMA((EXPERT_RING,)),
                        pltpu.SemaphoreType.DMA((EXPERT_RING,)), pltpu.SemaphoreType.DMA((EXPERT_RING,))],
    )
    return pl.pallas_call(
        functools.partial(_experts_kernel, spare_row0=n_slots - EXPERT_BLOCK),
        out_shape=jax.ShapeDtypeStruct((n_slots, half), jnp.uint32),
        grid_spec=grid_spec,
        compiler_params=_cparams(("arbitrary",)),
        name="experts",
    )(blk_start, blk_count, xs, inv2, w_gu, w_dn)


def _combine_kernel(x1_ref, h_ref, w_ref, g2_ref, wsgu_ref, wsdn_ref, buf, o_ref):
    su = _dot(h_ref[...].astype(BF16), wsgu_ref[...])
    y = _dot((_silu(su[:, :SHARED_FF]) * su[:, SHARED_FF:]).astype(BF16), wsdn_ref[...])
    w = w_ref[...]
    half = D_MODEL // 2
    ylo, yhi = y[:, :half], y[:, half:]
    for k in range(TOP_K):
        lo, hi = _unpack_halves(buf[k])
        wk = w[:, k:k + 1]
        ylo = ylo + wk * lo
        yhi = yhi + wk * hi
    g2 = g2_ref[0]
    o_ref[:, :half] = x1_ref[:, :half] + g2[:, :half] * ylo
    o_ref[:, half:] = x1_ref[:, half:] + g2[:, half:] * yhi


def _combine(x1, h2, wts, mod3, wsgu, wsdn, slots3, seq):
    t = x1.shape[0]
    tm = min(512, seq)
    tpb = seq // tm
    const = lambda i: (0, 0)
    row = lambda n: pl.BlockSpec((tm, n), lambda i: (i, 0))
    return pl.pallas_call(
        _combine_kernel,
        out_shape=jax.ShapeDtypeStruct((t, D_MODEL), F32),
        grid=(t // tm,),
        in_specs=[row(D_MODEL), row(D_MODEL), row(LANES),
                  pl.BlockSpec((1, 1, D_MODEL), lambda i: (i // tpb, 0, 5)),
                  pl.BlockSpec(wsgu.shape, const), pl.BlockSpec(wsdn.shape, const),
                  pl.BlockSpec((TOP_K, tm, D_MODEL // 2), lambda i: (0, i, 0))],
        out_specs=row(D_MODEL),
        compiler_params=_cparams(("arbitrary",)),
        name="combine",
    )(x1, h2, wts, mod3, wsgu, wsdn, slots3)


def _layer(x, c, layer, w_ada, b_ada, norm1_g, w_in, conv_w, a_log, dt_bias, gdn_norm_g, w_o_gdn,
           q_norm_g, k_norm_g, lambda_q1, lambda_k1, lambda_q2, lambda_k2, subln_g, w_o_diff,
           w_out, norm2_g, w_router, router_bias, w_exp_gate_up, w_exp_down,
           w_shared_gate_up, w_shared_down):
    nb, seq, d = x.shape
    t = nb * seq
    lam_init = 0.8 - 0.6 * math.exp(-0.3 * layer)
    x2 = x.reshape(t, d)

    c_pad = jnp.pad(c, ((0, SUBLANES - nb % SUBLANES if nb % SUBLANES else 0), (0, 0)))
    mod = _ada(c_pad, w_ada, b_ada.reshape(1, -1))[:nb]
    mod3 = mod.reshape(nb, 1, 6 * d)

    o_bd = 2 * GDN_QK + 2 * GDN_V
    o_att = o_bd + 2 * GDN_HEADS
    o_gate = o_att + 2 * DIFF_QK + DIFF_V
    w_gdn = w_in[:, :o_bd].astype(BF16)
    w_bd = jnp.pad(w_in[:, o_bd:o_att], ((0, 0), (0, LANES - 2 * GDN_HEADS))).astype(BF16)
    w_att = w_in[:, o_att:o_gate].astype(BF16)
    w_gate = w_in[:, o_gate:].astype(BF16)
    gdn_in, bd, att_in, gates = _inproj(x2, mod3, norm1_g.reshape(1, d), w_gdn, w_bd, w_att, w_gate, seq)

    aparams = jnp.zeros((SUBLANES, LANES), F32)
    aparams = aparams.at[0, GDN_HEADS:2 * GDN_HEADS].set(a_log).at[1, GDN_HEADS:2 * GDN_HEADS].set(dt_bias)
    oa = _gdn(gdn_in.reshape(nb, seq, -1), bd.reshape(nb, seq, LANES), conv_w, aparams,
              gdn_norm_g.reshape(1, GDN_DV)).reshape(t, GDN_V)

    tq = min(512, seq)
    slopes = [2.0 ** (-8.0 * (h + 1) / DIFF_HEADS) for h in range(DIFF_HEADS)]
    c3 = _bf16_split3(LOG2E)
    qfeat = np.zeros((1, 2 * DIFF_HEADS * DIFF_DH), np.float32)
    for g in range(2 * DIFF_HEADS):
        for j in range(6):
            qfeat[0, g * DIFF_DH + j] = slopes[g // 2] * c3[j // 2]
    grp = np.arange(DIFF_QK) // DIFF_DH
    bdm = jnp.asarray((grp[:, None] == grp[None, :]).astype(np.float32)).astype(BF16)
    qa, ka, vt = _aprep(att_in, jnp.tile(q_norm_g, 2 * DIFF_HEADS).reshape(1, -1),
                        jnp.tile(k_norm_g, 2 * DIFF_HEADS).reshape(1, -1), jnp.asarray(qfeat), bdm, nb, seq, tq)
    slopes2 = jnp.asarray([s * (c3[0] + c3[1] + c3[2]) for s in slopes], F32)
    lamv = jnp.zeros((SUBLANES, DIFF_DH), F32)
    lamv = lamv.at[0].set(lambda_q1).at[1].set(lambda_k1).at[2].set(lambda_q2).at[3].set(lambda_k2)
    ob = _attn(slopes2, qa, ka, vt, lamv, subln_g.reshape(-1, 1), nb, seq, tq, lam_init)

    x1, h2, hp = _post(x2, oa, ob, gates, mod3, norm2_g.reshape(1, d), w_o_gdn.astype(BF16),
                       w_o_diff.astype(BF16), w_out.astype(BF16), seq)

    wr_hi = w_router.T.astype(BF16)
    wr_lo = (w_router.T - wr_hi.astype(F32)).astype(BF16)
    idx_t, wts_t, counts = _router(h2, jnp.stack([wr_hi, wr_lo]), router_bias.reshape(-1, 1))
    idx = jnp.pad(idx_t.T, ((0, 0), (0, LANES - TOP_K)))
    wts = jnp.pad(wts_t.T, ((0, 0), (0, LANES - TOP_K)))
    cnt = counts[0].astype(jnp.int32)
    padded = (cnt + EXPERT_BLOCK - 1) // EXPERT_BLOCK * EXPERT_BLOCK
    pends = jnp.cumsum(padded)
    pstart = (pends - padded).astype(F32).reshape(1, -1)
    nblk = -(-(t * TOP_K) // EXPERT_BLOCK) + N_EXPERTS
    blk_start = ((pends - padded) // EXPERT_BLOCK).astype(jnp.int32)
    blk_count = (padded // EXPERT_BLOCK).astype(jnp.int32)
    dest = _rank(idx, pstart)
    dest_flat = dest[:, :TOP_K].reshape(-1)
    n_rows = nblk * EXPERT_BLOCK
    assert n_rows % t == 0
    last_blk = jnp.maximum(blk_start + blk_count - 1, 0)
    xs = _dispatch(dest_flat, last_blk, hp, n_rows)
    inv = _invert(dest_flat, n_rows, t)
    assert t >= EXPERT_BLOCK
    slots = _experts(blk_start, blk_count, xs, inv.reshape(nblk, EXPERT_BLOCK), w_exp_gate_up, w_exp_down,
                     TOP_K * t + n_rows + t)
    out = _combine(x1, h2, wts, mod3, w_shared_gate_up.astype(BF16), w_shared_down.astype(BF16),
                   slots.reshape(-1, t, d // 2), seq)
    return out.reshape(nb, seq, d)


def kernel(x, c, w_ada, b_ada, norm1_g, w_in, conv_w, a_log, dt_bias, gdn_norm_g, w_o_gdn, q_norm_g, k_norm_g, lambda_q1, lambda_k1, lambda_q2, lambda_k2, subln_g, w_o_diff, w_out, norm2_g, w_router, router_bias, w_exp_gate_up, w_exp_down, w_shared_gate_up, w_shared_down):
    params = (w_ada, b_ada, norm1_g, w_in, conv_w, a_log, dt_bias, gdn_norm_g, w_o_gdn, q_norm_g,
              k_norm_g, lambda_q1, lambda_k1, lambda_q2, lambda_k2, subln_g, w_o_diff, w_out, norm2_g,
              w_router, router_bias, w_exp_gate_up, w_exp_down, w_shared_gate_up, w_shared_down)
    for layer in range(w_ada.shape[0]):
        x = _layer(x, c, layer, *(p[layer] for p in params))
    return x
```

```python
import functools
import math

import jax
import jax.numpy as jnp
import numpy as np
from jax import lax
from jax.experimental import pallas as pl
from jax.experimental.pallas import tpu as pltpu
from jax.experimental.pallas import tpu_sc as plsc

F32 = jnp.float32
BF16 = jnp.bfloat16
HIGHEST = lax.Precision.HIGHEST

D_MODEL = 1024
CHUNK = 64
GDN_HEADS = 4
GDN_DK = 128
GDN_DV = 128
GDN_CONV = 4
DIFF_HEADS = 4
DIFF_DH = 64
N_EXPERTS = 256
TOP_K = 8
N_GROUPS = 8
TOPK_GROUPS = 4
EXPERT_FF = 256
SHARED_FF = 256
ROUTED_SCALE = 2.5
NORM_EPS = 1e-6
GROUP_SIZE = N_EXPERTS // N_GROUPS

GDN_QK = GDN_HEADS * GDN_DK
GDN_V = GDN_HEADS * GDN_DV
CONV_CH = 2 * GDN_QK + GDN_V
DIFF_QK = DIFF_HEADS * 2 * DIFF_DH
DIFF_V = DIFF_HEADS * 2 * DIFF_DH

LANES = 128
SUBLANES = 8
EXPERT_BLOCK = 128
INV_BASE = 8
ATTN_HEADS_PER_STEP = 2
LOG2E = math.log2(math.e)
VMEM_LIMIT = 56 * 1024 * 1024


def _cparams(sem):
    return pltpu.CompilerParams(dimension_semantics=sem, vmem_limit_bytes=VMEM_LIMIT)


def _dot(a, b):
    return jnp.dot(a, b, preferred_element_type=F32)


def _dot_nt(a, b):
    return lax.dot_general(a, b, (((1,), (1,)), ((), ())), preferred_element_type=F32)


def _dot_tn(a, b):
    return lax.dot_general(a, b, (((0,), (0,)), ((), ())), preferred_element_type=F32)


def _silu(x):
    return x * jax.nn.sigmoid(x)


def _bf16_split3(x):
    rnd = lambda v: float(np.float32(v).astype(BF16).astype(np.float32))
    a = rnd(x)
    b = rnd(x - a)
    c = rnd(x - a - b)
    return a, b, c


def _ada_kernel(c_ref, w_ref, b_ref, o_ref):
    s = _silu(c_ref[...])
    o_ref[...] = jnp.dot(s, w_ref[...], precision=HIGHEST, preferred_element_type=F32) + b_ref[...]


def _ada(c_pad, w_ada, b_ada):
    n = w_ada.shape[1]
    tn = 1024
    return pl.pallas_call(
        _ada_kernel,
        out_shape=jax.ShapeDtypeStruct((c_pad.shape[0], n), F32),
        grid=(n // tn,),
        in_specs=[pl.BlockSpec(c_pad.shape, lambda j: (0, 0)),
                  pl.BlockSpec((D_MODEL, tn), lambda j: (0, j)),
                  pl.BlockSpec((1, tn), lambda j: (0, j))],
        out_specs=pl.BlockSpec((c_pad.shape[0], tn), lambda j: (0, j)),
        compiler_params=_cparams(("arbitrary",)),
        name="ada",
    )(c_pad, w_ada, b_ada)


def _inproj_kernel(x_ref, shift_ref, scale_ref, g_ref, wg_ref, wbd_ref, wa_ref, wgt_ref,
                   gdn_ref, bd_ref, att_ref, gate_ref):
    x = x_ref[...]
    y = x * lax.rsqrt(jnp.mean(x * x, axis=-1, keepdims=True) + NORM_EPS) * g_ref[...]
    h = (y * (1.0 + scale_ref[0]) + shift_ref[0]).astype(BF16)
    gdn_ref[...] = _dot(h, wg_ref[...]).astype(gdn_ref.dtype)
    bd_ref[...] = _dot(h, wbd_ref[...])
    att_ref[...] = _dot(h, wa_ref[...]).astype(att_ref.dtype)
    gate_ref[...] = _dot(h, wgt_ref[...]).astype(gate_ref.dtype)


def _inproj(x2, mod3, norm1_g, w_gdn, w_bd, w_att, w_gate, seq):
    t = x2.shape[0]
    tm = min(512, seq)
    tiles_per_b = seq // tm
    const = lambda i: (0, 0)
    wspec = lambda w: pl.BlockSpec(w.shape, const, pipeline_mode=pl.Buffered(1))
    row = lambda n: pl.BlockSpec((tm, n), lambda i: (i, 0))
    return pl.pallas_call(
        _inproj_kernel,
        out_shape=(jax.ShapeDtypeStruct((t, w_gdn.shape[1]), BF16),
                   jax.ShapeDtypeStruct((t, LANES), F32),
                   jax.ShapeDtypeStruct((t, w_att.shape[1]), BF16),
                   jax.ShapeDtypeStruct((t, w_gate.shape[1]), BF16)),
        grid=(t // tm,),
        in_specs=[row(D_MODEL),
                  pl.BlockSpec((1, 1, D_MODEL), lambda i: (i // tiles_per_b, 0, 0)),
                  pl.BlockSpec((1, 1, D_MODEL), lambda i: (i // tiles_per_b, 0, 1)),
                  pl.BlockSpec((1, D_MODEL), const),
                  wspec(w_gdn), wspec(w_bd), wspec(w_att), wspec(w_gate)],
        out_specs=(row(w_gdn.shape[1]), row(LANES), row(w_att.shape[1]), row(w_gate.shape[1])),
        compiler_params=_cparams(("arbitrary",)),
        name="inproj",
    )(x2, mod3, mod3, norm1_g, w_gdn, w_bd, w_att, w_gate)


def _gdn_kernel(x_ref, bd_ref, cw_ref, ap_ref, ng_ref, o_ref, cbuf, state, *, nb, lg):
    step = pl.program_id(0)
    nc = lg // CHUNK

    @pl.when(step == 0)
    def _():
        cbuf[...] = jnp.zeros_like(cbuf)
        state[...] = jnp.zeros_like(state)

    ri = lax.broadcasted_iota(jnp.int32, (CHUNK, CHUNK), 0)
    ci = lax.broadcasted_iota(jnp.int32, (CHUNK, CHUNK), 1)
    causal = ri >= ci
    strict = ri > ci
    eye = (ri == ci).astype(F32)
    rl = lax.broadcasted_iota(jnp.int32, (lg, lg), 0)
    cl = lax.broadcasted_iota(jnp.int32, (lg, lg), 1)
    blocktri = ((rl >= cl) & ((rl // CHUNK) == (cl // CHUNK))).astype(F32)
    cw = cw_ref[...]
    a_row = ap_ref[0:1, :]
    dt_row = ap_ref[1:2, :]
    ng = ng_ref[...]

    units = []
    for b in range(nb):
        cbuf[b, SUBLANES:SUBLANES + lg, :] = x_ref[b, :, 0:CONV_CH].astype(F32)
        acc = cw[GDN_CONV - 1:GDN_CONV, :] * cbuf[b, SUBLANES:SUBLANES + lg, :]
        for j in range(GDN_CONV - 1):
            off = SUBLANES - (GDN_CONV - 1) + j
            acc = acc + cw[j:j + 1, :] * cbuf[b, off:off + lg, :]
        cbuf[b, 0:SUBLANES, :] = cbuf[b, lg:lg + SUBLANES, :]
        qkv = _silu(acc)

        bd = bd_ref[b]
        beta_t = jax.nn.sigmoid(bd)
        g_t = -jnp.exp(a_row) * jax.nn.softplus(bd + dt_row)
        gcum = jnp.dot(blocktri, g_t, precision=HIGHEST, preferred_element_type=F32)
        gcum_t = gcum.T
        egcum = jnp.exp(gcum)

        for h in range(GDN_HEADS):
            qh = qkv[:, h * GDN_DK:(h + 1) * GDN_DK]
            kh = qkv[:, GDN_QK + h * GDN_DK:GDN_QK + (h + 1) * GDN_DK]
            vh = qkv[:, 2 * GDN_QK + h * GDN_DV:2 * GDN_QK + (h + 1) * GDN_DV]
            qh = qh * lax.rsqrt(jnp.sum(qh * qh, axis=-1, keepdims=True) + NORM_EPS) * (GDN_DK ** -0.5)
            kh = kh * lax.rsqrt(jnp.sum(kh * kh, axis=-1, keepdims=True) + NORM_EPS)
            for c in range(nc):
                r0, r1 = c * CHUNK, (c + 1) * CHUNK
                q, k, v = qh[r0:r1], kh[r0:r1], vh[r0:r1]
                gc = gcum[r0:r1, 4 + h:5 + h]
                gr = gcum_t[4 + h:5 + h, r0:r1]
                egc = egcum[r0:r1, 4 + h:5 + h]
                bcol = beta_t[r0:r1, h:h + 1]
                gl = gcum[r1 - 1:r1, 4 + h:5 + h]
                kb = k.astype(BF16)
                units.append(dict(
                    b=b, h=h, c=c, kb=kb, bcol=bcol,
                    qkb=jnp.concatenate([q.astype(BF16), kb], axis=0),
                    decay=jnp.exp(jnp.where(causal, gc - gr, -jnp.inf)),
                    rhs=jnp.concatenate([v * bcol, k * (bcol * egc)], axis=-1).astype(BF16),
                    qg=(q * egc).astype(BF16), egl=jnp.exp(gl),
                    kdec=(k * jnp.exp(gl - gc)).astype(BF16)))

    for u in units:
        u["kq"] = _dot_nt(u["qkb"], u["kb"])
    def same_block(size):
        return (ri // size) == (ci // size)
    for u in units:
        lower = jnp.where(strict, u["bcol"] * u["kq"][CHUNK:] * u["decay"], 0.0)
        u["lower"] = lower
        diag = jnp.where(same_block(INV_BASE), lower, 0.0)
        u["xinv"] = eye - diag
        u["pw"] = diag.astype(BF16)
    for r in range(INV_BASE.bit_length() - 2):
        for u in units:
            u["pw"] = _dot(u["pw"], u["pw"]).astype(BF16)
        for u in units:
            u["xinv"] = u["xinv"] + _dot(u["xinv"].astype(BF16), u["pw"])
    size = INV_BASE
    while size < CHUNK:
        pair_off = same_block(2 * size) & jnp.logical_not(same_block(size))
        for u in units:
            u["xb"] = u["xinv"].astype(BF16)
            u["cx"] = _dot(jnp.where(pair_off, u["lower"], 0.0).astype(BF16), u["xb"]).astype(BF16)
        for u in units:
            u["xinv"] = u["xinv"] - _dot(u["xb"], u["cx"])
        size *= 2
    for u in units:
        u["sol"] = _dot(u["xinv"].astype(BF16), u["rhs"])
        u["a_qk"] = (u["kq"][:CHUNK] * u["decay"]).astype(BF16)

    st = {(b, h): state[b, h] for b in range(nb) for h in range(GDN_HEADS)}
    for c in range(nc):
        cu = [u for u in units if u["c"] == c]
        for u in cu:
            stb = st[(u["b"], u["h"])].astype(BF16)
            u["ws"] = _dot(jnp.concatenate([u["sol"][:, GDN_DV:].astype(BF16), u["qg"]], axis=0), stb)
        for u in cu:
            u["vnb"] = (u["sol"][:, :GDN_DV] - u["ws"][:CHUNK]).astype(BF16)
        for u in cu:
            u["out"] = u["ws"][CHUNK:] + _dot(u["a_qk"], u["vnb"])
        for u in cu:
            key = (u["b"], u["h"])
            st[key] = st[key] * u["egl"] + _dot_tn(u["kdec"], u["vnb"])
        for u in cu:
            b, h = u["b"], u["h"]
            r0, r1 = c * CHUNK, (c + 1) * CHUNK
            out = u["out"]
            on = out * lax.rsqrt(jnp.mean(out * out, axis=-1, keepdims=True) + NORM_EPS) * ng
            zh = x_ref[b, r0:r1, CONV_CH + h * GDN_DV:CONV_CH + (h + 1) * GDN_DV].astype(F32)
            o_ref[b, r0:r1, h * GDN_DV:(h + 1) * GDN_DV] = on * _silu(zh)
    for (b, h), s in st.items():
        state[b, h] = s


def _gdn(gdn_in3, bd3, conv_w, aparams, gdn_norm_g):
    nb, seq, _ = gdn_in3.shape
    lg = min(128, seq)
    const = lambda i: (0, 0)
    return pl.pallas_call(
        functools.partial(_gdn_kernel, nb=nb, lg=lg),
        out_shape=jax.ShapeDtypeStruct((nb, seq, GDN_V), F32),
        grid=(seq // lg,),
        in_specs=[pl.BlockSpec((nb, lg, gdn_in3.shape[2]), lambda i: (0, i, 0)),
                  pl.BlockSpec((nb, lg, LANES), lambda i: (0, i, 0)),
                  pl.BlockSpec(conv_w.shape, const),
                  pl.BlockSpec(aparams.shape, const),
                  pl.BlockSpec(gdn_norm_g.shape, const)],
        out_specs=pl.BlockSpec((nb, lg, GDN_V), lambda i: (0, i, 0)),
        scratch_shapes=[pltpu.VMEM((nb, lg + SUBLANES, CONV_CH), F32),
                        pltpu.VMEM((nb, GDN_HEADS, GDN_DK, GDN_DV), F32)],
        compiler_params=_cparams(("arbitrary",)),
        name="gdn",
    )(gdn_in3, bd3, conv_w, aparams, gdn_norm_g)


def _aprep_kernel(x_ref, gq_ref, gk_ref, qf_ref, bdm_ref, q_ref, k_ref, vt_ref, *, tk):
    x = x_ref[...].astype(F32)
    tm = x.shape[0]
    bdm = bdm_ref[...]

    def qknorm(v, gain):
        sq = v * v
        hi = sq.astype(BF16)
        lo = (sq - hi.astype(F32)).astype(BF16)
        ms = (_dot(hi, bdm) + _dot(lo, bdm)) * (1.0 / DIFF_DH)
        return v * lax.rsqrt(ms + NORM_EPS) * gain

    qn = qknorm(x[:, 0:DIFF_QK], gq_ref[...]) * (DIFF_DH ** -0.5 * LOG2E)
    kn = qknorm(x[:, DIFF_QK:2 * DIFF_QK], gk_ref[...])
    pos = pl.program_id(0) * tm + lax.broadcasted_iota(jnp.int32, (tm, DIFF_DH), 0)
    krel = pos % tk
    lane = lax.broadcasted_iota(jnp.int32, (tm, DIFF_DH), 1)
    hi = ((krel // 256) * 256).astype(F32)
    lo = (krel % 256).astype(F32)
    kfeat = jnp.where(lane < 6, jnp.where(lane % 2 == 0, hi, lo), 0.0)
    qfeat = qf_ref[...]
    qparts, kparts = [], []
    for g in range(2 * DIFF_HEADS):
        qparts += [qn[:, g * DIFF_DH:(g + 1) * DIFF_DH],
                   jnp.broadcast_to(qfeat[:, g * DIFF_DH:(g + 1) * DIFF_DH], (tm, DIFF_DH))]
        kparts += [kn[:, g * DIFF_DH:(g + 1) * DIFF_DH], kfeat]
    q_ref[...] = jnp.concatenate(qparts, axis=-1).astype(BF16)
    k_ref[...] = jnp.concatenate(kparts, axis=-1).astype(BF16)
    vt_ref[0, 0] = x[:, 2 * DIFF_QK:].T.astype(BF16)


def _aprep(att_in, gq, gk, qfeat, bdm, nb, seq, tk):
    t = att_in.shape[0]
    tm = tk
    nk = seq // tk
    const = lambda i: (0, 0)
    return pl.pallas_call(
        functools.partial(_aprep_kernel, tk=tk),
        out_shape=(jax.ShapeDtypeStruct((t, 2 * DIFF_QK), BF16),
                   jax.ShapeDtypeStruct((t, 2 * DIFF_QK), BF16),
                   jax.ShapeDtypeStruct((nb, nk, DIFF_V, tk), BF16)),
        grid=(t // tm,),
        in_specs=[pl.BlockSpec((tm, att_in.shape[1]), lambda i: (i, 0)),
                  pl.BlockSpec(gq.shape, const), pl.BlockSpec(gk.shape, const),
                  pl.BlockSpec(qfeat.shape, const), pl.BlockSpec(bdm.shape, const)],
        out_specs=(pl.BlockSpec((tm, 2 * DIFF_QK), lambda i: (i, 0)),
                   pl.BlockSpec((tm, 2 * DIFF_QK), lambda i: (i, 0)),
                   pl.BlockSpec((1, 1, DIFF_V, tk), lambda i: (i // nk, i % nk, 0, 0))),
        compiler_params=_cparams(("arbitrary",)),
        name="aprep",
    )(att_in, gq, gk, qfeat, bdm)


def _attn_kernel(slope_ref, q_ref, k_ref, vt_ref, lamv_ref, sg_ref, o_ref, m_s, l_s, acc_s, sa, sb, ma, mb,
                 *, tq, lam_init):
    hp = pl.program_id(1)
    qi = pl.program_id(2)
    dv = 2 * DIFF_DH
    nstream = 2 * ATTN_HEADS_PER_STEP
    slope2 = [slope_ref[hp * ATTN_HEADS_PER_STEP + c // 2] for c in range(nstream)]
    qs = [q_ref[:, c * LANES:(c + 1) * LANES] for c in range(nstream)]

    m_s[...] = jnp.full_like(m_s, -jnp.inf)
    l_s[...] = jnp.zeros_like(l_s)
    acc_s[...] = jnp.zeros_like(acc_s)

    def scores(kj):
        k0 = pl.multiple_of(kj * tq, tq)
        return tuple(_dot_nt(k_ref[pl.ds(k0, tq), c * LANES:(c + 1) * LANES], qs[c]) for c in range(nstream))

    def accumulate(ss, mx, kj):
        rel = ((kj - qi) * tq).astype(F32)
        for c in range(nstream):
            hl = c // 2
            vt = vt_ref[0, kj, hl * dv:(hl + 1) * dv, :]
            off = slope2[c] * rel
            s = ss[c]
            m_old = m_s[c]
            m_new = jnp.maximum(m_old, mx[c] + off)
            alpha = jnp.exp2(m_old - m_new)
            p = jnp.exp2(s - (m_new - off))
            l_s[c] = alpha * l_s[c] + jnp.sum(p, axis=0, keepdims=True)
            acc_s[c] = alpha * acc_s[c] + _dot(vt, p.astype(BF16))
            m_s[c] = m_new

    kpos = lax.broadcasted_iota(jnp.int32, (tq, tq), 0)
    qpos = lax.broadcasted_iota(jnp.int32, (tq, tq), 1)
    allowed = (kpos // CHUNK) <= (qpos // CHUNK)
    fut = jnp.maximum(kpos - qpos, 0).astype(F32)
    adds = [jnp.where(allowed, (-2.0 * slope2[2 * hl]) * fut, -jnp.inf) for hl in range(ATTN_HEADS_PER_STEP)]
    def colmax(ss):
        return tuple(jnp.max(s, axis=0, keepdims=True) for s in ss)

    def put(slot, mslot, ss):
        for c in range(nstream):
            slot[c] = ss[c]
            mslot[c] = jnp.max(ss[c], axis=0, keepdims=True)

    def get(slot):
        return tuple(slot[c] for c in range(nstream))

    diag = tuple(s + adds[c // 2] for c, s in enumerate(scores(qi)))
    npairs = qi // 2
    put(sa, ma, scores(0))
    accumulate(diag, colmax(diag), qi)

    def pair(j, carry):
        a = 2 * j
        put(sb, mb, scores(a + 1))
        accumulate(get(sa), get(ma), a)
        put(sa, ma, scores(jnp.minimum(a + 2, qi - 1)))
        accumulate(get(sb), get(mb), a + 1)
        return carry
    lax.fori_loop(0, npairs, pair, 0)

    @pl.when(qi % 2 == 1)
    def _():
        accumulate(get(sa), get(ma), qi - 1)

    lq1, lk1, lq2, lk2 = (lamv_ref[i:i + 1, :] for i in range(4))
    lam = (jnp.exp(jnp.sum(lq1 * lk1, axis=-1, keepdims=True))
           - jnp.exp(jnp.sum(lq2 * lk2, axis=-1, keepdims=True)) + lam_init)
    for hl in range(ATTN_HEADS_PER_STEP):
        o = acc_s[2 * hl] / l_s[2 * hl] - lam * (acc_s[2 * hl + 1] / l_s[2 * hl + 1])
        on = o * lax.rsqrt(jnp.mean(o * o, axis=0, keepdims=True) + NORM_EPS) * sg_ref[...] * (1.0 - lam_init)
        o_ref[:, hl * dv:(hl + 1) * dv] = on.T


def _attn(slopes2, qa, ka, vt, lamv, sg_col, nb, seq, tq, lam_init):
    t = qa.shape[0]
    nq = seq // tq
    dv = 2 * DIFF_DH
    hps = ATTN_HEADS_PER_STEP
    nstream = 2 * hps
    grid_spec = pltpu.PrefetchScalarGridSpec(
        num_scalar_prefetch=1,
        grid=(nb, DIFF_HEADS // hps, nq),
        in_specs=[pl.BlockSpec((tq, nstream * LANES), lambda b, h, i, s: (b * nq + i, h)),
                  pl.BlockSpec((seq, nstream * LANES), lambda b, h, i, s: (b, h)),
                  pl.BlockSpec((1, nq, hps * dv, tq), lambda b, h, i, s: (b, 0, h, 0)),
                  pl.BlockSpec(lamv.shape, lambda b, h, i, s: (0, 0)),
                  pl.BlockSpec(sg_col.shape, lambda b, h, i, s: (0, 0))],
        out_specs=pl.BlockSpec((tq, hps * dv), lambda b, h, i, s: (b * nq + i, h)),
        scratch_shapes=[pltpu.VMEM((nstream, 1, tq), F32), pltpu.VMEM((nstream, 1, tq), F32),
                        pltpu.VMEM((nstream, dv, tq), F32),
                        pltpu.VMEM((nstream, tq, tq), F32), pltpu.VMEM((nstream, tq, tq), F32),
                        pltpu.VMEM((nstream, 1, tq), F32), pltpu.VMEM((nstream, 1, tq), F32)],
    )
    return pl.pallas_call(
        functools.partial(_attn_kernel, tq=tq, lam_init=lam_init),
        out_shape=jax.ShapeDtypeStruct((t, DIFF_V), F32),
        grid_spec=grid_spec,
        compiler_params=_cparams(("arbitrary", "arbitrary", "arbitrary")),
        name="attn",
    )(slopes2, qa, ka, vt, lamv, sg_col)


def _pack_halves(x):
    n = x.shape[1] // 2
    lo = pltpu.bitcast(x[:, :n].astype(BF16).astype(F32), jnp.uint32)
    hi = pltpu.bitcast(x[:, n:].astype(BF16).astype(F32), jnp.uint32)
    return (lo >> 16) | (hi & jnp.uint32(0xFFFF0000))


def _unpack_halves(p):
    lo = pltpu.bitcast(p << 16, F32)
    hi = pltpu.bitcast(p & jnp.uint32(0xFFFF0000), F32)
    return lo, hi


def _post_kernel(x_ref, oa_ref, ob_ref, gate_ref, g1_ref, sh2_ref, sc2_ref, n2_ref,
                 woa_ref, wob_ref, wout_ref, x1_ref, h2_ref, hp_ref):
    ya = _dot(oa_ref[...].astype(BF16), woa_ref[...])
    yb = _dot(ob_ref[...].astype(BF16), wob_ref[...])
    merged = (jax.nn.sigmoid(gate_ref[:, 0:D_MODEL].astype(F32)) * ya
              + jax.nn.sigmoid(gate_ref[:, D_MODEL:].astype(F32)) * yb)
    y = _dot(merged.astype(BF16), wout_ref[...])
    x1 = x_ref[...] + g1_ref[0] * y
    x1_ref[...] = x1
    n = x1 * lax.rsqrt(jnp.mean(x1 * x1, axis=-1, keepdims=True) + NORM_EPS) * n2_ref[...]
    h2 = n * (1.0 + sc2_ref[0]) + sh2_ref[0]
    h2_ref[...] = h2
    hp_ref[...] = _pack_halves(h2)


def _post(x2, oa, ob, gates, mod3, norm2_g, woa, wob, wout, seq):
    t = x2.shape[0]
    tm = min(512, seq)
    tpb = seq // tm
    const = lambda i: (0, 0)
    row = lambda n: pl.BlockSpec((tm, n), lambda i: (i, 0))
    modspec = lambda j: pl.BlockSpec((1, 1, D_MODEL), lambda i: (i // tpb, 0, j))
    wspec = lambda w: pl.BlockSpec(w.shape, const, pipeline_mode=pl.Buffered(1))
    return pl.pallas_call(
        _post_kernel,
        out_shape=(jax.ShapeDtypeStruct((t, D_MODEL), F32), jax.ShapeDtypeStruct((t, D_MODEL), F32),
                   jax.ShapeDtypeStruct((t, D_MODEL // 2), jnp.uint32)),
        grid=(t // tm,),
        in_specs=[row(D_MODEL), row(GDN_V), row(DIFF_V), row(2 * D_MODEL),
                  modspec(2), modspec(3), modspec(4), pl.BlockSpec((1, D_MODEL), const),
                  wspec(woa), wspec(wob), wspec(wout)],
        out_specs=(row(D_MODEL), row(D_MODEL), row(D_MODEL // 2)),
        compiler_params=_cparams(("arbitrary",)),
        name="post",
    )(x2, oa, ob, gates, mod3, mod3, mod3, norm2_g, woa, wob, wout)


def _router_kernel(h_ref, wrt_ref, rb_ref, idx_ref, w_ref, cnt_ref, cnt_s):
    step = pl.program_id(0)

    @pl.when(step == 0)
    def _():
        cnt_s[...] = jnp.zeros_like(cnt_s)

    tm = h_ref.shape[0]
    neg = -jnp.inf
    h = h_ref[...]
    h_hi = h.astype(BF16)
    h_lo = (h - h_hi.astype(F32)).astype(BF16)
    logits = _dot_nt(wrt_ref[0], h_hi) + _dot_nt(wrt_ref[0], h_lo) + _dot_nt(wrt_ref[1], h_hi)
    scores = jax.nn.sigmoid(logits)
    choice = scores + rb_ref[...]
    eid = lax.broadcasted_iota(jnp.int32, (N_EXPERTS, tm), 0).astype(F32)
    c3 = choice.reshape(N_GROUPS, GROUP_SIZE, tm)
    e3 = eid.reshape(N_GROUPS, GROUP_SIZE, tm)
    m1 = jnp.max(c3, axis=1, keepdims=True)
    i1 = jnp.min(jnp.where(c3 == m1, e3, float(N_EXPERTS)), axis=1, keepdims=True)
    m2 = jnp.max(jnp.where(e3 == i1, neg, c3), axis=1, keepdims=True)
    gscore = (m1 + m2).reshape(N_GROUPS, tm)
    gid = lax.broadcasted_iota(jnp.int32, (N_GROUPS, tm), 0).astype(F32)
    gsel = jnp.zeros((N_GROUPS, tm), F32)
    for _ in range(TOPK_GROUPS):
        m = jnp.max(gscore, axis=0, keepdims=True)
        g = jnp.min(jnp.where(gscore == m, gid, float(N_GROUPS)), axis=0, keepdims=True)
        hit = gid == g
        gsel = jnp.where(hit, 1.0, gsel)
        gscore = jnp.where(hit, neg, gscore)
    masked = jnp.where(gsel.reshape(N_GROUPS, 1, tm) > 0.0, c3, neg).reshape(N_EXPERTS, tm)
    idx_rows, w_rows = [], []
    sel = jnp.zeros((N_EXPERTS, tm), F32)
    for k in range(TOP_K):
        m = jnp.max(masked, axis=0, keepdims=True)
        i = jnp.min(jnp.where(masked == m, eid, float(N_EXPERTS)), axis=0, keepdims=True)
        hit = eid == i
        idx_rows.append(i)
        w_rows.append(jnp.sum(jnp.where(hit, scores, 0.0), axis=0, keepdims=True))
        sel = jnp.where(hit, 1.0, sel)
        masked = jnp.where(hit, neg, masked)
    ws = jnp.concatenate(w_rows, axis=0)
    idx_ref[...] = jnp.concatenate(idx_rows, axis=0).astype(jnp.int32)
    w_ref[...] = ws / jnp.sum(ws, axis=0, keepdims=True) * ROUTED_SCALE
    ones = jnp.ones((SUBLANES, tm), BF16)
    cnt_s[...] = cnt_s[...] + _dot_nt(ones, sel.astype(BF16))[0:1, :]
    cnt_ref[...] = cnt_s[...]


def _router(h2, w_router_t, rbias_col):
    t = h2.shape[0]
    tm = min(256, t)
    const = lambda i: (0, 0)
    return pl.pallas_call(
        _router_kernel,
        out_shape=(jax.ShapeDtypeStruct((TOP_K, t), jnp.int32),
                   jax.ShapeDtypeStruct((TOP_K, t), F32),
                   jax.ShapeDtypeStruct((1, N_EXPERTS), F32)),
        grid=(t // tm,),
        in_specs=[pl.BlockSpec((tm, D_MODEL), lambda i: (i, 0)),
                  pl.BlockSpec(w_router_t.shape, lambda i: (0, 0, 0)), pl.BlockSpec(rbias_col.shape, const)],
        out_specs=(pl.BlockSpec((TOP_K, tm), lambda i: (0, i)),
                   pl.BlockSpec((TOP_K, tm), lambda i: (0, i)),
                   pl.BlockSpec((1, N_EXPERTS), const)),
        scratch_shapes=[pltpu.VMEM((1, N_EXPERTS), F32)],
        compiler_params=_cparams(("arbitrary",)),
        name="router",
    )(h2, w_router_t, rbias_col)


def _rank_kernel(idx_ref, ps_ref, dest_ref, run_s):
    step = pl.program_id(0)

    @pl.when(step == 0)
    def _():
        run_s[...] = jnp.zeros_like(run_s)

    tm = idx_ref.shape[0]
    idx = idx_ref[...]
    lane = lax.broadcasted_iota(jnp.int32, (tm, N_EXPERTS), 1)
    gl = lax.broadcasted_iota(jnp.int32, (tm, LANES), 1)
    hits = [lane == idx[:, k:k + 1] for k in range(TOP_K)]
    sel = jnp.zeros((tm, N_EXPERTS), F32)
    for hit in hits:
        sel = jnp.where(hit, 1.0, sel)
    ri = lax.broadcasted_iota(jnp.int32, (tm, tm), 0)
    ci = lax.broadcasted_iota(jnp.int32, (tm, tm), 1)
    before = _dot((ri > ci).astype(BF16), sel.astype(BF16))
    base = before + run_s[...] + ps_ref[...]
    dest = jnp.zeros((tm, LANES), F32)
    for k, hit in enumerate(hits):
        dk = jnp.sum(jnp.where(hit, base, 0.0), axis=-1, keepdims=True)
        dest = jnp.where(gl == k, dk, dest)
    dest_ref[...] = dest.astype(jnp.int32)
    run_s[...] = run_s[...] + jnp.sum(sel, axis=0, keepdims=True)


def _rank(idx, pstart):
    t = idx.shape[0]
    tm = min(256, t)
    const = lambda i: (0, 0)
    return pl.pallas_call(
        _rank_kernel,
        out_shape=jax.ShapeDtypeStruct((t, LANES), jnp.int32),
        grid=(t // tm,),
        in_specs=[pl.BlockSpec((tm, LANES), lambda i: (i, 0)), pl.BlockSpec(pstart.shape, const)],
        out_specs=pl.BlockSpec((tm, LANES), lambda i: (i, 0)),
        scratch_shapes=[pltpu.VMEM((1, N_EXPERTS), F32)],
        compiler_params=_cparams(("arbitrary",)),
        name="rank",
    )(idx, pstart)


def _row_copy(src, dst, sem):
    return pltpu.make_async_copy(src, dst, sem)


def _dispatch_kernel(dest_ref, lastblk_ref, h_ref, xs_ref, zbuf, sem, zsem, *, tm):
    @pl.when(pl.program_id(0) == 0)
    def _():
        zbuf[...] = jnp.zeros_like(zbuf)

        def zcopy(e):
            r0 = pl.multiple_of(lastblk_ref[e] * EXPERT_BLOCK, EXPERT_BLOCK)
            return pltpu.make_async_copy(zbuf, xs_ref.at[pl.ds(r0, EXPERT_BLOCK), :], zsem)

        def zstart(e, c):
            zcopy(e).start()
            return c

        def zwait(e, c):
            zcopy(e).wait()
            return c

        lax.fori_loop(0, N_EXPERTS, zstart, 0)
        lax.fori_loop(0, N_EXPERTS, zwait, 0)

    def start_row(r, c):
        for k in range(TOP_K):
            d = dest_ref[r * TOP_K + k]
            _row_copy(h_ref.at[pl.ds(r, 1), :], xs_ref.at[pl.ds(d, 1), :], sem).start(priority=k % 2)
        return c

    lax.fori_loop(0, tm, start_row, 0)
    for k in range(TOP_K):
        _row_copy(h_ref, xs_ref.at[pl.ds(0, tm), :], sem).wait()


def _dispatch(dest_flat, last_blk, h2, n_rows):
    t, half = h2.shape
    tm = min(256, t)
    return pl.pallas_call(
        functools.partial(_dispatch_kernel, tm=tm),
        out_shape=jax.ShapeDtypeStruct((n_rows, half), h2.dtype),
        grid=(t // tm,),
        in_specs=[pl.BlockSpec((tm * TOP_K,), lambda i: (i,), memory_space=pltpu.SMEM),
                  pl.BlockSpec(last_blk.shape, lambda i: (0,), memory_space=pltpu.SMEM),
                  pl.BlockSpec((tm, half), lambda i: (i, 0))],
        out_specs=pl.BlockSpec(memory_space=pl.ANY),
        scratch_shapes=[pltpu.VMEM((EXPERT_BLOCK, half), h2.dtype), pltpu.SemaphoreType.DMA(()),
                        pltpu.SemaphoreType.DMA(())],
        compiler_params=_cparams(("arbitrary",)),
        name="dispatch",
    )(dest_flat, last_blk, h2)


SC_CORES = 2
SC_SUBCORES = 16
SC_LANES = 16
INVERT_CHUNK = 4096


def _invert(dest_flat, n_rows, n_tokens):
    n_assign = dest_flat.shape[0]
    workers = SC_CORES * SC_SUBCORES
    rpw = n_rows // workers
    assert rpw * workers == n_rows and rpw % SC_LANES == 0 and n_assign % INVERT_CHUNK == 0
    scratch_base = TOP_K * n_tokens
    log2_k = TOP_K.bit_length() - 1

    def body(dest_hbm, inv_hbm, loc, chunk_v):
        wid = lax.axis_index("s") * SC_CORES + lax.axis_index("c")
        lo = wid * rpw
        lane = lax.iota(jnp.int32, SC_LANES)

        def init(i, c):
            loc[pl.ds(i * SC_LANES, SC_LANES)] = scratch_base + lo + i * SC_LANES + lane
            return c
        lax.fori_loop(0, rpw // SC_LANES, init, 0)

        def do_chunk(ci, c):
            pltpu.sync_copy(dest_hbm.at[pl.ds(ci * INVERT_CHUNK, INVERT_CHUNK)], chunk_v)

            def inner(j, cc):
                rel = chunk_v[pl.ds(j * SC_LANES, SC_LANES)] - lo
                mine = (rel >= 0) & (rel < rpw)
                a = ci * INVERT_CHUNK + j * SC_LANES + lane
                val = (a & (TOP_K - 1)) * n_tokens + lax.shift_right_logical(a, log2_k)
                plsc.store_scatter(loc, [jnp.where(mine, rel, 0)], val, mask=mine)
                return cc
            lax.fori_loop(0, INVERT_CHUNK // SC_LANES, inner, 0)
            return c
        lax.fori_loop(0, n_assign // INVERT_CHUNK, do_chunk, 0)
        pltpu.sync_copy(loc, inv_hbm.at[pl.ds(lo, rpw)])

    mesh = plsc.VectorSubcoreMesh(core_axis_name="c", subcore_axis_name="s",
                                  num_cores=SC_CORES, num_subcores=SC_SUBCORES)
    return pl.kernel(body, out_type=jax.ShapeDtypeStruct((n_rows,), jnp.int32), mesh=mesh,
                     scratch_types=[pltpu.VMEM((rpw,), jnp.int32), pltpu.VMEM((INVERT_CHUNK,), jnp.int32)],
                     compiler_params=pltpu.CompilerParams(needs_layout_passes=False),
                     name="invert")(dest_flat)


EXPERT_RING = 5
EXPERT_AHEAD = EXPERT_RING - 2
WEIGHT_RING = 3


def _experts_kernel(start_ref, blke_ref, nelist_ref, nepos_ref, meta_ref, xs_ref, inv_ref, wgu_ref, wdn_ref, g_ref,
                    wgu_f, wdn_f, wgu_s, wdn_s, xbuf, ybuf, inv_s, sem_w, sem_in, sem_inv, sem_out,
                    *, spare_row0):
    half = D_MODEL // 2
    used = meta_ref[0]
    n_owner = meta_ref[1]

    def rows(g):
        return pl.ds(pl.multiple_of(g * EXPERT_BLOCK, EXPERT_BLOCK), EXPERT_BLOCK)

    def in_copies(g, slot):
        gc = jnp.minimum(g, used - 1)
        return (pltpu.make_async_copy(xs_ref.at[rows(gc), :], xbuf.at[slot], sem_in.at[slot]),
                pltpu.make_async_copy(inv_ref.at[pl.ds(gc, 1), :], inv_s.at[slot], sem_inv.at[slot]))

    def out_wait(slot):
        pltpu.make_async_copy(ybuf.at[slot], g_ref.at[pl.ds(0, EXPERT_BLOCK), :], sem_out.at[slot]).wait()

    def w_copies(pos):
        e = nelist_ref[pos]
        slot = pos % WEIGHT_RING
        return (pltpu.make_async_copy(wgu_ref.at[e], wgu_f.at[slot], sem_w.at[0, slot]),
                pltpu.make_async_copy(wdn_ref.at[e], wdn_f.at[slot], sem_w.at[1, slot]))

    def issue_rows(slot):
        for r in range(EXPERT_BLOCK):
            d = inv_s[slot, 0, r]
            _row_copy(ybuf.at[slot, pl.ds(r, 1), :], g_ref.at[pl.ds(d, 1), :], sem_out.at[slot]).start(priority=r % 2)

    for p0 in range(WEIGHT_RING - 1):
        @pl.when(p0 < n_owner)
        def _():
            for cp in w_copies(p0):
                cp.start()
    for g in range(EXPERT_AHEAD):
        for cp in in_copies(g, g % EXPERT_RING):
            cp.start()
    ybuf[EXPERT_RING - 1] = jnp.zeros((EXPERT_BLOCK, half), jnp.uint32)
    for r in range(EXPERT_BLOCK):
        inv_s[EXPERT_RING - 1, 0, r] = spare_row0 + r

    def block(g, s):
        e = blke_ref[g]

        @pl.when(g == start_ref[e])
        def _():
            pos = nepos_ref[e]
            for cp in w_copies(pos):
                cp.wait()

            @pl.when(pos + WEIGHT_RING - 1 < n_owner)
            def _():
                for cp in w_copies(pos + WEIGHT_RING - 1):
                    cp.start()

            wslot = pos % WEIGHT_RING
            wgu_s[...] = wgu_f[wslot].astype(BF16)
            wdn_s[...] = wdn_f[wslot].astype(BF16)

        for cp in in_copies(g, s):
            cp.wait()
        for cp in in_copies(g + EXPERT_AHEAD, (s + EXPERT_AHEAD) % EXPERT_RING):
            cp.start()

        @pl.when(g >= EXPERT_RING - 1)
        def _():
            out_wait(s)

        issue_rows((s + EXPERT_RING - 1) % EXPERT_RING)
        lo, hi = _unpack_halves(xbuf[s])
        gu = _dot(lo.astype(BF16), wgu_s[0:half, :]) + _dot(hi.astype(BF16), wgu_s[half:, :])
        act = _silu(gu[:, :EXPERT_FF]) * gu[:, EXPERT_FF:]
        ybuf[s] = _pack_halves(_dot(act.astype(BF16), wdn_s[...]))

    def group(gi, carry):
        for s in range(EXPERT_RING):
            g = gi * EXPERT_RING + s

            @pl.when(g < used)
            def _():
                block(g, s)
        return carry

    lax.fori_loop(0, (used + EXPERT_RING - 1) // EXPERT_RING, group, 0)
    issue_rows((used + EXPERT_RING - 1) % EXPERT_RING)

    for j in range(EXPERT_AHEAD):
        for cp in in_copies(used + j, (used + j) % EXPERT_RING):
            cp.wait()
    for j in range(EXPERT_RING):
        @pl.when(used > j)
        def _():
            out_wait((used - 1 - j) % EXPERT_RING)

    @pl.when(used < EXPERT_RING)
    def _():
        out_wait(EXPERT_RING - 1)


def _experts(blk_start, blk_e, owner_list, owner_pos, meta, xs, inv2, w_gu, w_dn, n_slots):
    half = xs.shape[1]
    grid_spec = pltpu.PrefetchScalarGridSpec(
        num_scalar_prefetch=5,
        grid=(1,),
        in_specs=[pl.BlockSpec(memory_space=pl.ANY), pl.BlockSpec(memory_space=pl.ANY),
                  pl.BlockSpec(memory_space=pl.ANY), pl.BlockSpec(memory_space=pl.ANY)],
        out_specs=pl.BlockSpec(memory_space=pl.ANY),
        scratch_shapes=[pltpu.VMEM((WEIGHT_RING, D_MODEL, 2 * EXPERT_FF), F32),
                        pltpu.VMEM((WEIGHT_RING, EXPERT_FF, D_MODEL), F32),
                        pltpu.VMEM((D_MODEL, 2 * EXPERT_FF), BF16), pltpu.VMEM((EXPERT_FF, D_MODEL), BF16),
                        pltpu.VMEM((EXPERT_RING, EXPERT_BLOCK, half), jnp.uint32),
                        pltpu.VMEM((EXPERT_RING, EXPERT_BLOCK, half), jnp.uint32),
                        pltpu.SMEM((EXPERT_RING, 1, EXPERT_BLOCK), jnp.int32),
                        pltpu.SemaphoreType.DMA((2, WEIGHT_RING)), pltpu.SemaphoreType.DMA((EXPERT_RING,)),
                        pltpu.SemaphoreType.DMA((EXPERT_RING,)), pltpu.SemaphoreType.DMA((EXPERT_RING,))],
    )
    return pl.pallas_call(
        functools.partial(_experts_kernel, spare_row0=n_slots - EXPERT_BLOCK),
        out_shape=jax.ShapeDtypeStruct((n_slots, half), jnp.uint32),
        grid_spec=grid_spec,
        compiler_params=_cparams(("arbitrary",)),
        name="experts",
    )(blk_start, blk_e, owner_list, owner_pos, meta, xs, inv2, w_gu, w_dn)


def _combine_kernel(x1_ref, h_ref, w_ref, g2_ref, wsgu_ref, wsdn_ref, buf, o_ref):
    su = _dot(h_ref[...].astype(BF16), wsgu_ref[...])
    y = _dot((_silu(su[:, :SHARED_FF]) * su[:, SHARED_FF:]).astype(BF16), wsdn_ref[...])
    w = w_ref[...]
    half = D_MODEL // 2
    ylo, yhi = y[:, :half], y[:, half:]
    for k in range(TOP_K):
        lo, hi = _unpack_halves(buf[k])
        wk = w[:, k:k + 1]
        ylo = ylo + wk * lo
        yhi = yhi + wk * hi
    g2 = g2_ref[0]
    o_ref[:, :half] = x1_ref[:, :half] + g2[:, :half] * ylo
    o_ref[:, half:] = x1_ref[:, half:] + g2[:, half:] * yhi


def _combine(x1, h2, wts, mod3, wsgu, wsdn, slots3, seq):
    t = x1.shape[0]
    tm = min(512, seq)
    tpb = seq // tm
    const = lambda i: (0, 0)
    row = lambda n: pl.BlockSpec((tm, n), lambda i: (i, 0))
    return pl.pallas_call(
        _combine_kernel,
        out_shape=jax.ShapeDtypeStruct((t, D_MODEL), F32),
        grid=(t // tm,),
        in_specs=[row(D_MODEL), row(D_MODEL), row(LANES),
                  pl.BlockSpec((1, 1, D_MODEL), lambda i: (i // tpb, 0, 5)),
                  pl.BlockSpec(wsgu.shape, const), pl.BlockSpec(wsdn.shape, const),
                  pl.BlockSpec((TOP_K, tm, D_MODEL // 2), lambda i: (0, i, 0))],
        out_specs=row(D_MODEL),
        compiler_params=_cparams(("arbitrary",)),
        name="combine",
    )(x1, h2, wts, mod3, wsgu, wsdn, slots3)


def _layer(x, c, layer, w_ada, b_ada, norm1_g, w_in, conv_w, a_log, dt_bias, gdn_norm_g, w_o_gdn,
           q_norm_g, k_norm_g, lambda_q1, lambda_k1, lambda_q2, lambda_k2, subln_g, w_o_diff,
           w_out, norm2_g, w_router, router_bias, w_exp_gate_up, w_exp_down,
           w_shared_gate_up, w_shared_down):
    nb, seq, d = x.shape
    t = nb * seq
    lam_init = 0.8 - 0.6 * math.exp(-0.3 * layer)
    x2 = x.reshape(t, d)

    c_pad = jnp.pad(c, ((0, SUBLANES - nb % SUBLANES if nb % SUBLANES else 0), (0, 0)))
    mod = _ada(c_pad, w_ada, b_ada.reshape(1, -1))[:nb]
    mod3 = mod.reshape(nb, 1, 6 * d)

    o_bd = 2 * GDN_QK + 2 * GDN_V
    o_att = o_bd + 2 * GDN_HEADS
    o_gate = o_att + 2 * DIFF_QK + DIFF_V
    w_gdn = w_in[:, :o_bd].astype(BF16)
    w_bd = jnp.pad(w_in[:, o_bd:o_att], ((0, 0), (0, LANES - 2 * GDN_HEADS))).astype(BF16)
    w_att = w_in[:, o_att:o_gate].astype(BF16)
    w_gate = w_in[:, o_gate:].astype(BF16)
    gdn_in, bd, att_in, gates = _inproj(x2, mod3, norm1_g.reshape(1, d), w_gdn, w_bd, w_att, w_gate, seq)

    aparams = jnp.zeros((SUBLANES, LANES), F32)
    aparams = aparams.at[0, GDN_HEADS:2 * GDN_HEADS].set(a_log).at[1, GDN_HEADS:2 * GDN_HEADS].set(dt_bias)
    oa = _gdn(gdn_in.reshape(nb, seq, -1), bd.reshape(nb, seq, LANES), conv_w, aparams,
              gdn_norm_g.reshape(1, GDN_DV)).reshape(t, GDN_V)

    tq = min(512, seq)
    slopes = [2.0 ** (-8.0 * (h + 1) / DIFF_HEADS) for h in range(DIFF_HEADS)]
    c3 = _bf16_split3(LOG2E)
    qfeat = np.zeros((1, 2 * DIFF_HEADS * DIFF_DH), np.float32)
    for g in range(2 * DIFF_HEADS):
        for j in range(6):
            qfeat[0, g * DIFF_DH + j] = slopes[g // 2] * c3[j // 2]
    grp = np.arange(DIFF_QK) // DIFF_DH
    bdm = jnp.asarray((grp[:, None] == grp[None, :]).astype(np.float32)).astype(BF16)
    qa, ka, vt = _aprep(att_in, jnp.tile(q_norm_g, 2 * DIFF_HEADS).reshape(1, -1),
                        jnp.tile(k_norm_g, 2 * DIFF_HEADS).reshape(1, -1), jnp.asarray(qfeat), bdm, nb, seq, tq)
    slopes2 = jnp.asarray([s * (c3[0] + c3[1] + c3[2]) for s in slopes], F32)
    lamv = jnp.zeros((SUBLANES, DIFF_DH), F32)
    lamv = lamv.at[0].set(lambda_q1).at[1].set(lambda_k1).at[2].set(lambda_q2).at[3].set(lambda_k2)
    ob = _attn(slopes2, qa, ka, vt, lamv, subln_g.reshape(-1, 1), nb, seq, tq, lam_init)

    x1, h2, hp = _post(x2, oa, ob, gates, mod3, norm2_g.reshape(1, d), w_o_gdn.astype(BF16),
                       w_o_diff.astype(BF16), w_out.astype(BF16), seq)

    wr_hi = w_router.T.astype(BF16)
    wr_lo = (w_router.T - wr_hi.astype(F32)).astype(BF16)
    idx_t, wts_t, counts = _router(h2, jnp.stack([wr_hi, wr_lo]), router_bias.reshape(-1, 1))
    idx = jnp.pad(idx_t.T, ((0, 0), (0, LANES - TOP_K)))
    wts = jnp.pad(wts_t.T, ((0, 0), (0, LANES - TOP_K)))
    cnt = counts[0].astype(jnp.int32)
    padded = (cnt + EXPERT_BLOCK - 1) // EXPERT_BLOCK * EXPERT_BLOCK
    pends = jnp.cumsum(padded)
    pstart = (pends - padded).astype(F32).reshape(1, -1)
    nblk = -(-(t * TOP_K) // EXPERT_BLOCK) + N_EXPERTS
    blk_start = ((pends - padded) // EXPERT_BLOCK).astype(jnp.int32)
    blk_count = (padded // EXPERT_BLOCK).astype(jnp.int32)
    dest = _rank(idx, pstart)
    dest_flat = dest[:, :TOP_K].reshape(-1)
    n_rows = nblk * EXPERT_BLOCK
    assert n_rows % t == 0
    owns = blk_count > 0
    idle_rank = jnp.cumsum(jnp.logical_not(owns).astype(jnp.int32)) - 1
    last_blk = jnp.where(owns, blk_start + blk_count - 1, nblk - 1 - idle_rank).astype(jnp.int32)
    xs = _dispatch(dest_flat, last_blk, hp, n_rows)
    inv = _invert(dest_flat, n_rows, t)
    assert t >= EXPERT_BLOCK
    blk_e = jnp.minimum(jnp.sum((pends[None, :] <= (jnp.arange(nblk, dtype=jnp.int32) * EXPERT_BLOCK)[:, None])
                                .astype(jnp.int32), axis=1), N_EXPERTS - 1).astype(jnp.int32)
    owner_list = jnp.argsort(jnp.logical_not(owns), stable=True).astype(jnp.int32)
    owner_pos = (jnp.cumsum(owns.astype(jnp.int32)) - 1).astype(jnp.int32)
    meta = jnp.stack([blk_start[-1] + blk_count[-1], jnp.sum(owns.astype(jnp.int32))]).astype(jnp.int32)
    slots = _experts(blk_start, blk_e, owner_list, owner_pos, meta, xs, inv.reshape(nblk, EXPERT_BLOCK),
                     w_exp_gate_up, w_exp_down, TOP_K * t + n_rows + t)
    out = _combine(x1, h2, wts, mod3, w_shared_gate_up.astype(BF16), w_shared_down.astype(BF16),
                   slots.reshape(-1, t, d // 2), seq)
    return out.reshape(nb, seq, d)


def kernel(x, c, w_ada, b_ada, norm1_g, w_in, conv_w, a_log, dt_bias, gdn_norm_g, w_o_gdn, q_norm_g, k_norm_g, lambda_q1, lambda_k1, lambda_q2, lambda_k2, subln_g, w_o_diff, w_out, norm2_g, w_router, router_bias, w_exp_gate_up, w_exp_down, w_shared_gate_up, w_shared_down):
    params = (w_ada, b_ada, norm1_g, w_in, conv_w, a_log, dt_bias, gdn_norm_g, w_o_gdn, q_norm_g,
              k_norm_g, lambda_q1, lambda_k1, lambda_q2, lambda_k2, subln_g, w_o_diff, w_out, norm2_g,
              w_router, router_bias, w_exp_gate_up, w_exp_down, w_shared_gate_up, w_shared_down)
    for layer in range(w_ada.shape[0]):
        x = _layer(x, c, layer, *(p[layer] for p in params))
    return x
```

```python
import functools
import math

import jax
import jax.numpy as jnp
import numpy as np
from jax import lax
from jax.experimental import pallas as pl
from jax.experimental.pallas import tpu as pltpu
from jax.experimental.pallas import tpu_sc as plsc

F32 = jnp.float32
BF16 = jnp.bfloat16
HIGHEST = lax.Precision.HIGHEST

D_MODEL = 1024
CHUNK = 64
GDN_HEADS = 4
GDN_DK = 128
GDN_DV = 128
GDN_CONV = 4
DIFF_HEADS = 4
DIFF_DH = 64
N_EXPERTS = 256
TOP_K = 8
N_GROUPS = 8
TOPK_GROUPS = 4
EXPERT_FF = 256
SHARED_FF = 256
ROUTED_SCALE = 2.5
NORM_EPS = 1e-6
GROUP_SIZE = N_EXPERTS // N_GROUPS

GDN_QK = GDN_HEADS * GDN_DK
GDN_V = GDN_HEADS * GDN_DV
CONV_CH = 2 * GDN_QK + GDN_V
DIFF_QK = DIFF_HEADS * 2 * DIFF_DH
DIFF_V = DIFF_HEADS * 2 * DIFF_DH

LANES = 128
SUBLANES = 8
EXPERT_BLOCK = 128
INV_BASE = 8
ATTN_HEADS_PER_STEP = 2
LOG2E = math.log2(math.e)
VMEM_LIMIT = 56 * 1024 * 1024


def _cparams(sem):
    return pltpu.CompilerParams(dimension_semantics=sem, vmem_limit_bytes=VMEM_LIMIT)


def _dot(a, b):
    return jnp.dot(a, b, preferred_element_type=F32)


def _dot_nt(a, b):
    return lax.dot_general(a, b, (((1,), (1,)), ((), ())), preferred_element_type=F32)


def _dot_tn(a, b):
    return lax.dot_general(a, b, (((0,), (0,)), ((), ())), preferred_element_type=F32)


def _silu(x):
    return x * jax.nn.sigmoid(x)


def _bf16_split3(x):
    rnd = lambda v: float(np.float32(v).astype(BF16).astype(np.float32))
    a = rnd(x)
    b = rnd(x - a)
    c = rnd(x - a - b)
    return a, b, c


def _ada_kernel(c_ref, w_ref, b_ref, o_ref):
    s = _silu(c_ref[...])
    o_ref[...] = jnp.dot(s, w_ref[...], precision=HIGHEST, preferred_element_type=F32) + b_ref[...]


def _ada(c_pad, w_ada, b_ada):
    n = w_ada.shape[1]
    tn = 1024
    return pl.pallas_call(
        _ada_kernel,
        out_shape=jax.ShapeDtypeStruct((c_pad.shape[0], n), F32),
        grid=(n // tn,),
        in_specs=[pl.BlockSpec(c_pad.shape, lambda j: (0, 0)),
                  pl.BlockSpec((D_MODEL, tn), lambda j: (0, j)),
                  pl.BlockSpec((1, tn), lambda j: (0, j))],
        out_specs=pl.BlockSpec((c_pad.shape[0], tn), lambda j: (0, j)),
        compiler_params=_cparams(("arbitrary",)),
        name="ada",
    )(c_pad, w_ada, b_ada)


def _inproj_kernel(x_ref, shift_ref, scale_ref, g_ref, wg_ref, wbd_ref, wa_ref, wgt_ref,
                   gdn_ref, bd_ref, att_ref, gate_ref):
    x = x_ref[...]
    y = x * lax.rsqrt(jnp.mean(x * x, axis=-1, keepdims=True) + NORM_EPS) * g_ref[...]
    h = (y * (1.0 + scale_ref[0]) + shift_ref[0]).astype(BF16)
    gdn_ref[...] = _dot(h, wg_ref[...]).astype(gdn_ref.dtype)
    bd_ref[...] = _dot(h, wbd_ref[...])
    att_ref[...] = _dot(h, wa_ref[...]).astype(att_ref.dtype)
    gate_ref[...] = _dot(h, wgt_ref[...]).astype(gate_ref.dtype)


def _inproj(x2, mod3, norm1_g, w_gdn, w_bd, w_att, w_gate, seq):
    t = x2.shape[0]
    tm = min(512, seq)
    tiles_per_b = seq // tm
    const = lambda i: (0, 0)
    wspec = lambda w: pl.BlockSpec(w.shape, const, pipeline_mode=pl.Buffered(1))
    row = lambda n: pl.BlockSpec((tm, n), lambda i: (i, 0))
    return pl.pallas_call(
        _inproj_kernel,
        out_shape=(jax.ShapeDtypeStruct((t, w_gdn.shape[1]), BF16),
                   jax.ShapeDtypeStruct((t, LANES), F32),
                   jax.ShapeDtypeStruct((t, w_att.shape[1]), BF16),
                   jax.ShapeDtypeStruct((t, w_gate.shape[1]), BF16)),
        grid=(t // tm,),
        in_specs=[row(D_MODEL),
                  pl.BlockSpec((1, 1, D_MODEL), lambda i: (i // tiles_per_b, 0, 0)),
                  pl.BlockSpec((1, 1, D_MODEL), lambda i: (i // tiles_per_b, 0, 1)),
                  pl.BlockSpec((1, D_MODEL), const),
                  wspec(w_gdn), wspec(w_bd), wspec(w_att), wspec(w_gate)],
        out_specs=(row(w_gdn.shape[1]), row(LANES), row(w_att.shape[1]), row(w_gate.shape[1])),
        compiler_params=_cparams(("arbitrary",)),
        name="inproj",
    )(x2, mod3, mod3, norm1_g, w_gdn, w_bd, w_att, w_gate)


def _gdn_kernel(x_ref, bd_ref, cw_ref, ap_ref, ng_ref, o_ref, cbuf, state, *, nb, lg):
    step = pl.program_id(0)
    nc = lg // CHUNK

    @pl.when(step == 0)
    def _():
        cbuf[...] = jnp.zeros_like(cbuf)
        state[...] = jnp.zeros_like(state)

    ri = lax.broadcasted_iota(jnp.int32, (CHUNK, CHUNK), 0)
    ci = lax.broadcasted_iota(jnp.int32, (CHUNK, CHUNK), 1)
    causal = ri >= ci
    strict = ri > ci
    eye = (ri == ci).astype(F32)
    rl = lax.broadcasted_iota(jnp.int32, (lg, lg), 0)
    cl = lax.broadcasted_iota(jnp.int32, (lg, lg), 1)
    blocktri = ((rl >= cl) & ((rl // CHUNK) == (cl // CHUNK))).astype(F32)
    cw = cw_ref[...]
    a_row = ap_ref[0:1, :]
    dt_row = ap_ref[1:2, :]
    ng = ng_ref[...]

    units = []
    for b in range(nb):
        cbuf[b, SUBLANES:SUBLANES + lg, :] = x_ref[b, :, 0:CONV_CH].astype(F32)
        acc = cw[GDN_CONV - 1:GDN_CONV, :] * cbuf[b, SUBLANES:SUBLANES + lg, :]
        for j in range(GDN_CONV - 1):
            off = SUBLANES - (GDN_CONV - 1) + j
            acc = acc + cw[j:j + 1, :] * cbuf[b, off:off + lg, :]
        cbuf[b, 0:SUBLANES, :] = cbuf[b, lg:lg + SUBLANES, :]
        qkv = _silu(acc)

        bd = bd_ref[b]
        beta_t = jax.nn.sigmoid(bd)
        g_t = -jnp.exp(a_row) * jax.nn.softplus(bd + dt_row)
        gcum = jnp.dot(blocktri, g_t, precision=HIGHEST, preferred_element_type=F32)
        gcum_t = gcum.T
        egcum = jnp.exp(gcum)

        for h in range(GDN_HEADS):
            qh = qkv[:, h * GDN_DK:(h + 1) * GDN_DK]
            kh = qkv[:, GDN_QK + h * GDN_DK:GDN_QK + (h + 1) * GDN_DK]
            vh = qkv[:, 2 * GDN_QK + h * GDN_DV:2 * GDN_QK + (h + 1) * GDN_DV]
            qh = qh * lax.rsqrt(jnp.sum(qh * qh, axis=-1, keepdims=True) + NORM_EPS) * (GDN_DK ** -0.5)
            kh = kh * lax.rsqrt(jnp.sum(kh * kh, axis=-1, keepdims=True) + NORM_EPS)
            for c in range(nc):
                r0, r1 = c * CHUNK, (c + 1) * CHUNK
                q, k, v = qh[r0:r1], kh[r0:r1], vh[r0:r1]
                gc = gcum[r0:r1, 4 + h:5 + h]
                gr = gcum_t[4 + h:5 + h, r0:r1]
                egc = egcum[r0:r1, 4 + h:5 + h]
                bcol = beta_t[r0:r1, h:h + 1]
                gl = gcum[r1 - 1:r1, 4 + h:5 + h]
                kb = k.astype(BF16)
                units.append(dict(
                    b=b, h=h, c=c, kb=kb, bcol=bcol,
                    qkb=jnp.concatenate([q.astype(BF16), kb], axis=0),
                    decay=jnp.exp(jnp.where(causal, gc - gr, -jnp.inf)),
                    rhs=jnp.concatenate([v * bcol, k * (bcol * egc)], axis=-1).astype(BF16),
                    qg=(q * egc).astype(BF16), egl=jnp.exp(gl),
                    kdec=(k * jnp.exp(gl - gc)).astype(BF16)))

    for u in units:
        u["kq"] = _dot_nt(u["qkb"], u["kb"])
    def same_block(size):
        return (ri // size) == (ci // size)
    for u in units:
        lower = jnp.where(strict, u["bcol"] * u["kq"][CHUNK:] * u["decay"], 0.0)
        u["lower"] = lower
        diag = jnp.where(same_block(INV_BASE), lower, 0.0)
        u["xinv"] = eye - diag
        u["pw"] = diag.astype(BF16)
    for r in range(INV_BASE.bit_length() - 2):
        for u in units:
            u["pw"] = _dot(u["pw"], u["pw"]).astype(BF16)
        for u in units:
            u["xinv"] = u["xinv"] + _dot(u["xinv"].astype(BF16), u["pw"])
    size = INV_BASE
    while size < CHUNK:
        pair_off = same_block(2 * size) & jnp.logical_not(same_block(size))
        for u in units:
            u["xb"] = u["xinv"].astype(BF16)
            u["cx"] = _dot(jnp.where(pair_off, u["lower"], 0.0).astype(BF16), u["xb"]).astype(BF16)
        for u in units:
            u["xinv"] = u["xinv"] - _dot(u["xb"], u["cx"])
        size *= 2
    for u in units:
        u["sol"] = _dot(u["xinv"].astype(BF16), u["rhs"])
        u["a_qk"] = (u["kq"][:CHUNK] * u["decay"]).astype(BF16)

    st = {(b, h): state[b, h] for b in range(nb) for h in range(GDN_HEADS)}
    for c in range(nc):
        cu = [u for u in units if u["c"] == c]
        for u in cu:
            stb = st[(u["b"], u["h"])].astype(BF16)
            u["ws"] = _dot(jnp.concatenate([u["sol"][:, GDN_DV:].astype(BF16), u["qg"]], axis=0), stb)
        for u in cu:
            u["vnb"] = (u["sol"][:, :GDN_DV] - u["ws"][:CHUNK]).astype(BF16)
        for u in cu:
            u["out"] = u["ws"][CHUNK:] + _dot(u["a_qk"], u["vnb"])
        for u in cu:
            key = (u["b"], u["h"])
            st[key] = st[key] * u["egl"] + _dot_tn(u["kdec"], u["vnb"])
        for u in cu:
            b, h = u["b"], u["h"]
            r0, r1 = c * CHUNK, (c + 1) * CHUNK
            out = u["out"]
            on = out * lax.rsqrt(jnp.mean(out * out, axis=-1, keepdims=True) + NORM_EPS) * ng
            zh = x_ref[b, r0:r1, CONV_CH + h * GDN_DV:CONV_CH + (h + 1) * GDN_DV].astype(F32)
            o_ref[b, r0:r1, h * GDN_DV:(h + 1) * GDN_DV] = on * _silu(zh)
    for (b, h), s in st.items():
        state[b, h] = s


def _gdn(gdn_in3, bd3, conv_w, aparams, gdn_norm_g):
    nb, seq, _ = gdn_in3.shape
    lg = min(128, seq)
    const = lambda i: (0, 0)
    return pl.pallas_call(
        functools.partial(_gdn_kernel, nb=nb, lg=lg),
        out_shape=jax.ShapeDtypeStruct((nb, seq, GDN_V), F32),
        grid=(seq // lg,),
        in_specs=[pl.BlockSpec((nb, lg, gdn_in3.shape[2]), lambda i: (0, i, 0)),
                  pl.BlockSpec((nb, lg, LANES), lambda i: (0, i, 0)),
                  pl.BlockSpec(conv_w.shape, const),
                  pl.BlockSpec(aparams.shape, const),
                  pl.BlockSpec(gdn_norm_g.shape, const)],
        out_specs=pl.BlockSpec((nb, lg, GDN_V), lambda i: (0, i, 0)),
        scratch_shapes=[pltpu.VMEM((nb, lg + SUBLANES, CONV_CH), F32),
                        pltpu.VMEM((nb, GDN_HEADS, GDN_DK, GDN_DV), F32)],
        compiler_params=_cparams(("arbitrary",)),
        name="gdn",
    )(gdn_in3, bd3, conv_w, aparams, gdn_norm_g)


def _aprep_kernel(x_ref, gq_ref, gk_ref, qf_ref, bdm_ref, q_ref, k_ref, vt_ref, *, tk):
    x = x_ref[...].astype(F32)
    tm = x.shape[0]
    bdm = bdm_ref[...]

    def qknorm(v, gain):
        sq = v * v
        hi = sq.astype(BF16)
        lo = (sq - hi.astype(F32)).astype(BF16)
        ms = (_dot(hi, bdm) + _dot(lo, bdm)) * (1.0 / DIFF_DH)
        return v * lax.rsqrt(ms + NORM_EPS) * gain

    qn = qknorm(x[:, 0:DIFF_QK], gq_ref[...]) * (DIFF_DH ** -0.5 * LOG2E)
    kn = qknorm(x[:, DIFF_QK:2 * DIFF_QK], gk_ref[...])
    pos = pl.program_id(0) * tm + lax.broadcasted_iota(jnp.int32, (tm, DIFF_DH), 0)
    krel = pos % tk
    lane = lax.broadcasted_iota(jnp.int32, (tm, DIFF_DH), 1)
    hi = ((krel // 256) * 256).astype(F32)
    lo = (krel % 256).astype(F32)
    kfeat = jnp.where(lane < 6, jnp.where(lane % 2 == 0, hi, lo), 0.0)
    qfeat = qf_ref[...]
    qparts, kparts = [], []
    for g in range(2 * DIFF_HEADS):
        qparts += [qn[:, g * DIFF_DH:(g + 1) * DIFF_DH],
                   jnp.broadcast_to(qfeat[:, g * DIFF_DH:(g + 1) * DIFF_DH], (tm, DIFF_DH))]
        kparts += [kn[:, g * DIFF_DH:(g + 1) * DIFF_DH], kfeat]
    q_ref[...] = jnp.concatenate(qparts, axis=-1).astype(BF16)
    k_ref[...] = jnp.concatenate(kparts, axis=-1).astype(BF16)
    vt_ref[0, 0] = x[:, 2 * DIFF_QK:].T.astype(BF16)


def _aprep(att_in, gq, gk, qfeat, bdm, nb, seq, tk):
    t = att_in.shape[0]
    tm = tk
    nk = seq // tk
    const = lambda i: (0, 0)
    return pl.pallas_call(
        functools.partial(_aprep_kernel, tk=tk),
        out_shape=(jax.ShapeDtypeStruct((t, 2 * DIFF_QK), BF16),
                   jax.ShapeDtypeStruct((t, 2 * DIFF_QK), BF16),
                   jax.ShapeDtypeStruct((nb, nk, DIFF_V, tk), BF16)),
        grid=(t // tm,),
        in_specs=[pl.BlockSpec((tm, att_in.shape[1]), lambda i: (i, 0)),
                  pl.BlockSpec(gq.shape, const), pl.BlockSpec(gk.shape, const),
                  pl.BlockSpec(qfeat.shape, const), pl.BlockSpec(bdm.shape, const)],
        out_specs=(pl.BlockSpec((tm, 2 * DIFF_QK), lambda i: (i, 0)),
                   pl.BlockSpec((tm, 2 * DIFF_QK), lambda i: (i, 0)),
                   pl.BlockSpec((1, 1, DIFF_V, tk), lambda i: (i // nk, i % nk, 0, 0))),
        compiler_params=_cparams(("arbitrary",)),
        name="aprep",
    )(att_in, gq, gk, qfeat, bdm)


def _attn_kernel(slope_ref, q_ref, k_ref, vt_ref, lamv_ref, sg_ref, o_ref, m_s, l_s, acc_s, sa, sb, ma, mb,
                 *, tq, lam_init):
    hp = pl.program_id(1)
    qi = pl.program_id(2)
    dv = 2 * DIFF_DH
    nstream = 2 * ATTN_HEADS_PER_STEP
    slope2 = [slope_ref[hp * ATTN_HEADS_PER_STEP + c // 2] for c in range(nstream)]
    qs = [q_ref[:, c * LANES:(c + 1) * LANES] for c in range(nstream)]

    m_s[...] = jnp.full_like(m_s, -jnp.inf)
    l_s[...] = jnp.zeros_like(l_s)
    acc_s[...] = jnp.zeros_like(acc_s)

    def scores(kj):
        k0 = pl.multiple_of(kj * tq, tq)
        return tuple(_dot_nt(k_ref[pl.ds(k0, tq), c * LANES:(c + 1) * LANES], qs[c]) for c in range(nstream))

    def accumulate(ss, mx, kj):
        rel = ((kj - qi) * tq).astype(F32)
        for c in range(nstream):
            hl = c // 2
            vt = vt_ref[0, kj, hl * dv:(hl + 1) * dv, :]
            off = slope2[c] * rel
            s = ss[c]
            m_old = m_s[c]
            m_new = jnp.maximum(m_old, mx[c] + off)
            alpha = jnp.exp2(m_old - m_new)
            p = jnp.exp2(s - (m_new - off))
            l_s[c] = alpha * l_s[c] + jnp.sum(p, axis=0, keepdims=True)
            acc_s[c] = alpha * acc_s[c] + _dot(vt, p.astype(BF16))
            m_s[c] = m_new

    kpos = lax.broadcasted_iota(jnp.int32, (tq, tq), 0)
    qpos = lax.broadcasted_iota(jnp.int32, (tq, tq), 1)
    allowed = (kpos // CHUNK) <= (qpos // CHUNK)
    fut = jnp.maximum(kpos - qpos, 0).astype(F32)
    adds = [jnp.where(allowed, (-2.0 * slope2[2 * hl]) * fut, -jnp.inf) for hl in range(ATTN_HEADS_PER_STEP)]
    def colmax(ss):
        return tuple(jnp.max(s, axis=0, keepdims=True) for s in ss)

    def put(slot, mslot, ss):
        for c in range(nstream):
            slot[c] = ss[c]
            mslot[c] = jnp.max(ss[c], axis=0, keepdims=True)

    def get(slot):
        return tuple(slot[c] for c in range(nstream))

    diag = tuple(s + adds[c // 2] for c, s in enumerate(scores(qi)))
    npairs = qi // 2
    put(sa, ma, scores(0))
    accumulate(diag, colmax(diag), qi)

    def pair(j, carry):
        a = 2 * j
        put(sb, mb, scores(a + 1))
        accumulate(get(sa), get(ma), a)
        put(sa, ma, scores(jnp.minimum(a + 2, qi - 1)))
        accumulate(get(sb), get(mb), a + 1)
        return carry
    lax.fori_loop(0, npairs, pair, 0)

    @pl.when(qi % 2 == 1)
    def _():
        accumulate(get(sa), get(ma), qi - 1)

    lq1, lk1, lq2, lk2 = (lamv_ref[i:i + 1, :] for i in range(4))
    lam = (jnp.exp(jnp.sum(lq1 * lk1, axis=-1, keepdims=True))
           - jnp.exp(jnp.sum(lq2 * lk2, axis=-1, keepdims=True)) + lam_init)
    for hl in range(ATTN_HEADS_PER_STEP):
        o = acc_s[2 * hl] / l_s[2 * hl] - lam * (acc_s[2 * hl + 1] / l_s[2 * hl + 1])
        on = o * lax.rsqrt(jnp.mean(o * o, axis=0, keepdims=True) + NORM_EPS) * sg_ref[...] * (1.0 - lam_init)
        o_ref[:, hl * dv:(hl + 1) * dv] = on.T


def _attn(slopes2, qa, ka, vt, lamv, sg_col, nb, seq, tq, lam_init):
    t = qa.shape[0]
    nq = seq // tq
    dv = 2 * DIFF_DH
    hps = ATTN_HEADS_PER_STEP
    nstream = 2 * hps
    grid_spec = pltpu.PrefetchScalarGridSpec(
        num_scalar_prefetch=1,
        grid=(nb, DIFF_HEADS // hps, nq),
        in_specs=[pl.BlockSpec((tq, nstream * LANES), lambda b, h, i, s: (b * nq + i, h)),
                  pl.BlockSpec((seq, nstream * LANES), lambda b, h, i, s: (b, h)),
                  pl.BlockSpec((1, nq, hps * dv, tq), lambda b, h, i, s: (b, 0, h, 0)),
                  pl.BlockSpec(lamv.shape, lambda b, h, i, s: (0, 0)),
                  pl.BlockSpec(sg_col.shape, lambda b, h, i, s: (0, 0))],
        out_specs=pl.BlockSpec((tq, hps * dv), lambda b, h, i, s: (b * nq + i, h)),
        scratch_shapes=[pltpu.VMEM((nstream, 1, tq), F32), pltpu.VMEM((nstream, 1, tq), F32),
                        pltpu.VMEM((nstream, dv, tq), F32),
                        pltpu.VMEM((nstream, tq, tq), F32), pltpu.VMEM((nstream, tq, tq), F32),
                        pltpu.VMEM((nstream, 1, tq), F32), pltpu.VMEM((nstream, 1, tq), F32)],
    )
    return pl.pallas_call(
        functools.partial(_attn_kernel, tq=tq, lam_init=lam_init),
        out_shape=jax.ShapeDtypeStruct((t, DIFF_V), F32),
        grid_spec=grid_spec,
        compiler_params=_cparams(("arbitrary", "arbitrary", "arbitrary")),
        name="attn",
    )(slopes2, qa, ka, vt, lamv, sg_col)


def _pack_halves(x):
    n = x.shape[1] // 2
    lo = pltpu.bitcast(x[:, :n].astype(BF16).astype(F32), jnp.uint32)
    hi = pltpu.bitcast(x[:, n:].astype(BF16).astype(F32), jnp.uint32)
    return (lo >> 16) | (hi & jnp.uint32(0xFFFF0000))


def _unpack_halves(p):
    lo = pltpu.bitcast(p << 16, F32)
    hi = pltpu.bitcast(p & jnp.uint32(0xFFFF0000), F32)
    return lo, hi


def _post_kernel(x_ref, oa_ref, ob_ref, gate_ref, g1_ref, sh2_ref, sc2_ref, n2_ref,
                 woa_ref, wob_ref, wout_ref, x1_ref, h2_ref, hp_ref):
    ya = _dot(oa_ref[...].astype(BF16), woa_ref[...])
    yb = _dot(ob_ref[...].astype(BF16), wob_ref[...])
    merged = (jax.nn.sigmoid(gate_ref[:, 0:D_MODEL].astype(F32)) * ya
              + jax.nn.sigmoid(gate_ref[:, D_MODEL:].astype(F32)) * yb)
    y = _dot(merged.astype(BF16), wout_ref[...])
    x1 = x_ref[...] + g1_ref[0] * y
    x1_ref[...] = x1
    n = x1 * lax.rsqrt(jnp.mean(x1 * x1, axis=-1, keepdims=True) + NORM_EPS) * n2_ref[...]
    h2 = n * (1.0 + sc2_ref[0]) + sh2_ref[0]
    h2_ref[...] = h2
    hp_ref[...] = _pack_halves(h2)


def _post(x2, oa, ob, gates, mod3, norm2_g, woa, wob, wout, seq):
    t = x2.shape[0]
    tm = min(512, seq)
    tpb = seq // tm
    const = lambda i: (0, 0)
    row = lambda n: pl.BlockSpec((tm, n), lambda i: (i, 0))
    modspec = lambda j: pl.BlockSpec((1, 1, D_MODEL), lambda i: (i // tpb, 0, j))
    wspec = lambda w: pl.BlockSpec(w.shape, const, pipeline_mode=pl.Buffered(1))
    return pl.pallas_call(
        _post_kernel,
        out_shape=(jax.ShapeDtypeStruct((t, D_MODEL), F32), jax.ShapeDtypeStruct((t, D_MODEL), F32),
                   jax.ShapeDtypeStruct((t, D_MODEL // 2), jnp.uint32)),
        grid=(t // tm,),
        in_specs=[row(D_MODEL), row(GDN_V), row(DIFF_V), row(2 * D_MODEL),
                  modspec(2), modspec(3), modspec(4), pl.BlockSpec((1, D_MODEL), const),
                  wspec(woa), wspec(wob), wspec(wout)],
        out_specs=(row(D_MODEL), row(D_MODEL), row(D_MODEL // 2)),
        compiler_params=_cparams(("arbitrary",)),
        name="post",
    )(x2, oa, ob, gates, mod3, mod3, mod3, norm2_g, woa, wob, wout)


def _router_kernel(h_ref, wrt_ref, rb_ref, idx_ref, w_ref, cnt_ref, cnt_s):
    step = pl.program_id(0)

    @pl.when(step == 0)
    def _():
        cnt_s[...] = jnp.zeros_like(cnt_s)

    tm = h_ref.shape[0]
    neg = -jnp.inf
    h = h_ref[...]
    h_hi = h.astype(BF16)
    h_lo = (h - h_hi.astype(F32)).astype(BF16)
    logits = _dot_nt(wrt_ref[0], h_hi) + _dot_nt(wrt_ref[0], h_lo) + _dot_nt(wrt_ref[1], h_hi)
    scores = jax.nn.sigmoid(logits)
    choice = scores + rb_ref[...]
    eid = lax.broadcasted_iota(jnp.int32, (N_EXPERTS, tm), 0).astype(F32)
    c3 = choice.reshape(N_GROUPS, GROUP_SIZE, tm)
    e3 = eid.reshape(N_GROUPS, GROUP_SIZE, tm)
    m1 = jnp.max(c3, axis=1, keepdims=True)
    i1 = jnp.min(jnp.where(c3 == m1, e3, float(N_EXPERTS)), axis=1, keepdims=True)
    m2 = jnp.max(jnp.where(e3 == i1, neg, c3), axis=1, keepdims=True)
    gscore = (m1 + m2).reshape(N_GROUPS, tm)
    gid = lax.broadcasted_iota(jnp.int32, (N_GROUPS, tm), 0).astype(F32)
    gsel = jnp.zeros((N_GROUPS, tm), F32)
    for _ in range(TOPK_GROUPS):
        m = jnp.max(gscore, axis=0, keepdims=True)
        g = jnp.min(jnp.where(gscore == m, gid, float(N_GROUPS)), axis=0, keepdims=True)
        hit = gid == g
        gsel = jnp.where(hit, 1.0, gsel)
        gscore = jnp.where(hit, neg, gscore)
    masked = jnp.where(gsel.reshape(N_GROUPS, 1, tm) > 0.0, c3, neg).reshape(N_EXPERTS, tm)
    idx_rows, w_rows = [], []
    sel = jnp.zeros((N_EXPERTS, tm), F32)
    for k in range(TOP_K):
        m = jnp.max(masked, axis=0, keepdims=True)
        i = jnp.min(jnp.where(masked == m, eid, float(N_EXPERTS)), axis=0, keepdims=True)
        hit = eid == i
        idx_rows.append(i)
        w_rows.append(jnp.sum(jnp.where(hit, scores, 0.0), axis=0, keepdims=True))
        sel = jnp.where(hit, 1.0, sel)
        masked = jnp.where(hit, neg, masked)
    ws = jnp.concatenate(w_rows, axis=0)
    idx_ref[...] = jnp.concatenate(idx_rows, axis=0).astype(jnp.int32)
    w_ref[...] = ws / jnp.sum(ws, axis=0, keepdims=True) * ROUTED_SCALE
    ones = jnp.ones((SUBLANES, tm), BF16)
    cnt_s[...] = cnt_s[...] + _dot_nt(ones, sel.astype(BF16))[0:1, :]
    cnt_ref[...] = cnt_s[...]


def _router(h2, w_router_t, rbias_col):
    t = h2.shape[0]
    tm = min(256, t)
    const = lambda i: (0, 0)
    return pl.pallas_call(
        _router_kernel,
        out_shape=(jax.ShapeDtypeStruct((TOP_K, t), jnp.int32),
                   jax.ShapeDtypeStruct((TOP_K, t), F32),
                   jax.ShapeDtypeStruct((1, N_EXPERTS), F32)),
        grid=(t // tm,),
        in_specs=[pl.BlockSpec((tm, D_MODEL), lambda i: (i, 0)),
                  pl.BlockSpec(w_router_t.shape, lambda i: (0, 0, 0)), pl.BlockSpec(rbias_col.shape, const)],
        out_specs=(pl.BlockSpec((TOP_K, tm), lambda i: (0, i)),
                   pl.BlockSpec((TOP_K, tm), lambda i: (0, i)),
                   pl.BlockSpec((1, N_EXPERTS), const)),
        scratch_shapes=[pltpu.VMEM((1, N_EXPERTS), F32)],
        compiler_params=_cparams(("arbitrary",)),
        name="router",
    )(h2, w_router_t, rbias_col)


def _rank_kernel(idx_ref, ps_ref, dest_ref, run_s):
    step = pl.program_id(0)

    @pl.when(step == 0)
    def _():
        run_s[...] = jnp.zeros_like(run_s)

    tm = idx_ref.shape[0]
    idx = idx_ref[...]
    lane = lax.broadcasted_iota(jnp.int32, (tm, N_EXPERTS), 1)
    gl = lax.broadcasted_iota(jnp.int32, (tm, LANES), 1)
    hits = [lane == idx[:, k:k + 1] for k in range(TOP_K)]
    sel = jnp.zeros((tm, N_EXPERTS), F32)
    for hit in hits:
        sel = jnp.where(hit, 1.0, sel)
    ri = lax.broadcasted_iota(jnp.int32, (tm, tm), 0)
    ci = lax.broadcasted_iota(jnp.int32, (tm, tm), 1)
    before = _dot((ri > ci).astype(BF16), sel.astype(BF16))
    base = before + run_s[...] + ps_ref[...]
    dest = jnp.zeros((tm, LANES), F32)
    for k, hit in enumerate(hits):
        dk = jnp.sum(jnp.where(hit, base, 0.0), axis=-1, keepdims=True)
        dest = jnp.where(gl == k, dk, dest)
    dest_ref[...] = dest.astype(jnp.int32)
    run_s[...] = run_s[...] + jnp.sum(sel, axis=0, keepdims=True)


def _rank(idx, pstart):
    t = idx.shape[0]
    tm = min(256, t)
    const = lambda i: (0, 0)
    return pl.pallas_call(
        _rank_kernel,
        out_shape=jax.ShapeDtypeStruct((t, LANES), jnp.int32),
        grid=(t // tm,),
        in_specs=[pl.BlockSpec((tm, LANES), lambda i: (i, 0)), pl.BlockSpec(pstart.shape, const)],
        out_specs=pl.BlockSpec((tm, LANES), lambda i: (i, 0)),
        scratch_shapes=[pltpu.VMEM((1, N_EXPERTS), F32)],
        compiler_params=_cparams(("arbitrary",)),
        name="rank",
    )(idx, pstart)


def _row_copy(src, dst, sem):
    return pltpu.make_async_copy(src, dst, sem)


def _dispatch_kernel(dest_ref, zblk_ref, nz_ref, h_ref, xs_ref, zbuf, sem, zsem, *, tm):
    @pl.when(pl.program_id(0) == 0)
    def _():
        zbuf[...] = jnp.zeros_like(zbuf)

        def zcopy(j):
            r0 = pl.multiple_of(zblk_ref[j] * EXPERT_BLOCK, EXPERT_BLOCK)
            return pltpu.make_async_copy(zbuf, xs_ref.at[pl.ds(r0, EXPERT_BLOCK), :], zsem)

        def zstart(j, c):
            zcopy(j).start()
            return c

        def zwait(j, c):
            zcopy(j).wait()
            return c

        lax.fori_loop(0, nz_ref[0], zstart, 0)
        lax.fori_loop(0, nz_ref[0], zwait, 0)

    def start_row(r, c):
        for k in range(TOP_K):
            d = dest_ref[r * TOP_K + k]
            _row_copy(h_ref.at[pl.ds(r, 1), :], xs_ref.at[pl.ds(d, 1), :], sem).start(priority=k % 2)
        return c

    lax.fori_loop(0, tm, start_row, 0)
    for k in range(TOP_K):
        _row_copy(h_ref, xs_ref.at[pl.ds(0, tm), :], sem).wait()


def _dispatch(dest_flat, zero_blk, n_zero, h2, n_rows):
    t, half = h2.shape
    tm = min(256, t)
    return pl.pallas_call(
        functools.partial(_dispatch_kernel, tm=tm),
        out_shape=jax.ShapeDtypeStruct((n_rows, half), h2.dtype),
        grid=(t // tm,),
        in_specs=[pl.BlockSpec((tm * TOP_K,), lambda i: (i,), memory_space=pltpu.SMEM),
                  pl.BlockSpec(zero_blk.shape, lambda i: (0,), memory_space=pltpu.SMEM),
                  pl.BlockSpec(n_zero.shape, lambda i: (0,), memory_space=pltpu.SMEM),
                  pl.BlockSpec((tm, half), lambda i: (i, 0))],
        out_specs=pl.BlockSpec(memory_space=pl.ANY),
        scratch_shapes=[pltpu.VMEM((EXPERT_BLOCK, half), h2.dtype), pltpu.SemaphoreType.DMA(()),
                        pltpu.SemaphoreType.DMA(())],
        compiler_params=_cparams(("arbitrary",)),
        name="dispatch",
    )(dest_flat, zero_blk, n_zero, h2)


SC_CORES = 2
SC_SUBCORES = 16
SC_LANES = 16
INVERT_CHUNK = 4096


def _invert(dest_flat, n_rows, n_tokens):
    n_assign = dest_flat.shape[0]
    workers = SC_CORES * SC_SUBCORES
    rpw = n_rows // workers
    assert rpw * workers == n_rows and rpw % SC_LANES == 0 and n_assign % INVERT_CHUNK == 0
    scratch_base = TOP_K * n_tokens
    log2_k = TOP_K.bit_length() - 1

    def body(dest_hbm, inv_hbm, loc, chunk_v):
        wid = lax.axis_index("s") * SC_CORES + lax.axis_index("c")
        lo = wid * rpw
        lane = lax.iota(jnp.int32, SC_LANES)

        def init(i, c):
            loc[pl.ds(i * SC_LANES, SC_LANES)] = scratch_base + lo + i * SC_LANES + lane
            return c
        lax.fori_loop(0, rpw // SC_LANES, init, 0)

        def do_chunk(ci, c):
            pltpu.sync_copy(dest_hbm.at[pl.ds(ci * INVERT_CHUNK, INVERT_CHUNK)], chunk_v)

            def inner(j, cc):
                rel = chunk_v[pl.ds(j * SC_LANES, SC_LANES)] - lo
                mine = (rel >= 0) & (rel < rpw)
                a = ci * INVERT_CHUNK + j * SC_LANES + lane
                val = (a & (TOP_K - 1)) * n_tokens + lax.shift_right_logical(a, log2_k)
                plsc.store_scatter(loc, [jnp.where(mine, rel, 0)], val, mask=mine)
                return cc
            lax.fori_loop(0, INVERT_CHUNK // SC_LANES, inner, 0)
            return c
        lax.fori_loop(0, n_assign // INVERT_CHUNK, do_chunk, 0)
        pltpu.sync_copy(loc, inv_hbm.at[pl.ds(lo, rpw)])

    mesh = plsc.VectorSubcoreMesh(core_axis_name="c", subcore_axis_name="s",
                                  num_cores=SC_CORES, num_subcores=SC_SUBCORES)
    return pl.kernel(body, out_type=jax.ShapeDtypeStruct((n_rows,), jnp.int32), mesh=mesh,
                     scratch_types=[pltpu.VMEM((rpw,), jnp.int32), pltpu.VMEM((INVERT_CHUNK,), jnp.int32)],
                     compiler_params=pltpu.CompilerParams(needs_layout_passes=False),
                     name="invert")(dest_flat)


EXPERT_RING = 5
EXPERT_AHEAD = EXPERT_RING - 2
WEIGHT_RING = 3


def _experts_kernel(start_ref, blke_ref, nelist_ref, nepos_ref, meta_ref, xs_ref, inv_ref, wgu_ref, wdn_ref, g_ref,
                    wgu_f, wdn_f, wgu_s, wdn_s, xbuf, ybuf, inv_s, sem_w, sem_in, sem_inv, sem_out,
                    *, spare_row0):
    half = D_MODEL // 2
    used = meta_ref[0]
    n_owner = meta_ref[1]

    def rows(g):
        return pl.ds(pl.multiple_of(g * EXPERT_BLOCK, EXPERT_BLOCK), EXPERT_BLOCK)

    def in_copies(g, slot):
        gc = jnp.minimum(g, used - 1)
        return (pltpu.make_async_copy(xs_ref.at[rows(gc), :], xbuf.at[slot], sem_in.at[slot]),
                pltpu.make_async_copy(inv_ref.at[pl.ds(gc, 1), :], inv_s.at[slot], sem_inv.at[slot]))

    def out_wait(slot):
        pltpu.make_async_copy(ybuf.at[slot], g_ref.at[pl.ds(0, EXPERT_BLOCK), :], sem_out.at[slot]).wait()

    def w_copies(pos):
        e = nelist_ref[pos]
        slot = pos % WEIGHT_RING
        return (pltpu.make_async_copy(wgu_ref.at[e], wgu_f.at[slot], sem_w.at[0, slot]),
                pltpu.make_async_copy(wdn_ref.at[e], wdn_f.at[slot], sem_w.at[1, slot]))

    def issue_rows(slot):
        for r in range(EXPERT_BLOCK):
            d = inv_s[slot, 0, r]
            _row_copy(ybuf.at[slot, pl.ds(r, 1), :], g_ref.at[pl.ds(d, 1), :], sem_out.at[slot]).start(priority=r % 2)

    for p0 in range(WEIGHT_RING - 1):
        @pl.when(p0 < n_owner)
        def _():
            for cp in w_copies(p0):
                cp.start()
    for g in range(EXPERT_AHEAD):
        for cp in in_copies(g, g % EXPERT_RING):
            cp.start()
    ybuf[EXPERT_RING - 1] = jnp.zeros((EXPERT_BLOCK, half), jnp.uint32)
    for r in range(EXPERT_BLOCK):
        inv_s[EXPERT_RING - 1, 0, r] = spare_row0 + r

    def block(g, s):
        e = blke_ref[g]

        @pl.when(g == start_ref[e])
        def _():
            pos = nepos_ref[e]
            for cp in w_copies(pos):
                cp.wait()

            @pl.when(pos + WEIGHT_RING - 1 < n_owner)
            def _():
                for cp in w_copies(pos + WEIGHT_RING - 1):
                    cp.start()

            wslot = pos % WEIGHT_RING
            wgu_s[...] = wgu_f[wslot].astype(BF16)
            wdn_s[...] = wdn_f[wslot].astype(BF16)

        for cp in in_copies(g, s):
            cp.wait()
        for cp in in_copies(g + EXPERT_AHEAD, (s + EXPERT_AHEAD) % EXPERT_RING):
            cp.start()

        @pl.when(g >= EXPERT_RING - 1)
        def _():
            out_wait(s)

        issue_rows((s + EXPERT_RING - 1) % EXPERT_RING)
        lo, hi = _unpack_halves(xbuf[s])
        gu = _dot(lo.astype(BF16), wgu_s[0:half, :]) + _dot(hi.astype(BF16), wgu_s[half:, :])
        act = _silu(gu[:, :EXPERT_FF]) * gu[:, EXPERT_FF:]
        ybuf[s] = _pack_halves(_dot(act.astype(BF16), wdn_s[...]))

    def group(gi, carry):
        for s in range(EXPERT_RING):
            g = gi * EXPERT_RING + s

            @pl.when(g < used)
            def _():
                block(g, s)
        return carry

    lax.fori_loop(0, (used + EXPERT_RING - 1) // EXPERT_RING, group, 0)
    issue_rows((used + EXPERT_RING - 1) % EXPERT_RING)

    for j in range(EXPERT_AHEAD):
        for cp in in_copies(used + j, (used + j) % EXPERT_RING):
            cp.wait()
    for j in range(EXPERT_RING):
        @pl.when(used > j)
        def _():
            out_wait((used - 1 - j) % EXPERT_RING)

    @pl.when(used < EXPERT_RING)
    def _():
        out_wait(EXPERT_RING - 1)


def _experts(blk_start, blk_e, owner_list, owner_pos, meta, xs, inv2, w_gu, w_dn, n_slots):
    half = xs.shape[1]
    grid_spec = pltpu.PrefetchScalarGridSpec(
        num_scalar_prefetch=5,
        grid=(1,),
        in_specs=[pl.BlockSpec(memory_space=pl.ANY), pl.BlockSpec(memory_space=pl.ANY),
                  pl.BlockSpec(memory_space=pl.ANY), pl.BlockSpec(memory_space=pl.ANY)],
        out_specs=pl.BlockSpec(memory_space=pl.ANY),
        scratch_shapes=[pltpu.VMEM((WEIGHT_RING, D_MODEL, 2 * EXPERT_FF), F32),
                        pltpu.VMEM((WEIGHT_RING, EXPERT_FF, D_MODEL), F32),
                        pltpu.VMEM((D_MODEL, 2 * EXPERT_FF), BF16), pltpu.VMEM((EXPERT_FF, D_MODEL), BF16),
                        pltpu.VMEM((EXPERT_RING, EXPERT_BLOCK, half), jnp.uint32),
                        pltpu.VMEM((EXPERT_RING, EXPERT_BLOCK, half), jnp.uint32),
                        pltpu.SMEM((EXPERT_RING, 1, EXPERT_BLOCK), jnp.int32),
                        pltpu.SemaphoreType.DMA((2, WEIGHT_RING)), pltpu.SemaphoreType.DMA((EXPERT_RING,)),
                        pltpu.SemaphoreType.DMA((EXPERT_RING,)), pltpu.SemaphoreType.DMA((EXPERT_RING,))],
    )
    return pl.pallas_call(
        functools.partial(_experts_kernel, spare_row0=n_slots - EXPERT_BLOCK),
        out_shape=jax.ShapeDtypeStruct((n_slots, half), jnp.uint32),
        grid_spec=grid_spec,
        compiler_params=_cparams(("arbitrary",)),
        name="experts",
    )(blk_start, blk_e, owner_list, owner_pos, meta, xs, inv2, w_gu, w_dn)


def _combine_kernel(x1_ref, h_ref, w_ref, g2_ref, wsgu_ref, wsdn_ref, buf, o_ref):
    su = _dot(h_ref[...].astype(BF16), wsgu_ref[...])
    y = _dot((_silu(su[:, :SHARED_FF]) * su[:, SHARED_FF:]).astype(BF16), wsdn_ref[...])
    w = w_ref[...]
    half = D_MODEL // 2
    ylo, yhi = y[:, :half], y[:, half:]
    for k in range(TOP_K):
        lo, hi = _unpack_halves(buf[k])
        wk = w[:, k:k + 1]
        ylo = ylo + wk * lo
        yhi = yhi + wk * hi
    g2 = g2_ref[0]
    o_ref[:, :half] = x1_ref[:, :half] + g2[:, :half] * ylo
    o_ref[:, half:] = x1_ref[:, half:] + g2[:, half:] * yhi


def _combine(x1, h2, wts, mod3, wsgu, wsdn, slots3, seq):
    t = x1.shape[0]
    tm = min(512, seq)
    tpb = seq // tm
    const = lambda i: (0, 0)
    row = lambda n: pl.BlockSpec((tm, n), lambda i: (i, 0))
    return pl.pallas_call(
        _combine_kernel,
        out_shape=jax.ShapeDtypeStruct((t, D_MODEL), F32),
        grid=(t // tm,),
        in_specs=[row(D_MODEL), row(D_MODEL), row(LANES),
                  pl.BlockSpec((1, 1, D_MODEL), lambda i: (i // tpb, 0, 5)),
                  pl.BlockSpec(wsgu.shape, const), pl.BlockSpec(wsdn.shape, const),
                  pl.BlockSpec((TOP_K, tm, D_MODEL // 2), lambda i: (0, i, 0))],
        out_specs=row(D_MODEL),
        compiler_params=_cparams(("arbitrary",)),
        name="combine",
    )(x1, h2, wts, mod3, wsgu, wsdn, slots3)


def _layer(x, c, layer, w_ada, b_ada, norm1_g, w_in, conv_w, a_log, dt_bias, gdn_norm_g, w_o_gdn,
           q_norm_g, k_norm_g, lambda_q1, lambda_k1, lambda_q2, lambda_k2, subln_g, w_o_diff,
           w_out, norm2_g, w_router, router_bias, w_exp_gate_up, w_exp_down,
           w_shared_gate_up, w_shared_down):
    nb, seq, d = x.shape
    t = nb * seq
    lam_init = 0.8 - 0.6 * math.exp(-0.3 * layer)
    x2 = x.reshape(t, d)

    c_pad = jnp.pad(c, ((0, SUBLANES - nb % SUBLANES if nb % SUBLANES else 0), (0, 0)))
    mod = _ada(c_pad, w_ada, b_ada.reshape(1, -1))[:nb]
    mod3 = mod.reshape(nb, 1, 6 * d)

    o_bd = 2 * GDN_QK + 2 * GDN_V
    o_att = o_bd + 2 * GDN_HEADS
    o_gate = o_att + 2 * DIFF_QK + DIFF_V
    w_gdn = w_in[:, :o_bd].astype(BF16)
    w_bd = jnp.pad(w_in[:, o_bd:o_att], ((0, 0), (0, LANES - 2 * GDN_HEADS))).astype(BF16)
    w_att = w_in[:, o_att:o_gate].astype(BF16)
    w_gate = w_in[:, o_gate:].astype(BF16)
    gdn_in, bd, att_in, gates = _inproj(x2, mod3, norm1_g.reshape(1, d), w_gdn, w_bd, w_att, w_gate, seq)

    aparams = jnp.zeros((SUBLANES, LANES), F32)
    aparams = aparams.at[0, GDN_HEADS:2 * GDN_HEADS].set(a_log).at[1, GDN_HEADS:2 * GDN_HEADS].set(dt_bias)
    oa = _gdn(gdn_in.reshape(nb, seq, -1), bd.reshape(nb, seq, LANES), conv_w, aparams,
              gdn_norm_g.reshape(1, GDN_DV)).reshape(t, GDN_V)

    tq = min(512, seq)
    slopes = [2.0 ** (-8.0 * (h + 1) / DIFF_HEADS) for h in range(DIFF_HEADS)]
    c3 = _bf16_split3(LOG2E)
    qfeat = np.zeros((1, 2 * DIFF_HEADS * DIFF_DH), np.float32)
    for g in range(2 * DIFF_HEADS):
        for j in range(6):
            qfeat[0, g * DIFF_DH + j] = slopes[g // 2] * c3[j // 2]
    grp = np.arange(DIFF_QK) // DIFF_DH
    bdm = jnp.asarray((grp[:, None] == grp[None, :]).astype(np.float32)).astype(BF16)
    qa, ka, vt = _aprep(att_in, jnp.tile(q_norm_g, 2 * DIFF_HEADS).reshape(1, -1),
                        jnp.tile(k_norm_g, 2 * DIFF_HEADS).reshape(1, -1), jnp.asarray(qfeat), bdm, nb, seq, tq)
    slopes2 = jnp.asarray([s * (c3[0] + c3[1] + c3[2]) for s in slopes], F32)
    lamv = jnp.zeros((SUBLANES, DIFF_DH), F32)
    lamv = lamv.at[0].set(lambda_q1).at[1].set(lambda_k1).at[2].set(lambda_q2).at[3].set(lambda_k2)
    ob = _attn(slopes2, qa, ka, vt, lamv, subln_g.reshape(-1, 1), nb, seq, tq, lam_init)

    x1, h2, hp = _post(x2, oa, ob, gates, mod3, norm2_g.reshape(1, d), w_o_gdn.astype(BF16),
                       w_o_diff.astype(BF16), w_out.astype(BF16), seq)

    wr_hi = w_router.T.astype(BF16)
    wr_lo = (w_router.T - wr_hi.astype(F32)).astype(BF16)
    idx_t, wts_t, counts = _router(h2, jnp.stack([wr_hi, wr_lo]), router_bias.reshape(-1, 1))
    idx = jnp.pad(idx_t.T, ((0, 0), (0, LANES - TOP_K)))
    wts = jnp.pad(wts_t.T, ((0, 0), (0, LANES - TOP_K)))
    cnt = counts[0].astype(jnp.int32)
    padded = (cnt + EXPERT_BLOCK - 1) // EXPERT_BLOCK * EXPERT_BLOCK
    pends = jnp.cumsum(padded)
    pstart = (pends - padded).astype(F32).reshape(1, -1)
    nblk = -(-(t * TOP_K) // EXPERT_BLOCK) + N_EXPERTS
    blk_start = ((pends - padded) // EXPERT_BLOCK).astype(jnp.int32)
    blk_count = (padded // EXPERT_BLOCK).astype(jnp.int32)
    dest = _rank(idx, pstart)
    dest_flat = dest[:, :TOP_K].reshape(-1)
    n_rows = nblk * EXPERT_BLOCK
    assert n_rows % t == 0
    owns = blk_count > 0
    blk_ids = jnp.arange(nblk, dtype=jnp.int32)
    is_last = jnp.any(owns[None, :] & ((blk_start + blk_count - 1)[None, :] == blk_ids[:, None]), axis=1)
    to_zero = is_last | (blk_ids >= blk_start[-1] + blk_count[-1])
    zero_blk = jnp.argsort(jnp.logical_not(to_zero), stable=True).astype(jnp.int32)
    n_zero = jnp.sum(to_zero.astype(jnp.int32)).reshape(1)
    xs = _dispatch(dest_flat, zero_blk, n_zero, hp, n_rows)
    inv = _invert(dest_flat, n_rows, t)
    assert t >= EXPERT_BLOCK
    blk_e = jnp.minimum(jnp.sum((pends[None, :] <= (jnp.arange(nblk, dtype=jnp.int32) * EXPERT_BLOCK)[:, None])
                                .astype(jnp.int32), axis=1), N_EXPERTS - 1).astype(jnp.int32)
    owner_list = jnp.argsort(jnp.logical_not(owns), stable=True).astype(jnp.int32)
    owner_pos = (jnp.cumsum(owns.astype(jnp.int32)) - 1).astype(jnp.int32)
    meta = jnp.stack([blk_start[-1] + blk_count[-1], jnp.sum(owns.astype(jnp.int32))]).astype(jnp.int32)
    slots = _experts(blk_start, blk_e, owner_list, owner_pos, meta, xs, inv.reshape(nblk, EXPERT_BLOCK),
                     w_exp_gate_up, w_exp_down, TOP_K * t + n_rows + t)
    out = _combine(x1, h2, wts, mod3, w_shared_gate_up.astype(BF16), w_shared_down.astype(BF16),
                   slots.reshape(-1, t, d // 2), seq)
    return out.reshape(nb, seq, d)


def kernel(x, c, w_ada, b_ada, norm1_g, w_in, conv_w, a_log, dt_bias, gdn_norm_g, w_o_gdn, q_norm_g, k_norm_g, lambda_q1, lambda_k1, lambda_q2, lambda_k2, subln_g, w_o_diff, w_out, norm2_g, w_router, router_bias, w_exp_gate_up, w_exp_down, w_shared_gate_up, w_shared_down):
    params = (w_ada, b_ada, norm1_g, w_in, conv_w, a_log, dt_bias, gdn_norm_g, w_o_gdn, q_norm_g,
              k_norm_g, lambda_q1, lambda_k1, lambda_q2, lambda_k2, subln_g, w_o_diff, w_out, norm2_g,
              w_router, router_bias, w_exp_gate_up, w_exp_down, w_shared_gate_up, w_shared_down)
    for layer in range(w_ada.shape[0]):
        x = _layer(x, c, layer, *(p[layer] for p in params))
    return x
```
